```python
import jax, jax.numpy as jnp
from jax import lax
import numpy as np

D_MODEL = 1024
BATCH = 8
SEQ = 4096
DEPTH = 1

D_CONV = D_MODEL // 2
CONV_WIDTH = 31
GROUP_DIM = 64
N_HEADS = 8
N_KV_HEADS = 2
HEAD_DIM = 64
D_ATTN = N_HEADS * HEAD_DIM
D_KV = N_KV_HEADS * HEAD_DIM
WINDOW = 128
BLOCK = 128
D_IN = 2 * D_CONV + D_ATTN + 2 * D_KV

N_GROUPS = 4
EXPERTS_PER_GROUP = 8
N_EXPERTS = N_GROUPS * EXPERTS_PER_GROUP
TOP_K = 2
D_EXPERT = D_MODEL // 4

ALPHA = (2.0 * DEPTH) ** 0.25
BETA = (8.0 * DEPTH) ** -0.25
EPS = 1e-5
NEG_INF = -1e30

kernel_name = "hymba_conformer_swa_sink_hmoe_deepnorm_adaln"


def layer_norm(x, g, b):
    xf = x.astype(jnp.float32)
    mu = jnp.mean(xf, axis=-1, keepdims=True)
    var = jnp.mean(jnp.square(xf - mu), axis=-1, keepdims=True)
    y = (xf - mu) * lax.rsqrt(var + EPS)
    return (y * g.astype(jnp.float32) + b.astype(jnp.float32)).astype(x.dtype)


def group_rms_norm(y, g):
    shp = y.shape
    yf = y.astype(jnp.float32).reshape(shp[:-1] + (shp[-1] // GROUP_DIM, GROUP_DIM))
    yf = yf * lax.rsqrt(jnp.mean(jnp.square(yf), axis=-1, keepdims=True) + EPS)
    return (yf.reshape(shp) * g.astype(jnp.float32)).astype(y.dtype)


def conformer_conv(u_a, u_b, conv_w, conv_b, ln_g, ln_b):
    v = u_a * jax.nn.sigmoid(u_b)
    y = lax.conv_general_dilated(
        v, conv_w[:, None, :].astype(v.dtype), window_strides=(1,),
        padding=[(CONV_WIDTH - 1, 0)],
        dimension_numbers=("NWC", "WIO", "NWC"),
        feature_group_count=D_CONV) + conv_b
    y = layer_norm(y, ln_g, ln_b)
    return jax.nn.silu(y)


def sliding_window_sink_attention(q, k, v, sinks):
    B, S, H, Dh = q.shape
    KV = k.shape[2]
    G = H // KV
    NB = S // BLOCK
    qb = q.reshape(B, NB, BLOCK, KV, G, Dh)
    pad = jnp.zeros((B, BLOCK, KV, Dh), k.dtype)
    kp = jnp.concatenate([pad, k], axis=1).reshape(B, NB + 1, BLOCK, KV, Dh)
    vp = jnp.concatenate([pad, v], axis=1).reshape(B, NB + 1, BLOCK, KV, Dh)
    kb = jnp.concatenate([kp[:, :-1], kp[:, 1:]], axis=2)
    vb = jnp.concatenate([vp[:, :-1], vp[:, 1:]], axis=2)
    s = jnp.einsum("bnqkgd,bnskd->bnkgqs", qb, kb).astype(jnp.float32) * (Dh ** -0.5)
    qi = jnp.arange(BLOCK)[:, None]
    si = jnp.arange(2 * BLOCK)[None, :]
    diff = qi + BLOCK - si
    band = (diff >= 0) & (diff < WINDOW)
    key_pos = jnp.arange(NB)[:, None, None] * BLOCK - BLOCK + si[None]
    valid = band[None] & (key_pos >= 0)
    s = jnp.where(valid[None, :, None, None], s, NEG_INF)
    sink = jnp.broadcast_to(sinks.astype(jnp.float32).reshape(1, 1, KV, G, 1, 1),
                            s.shape[:-1] + (1,))
    p = jax.nn.softmax(jnp.concatenate([s, sink], axis=-1), axis=-1)[..., :-1]
    o = jnp.einsum("bnkgqs,bnskd->bnqkgd", p.astype(v.dtype), vb)
    return o.reshape(B, S, H * Dh)


def hierarchical_moe(h, w_rg, b_rg, w_re, b_re, w_gate, w_up, w_down):
    B, S, D = h.shape
    t = h.reshape(-1, D)
    T = t.shape[0]
    p_group = jax.nn.softmax((t @ w_rg + b_rg).astype(jnp.float32), axis=-1)
    p_top, g_idx = lax.top_k(p_group, 1)
    e_logits = (t @ w_re + b_re).astype(jnp.float32).reshape(T, N_GROUPS, EXPERTS_PER_GROUP)
    sel = jnp.take_along_axis(e_logits, g_idx[:, :, None], axis=1)[:, 0]
    p_exp = jax.nn.softmax(sel, axis=-1)
    w_top, e_idx = lax.top_k(p_exp, TOP_K)
    w_top = w_top / jnp.sum(w_top, axis=-1, keepdims=True) * p_top
    eid = g_idx * EXPERTS_PER_GROUP + e_idx
    gates = jnp.sum(jax.nn.one_hot(eid, N_EXPERTS, dtype=jnp.float32) * w_top[..., None],
                    axis=1).astype(t.dtype)
    y = jnp.zeros_like(t)
    for e in range(N_EXPERTS):
        a = jax.nn.silu(t @ w_gate[e]) * (t @ w_up[e])
        y = y + gates[:, e:e + 1] * (a @ w_down[e])
    return y.reshape(B, S, D)


def setup_inputs(seed: int = 0) -> dict:
    key = jax.random.key(seed)
    ks = jax.random.split(key, 32)
    L, D = DEPTH, D_MODEL
    nrm = lambda k, shp, s: jax.random.normal(k, shp, jnp.float32) * s
    col_scale = jnp.concatenate([
        jnp.full((2 * D_CONV,), BETA), jnp.ones((D_ATTN + D_KV,)), jnp.full((D_KV,), BETA)])
    return {
        "x": nrm(ks[0], (BATCH, SEQ, D), 1.0),
        "c": nrm(ks[1], (BATCH, D), 1.0),
        "w_ada": nrm(ks[2], (L, D, 6 * D), 0.5 * D ** -0.5),
        "b_ada": nrm(ks[3], (L, 6 * D), 0.02),
        "w_in": nrm(ks[4], (L, D, D_IN), D ** -0.5) * col_scale,
        "b_in": nrm(ks[5], (L, D_IN), 0.02),
        "conv_w": nrm(ks[6], (L, CONV_WIDTH, D_CONV), CONV_WIDTH ** -0.5),
        "conv_b": nrm(ks[7], (L, D_CONV), 0.02),
        "conv_ln_g": 1.0 + nrm(ks[8], (L, D_CONV), 0.02),
        "conv_ln_b": nrm(ks[9], (L, D_CONV), 0.02),
        "conv_out_g": 1.0 + nrm(ks[10], (L, D_CONV), 0.02),
        "sinks": nrm(ks[11], (L, N_HEADS), 0.5),
        "attn_out_g": 1.0 + nrm(ks[12], (L, D_ATTN), 0.02),
        "w_out": nrm(ks[13], (L, D_CONV + D_ATTN, D), (D_CONV + D_ATTN) ** -0.5 * BETA),
        "b_out": nrm(ks[14], (L, D), 0.02),
        "ln1_g": 1.0 + nrm(ks[15], (L, D), 0.02),
        "ln1_b": nrm(ks[16], (L, D), 0.02),
        "w_router_group": nrm(ks[17], (L, D, N_GROUPS), D ** -0.5),
        "b_router_group": nrm(ks[18], (L, N_GROUPS), 0.01),
        "w_router_expert": nrm(ks[19], (L, D, N_EXPERTS), D ** -0.5),
        "b_router_expert": nrm(ks[20], (L, N_EXPERTS), 0.01),
        "w_gate": nrm(ks[21], (L, N_EXPERTS, D, D_EXPERT), D ** -0.5 * BETA),
        "w_up": nrm(ks[22], (L, N_EXPERTS, D, D_EXPERT), D ** -0.5 * BETA),
        "w_down": nrm(ks[23], (L, N_EXPERTS, D_EXPERT, D), D_EXPERT ** -0.5 * BETA),
        "ln2_g": 1.0 + nrm(ks[24], (L, D), 0.02),
        "ln2_b": nrm(ks[25], (L, D), 0.02),
    }


def reference(x, c, w_ada, b_ada, w_in, b_in, conv_w, conv_b, conv_ln_g, conv_ln_b,
              conv_out_g, sinks, attn_out_g, w_out, b_out, ln1_g, ln1_b,
              w_router_group, b_router_group, w_router_expert, b_router_expert,
              w_gate, w_up, w_down, ln2_g, ln2_b):
    B, S, _ = x.shape
    c_act = jax.nn.silu(c)
    for l in range(DEPTH):
        mod = c_act @ w_ada[l] + b_ada[l]
        sh1, sc1, g1, sh2, sc2, g2 = [m[:, None, :] for m in jnp.split(mod, 6, axis=-1)]

        h = x * (1.0 + sc1) + sh1
        proj = h @ w_in[l] + b_in[l]
        u_a, u_b, q, k, v = jnp.split(
            proj, [D_CONV, 2 * D_CONV, 2 * D_CONV + D_ATTN, 2 * D_CONV + D_ATTN + D_KV], axis=-1)
        y_conv = conformer_conv(u_a, u_b, conv_w[l], conv_b[l], conv_ln_g[l], conv_ln_b[l])
        y_conv = group_rms_norm(y_conv, conv_out_g[l])
        y_attn = sliding_window_sink_attention(
            q.reshape(B, S, N_HEADS, HEAD_DIM), k.reshape(B, S, N_KV_HEADS, HEAD_DIM),
            v.reshape(B, S, N_KV_HEADS, HEAD_DIM), sinks[l])
        y_attn = group_rms_norm(y_attn, attn_out_g[l])
        mix = jnp.concatenate([y_conv, y_attn], axis=-1) @ w_out[l] + b_out[l]
        x = layer_norm(ALPHA * x + g1 * mix, ln1_g[l], ln1_b[l])

        h2 = x * (1.0 + sc2) + sh2
        ffn = hierarchical_moe(h2, w_router_group[l], b_router_group[l], w_router_expert[l],
                               b_router_expert[l], w_gate[l], w_up[l], w_down[l])
        x = layer_norm(ALPHA * x + g2 * ffn, ln2_g[l], ln2_b[l])
    return x
```

```python
import functools

import jax
import jax.numpy as jnp
from jax import lax
from jax.experimental import pallas as pl
from jax.experimental.pallas import tpu as pltpu

F32 = jnp.float32
BF16 = jnp.bfloat16

D_MODEL = 1024
D_CONV = 512
CONV_WIDTH = 31
GROUP_DIM = 64
N_HEADS = 8
N_KV_HEADS = 2
HEAD_DIM = 64
D_ATTN = N_HEADS * HEAD_DIM
D_KV = N_KV_HEADS * HEAD_DIM
BLOCK = 128
D_IN = 2 * D_CONV + D_ATTN + 2 * D_KV
N_GROUPS = 4
EXPERTS_PER_GROUP = 8
N_EXPERTS = N_GROUPS * EXPERTS_PER_GROUP
D_EXPERT = D_MODEL // 4
DEPTH = 1
ALPHA = (2.0 * DEPTH) ** 0.25
EPS = 1e-5
NEG = -1e30

LANES = 128
SUBLANES = 8
TS = 512
HALO = 32
RC = 32
TM = 1024
ROUTER_COLS = LANES
VMEM_LIMIT = 56 * 1024 * 1024


def _dot(a, b):
    return jnp.dot(a, b, preferred_element_type=F32)


def _dot_nt(a, b):
    return lax.dot_general(a, b, (((1,), (1,)), ((), ())), preferred_element_type=F32)


def _fill_group_maps(bsel_ref, bexp_ref):
    c = bsel_ref.shape[0]
    ch = lax.broadcasted_iota(jnp.int32, (c, LANES), 0) // GROUP_DIM
    gi = lax.broadcasted_iota(jnp.int32, (c, LANES), 1)
    bsel_ref[...] = jnp.where(ch == gi, 1.0 / GROUP_DIM, 0.0).astype(BF16)
    gi2 = lax.broadcasted_iota(jnp.int32, (LANES, c), 0)
    ch2 = lax.broadcasted_iota(jnp.int32, (LANES, c), 1) // GROUP_DIM
    bexp_ref[...] = jnp.where(ch2 == gi2, 1.0, 0.0).astype(BF16)


def _group_rms(y, bsel, bexp):
    ms = _dot((y * y).astype(BF16), bsel)
    r = lax.rsqrt(ms + EPS)
    r_hi = r.astype(BF16)
    r_lo = (r - r_hi.astype(F32)).astype(BF16)
    return y * (_dot(r_hi, bexp) + _dot(r_lo, bexp))


def _layer_norm(y, g, b):
    mu = jnp.mean(y, axis=-1, keepdims=True)
    d = y - mu
    var = jnp.mean(d * d, axis=-1, keepdims=True)
    return d * lax.rsqrt(var + EPS) * g + b


def _mod_kernel(c_ref, w_ref, b_ref, o_ref):
    c = c_ref[...]
    c_act = c * jax.nn.sigmoid(c)
    o_ref[...] = jnp.dot(c_act, w_ref[...], preferred_element_type=F32,
                         precision=lax.Precision.HIGHEST) + b_ref[...]


def _modulation(c, w_ada, b_ada):
    bsz = c.shape[0]
    n = w_ada.shape[1]
    return pl.pallas_call(
        _mod_kernel,
        grid=(n // D_MODEL,),
        in_specs=[pl.BlockSpec((bsz, D_MODEL), lambda j: (0, 0)),
                  pl.BlockSpec((D_MODEL, D_MODEL), lambda j: (0, j)),
                  pl.BlockSpec((1, D_MODEL), lambda j: (0, j))],
        out_specs=pl.BlockSpec((bsz, D_MODEL), lambda j: (0, j)),
        out_shape=jax.ShapeDtypeStruct((bsz, n), F32),
        name="adaln_mod",
    )(c, w_ada, b_ada)


def _inproj_conv_kernel(x_ref, sh_ref, sc_ref, w_ref, b_ref, cw_ref, cb_ref, lg_ref, lb_ref, og_ref,
                        yc_ref, q_ref, kv_ref, wbf, gs, wb, bsel, bexp):
    first = (pl.program_id(0) == 0) & (pl.program_id(1) == 0)
    s = pl.program_id(1)

    @pl.when(first)
    def _():
        wbf[...] = w_ref[...].astype(BF16)
        wb[...] = jnp.broadcast_to(cw_ref[...][:, None, :], wb.shape)
        _fill_group_maps(bsel, bexp)

    @pl.when(s == 0)
    def _():
        gs[0, 0:HALO, :] = jnp.zeros((HALO, D_CONV), F32)

    @pl.when(s > 0)
    def _():
        gs[0, 0:HALO, :] = gs[0, TS:TS + HALO, :]

    h = (x_ref[...] * (1.0 + sc_ref[...]) + sh_ref[...]).astype(BF16)
    u_a = _dot(h, wbf[:, 0:D_CONV]) + b_ref[:, 0:D_CONV]
    u_b = _dot(h, wbf[:, D_CONV:2 * D_CONV]) + b_ref[:, D_CONV:2 * D_CONV]
    gs[0, HALO:HALO + TS, :] = u_a * jax.nn.sigmoid(u_b)
    q0 = 2 * D_CONV
    q = _dot(h, wbf[:, q0:q0 + D_ATTN]) + b_ref[:, q0:q0 + D_ATTN]
    q_ref[...] = (q * (HEAD_DIM ** -0.5)).astype(BF16)
    k0 = q0 + D_ATTN
    kv_ref[...] = (_dot(h, wbf[:, k0:k0 + 2 * D_KV]) + b_ref[:, k0:k0 + 2 * D_KV]).astype(BF16)

    n_sh = TS + HALO - SUBLANES
    for j in range(1, SUBLANES):
        gs[j, 0:n_sh, :] = gs[0, j:j + n_sh, :]

    cb = cb_ref[...]
    lg = lg_ref[...]
    lb = lb_ref[...]
    og = og_ref[...]

    def chunk(i, carry):
        r0 = pl.multiple_of(i * RC, RC)
        acc = jnp.broadcast_to(cb, (RC, D_CONV)).reshape(RC // SUBLANES, SUBLANES, D_CONV)
        for k in range(CONV_WIDTH):
            off = HALO - (CONV_WIDTH - 1) + k
            seg = gs[off % SUBLANES, pl.ds(r0 + (off // SUBLANES) * SUBLANES, RC), :]
            acc = acc + wb[k][None] * seg.reshape(RC // SUBLANES, SUBLANES, D_CONV)
        y = _layer_norm(acc.reshape(RC, D_CONV), lg, lb)
        y = y * jax.nn.sigmoid(y)
        y = _group_rms(y, bsel[...], bexp[...]) * og
        yc_ref[pl.ds(r0, RC), :] = y.astype(BF16)
        return carry

    lax.fori_loop(0, TS // RC, chunk, 0)


def _inproj_conv(x, sh1, sc1, w_in, b_in, conv_w, conv_b, ln_g, ln_b, out_g):
    bsz, seq, _ = x.shape
    vec = lambda n: pl.BlockSpec((1, n), lambda b, s: (0, 0))
    mod = pl.BlockSpec((None, None, 1, D_MODEL), lambda b, s: (b, 0, 0, 0))
    tile = lambda n: pl.BlockSpec((None, TS, n), lambda b, s: (b, s, 0))
    return pl.pallas_call(
        _inproj_conv_kernel,
        grid=(bsz, seq // TS),
        in_specs=[tile(D_MODEL), mod, mod,
                  pl.BlockSpec((D_MODEL, D_IN), lambda b, s: (0, 0)), vec(D_IN),
                  pl.BlockSpec((CONV_WIDTH, D_CONV), lambda b, s: (0, 0)),
                  vec(D_CONV), vec(D_CONV), vec(D_CONV), vec(D_CONV)],
        out_specs=[tile(D_CONV), tile(D_ATTN), tile(2 * D_KV)],
        out_shape=[jax.ShapeDtypeStruct((bsz, seq, D_CONV), BF16),
                   jax.ShapeDtypeStruct((bsz, seq, D_ATTN), BF16),
                   jax.ShapeDtypeStruct((bsz, seq, 2 * D_KV), BF16)],
        scratch_shapes=[pltpu.VMEM((D_MODEL, D_IN), BF16),
                        pltpu.VMEM((SUBLANES, TS + HALO, D_CONV), F32),
                        pltpu.VMEM((CONV_WIDTH, SUBLANES, D_CONV), F32),
                        pltpu.VMEM((D_CONV, LANES), BF16),
                        pltpu.VMEM((LANES, D_CONV), BF16)],
        compiler_params=pltpu.CompilerParams(
            dimension_semantics=("arbitrary", "arbitrary"), vmem_limit_bytes=VMEM_LIMIT),
        name="inproj_conv",
    )(x, sh1, sc1, w_in, b_in, conv_w, conv_b, ln_g, ln_b, out_g)


def _attn_kernel(sinks_ref, q_ref, kvc_ref, kvp_ref, og_ref, o_ref, bsel, bexp):
    n = pl.program_id(1)

    @pl.when((pl.program_id(0) == 0) & (n == 0))
    def _():
        _fill_group_maps(bsel, bexp)

    qi = lax.broadcasted_iota(jnp.int32, (BLOCK, BLOCK), 0)
    kj = lax.broadcasted_iota(jnp.int32, (BLOCK, BLOCK), 1)
    mask_cur = kj <= qi
    mask_prev = (kj > qi) & (n > 0)
    q = q_ref[...]
    kvc = kvc_ref[...]
    kvp = kvp_ref[...]
    outs = []
    for hd in range(N_HEADS):
        g = hd // (N_HEADS // N_KV_HEADS)
        qh = q[:, hd * HEAD_DIM:(hd + 1) * HEAD_DIM]
        kc = kvc[:, g * HEAD_DIM:(g + 1) * HEAD_DIM]
        kp = kvp[:, g * HEAD_DIM:(g + 1) * HEAD_DIM]
        vc = kvc[:, D_KV + g * HEAD_DIM:D_KV + (g + 1) * HEAD_DIM]
        vp = kvp[:, D_KV + g * HEAD_DIM:D_KV + (g + 1) * HEAD_DIM]
        s_c = jnp.where(mask_cur, _dot_nt(qh, kc), NEG)
        s_p = jnp.where(mask_prev, _dot_nt(qh, kp), NEG)
        sink = sinks_ref[hd]
        m = jnp.maximum(jnp.maximum(jnp.max(s_c, axis=-1, keepdims=True),
                                    jnp.max(s_p, axis=-1, keepdims=True)), sink)
        p_c = jnp.exp(s_c - m)
        p_p = jnp.exp(s_p - m)
        den = (jnp.sum(p_c, axis=-1, keepdims=True) + jnp.sum(p_p, axis=-1, keepdims=True)
               + jnp.exp(sink - m))
        o = _dot(p_c.astype(BF16), vc) + _dot(p_p.astype(BF16), vp)
        outs.append(o / den)
    y = jnp.concatenate(outs, axis=-1)
    o_ref[...] = (_group_rms(y, bsel[...], bexp[...]) * og_ref[...]).astype(BF16)


def _attention(q, kv, sinks, out_g):
    bsz, seq, _ = q.shape
    grid_spec = pltpu.PrefetchScalarGridSpec(
        num_scalar_prefetch=1,
        grid=(bsz, seq // BLOCK),
        in_specs=[pl.BlockSpec((None, BLOCK, D_ATTN), lambda b, n, sk: (b, n, 0)),
                  pl.BlockSpec((None, BLOCK, 2 * D_KV), lambda b, n, sk: (b, n, 0)),
                  pl.BlockSpec((None, BLOCK, 2 * D_KV), lambda b, n, sk: (b, jnp.maximum(n - 1, 0), 0)),
                  pl.BlockSpec((1, D_ATTN), lambda b, n, sk: (0, 0))],
        out_specs=pl.BlockSpec((None, BLOCK, D_ATTN), lambda b, n, sk: (b, n, 0)),
        scratch_shapes=[pltpu.VMEM((D_ATTN, LANES), BF16), pltpu.VMEM((LANES, D_ATTN), BF16)],
    )
    return pl.pallas_call(
        _attn_kernel,
        grid_spec=grid_spec,
        out_shape=jax.ShapeDtypeStruct((bsz, seq, D_ATTN), BF16),
        compiler_params=pltpu.CompilerParams(dimension_semantics=("arbitrary", "arbitrary")),
        name="swa_attention",
    )(sinks, q, kv, kv, out_g)


def _route(logits):
    rows = logits.shape[0]
    lane = lax.broadcasted_iota(jnp.int32, (rows, ROUTER_COLS), 1)
    lanef = lane.astype(F32)
    is_g = (lane >= N_EXPERTS) & (lane < N_EXPERTS + N_GROUPS)
    gl = jnp.where(is_g, logits, NEG)
    gmax = jnp.max(gl, axis=-1, keepdims=True)
    p_top = 1.0 / jnp.sum(jnp.where(is_g, jnp.exp(gl - gmax), 0.0), axis=-1, keepdims=True)
    gidx = jnp.min(jnp.where(is_g & (gl == gmax), lanef - N_EXPERTS, 99.0), axis=-1, keepdims=True)
    in_grp = (lane < N_EXPERTS) & ((lane // EXPERTS_PER_GROUP) == gidx.astype(jnp.int32))
    el = jnp.where(in_grp, logits, NEG)
    m1 = jnp.max(el, axis=-1, keepdims=True)
    i1 = jnp.min(jnp.where(in_grp & (el == m1), lanef, 999.0), axis=-1, keepdims=True)
    rest = in_grp & (lanef != i1)
    el2 = jnp.where(rest, logits, NEG)
    m2 = jnp.max(el2, axis=-1, keepdims=True)
    i2 = jnp.min(jnp.where(rest & (el2 == m2), lanef, 999.0), axis=-1, keepdims=True)
    e2 = jnp.exp(m2 - m1)
    w1 = p_top / (1.0 + e2)
    return jnp.where(lanef == i1, w1, 0.0) + jnp.where(lanef == i2, w1 * e2, 0.0)


def _outproj_kernel(x_ref, yc_ref, ya_ref, w_ref, b_ref, g1_ref, lg_ref, lb_ref, sc2_ref, sh2_ref,
                    wr_ref, br_ref, x1_ref, h2_ref, gates_ref, wbf, wrbf):
    @pl.when((pl.program_id(0) == 0) & (pl.program_id(1) == 0))
    def _():
        wbf[...] = w_ref[...].astype(BF16)
        wrbf[...] = wr_ref[...].astype(BF16)

    mix = _dot(yc_ref[...], wbf[0:D_CONV, :]) + _dot(ya_ref[...], wbf[D_CONV:, :]) + b_ref[...]
    x1 = _layer_norm(ALPHA * x_ref[...] + g1_ref[...] * mix, lg_ref[...], lb_ref[...])
    x1_ref[...] = x1
    h2 = (x1 * (1.0 + sc2_ref[...]) + sh2_ref[...]).astype(BF16)
    h2_ref[...] = h2
    gates_ref[...] = _route(_dot(h2, wrbf[...]) + br_ref[...])


def _outproj_route(x, y_conv, y_attn, w_out, b_out, g1, ln_g, ln_b, sc2, sh2, w_r, b_r):
    bsz, seq, _ = x.shape
    vec = lambda n: pl.BlockSpec((1, n), lambda b, s: (0, 0))
    mod = pl.BlockSpec((None, None, 1, D_MODEL), lambda b, s: (b, 0, 0, 0))
    tile = lambda n: pl.BlockSpec((None, TS, n), lambda b, s: (b, s, 0))
    return pl.pallas_call(
        _outproj_kernel,
        grid=(bsz, seq // TS),
        in_specs=[tile(D_MODEL), tile(D_CONV), tile(D_ATTN),
                  pl.BlockSpec((D_MODEL, D_MODEL), lambda b, s: (0, 0)), vec(D_MODEL),
                  mod, vec(D_MODEL), vec(D_MODEL), mod, mod,
                  pl.BlockSpec((D_MODEL, ROUTER_COLS), lambda b, s: (0, 0)), vec(ROUTER_COLS)],
        out_specs=[tile(D_MODEL), tile(D_MODEL), tile(ROUTER_COLS)],
        out_shape=[jax.ShapeDtypeStruct((bsz, seq, D_MODEL), F32),
                   jax.ShapeDtypeStruct((bsz, seq, D_MODEL), BF16),
                   jax.ShapeDtypeStruct((bsz, seq, ROUTER_COLS), F32)],
        scratch_shapes=[pltpu.VMEM((D_MODEL, D_MODEL), BF16), pltpu.VMEM((D_MODEL, ROUTER_COLS), BF16)],
        compiler_params=pltpu.CompilerParams(
            dimension_semantics=("arbitrary", "arbitrary"), vmem_limit_bytes=VMEM_LIMIT),
        name="outproj_route",
    )(x, y_conv, y_attn, w_out, b_out, g1, ln_g, ln_b, sc2, sh2, w_r, b_r)


def _moe_kernel(h2_ref, gates_ref, x1_ref, wg_ref, wu_ref, wd_ref, g2_ref, lg_ref, lb_ref, o_ref, acc):
    e = pl.program_id(2)

    @pl.when(e == 0)
    def _():
        acc[...] = jnp.zeros_like(acc)

    h2 = h2_ref[...]
    a = _dot(h2, wg_ref[...].astype(BF16))
    a = a * jax.nn.sigmoid(a) * _dot(h2, wu_ref[...].astype(BF16))
    y = _dot(a.astype(BF16), wd_ref[...].astype(BF16))
    lane = lax.broadcasted_iota(jnp.int32, gates_ref.shape, 1)
    gate = jnp.sum(jnp.where(lane == e, gates_ref[...], 0.0), axis=-1, keepdims=True)
    acc[...] += gate * y

    @pl.when(e == N_EXPERTS - 1)
    def _():
        o_ref[...] = _layer_norm(ALPHA * x1_ref[...] + g2_ref[...] * acc[...], lg_ref[...], lb_ref[...])


def _moe(h2, gates, x1, w_gate, w_up, w_down, g2, ln_g, ln_b):
    bsz, seq, _ = x1.shape
    vec = lambda n: pl.BlockSpec((1, n), lambda b, s, e: (0, 0))
    mod = pl.BlockSpec((None, None, 1, D_MODEL), lambda b, s, e: (b, 0, 0, 0))
    tile = lambda n: pl.BlockSpec((None, TM, n), lambda b, s, e: (b, s, 0))
    return pl.pallas_call(
        _moe_kernel,
        grid=(bsz, seq // TM, N_EXPERTS),
        in_specs=[tile(D_MODEL), tile(ROUTER_COLS), tile(D_MODEL),
                  pl.BlockSpec((None, D_MODEL, D_EXPERT), lambda b, s, e: (e, 0, 0)),
                  pl.BlockSpec((None, D_MODEL, D_EXPERT), lambda b, s, e: (e, 0, 0)),
                  pl.BlockSpec((None, D_EXPERT, D_MODEL), lambda b, s, e: (e, 0, 0)),
                  mod, vec(D_MODEL), vec(D_MODEL)],
        out_specs=tile(D_MODEL),
        out_shape=jax.ShapeDtypeStruct((bsz, seq, D_MODEL), F32),
        scratch_shapes=[pltpu.VMEM((TM, D_MODEL), F32)],
        compiler_params=pltpu.CompilerParams(
            dimension_semantics=("arbitrary", "arbitrary", "arbitrary"), vmem_limit_bytes=VMEM_LIMIT),
        name="moe_dense",
    )(h2, gates, x1, w_gate, w_up, w_down, g2, ln_g, ln_b)


def kernel(x, c, w_ada, b_ada, w_in, b_in, conv_w, conv_b, conv_ln_g, conv_ln_b, conv_out_g, sinks,
           attn_out_g, w_out, b_out, ln1_g, ln1_b, w_router_group, b_router_group, w_router_expert,
           b_router_expert, w_gate, w_up, w_down, ln2_g, ln2_b):
    assert w_ada.shape[0] == DEPTH
    bsz = x.shape[0]
    row = lambda v: v[0][None, :]

    mod = _modulation(c, w_ada[0], row(b_ada)).reshape(bsz, 6, 1, D_MODEL)
    sh1, sc1, g1, sh2, sc2, g2 = [mod[:, j:j + 1] for j in range(6)]

    y_conv, q, kv = _inproj_conv(x, sh1, sc1, w_in[0], row(b_in), conv_w[0], row(conv_b),
                                 row(conv_ln_g), row(conv_ln_b), row(conv_out_g))
    y_attn = _attention(q, kv, sinks[0], row(attn_out_g))

    pad = ROUTER_COLS - N_EXPERTS - N_GROUPS
    w_r = jnp.concatenate([w_router_expert[0], w_router_group[0], jnp.zeros((D_MODEL, pad), F32)], axis=1)
    b_r = jnp.concatenate([b_router_expert[0], b_router_group[0], jnp.zeros((pad,), F32)])[None, :]
    x1, h2, gates = _outproj_route(x, y_conv, y_attn, w_out[0], row(b_out), g1, row(ln1_g), row(ln1_b),
                                   sc2, sh2, w_r, b_r)
    return _moe(h2, gates, x1, w_gate[0], w_up[0], w_down[0], g2, row(ln2_g), row(ln2_b))
```

```python
import functools

import jax
import jax.numpy as jnp
from jax import lax
from jax.experimental import pallas as pl
from jax.experimental.pallas import tpu as pltpu

F32 = jnp.float32
BF16 = jnp.bfloat16

D_MODEL = 1024
D_CONV = 512
CONV_WIDTH = 31
GROUP_DIM = 64
N_HEADS = 8
N_KV_HEADS = 2
HEAD_DIM = 64
D_ATTN = N_HEADS * HEAD_DIM
D_KV = N_KV_HEADS * HEAD_DIM
BLOCK = 128
D_IN = 2 * D_CONV + D_ATTN + 2 * D_KV
N_GROUPS = 4
EXPERTS_PER_GROUP = 8
N_EXPERTS = N_GROUPS * EXPERTS_PER_GROUP
D_EXPERT = D_MODEL // 4
DEPTH = 1
ALPHA = (2.0 * DEPTH) ** 0.25
EPS = 1e-5
NEG = -1e30

LANES = 128
SUBLANES = 8
TS = 512
HALO = 32
RC = 32
ROUTER_COLS = LANES
VMEM_LIMIT = 56 * 1024 * 1024

PAIRS = [(lo, hi) for lo in range(EXPERTS_PER_GROUP) for hi in range(lo + 1, EXPERTS_PER_GROUP)]
N_PAIRS = len(PAIRS)
N_BUCKETS = N_GROUPS * N_PAIRS
TMM = 128
DMA_UNROLL = 8
CODE_SHIFT = 16


def _dot(a, b):
    return jnp.dot(a, b, preferred_element_type=F32)


def _dot_nt(a, b):
    return lax.dot_general(a, b, (((1,), (1,)), ((), ())), preferred_element_type=F32)


def _fill_group_maps(bsel_ref, bexp_ref):
    c = bsel_ref.shape[0]
    ch = lax.broadcasted_iota(jnp.int32, (c, LANES), 0) // GROUP_DIM
    gi = lax.broadcasted_iota(jnp.int32, (c, LANES), 1)
    bsel_ref[...] = jnp.where(ch == gi, 1.0 / GROUP_DIM, 0.0).astype(BF16)
    gi2 = lax.broadcasted_iota(jnp.int32, (LANES, c), 0)
    ch2 = lax.broadcasted_iota(jnp.int32, (LANES, c), 1) // GROUP_DIM
    bexp_ref[...] = jnp.where(ch2 == gi2, 1.0, 0.0).astype(BF16)


def _group_rms(y, bsel, bexp):
    ms = _dot((y * y).astype(BF16), bsel)
    r = lax.rsqrt(ms + EPS)
    r_hi = r.astype(BF16)
    r_lo = (r - r_hi.astype(F32)).astype(BF16)
    return y * (_dot(r_hi, bexp) + _dot(r_lo, bexp))


def _layer_norm(y, g, b):
    mu = jnp.mean(y, axis=-1, keepdims=True)
    d = y - mu
    var = jnp.mean(d * d, axis=-1, keepdims=True)
    return d * lax.rsqrt(var + EPS) * g + b


def _mod_kernel(c_ref, w_ref, b_ref, o_ref):
    c = c_ref[...]
    c_act = c * jax.nn.sigmoid(c)
    o_ref[...] = jnp.dot(c_act, w_ref[...], preferred_element_type=F32,
                         precision=lax.Precision.HIGHEST) + b_ref[...]


def _modulation(c, w_ada, b_ada):
    bsz = c.shape[0]
    n = w_ada.shape[1]
    return pl.pallas_call(
        _mod_kernel,
        grid=(n // D_MODEL,),
        in_specs=[pl.BlockSpec((bsz, D_MODEL), lambda j: (0, 0)),
                  pl.BlockSpec((D_MODEL, D_MODEL), lambda j: (0, j)),
                  pl.BlockSpec((1, D_MODEL), lambda j: (0, j))],
        out_specs=pl.BlockSpec((bsz, D_MODEL), lambda j: (0, j)),
        out_shape=jax.ShapeDtypeStruct((bsz, n), F32),
        name="adaln_mod",
    )(c, w_ada, b_ada)


def _inproj_conv_kernel(x_ref, sh_ref, sc_ref, w_ref, b_ref, cw_ref, cb_ref, lg_ref, lb_ref, og_ref,
                        yc_ref, q_ref, kv_ref, wbf, gs, wb, bsel, bexp):
    first = (pl.program_id(0) == 0) & (pl.program_id(1) == 0)
    s = pl.program_id(1)

    @pl.when(first)
    def _():
        wbf[...] = w_ref[...].astype(BF16)
        wb[...] = jnp.broadcast_to(cw_ref[...][:, None, :], wb.shape)
        _fill_group_maps(bsel, bexp)

    @pl.when(s == 0)
    def _():
        gs[0, 0:HALO, :] = jnp.zeros((HALO, D_CONV), F32)

    @pl.when(s > 0)
    def _():
        gs[0, 0:HALO, :] = gs[0, TS:TS + HALO, :]

    h = (x_ref[...] * (1.0 + sc_ref[...]) + sh_ref[...]).astype(BF16)
    u_a = _dot(h, wbf[:, 0:D_CONV]) + b_ref[:, 0:D_CONV]
    u_b = _dot(h, wbf[:, D_CONV:2 * D_CONV]) + b_ref[:, D_CONV:2 * D_CONV]
    gs[0, HALO:HALO + TS, :] = u_a * jax.nn.sigmoid(u_b)
    q0 = 2 * D_CONV
    q = _dot(h, wbf[:, q0:q0 + D_ATTN]) + b_ref[:, q0:q0 + D_ATTN]
    q_ref[...] = (q * (HEAD_DIM ** -0.5)).astype(BF16)
    k0 = q0 + D_ATTN
    kv_ref[...] = (_dot(h, wbf[:, k0:k0 + 2 * D_KV]) + b_ref[:, k0:k0 + 2 * D_KV]).astype(BF16)

    n_sh = TS + HALO - SUBLANES
    for j in range(1, SUBLANES):
        gs[j, 0:n_sh, :] = gs[0, j:j + n_sh, :]

    cb = cb_ref[...]
    lg = lg_ref[...]
    lb = lb_ref[...]
    og = og_ref[...]

    def chunk(i, carry):
        r0 = pl.multiple_of(i * RC, RC)
        acc = jnp.broadcast_to(cb, (RC, D_CONV)).reshape(RC // SUBLANES, SUBLANES, D_CONV)
        for k in range(CONV_WIDTH):
            off = HALO - (CONV_WIDTH - 1) + k
            seg = gs[off % SUBLANES, pl.ds(r0 + (off // SUBLANES) * SUBLANES, RC), :]
            acc = acc + wb[k][None] * seg.reshape(RC // SUBLANES, SUBLANES, D_CONV)
        y = _layer_norm(acc.reshape(RC, D_CONV), lg, lb)
        y = y * jax.nn.sigmoid(y)
        y = _group_rms(y, bsel[...], bexp[...]) * og
        yc_ref[pl.ds(r0, RC), :] = y.astype(BF16)
        return carry

    lax.fori_loop(0, TS // RC, chunk, 0)


def _inproj_conv(x, sh1, sc1, w_in, b_in, conv_w, conv_b, ln_g, ln_b, out_g):
    bsz, seq, _ = x.shape
    vec = lambda n: pl.BlockSpec((1, n), lambda b, s: (0, 0))
    mod = pl.BlockSpec((None, None, 1, D_MODEL), lambda b, s: (b, 0, 0, 0))
    tile = lambda n: pl.BlockSpec((None, TS, n), lambda b, s: (b, s, 0))
    return pl.pallas_call(
        _inproj_conv_kernel,
        grid=(bsz, seq // TS),
        in_specs=[tile(D_MODEL), mod, mod,
                  pl.BlockSpec((D_MODEL, D_IN), lambda b, s: (0, 0)), vec(D_IN),
                  pl.BlockSpec((CONV_WIDTH, D_CONV), lambda b, s: (0, 0)),
                  vec(D_CONV), vec(D_CONV), vec(D_CONV), vec(D_CONV)],
        out_specs=[tile(D_CONV), tile(D_ATTN), tile(2 * D_KV)],
        out_shape=[jax.ShapeDtypeStruct((bsz, seq, D_CONV), BF16),
                   jax.ShapeDtypeStruct((bsz, seq, D_ATTN), BF16),
                   jax.ShapeDtypeStruct((bsz, seq, 2 * D_KV), BF16)],
        scratch_shapes=[pltpu.VMEM((D_MODEL, D_IN), BF16),
                        pltpu.VMEM((SUBLANES, TS + HALO, D_CONV), F32),
                        pltpu.VMEM((CONV_WIDTH, SUBLANES, D_CONV), F32),
                        pltpu.VMEM((D_CONV, LANES), BF16),
                        pltpu.VMEM((LANES, D_CONV), BF16)],
        compiler_params=pltpu.CompilerParams(
            dimension_semantics=("arbitrary", "arbitrary"), vmem_limit_bytes=VMEM_LIMIT),
        name="inproj_conv",
    )(x, sh1, sc1, w_in, b_in, conv_w, conv_b, ln_g, ln_b, out_g)


def _attn_kernel(sinks_ref, q_ref, kvc_ref, kvp_ref, og_ref, o_ref, bsel, bexp):
    n = pl.program_id(1)

    @pl.when((pl.program_id(0) == 0) & (n == 0))
    def _():
        _fill_group_maps(bsel, bexp)

    qi = lax.broadcasted_iota(jnp.int32, (BLOCK, BLOCK), 0)
    kj = lax.broadcasted_iota(jnp.int32, (BLOCK, BLOCK), 1)
    mask_cur = kj <= qi
    mask_prev = (kj > qi) & (n > 0)
    q = q_ref[...]
    kvc = kvc_ref[...]
    kvp = kvp_ref[...]
    outs = []
    for hd in range(N_HEADS):
        g = hd // (N_HEADS // N_KV_HEADS)
        qh = q[:, hd * HEAD_DIM:(hd + 1) * HEAD_DIM]
        kc = kvc[:, g * HEAD_DIM:(g + 1) * HEAD_DIM]
        kp = kvp[:, g * HEAD_DIM:(g + 1) * HEAD_DIM]
        vc = kvc[:, D_KV + g * HEAD_DIM:D_KV + (g + 1) * HEAD_DIM]
        vp = kvp[:, D_KV + g * HEAD_DIM:D_KV + (g + 1) * HEAD_DIM]
        s_c = jnp.where(mask_cur, _dot_nt(qh, kc), NEG)
        s_p = jnp.where(mask_prev, _dot_nt(qh, kp), NEG)
        sink = sinks_ref[hd]
        m = jnp.maximum(jnp.maximum(jnp.max(s_c, axis=-1, keepdims=True),
                                    jnp.max(s_p, axis=-1, keepdims=True)), sink)
        p_c = jnp.exp(s_c - m)
        p_p = jnp.exp(s_p - m)
        den = (jnp.sum(p_c, axis=-1, keepdims=True) + jnp.sum(p_p, axis=-1, keepdims=True)
               + jnp.exp(sink - m))
        o = _dot(p_c.astype(BF16), vc) + _dot(p_p.astype(BF16), vp)
        outs.append(o / den)
    y = jnp.concatenate(outs, axis=-1)
    o_ref[...] = (_group_rms(y, bsel[...], bexp[...]) * og_ref[...]).astype(BF16)


def _attention(q, kv, sinks, out_g):
    bsz, seq, _ = q.shape
    grid_spec = pltpu.PrefetchScalarGridSpec(
        num_scalar_prefetch=1,
        grid=(bsz, seq // BLOCK),
        in_specs=[pl.BlockSpec((None, BLOCK, D_ATTN), lambda b, n, sk: (b, n, 0)),
                  pl.BlockSpec((None, BLOCK, 2 * D_KV), lambda b, n, sk: (b, n, 0)),
                  pl.BlockSpec((None, BLOCK, 2 * D_KV), lambda b, n, sk: (b, jnp.maximum(n - 1, 0), 0)),
                  pl.BlockSpec((1, D_ATTN), lambda b, n, sk: (0, 0))],
        out_specs=pl.BlockSpec((None, BLOCK, D_ATTN), lambda b, n, sk: (b, n, 0)),
        scratch_shapes=[pltpu.VMEM((D_ATTN, LANES), BF16), pltpu.VMEM((LANES, D_ATTN), BF16)],
    )
    return pl.pallas_call(
        _attn_kernel,
        grid_spec=grid_spec,
        out_shape=jax.ShapeDtypeStruct((bsz, seq, D_ATTN), BF16),
        compiler_params=pltpu.CompilerParams(dimension_semantics=("arbitrary", "arbitrary")),
        name="swa_attention",
    )(sinks, q, kv, kv, out_g)


def _group_softmax_top(logits, lane):
    is_g = (lane >= N_EXPERTS) & (lane < N_EXPERTS + N_GROUPS)
    gl = jnp.where(is_g, logits, NEG)
    gmax = jnp.max(gl, axis=-1, keepdims=True)
    p_top = 1.0 / jnp.sum(jnp.where(is_g, jnp.exp(gl - gmax), 0.0), axis=-1, keepdims=True)
    return is_g, gl, gmax, p_top


def _route_bucket(logits):
    rows = logits.shape[0]
    lane = lax.broadcasted_iota(jnp.int32, (rows, ROUTER_COLS), 1)
    lanef = lane.astype(F32)
    is_g, gl, gmax, _ = _group_softmax_top(logits, lane)
    gidx = jnp.min(jnp.where(is_g & (gl == gmax), lanef - N_EXPERTS, 99.0), axis=-1, keepdims=True)
    in_grp = (lane < N_EXPERTS) & ((lane // EXPERTS_PER_GROUP) == gidx.astype(jnp.int32))
    el = jnp.where(in_grp, logits, NEG)
    m1 = jnp.max(el, axis=-1, keepdims=True)
    i1 = jnp.min(jnp.where(in_grp & (el == m1), lanef, 999.0), axis=-1, keepdims=True)
    rest = in_grp & (lanef != i1)
    el2 = jnp.where(rest, logits, NEG)
    m2 = jnp.max(el2, axis=-1, keepdims=True)
    i2 = jnp.min(jnp.where(rest & (el2 == m2), lanef, 999.0), axis=-1, keepdims=True)
    lo = jnp.minimum(i1, i2) - EXPERTS_PER_GROUP * gidx
    hi = jnp.maximum(i1, i2) - EXPERTS_PER_GROUP * gidx
    pair = lo * (EXPERTS_PER_GROUP - 1) - lo * (lo - 1.0) * 0.5 + (hi - lo - 1.0)
    return gidx * N_PAIRS + pair, lanef


def _outproj_kernel(x_ref, yc_ref, ya_ref, w_ref, b_ref, g1_ref, lg_ref, lb_ref, sc2_ref, sh2_ref,
                    wr_ref, br_ref, x1_ref, code_ref, counts_ref, wbf, wrbf, ltri, running):
    @pl.when((pl.program_id(0) == 0) & (pl.program_id(1) == 0))
    def _():
        wbf[...] = w_ref[...].astype(BF16)
        wrbf[...] = wr_ref[...].astype(BF16)
        r = lax.broadcasted_iota(jnp.int32, (TS, TS), 0)
        c = lax.broadcasted_iota(jnp.int32, (TS, TS), 1)
        ltri[...] = jnp.where(c < r, 1.0, 0.0).astype(BF16)
        running[...] = jnp.zeros_like(running)

    mix = _dot(yc_ref[...], wbf[0:D_CONV, :]) + _dot(ya_ref[...], wbf[D_CONV:, :]) + b_ref[...]
    x1 = _layer_norm(ALPHA * x_ref[...] + g1_ref[...] * mix, lg_ref[...], lb_ref[...])
    x1_ref[...] = x1
    h2 = (x1 * (1.0 + sc2_ref[...]) + sh2_ref[...]).astype(BF16)
    bucket, lanef = _route_bucket(_dot(h2, wrbf[...]) + br_ref[...])
    mine = lanef == bucket
    onehot = jnp.where(mine, 1.0, 0.0)
    earlier = _dot(ltri[...], onehot.astype(BF16)) + running[...]
    rank = jnp.sum(jnp.where(mine, earlier, 0.0), axis=-1, keepdims=True)
    running[...] += jnp.sum(onehot, axis=0, keepdims=True)
    code = bucket.astype(jnp.int32) * (1 << CODE_SHIFT) + rank.astype(jnp.int32)
    code_ref[...] = jnp.broadcast_to(code, code_ref.shape)
    counts_ref[...] = running[...]


def _outproj_route(x, y_conv, y_attn, w_out, b_out, g1, ln_g, ln_b, sc2, sh2, w_r, b_r):
    bsz, seq, _ = x.shape
    vec = lambda n: pl.BlockSpec((1, n), lambda b, s: (0, 0))
    mod = pl.BlockSpec((None, None, 1, D_MODEL), lambda b, s: (b, 0, 0, 0))
    tile = lambda n: pl.BlockSpec((None, TS, n), lambda b, s: (b, s, 0))
    return pl.pallas_call(
        _outproj_kernel,
        grid=(bsz, seq // TS),
        in_specs=[tile(D_MODEL), tile(D_CONV), tile(D_ATTN),
                  pl.BlockSpec((D_MODEL, D_MODEL), lambda b, s: (0, 0)), vec(D_MODEL),
                  mod, vec(D_MODEL), vec(D_MODEL), mod, mod,
                  pl.BlockSpec((D_MODEL, ROUTER_COLS), lambda b, s: (0, 0)), vec(ROUTER_COLS)],
        out_specs=[tile(D_MODEL), tile(LANES), vec(LANES)],
        out_shape=[jax.ShapeDtypeStruct((bsz, seq, D_MODEL), F32),
                   jax.ShapeDtypeStruct((bsz, seq, LANES), jnp.int32),
                   jax.ShapeDtypeStruct((1, LANES), F32)],
        scratch_shapes=[pltpu.VMEM((D_MODEL, D_MODEL), BF16), pltpu.VMEM((D_MODEL, ROUTER_COLS), BF16),
                        pltpu.VMEM((TS, TS), BF16), pltpu.VMEM((1, LANES), F32)],
        compiler_params=pltpu.CompilerParams(
            dimension_semantics=("arbitrary", "arbitrary"), vmem_limit_bytes=VMEM_LIMIT),
        name="outproj_route",
    )(x, y_conv, y_attn, w_out, b_out, g1, ln_g, ln_b, sc2, sh2, w_r, b_r)


def _row_of(starts_ref, code):
    return (starts_ref[lax.shift_right_logical(code, jnp.int32(CODE_SHIFT))]
            + (code & ((1 << CODE_SHIFT) - 1)))


def _row_copy(src, src_row, dst, dst_row, sem):
    return pltpu.make_async_copy(src.at[pl.ds(src_row, 1)], dst.at[pl.ds(dst_row, 1)], sem)


def _scatter_kernel(starts_ref, zrow_ref, nt_ref, code_ref, x1_ref, sc2_ref, sh2_ref, hs_ref, hbuf, zeros, sems, zsem):
    nsteps = pl.num_programs(0) * pl.num_programs(1)
    step = pl.program_id(0) * pl.num_programs(1) + pl.program_id(1)
    slot = step % 2

    def zero_copy(b):
        return pltpu.make_async_copy(zeros, hs_ref.at[pl.ds(pl.multiple_of(zrow_ref[b], TMM), TMM)], zsem)

    def tail_copy(t):
        return pltpu.make_async_copy(zeros, hs_ref.at[pl.ds(pl.multiple_of(t * TMM, TMM), TMM)], zsem)

    @pl.when(step == 0)
    def _():
        zeros[...] = jnp.zeros_like(zeros)
        n_tiles = hs_ref.shape[0] // TMM

        def start(b, c):
            @pl.when(zrow_ref[b] >= 0)
            def _():
                zero_copy(b).start()
            return c

        def wait(b, c):
            @pl.when(zrow_ref[b] >= 0)
            def _():
                zero_copy(b).wait()
            return c

        def start_tail(t, c):
            tail_copy(t).start()
            return c

        def wait_tail(t, c):
            tail_copy(t).wait()
            return c

        lax.fori_loop(0, N_BUCKETS, start, 0)
        lax.fori_loop(nt_ref[0], n_tiles, start_tail, 0)
        lax.fori_loop(0, N_BUCKETS, wait, 0)
        lax.fori_loop(nt_ref[0], n_tiles, wait_tail, 0)

    def wait_slot(sl):
        pltpu.make_async_copy(hbuf.at[sl], hs_ref.at[pl.ds(0, TS)], sems.at[sl]).wait()

    @pl.when(step >= 2)
    def _():
        wait_slot(slot)

    hbuf[slot] = x1_ref[...] * (1.0 + sc2_ref[...]) + sh2_ref[...]

    def issue(i, c):
        for u in range(DMA_UNROLL):
            r = i * DMA_UNROLL + u
            _row_copy(hbuf.at[slot], r, hs_ref, _row_of(starts_ref, code_ref[0, r]), sems.at[slot]).start()
        return c

    lax.fori_loop(0, TS // DMA_UNROLL, issue, 0)

    @pl.when(step == nsteps - 1)
    def _():
        wait_slot(slot)
        wait_slot(1 - slot)


def _scatter_rows(starts, zrow, nt, code, x1, sc2, sh2, n_rows):
    bsz, seq, _ = x1.shape
    nst = seq // TS
    grid_spec = pltpu.PrefetchScalarGridSpec(
        num_scalar_prefetch=3,
        grid=(bsz, nst),
        in_specs=[pl.BlockSpec((None, 1, TS), lambda b, s, *_: (b * nst + s, 0, 0), memory_space=pltpu.SMEM),
                  pl.BlockSpec((None, TS, D_MODEL), lambda b, s, *_: (b, s, 0)),
                  pl.BlockSpec((None, None, 1, D_MODEL), lambda b, s, *_: (b, 0, 0, 0)),
                  pl.BlockSpec((None, None, 1, D_MODEL), lambda b, s, *_: (b, 0, 0, 0))],
        out_specs=pl.BlockSpec(memory_space=pl.ANY),
        scratch_shapes=[pltpu.VMEM((2, TS, D_MODEL), F32), pltpu.VMEM((TMM, D_MODEL), F32),
                        pltpu.SemaphoreType.DMA((2,)), pltpu.SemaphoreType.DMA],
    )
    return pl.pallas_call(
        _scatter_kernel,
        grid_spec=grid_spec,
        out_shape=jax.ShapeDtypeStruct((n_rows, D_MODEL), F32),
        compiler_params=pltpu.CompilerParams(dimension_semantics=("arbitrary", "arbitrary")),
        name="moe_scatter",
    )(starts, zrow, nt, code.reshape(bsz * nst, 1, TS), x1, sc2, sh2)


def _moe_kernel(tlo_ref, thi_ref, blk_ref, nt_ref, h_ref, wgl_ref, wul_ref, wdl_ref, wgh_ref, wuh_ref, wdh_ref,
                wr_ref, br_ref, y_ref, wrbf):
    i = pl.program_id(0)

    @pl.when(i == 0)
    def _():
        wrbf[...] = wr_ref[...].astype(BF16)

    @pl.when(i < nt_ref[0])
    def _():
        h = h_ref[...].astype(BF16)
        logits = _dot(h, wrbf[...]) + br_ref[...]
        lane = lax.broadcasted_iota(jnp.int32, logits.shape, 1)
        p_top = _group_softmax_top(logits, lane)[3]
        l_lo = jnp.sum(jnp.where(lane == tlo_ref[i], logits, 0.0), axis=-1, keepdims=True)
        l_hi = jnp.sum(jnp.where(lane == thi_ref[i], logits, 0.0), axis=-1, keepdims=True)
        w_lo = p_top / (1.0 + jnp.exp(l_hi - l_lo))
        w_hi = p_top / (1.0 + jnp.exp(l_lo - l_hi))

        def ffn(wg, wu, wd):
            a = _dot(h, wg[...].astype(BF16))
            a = a * jax.nn.sigmoid(a) * _dot(h, wu[...].astype(BF16))
            return _dot(a.astype(BF16), wd[...].astype(BF16))

        y_ref[...] = w_lo * ffn(wgl_ref, wul_ref, wdl_ref) + w_hi * ffn(wgh_ref, wuh_ref, wdh_ref)

    @pl.when(i >= nt_ref[0])
    def _():
        y_ref[...] = jnp.zeros_like(y_ref)


def _moe_sorted(tlo, thi, blk, nt, h_sorted, w_gate, w_up, w_down, w_r, b_r):
    n_rows = h_sorted.shape[0]
    rows = pl.BlockSpec((TMM, D_MODEL), lambda i, tlo, thi, blk, nt: (blk[i], 0))
    w_in_lo = pl.BlockSpec((None, D_MODEL, D_EXPERT), lambda i, tlo, thi, blk, nt: (tlo[i], 0, 0))
    w_out_lo = pl.BlockSpec((None, D_EXPERT, D_MODEL), lambda i, tlo, thi, blk, nt: (tlo[i], 0, 0))
    w_in_hi = pl.BlockSpec((None, D_MODEL, D_EXPERT), lambda i, tlo, thi, blk, nt: (thi[i], 0, 0))
    w_out_hi = pl.BlockSpec((None, D_EXPERT, D_MODEL), lambda i, tlo, thi, blk, nt: (thi[i], 0, 0))
    grid_spec = pltpu.PrefetchScalarGridSpec(
        num_scalar_prefetch=4,
        grid=(n_rows // TMM,),
        in_specs=[rows, w_in_lo, w_in_lo, w_out_lo, w_in_hi, w_in_hi, w_out_hi,
                  pl.BlockSpec((D_MODEL, ROUTER_COLS), lambda i, *_: (0, 0)),
                  pl.BlockSpec((1, ROUTER_COLS), lambda i, *_: (0, 0))],
        out_specs=pl.BlockSpec((TMM, D_MODEL), lambda i, *_: (i, 0)),
        scratch_shapes=[pltpu.VMEM((D_MODEL, ROUTER_COLS), BF16)],
    )
    return pl.pallas_call(
        _moe_kernel,
        grid_spec=grid_spec,
        out_shape=jax.ShapeDtypeStruct((n_rows, D_MODEL), F32),
        compiler_params=pltpu.CompilerParams(dimension_semantics=("arbitrary",), vmem_limit_bytes=VMEM_LIMIT),
        name="moe_sorted",
    )(tlo, thi, blk, nt, h_sorted, w_gate, w_up, w_down, w_gate, w_up, w_down, w_r, b_r)


def _final_kernel(starts_ref, code_ref, x1_ref, g2_ref, lg_ref, lb_ref, ys_ref, o_ref, ybuf, sem):
    def issue(i, c):
        for u in range(DMA_UNROLL):
            r = i * DMA_UNROLL + u
            _row_copy(ys_ref, _row_of(starts_ref, code_ref[0, r]), ybuf, r, sem).start()
        return c

    lax.fori_loop(0, TS // DMA_UNROLL, issue, 0)
    pltpu.make_async_copy(ys_ref.at[pl.ds(0, TS)], ybuf, sem).wait()
    o_ref[...] = _layer_norm(ALPHA * x1_ref[...] + g2_ref[...] * ybuf[...], lg_ref[...], lb_ref[...])


def _unsort_ln2(starts, code, x1, g2, ln_g, ln_b, y_sorted):
    bsz, seq, _ = x1.shape
    nst = seq // TS
    grid_spec = pltpu.PrefetchScalarGridSpec(
        num_scalar_prefetch=1,
        grid=(bsz, nst),
        in_specs=[pl.BlockSpec((None, 1, TS), lambda b, s, *_: (b * nst + s, 0, 0), memory_space=pltpu.SMEM),
                  pl.BlockSpec((None, TS, D_MODEL), lambda b, s, *_: (b, s, 0)),
                  pl.BlockSpec((None, None, 1, D_MODEL), lambda b, s, *_: (b, 0, 0, 0)),
                  pl.BlockSpec((1, D_MODEL), lambda b, s, *_: (0, 0)),
                  pl.BlockSpec((1, D_MODEL), lambda b, s, *_: (0, 0)),
                  pl.BlockSpec(memory_space=pl.ANY)],
        out_specs=pl.BlockSpec((None, TS, D_MODEL), lambda b, s, *_: (b, s, 0)),
        scratch_shapes=[pltpu.VMEM((TS, D_MODEL), F32), pltpu.SemaphoreType.DMA],
    )
    return pl.pallas_call(
        _final_kernel,
        grid_spec=grid_spec,
        out_shape=jax.ShapeDtypeStruct((bsz, seq, D_MODEL), F32),
        compiler_params=pltpu.CompilerParams(dimension_semantics=("arbitrary", "arbitrary")),
        name="unsort_ln2",
    )(starts, code.reshape(bsz * nst, 1, TS), x1, g2, ln_g, ln_b, y_sorted)


def _tile_tables(counts, n_tiles_max):
    n_tile_b = (counts + (TMM - 1)) // TMM
    tile_end = jnp.cumsum(n_tile_b)
    total = tile_end[-1]
    starts = (tile_end - n_tile_b) * TMM
    zrow = jnp.where(n_tile_b > 0, (tile_end - 1) * TMM, -1)
    blk = jnp.minimum(jnp.arange(n_tiles_max, dtype=jnp.int32), total - 1)
    bucket = jnp.searchsorted(tile_end, blk, side="right").astype(jnp.int32)
    pair = jnp.asarray(PAIRS, jnp.int32)[bucket % N_PAIRS]
    base = (bucket // N_PAIRS) * EXPERTS_PER_GROUP
    pad = jnp.zeros((LANES - N_BUCKETS,), jnp.int32)
    return (jnp.concatenate([starts, pad]), jnp.concatenate([zrow, pad - 1]), base + pair[:, 0], base + pair[:, 1],
            blk, total.reshape(1))


def kernel(x, c, w_ada, b_ada, w_in, b_in, conv_w, conv_b, conv_ln_g, conv_ln_b, conv_out_g, sinks,
           attn_out_g, w_out, b_out, ln1_g, ln1_b, w_router_group, b_router_group, w_router_expert,
           b_router_expert, w_gate, w_up, w_down, ln2_g, ln2_b):
    assert w_ada.shape[0] == DEPTH
    bsz, seq, _ = x.shape
    row = lambda v: v[0][None, :]

    mod = _modulation(c, w_ada[0], row(b_ada)).reshape(bsz, 6, 1, D_MODEL)
    sh1, sc1, g1, sh2, sc2, g2 = [mod[:, j:j + 1] for j in range(6)]

    y_conv, q, kv = _inproj_conv(x, sh1, sc1, w_in[0], row(b_in), conv_w[0], row(conv_b),
                                 row(conv_ln_g), row(conv_ln_b), row(conv_out_g))
    y_attn = _attention(q, kv, sinks[0], row(attn_out_g))

    pad = ROUTER_COLS - N_EXPERTS - N_GROUPS
    w_r = jnp.concatenate([w_router_expert[0], w_router_group[0], jnp.zeros((D_MODEL, pad), F32)], axis=1)
    b_r = jnp.concatenate([b_router_expert[0], b_router_group[0], jnp.zeros((pad,), F32)])[None, :]
    x1, code, counts = _outproj_route(x, y_conv, y_attn, w_out[0], row(b_out), g1, row(ln1_g), row(ln1_b),
                                      sc2, sh2, w_r, b_r)

    n_tiles_max = (bsz * seq) // TMM + N_BUCKETS
    code = code[:, :, 0]
    starts, zrow, tlo, thi, blk, nt = _tile_tables(counts[0, :N_BUCKETS].astype(jnp.int32), n_tiles_max)
    h_sorted = _scatter_rows(starts, zrow, nt, code, x1, sc2, sh2, n_tiles_max * TMM)
    y_sorted = _moe_sorted(tlo, thi, blk, nt, h_sorted, w_gate[0], w_up[0], w_down[0], w_r, b_r)
    return _unsort_ln2(starts, code, x1, g2, row(ln2_g), row(ln2_b), y_sorted)
```

```python
import jax
import jax.numpy as jnp
from jax import lax
from jax.experimental import pallas as pl
from jax.experimental.pallas import tpu as pltpu

F32 = jnp.float32
BF16 = jnp.bfloat16

D_MODEL = 1024
D_CONV = 512
CONV_WIDTH = 31
GROUP_DIM = 64
N_HEADS = 8
N_KV_HEADS = 2
HEAD_DIM = 64
D_ATTN = N_HEADS * HEAD_DIM
D_KV = N_KV_HEADS * HEAD_DIM
BLOCK = 128
D_IN = 2 * D_CONV + D_ATTN + 2 * D_KV
N_GROUPS = 4
EXPERTS_PER_GROUP = 8
N_EXPERTS = N_GROUPS * EXPERTS_PER_GROUP
D_EXPERT = D_MODEL // 4
DEPTH = 1
ALPHA = (2.0 * DEPTH) ** 0.25
EPS = 1e-5
NEG = -1e30

LANES = 128
SUBLANES = 8
TS = 512
HALO = 32
CR = 128
AB = 2
PAIR_SLOTS = [j + (N_HEADS // N_KV_HEADS) * g for j in range(N_HEADS // N_KV_HEADS) for g in range(N_KV_HEADS)]
ROUTER_COLS = LANES
VMEM_LIMIT = 56 * 1024 * 1024

PAIRS = [(lo, hi) for lo in range(EXPERTS_PER_GROUP) for hi in range(lo + 1, EXPERTS_PER_GROUP)]
N_PAIRS = len(PAIRS)
N_BUCKETS = N_GROUPS * N_PAIRS
TMM = 128
CODE_SHIFT = 16


def _dot(a, b):
    return jnp.dot(a, b, preferred_element_type=F32)


def _dot_nt(a, b):
    return lax.dot_general(a, b, (((1,), (1,)), ((), ())), preferred_element_type=F32)


def _fill_group_maps(bsel_ref, bexp_ref):
    c = bsel_ref.shape[0]
    ch = lax.broadcasted_iota(jnp.int32, (c, LANES), 0) // GROUP_DIM
    gi = lax.broadcasted_iota(jnp.int32, (c, LANES), 1)
    bsel_ref[...] = jnp.where(ch == gi, 1.0 / GROUP_DIM, 0.0).astype(BF16)
    gi2 = lax.broadcasted_iota(jnp.int32, (LANES, c), 0)
    ch2 = lax.broadcasted_iota(jnp.int32, (LANES, c), 1) // GROUP_DIM
    bexp_ref[...] = jnp.where(ch2 == gi2, 1.0, 0.0).astype(BF16)


def _group_rms(y, bsel, bexp):
    ms = _dot((y * y).astype(BF16), bsel)
    r = lax.rsqrt(ms + EPS)
    r_hi = r.astype(BF16)
    r_lo = (r - r_hi.astype(F32)).astype(BF16)
    return y * (_dot(r_hi, bexp) + _dot(r_lo, bexp))


def _layer_norm(y, g, b):
    mu = jnp.mean(y, axis=-1, keepdims=True)
    d = y - mu
    var = jnp.mean(d * d, axis=-1, keepdims=True)
    return d * lax.rsqrt(var + EPS) * g + b


def _mod_kernel(c_ref, w_ref, b_ref, o_ref):
    c = c_ref[...]
    c_act = c * jax.nn.sigmoid(c)
    o_ref[...] = jnp.dot(c_act, w_ref[...], preferred_element_type=F32,
                         precision=lax.Precision.HIGHEST) + b_ref[...]


def _modulation(c, w_ada, b_ada):
    bsz = c.shape[0]
    n = w_ada.shape[1]
    return pl.pallas_call(
        _mod_kernel,
        grid=(n // D_MODEL,),
        in_specs=[pl.BlockSpec((bsz, D_MODEL), lambda j: (0, 0)),
                  pl.BlockSpec((D_MODEL, D_MODEL), lambda j: (0, j)),
                  pl.BlockSpec((1, D_MODEL), lambda j: (0, j))],
        out_specs=pl.BlockSpec((bsz, D_MODEL), lambda j: (0, j)),
        out_shape=jax.ShapeDtypeStruct((bsz, n), F32),
        name="adaln_mod",
    )(c, w_ada, b_ada)


def _inproj_conv_kernel(x_ref, sh_ref, sc_ref, w_ref, b_ref, wq_ref, bq_ref, cw_ref, cb_ref, lg_ref, lb_ref, og_ref,
                        yc_ref, q_ref, kv_ref, wbf, gs, cacc, wb, bsel, bexp):
    first = (pl.program_id(0) == 0) & (pl.program_id(1) == 0)
    s = pl.program_id(1)

    @pl.when(first)
    def _():
        wbf[...] = w_ref[...].astype(BF16)
        wbf[:, 2 * D_CONV:2 * D_CONV + D_ATTN] = wq_ref[...].astype(BF16)
        wb[...] = jnp.broadcast_to(cw_ref[...][:, None, :], wb.shape)
        _fill_group_maps(bsel, bexp)

    @pl.when(s == 0)
    def _():
        gs[0, 0:HALO, :] = jnp.zeros((HALO, D_CONV), F32)

    @pl.when(s > 0)
    def _():
        gs[0, 0:HALO, :] = gs[0, TS:TS + HALO, :]

    h = (x_ref[...] * (1.0 + sc_ref[...]) + sh_ref[...]).astype(BF16)
    u_a = _dot(h, wbf[:, 0:D_CONV]) + b_ref[:, 0:D_CONV]
    u_b = _dot(h, wbf[:, D_CONV:2 * D_CONV]) + b_ref[:, D_CONV:2 * D_CONV]
    gs[0, HALO:HALO + TS, :] = u_a * jax.nn.sigmoid(u_b)
    q0 = 2 * D_CONV
    q = _dot(h, wbf[:, q0:q0 + D_ATTN]) + bq_ref[...]
    q_ref[...] = (q * (HEAD_DIM ** -0.5)).astype(BF16)
    k0 = q0 + D_ATTN
    kv_ref[...] = (_dot(h, wbf[:, k0:k0 + 2 * D_KV]) + b_ref[:, k0:k0 + 2 * D_KV]).astype(BF16)

    n_sh = TS + HALO - SUBLANES
    for j in range(1, SUBLANES):
        gs[j, 0:n_sh, :] = gs[0, j:j + n_sh, :]

    cb = cb_ref[...]
    lg = lg_ref[...]
    lb = lb_ref[...]
    og = og_ref[...]

    def conv_rows(i, carry):
        r0 = pl.multiple_of(i * CR, CR)
        for c in range(D_CONV // LANES):
            ls = slice(c * LANES, (c + 1) * LANES)
            acc = jnp.broadcast_to(cb[:, ls], (CR, LANES)).reshape(CR // SUBLANES, SUBLANES, LANES)
            for k in range(CONV_WIDTH):
                off = HALO - (CONV_WIDTH - 1) + k
                seg = gs[off % SUBLANES, pl.ds(r0 + (off // SUBLANES) * SUBLANES, CR), ls]
                acc = acc + wb[k, :, ls][None] * seg.reshape(CR // SUBLANES, SUBLANES, LANES)
            cacc[pl.ds(r0, CR), ls] = acc.reshape(CR, LANES)
        return carry

    lax.fori_loop(0, TS // CR, conv_rows, 0)
    y = _layer_norm(cacc[...], lg, lb)
    y = y * jax.nn.sigmoid(y)
    yc_ref[...] = (_group_rms(y, bsel[...], bexp[...]) * og).astype(BF16)


def _inproj_conv(x, sh1, sc1, w_in, b_in, w_q, b_q, conv_w, conv_b, ln_g, ln_b, out_g):
    bsz, seq, _ = x.shape
    vec = lambda n: pl.BlockSpec((1, n), lambda b, s: (0, 0))
    mod = pl.BlockSpec((None, None, 1, D_MODEL), lambda b, s: (b, 0, 0, 0))
    tile = lambda n: pl.BlockSpec((None, TS, n), lambda b, s: (b, s, 0))
    return pl.pallas_call(
        _inproj_conv_kernel,
        grid=(bsz, seq // TS),
        in_specs=[tile(D_MODEL), mod, mod,
                  pl.BlockSpec((D_MODEL, D_IN), lambda b, s: (0, 0)), vec(D_IN),
                  pl.BlockSpec((D_MODEL, D_ATTN), lambda b, s: (0, 0)), vec(D_ATTN),
                  pl.BlockSpec((CONV_WIDTH, D_CONV), lambda b, s: (0, 0)),
                  vec(D_CONV), vec(D_CONV), vec(D_CONV), vec(D_CONV)],
        out_specs=[tile(D_CONV), tile(D_ATTN), tile(2 * D_KV)],
        out_shape=[jax.ShapeDtypeStruct((bsz, seq, D_CONV), BF16),
                   jax.ShapeDtypeStruct((bsz, seq, D_ATTN), BF16),
                   jax.ShapeDtypeStruct((bsz, seq, 2 * D_KV), BF16)],
        scratch_shapes=[pltpu.VMEM((D_MODEL, D_IN), BF16),
                        pltpu.VMEM((SUBLANES, TS + HALO, D_CONV), F32),
                        pltpu.VMEM((TS, D_CONV), F32),
                        pltpu.VMEM((CONV_WIDTH, SUBLANES, D_CONV), F32),
                        pltpu.VMEM((D_CONV, LANES), BF16),
                        pltpu.VMEM((LANES, D_CONV), BF16)],
        compiler_params=pltpu.CompilerParams(
            dimension_semantics=("arbitrary", "arbitrary"), vmem_limit_bytes=VMEM_LIMIT),
        name="inproj_conv",
    )(x, sh1, sc1, w_in, b_in, w_q, b_q, conv_w, conv_b, ln_g, ln_b, out_g)


def _attn_kernel(sinks_ref, q_ref, kvc_ref, kvp_ref, og_ref, o_ref, bsel, bexp):
    n = pl.program_id(1)

    @pl.when((pl.program_id(0) == 0) & (n == 0))
    def _():
        _fill_group_maps(bsel, bexp)

    qr = lax.broadcasted_iota(jnp.int32, (BLOCK, 2 * BLOCK), 0)
    kc = lax.broadcasted_iota(jnp.int32, (BLOCK, 2 * BLOCK), 1)
    band = (kc > qr) & (kc <= qr + BLOCK)
    lane = lax.broadcasted_iota(jnp.int32, (BLOCK, LANES), 1)
    lower = lane < HEAD_DIM
    lower_bf = jnp.where(lower, 1.0, 0.0).astype(BF16)
    upper_bf = jnp.where(lower, 0.0, 1.0).astype(BF16)
    og = og_ref[...]
    for blk in range(AB):
        r0 = blk * BLOCK
        if blk == 0:
            kv = jnp.concatenate([kvp_ref[...], kvc_ref[0:BLOCK, :]], axis=0)
            mask = band & ((n > 0) | (kc >= BLOCK))
        else:
            kv = kvc_ref[r0 - BLOCK:r0 + BLOCK, :]
            mask = band
        keys = kv[:, 0:D_KV]
        vals = kv[:, D_KV:2 * D_KV]
        parts = []
        for j in range(D_ATTN // LANES):
            col = q_ref[r0:r0 + BLOCK, j * LANES:(j + 1) * LANES]
            parts += [col * lower_bf, col * upper_bf]
        s_all = _dot_nt(jnp.concatenate(parts, axis=0), keys)
        probs = []
        den = jnp.ones((BLOCK, LANES), F32)
        for slot in range(N_HEADS):
            sc = jnp.where(mask, s_all[slot * BLOCK:(slot + 1) * BLOCK], NEG)
            sink = sinks_ref[PAIR_SLOTS[slot]]
            m = jnp.maximum(jnp.max(sc, axis=-1, keepdims=True), sink)
            p = jnp.exp(sc - m)
            den = jnp.where(lane == slot, jnp.sum(p, axis=-1, keepdims=True) + jnp.exp(sink - m), den)
            probs.append(p.astype(BF16))
        pv = _dot(jnp.concatenate(probs, axis=0), vals)
        o = jnp.concatenate(
            [jnp.where(lower, pv[(2 * j) * BLOCK:(2 * j + 1) * BLOCK], pv[(2 * j + 1) * BLOCK:(2 * j + 2) * BLOCK])
             for j in range(D_ATTN // LANES)], axis=1)
        ms = _dot((o * o).astype(BF16), bsel[...])
        r = lax.rsqrt(ms + EPS * den * den)
        r_hi = r.astype(BF16)
        r_lo = (r - r_hi.astype(F32)).astype(BF16)
        scale = _dot(r_hi, bexp[...]) + _dot(r_lo, bexp[...])
        o_ref[r0:r0 + BLOCK, :] = (o * scale * og).astype(BF16)


def _attention(q, kv, sinks, out_g):
    bsz, seq, _ = q.shape
    rows = AB * BLOCK
    grid_spec = pltpu.PrefetchScalarGridSpec(
        num_scalar_prefetch=1,
        grid=(bsz, seq // rows),
        in_specs=[pl.BlockSpec((None, rows, D_ATTN), lambda b, n, sk: (b, n, 0)),
                  pl.BlockSpec((None, rows, 2 * D_KV), lambda b, n, sk: (b, n, 0)),
                  pl.BlockSpec((None, BLOCK, 2 * D_KV), lambda b, n, sk: (b, jnp.maximum(AB * n - 1, 0), 0)),
                  pl.BlockSpec((1, D_ATTN), lambda b, n, sk: (0, 0))],
        out_specs=pl.BlockSpec((None, rows, D_ATTN), lambda b, n, sk: (b, n, 0)),
        scratch_shapes=[pltpu.VMEM((D_ATTN, LANES), BF16), pltpu.VMEM((LANES, D_ATTN), BF16)],
    )
    return pl.pallas_call(
        _attn_kernel,
        grid_spec=grid_spec,
        out_shape=jax.ShapeDtypeStruct((bsz, seq, D_ATTN), BF16),
        compiler_params=pltpu.CompilerParams(dimension_semantics=("arbitrary", "arbitrary")),
        name="swa_attention",
    )(sinks, q, kv, kv, out_g)


def _group_softmax_top(logits, lane):
    is_g = (lane >= N_EXPERTS) & (lane < N_EXPERTS + N_GROUPS)
    gl = jnp.where(is_g, logits, NEG)
    gmax = jnp.max(gl, axis=-1, keepdims=True)
    p_top = 1.0 / jnp.sum(jnp.where(is_g, jnp.exp(gl - gmax), 0.0), axis=-1, keepdims=True)
    return is_g, gl, gmax, p_top


def _route_bucket(logits):
    rows = logits.shape[0]
    lane = lax.broadcasted_iota(jnp.int32, (rows, ROUTER_COLS), 1)
    lanef = lane.astype(F32)
    is_g, gl, gmax, _ = _group_softmax_top(logits, lane)
    gidx = jnp.min(jnp.where(is_g & (gl == gmax), lanef - N_EXPERTS, 99.0), axis=-1, keepdims=True)
    in_grp = (lane < N_EXPERTS) & ((lane // EXPERTS_PER_GROUP) == gidx.astype(jnp.int32))
    el = jnp.where(in_grp, logits, NEG)
    m1 = jnp.max(el, axis=-1, keepdims=True)
    i1 = jnp.min(jnp.where(in_grp & (el == m1), lanef, 999.0), axis=-1, keepdims=True)
    rest = in_grp & (lanef != i1)
    el2 = jnp.where(rest, logits, NEG)
    m2 = jnp.max(el2, axis=-1, keepdims=True)
    i2 = jnp.min(jnp.where(rest & (el2 == m2), lanef, 999.0), axis=-1, keepdims=True)
    lo = jnp.minimum(i1, i2) - EXPERTS_PER_GROUP * gidx
    hi = jnp.maximum(i1, i2) - EXPERTS_PER_GROUP * gidx
    pair = lo * (EXPERTS_PER_GROUP - 1) - lo * (lo - 1.0) * 0.5 + (hi - lo - 1.0)
    return gidx * N_PAIRS + pair, lanef


def _outproj_kernel(x_ref, yc_ref, ya_ref, w_ref, b_ref, g1_ref, lg_ref, lb_ref, sc2_ref, sh2_ref,
                    wr_ref, br_ref, x1_ref, code_ref, counts_ref, wbf, wrbf, ltri, running):
    @pl.when((pl.program_id(0) == 0) & (pl.program_id(1) == 0))
    def _():
        wbf[...] = w_ref[...].astype(BF16)
        wrbf[...] = wr_ref[...].astype(BF16)
        r = lax.broadcasted_iota(jnp.int32, (TS, TS), 0)
        c = lax.broadcasted_iota(jnp.int32, (TS, TS), 1)
        ltri[...] = jnp.where(c < r, 1.0, 0.0).astype(BF16)
        running[...] = jnp.zeros_like(running)

    mix = _dot(yc_ref[...], wbf[0:D_CONV, :]) + _dot(ya_ref[...], wbf[D_CONV:, :]) + b_ref[...]
    x1 = _layer_norm(ALPHA * x_ref[...] + g1_ref[...] * mix, lg_ref[...], lb_ref[...])
    x1_ref[...] = x1
    h2 = (x1 * (1.0 + sc2_ref[...]) + sh2_ref[...]).astype(BF16)
    bucket, lanef = _route_bucket(_dot(h2, wrbf[...]) + br_ref[...])
    mine = lanef == bucket
    onehot = jnp.where(mine, 1.0, 0.0)
    earlier = _dot(ltri[...], onehot.astype(BF16)) + running[...]
    rank = jnp.sum(jnp.where(mine, earlier, 0.0), axis=-1, keepdims=True)
    running[...] += jnp.sum(onehot, axis=0, keepdims=True)
    code = bucket.astype(jnp.int32) * (1 << CODE_SHIFT) + rank.astype(jnp.int32)
    code_ref[...] = jnp.broadcast_to(code, code_ref.shape)
    counts_ref[...] = running[...]


def _outproj_route(x, y_conv, y_attn, w_out, b_out, g1, ln_g, ln_b, sc2, sh2, w_r, b_r):
    bsz, seq, _ = x.shape
    vec = lambda n: pl.BlockSpec((1, n), lambda b, s: (0, 0))
    mod = pl.BlockSpec((None, None, 1, D_MODEL), lambda b, s: (b, 0, 0, 0))
    tile = lambda n: pl.BlockSpec((None, TS, n), lambda b, s: (b, s, 0))
    return pl.pallas_call(
        _outproj_kernel,
        grid=(bsz, seq // TS),
        in_specs=[tile(D_MODEL), tile(D_CONV), tile(D_ATTN),
                  pl.BlockSpec((D_MODEL, D_MODEL), lambda b, s: (0, 0)), vec(D_MODEL),
                  mod, vec(D_MODEL), vec(D_MODEL), mod, mod,
                  pl.BlockSpec((D_MODEL, ROUTER_COLS), lambda b, s: (0, 0)), vec(ROUTER_COLS)],
        out_specs=[tile(D_MODEL), tile(LANES), vec(LANES)],
        out_shape=[jax.ShapeDtypeStruct((bsz, seq, D_MODEL), F32),
                   jax.ShapeDtypeStruct((bsz, seq, LANES), jnp.int32),
                   jax.ShapeDtypeStruct((1, LANES), F32)],
        scratch_shapes=[pltpu.VMEM((D_MODEL, D_MODEL), BF16), pltpu.VMEM((D_MODEL, ROUTER_COLS), BF16),
                        pltpu.VMEM((TS, TS), BF16), pltpu.VMEM((1, LANES), F32)],
        compiler_params=pltpu.CompilerParams(
            dimension_semantics=("arbitrary", "arbitrary"), vmem_limit_bytes=VMEM_LIMIT),
        name="outproj_route",
    )(x, y_conv, y_attn, w_out, b_out, g1, ln_g, ln_b, sc2, sh2, w_r, b_r)


def _row_of(starts_ref, code):
    return (starts_ref[lax.shift_right_logical(code, jnp.int32(CODE_SHIFT))]
            + (code & ((1 << CODE_SHIFT) - 1)))


def _wait_rows(hbm_ref, n, sem):
    pltpu.make_async_copy(hbm_ref.at[pl.ds(0, n)], hbm_ref.at[pl.ds(0, n)], sem).wait()


def _scatter_kernel(starts_ref, zrow_ref, nt_ref, code_ref, x1_ref, sc2_ref, sh2_ref, hs_ref, pos_ref,
                    hbuf, zeros, sems, zsem):
    nsteps = pl.num_programs(0) * pl.num_programs(1)
    step = pl.program_id(0) * pl.num_programs(1) + pl.program_id(1)
    slot = step % 2

    def zero_copy(b):
        return pltpu.make_async_copy(zeros, hs_ref.at[pl.ds(pl.multiple_of(zrow_ref[b], TMM), TMM)], zsem)

    def tail_copy(t):
        return pltpu.make_async_copy(zeros, hs_ref.at[pl.ds(pl.multiple_of(t * TMM, TMM), TMM)], zsem)

    @pl.when(step == 0)
    def _():
        zeros[...] = jnp.zeros_like(zeros)
        n_tiles = hs_ref.shape[0] // TMM

        def start(b, c):
            @pl.when(zrow_ref[b] >= 0)
            def _():
                zero_copy(b).start()
            return c

        def wait(b, c):
            @pl.when(zrow_ref[b] >= 0)
            def _():
                zero_copy(b).wait()
            return c

        def start_tail(t, c):
            tail_copy(t).start()
            return c

        def wait_tail(t, c):
            tail_copy(t).wait()
            return c

        lax.fori_loop(0, N_BUCKETS, start, 0)
        lax.fori_loop(nt_ref[0], n_tiles, start_tail, 0)
        lax.fori_loop(0, N_BUCKETS, wait, 0)
        lax.fori_loop(nt_ref[0], n_tiles, wait_tail, 0)

    @pl.when(step >= 2)
    def _():
        _wait_rows(hs_ref, TS, sems.at[slot])

    h2 = x1_ref[...] * (1.0 + sc2_ref[...]) + sh2_ref[...]
    hbuf[slot] = h2.reshape(TS // SUBLANES, SUBLANES, D_MODEL)

    def issue(i, c):
        for u in range(SUBLANES):
            r = i * SUBLANES + u
            pos = _row_of(starts_ref, code_ref[0, r])
            pos_ref[0, r] = pos
            pltpu.make_async_copy(hbuf.at[slot, i, pl.ds(u, 1)], hs_ref.at[pl.ds(pos, 1)],
                                  sems.at[slot]).start(priority=u % 2)
        return c

    lax.fori_loop(0, TS // SUBLANES, issue, 0)

    @pl.when(step == nsteps - 1)
    def _():
        _wait_rows(hs_ref, TS, sems.at[slot])
        _wait_rows(hs_ref, TS, sems.at[1 - slot])


def _scatter_rows(starts, zrow, nt, code, x1, sc2, sh2, n_rows):
    bsz, seq, _ = x1.shape
    nst = seq // TS
    grid_spec = pltpu.PrefetchScalarGridSpec(
        num_scalar_prefetch=3,
        grid=(bsz, nst),
        in_specs=[pl.BlockSpec((None, 1, TS), lambda b, s, *_: (b * nst + s, 0, 0), memory_space=pltpu.SMEM),
                  pl.BlockSpec((None, TS, D_MODEL), lambda b, s, *_: (b, s, 0)),
                  pl.BlockSpec((None, None, 1, D_MODEL), lambda b, s, *_: (b, 0, 0, 0)),
                  pl.BlockSpec((None, None, 1, D_MODEL), lambda b, s, *_: (b, 0, 0, 0))],
        out_specs=[pl.BlockSpec(memory_space=pl.ANY),
                   pl.BlockSpec((None, 1, TS), lambda b, s, *_: (b * nst + s, 0, 0), memory_space=pltpu.SMEM)],
        scratch_shapes=[pltpu.VMEM((2, TS // SUBLANES, SUBLANES, D_MODEL), F32), pltpu.VMEM((TMM, D_MODEL), F32),
                        pltpu.SemaphoreType.DMA((2,)), pltpu.SemaphoreType.DMA],
    )
    return pl.pallas_call(
        _scatter_kernel,
        grid_spec=grid_spec,
        out_shape=[jax.ShapeDtypeStruct((n_rows, D_MODEL), F32),
                   jax.ShapeDtypeStruct((bsz * nst, 1, TS), jnp.int32)],
        compiler_params=pltpu.CompilerParams(dimension_semantics=("arbitrary", "arbitrary")),
        name="moe_scatter",
    )(starts, zrow, nt, code.reshape(bsz * nst, 1, TS), x1, sc2, sh2)


def _moe_kernel(tlo_ref, thi_ref, blk_ref, nt_ref, h_ref, wgl_ref, wul_ref, wdl_ref, wgh_ref, wuh_ref, wdh_ref,
                wr_ref, br_ref, y_ref, wrbf, wgl, wul, wdl, wgh, wuh, wdh):
    i = pl.program_id(0)
    prev = jnp.maximum(i - 1, 0)

    @pl.when(i == 0)
    def _():
        wrbf[...] = wr_ref[...].astype(BF16)

    @pl.when((i == 0) | (tlo_ref[i] != tlo_ref[prev]))
    def _():
        wgl[...] = wgl_ref[...].astype(BF16)
        wul[...] = wul_ref[...].astype(BF16)
        wdl[...] = wdl_ref[...].astype(BF16)

    @pl.when((i == 0) | (thi_ref[i] != thi_ref[prev]))
    def _():
        wgh[...] = wgh_ref[...].astype(BF16)
        wuh[...] = wuh_ref[...].astype(BF16)
        wdh[...] = wdh_ref[...].astype(BF16)

    @pl.when(i < nt_ref[0])
    def _():
        h = h_ref[...].astype(BF16)
        logits = _dot(h, wrbf[...]) + br_ref[...]
        lane = lax.broadcasted_iota(jnp.int32, logits.shape, 1)
        p_top = _group_softmax_top(logits, lane)[3]
        l_lo = jnp.sum(jnp.where(lane == tlo_ref[i], logits, 0.0), axis=-1, keepdims=True)
        l_hi = jnp.sum(jnp.where(lane == thi_ref[i], logits, 0.0), axis=-1, keepdims=True)
        w_lo = p_top / (1.0 + jnp.exp(l_hi - l_lo))
        w_hi = p_top / (1.0 + jnp.exp(l_lo - l_hi))

        def ffn(wg, wu, wd):
            a = _dot(h, wg[...])
            a = a * jax.nn.sigmoid(a) * _dot(h, wu[...])
            return _dot(a.astype(BF16), wd[...])

        y_ref[...] = w_lo * ffn(wgl, wul, wdl) + w_hi * ffn(wgh, wuh, wdh)

    @pl.when(i >= nt_ref[0])
    def _():
        y_ref[...] = jnp.zeros_like(y_ref)


def _moe_sorted(tlo, thi, blk, nt, h_sorted, w_gate, w_up, w_down, w_r, b_r):
    n_rows = h_sorted.shape[0]
    rows = pl.BlockSpec((TMM, D_MODEL), lambda i, tlo, thi, blk, nt: (blk[i], 0))
    w_in_lo = pl.BlockSpec((None, D_MODEL, D_EXPERT), lambda i, tlo, thi, blk, nt: (tlo[i], 0, 0))
    w_out_lo = pl.BlockSpec((None, D_EXPERT, D_MODEL), lambda i, tlo, thi, blk, nt: (tlo[i], 0, 0))
    w_in_hi = pl.BlockSpec((None, D_MODEL, D_EXPERT), lambda i, tlo, thi, blk, nt: (thi[i], 0, 0))
    w_out_hi = pl.BlockSpec((None, D_EXPERT, D_MODEL), lambda i, tlo, thi, blk, nt: (thi[i], 0, 0))
    grid_spec = pltpu.PrefetchScalarGridSpec(
        num_scalar_prefetch=4,
        grid=(n_rows // TMM,),
        in_specs=[rows, w_in_lo, w_in_lo, w_out_lo, w_in_hi, w_in_hi, w_out_hi,
                  pl.BlockSpec((D_MODEL, ROUTER_COLS), lambda i, *_: (0, 0)),
                  pl.BlockSpec((1, ROUTER_COLS), lambda i, *_: (0, 0))],
        out_specs=pl.BlockSpec((TMM, D_MODEL), lambda i, *_: (i, 0)),
        scratch_shapes=[pltpu.VMEM((D_MODEL, ROUTER_COLS), BF16)]
        + 2 * [pltpu.VMEM((D_MODEL, D_EXPERT), BF16), pltpu.VMEM((D_MODEL, D_EXPERT), BF16),
               pltpu.VMEM((D_EXPERT, D_MODEL), BF16)],
    )
    return pl.pallas_call(
        _moe_kernel,
        grid_spec=grid_spec,
        out_shape=jax.ShapeDtypeStruct((n_rows, D_MODEL), F32),
        compiler_params=pltpu.CompilerParams(dimension_semantics=("arbitrary",), vmem_limit_bytes=VMEM_LIMIT),
        name="moe_sorted",
    )(tlo, thi, blk, nt, h_sorted, w_gate, w_up, w_down, w_gate, w_up, w_down, w_r, b_r)


def _final_kernel(pos_ref, posn_ref, x1_ref, g2_ref, lg_ref, lb_ref, ys_ref, o_ref, ybuf, sems):
    nsteps = pl.num_programs(0) * pl.num_programs(1)
    step = pl.program_id(0) * pl.num_programs(1) + pl.program_id(1)
    slot = step % 2

    def gather(rows_ref, sl):
        def issue(i, c):
            for u in range(SUBLANES):
                pltpu.make_async_copy(ys_ref.at[pl.ds(rows_ref[0, i * SUBLANES + u], 1)],
                                      ybuf.at[sl, i, pl.ds(u, 1)], sems.at[sl]).start(priority=u % 2)
            return c

        lax.fori_loop(0, TS // SUBLANES, issue, 0)

    @pl.when(step == 0)
    def _():
        gather(pos_ref, slot)

    @pl.when(step + 1 < nsteps)
    def _():
        gather(posn_ref, 1 - slot)

    _wait_rows(ys_ref, TS, sems.at[slot])
    y = ybuf[slot].reshape(TS, D_MODEL)
    o_ref[...] = _layer_norm(ALPHA * x1_ref[...] + g2_ref[...] * y, lg_ref[...], lb_ref[...])


def _unsort_ln2(pos, x1, g2, ln_g, ln_b, y_sorted):
    bsz, seq, _ = x1.shape
    nst = seq // TS
    last = bsz * nst - 1
    return pl.pallas_call(
        _final_kernel,
        grid=(bsz, nst),
        in_specs=[pl.BlockSpec((None, 1, TS), lambda b, s: (b * nst + s, 0, 0), memory_space=pltpu.SMEM),
                  pl.BlockSpec((None, 1, TS), lambda b, s: (jnp.minimum(b * nst + s + 1, last), 0, 0),
                               memory_space=pltpu.SMEM),
                  pl.BlockSpec((None, TS, D_MODEL), lambda b, s: (b, s, 0)),
                  pl.BlockSpec((None, None, 1, D_MODEL), lambda b, s: (b, 0, 0, 0)),
                  pl.BlockSpec((1, D_MODEL), lambda b, s: (0, 0)),
                  pl.BlockSpec((1, D_MODEL), lambda b, s: (0, 0)),
                  pl.BlockSpec(memory_space=pl.ANY)],
        out_specs=pl.BlockSpec((None, TS, D_MODEL), lambda b, s: (b, s, 0)),
        out_shape=jax.ShapeDtypeStruct((bsz, seq, D_MODEL), F32),
        scratch_shapes=[pltpu.VMEM((2, TS // SUBLANES, SUBLANES, D_MODEL), F32), pltpu.SemaphoreType.DMA((2,))],
        compiler_params=pltpu.CompilerParams(dimension_semantics=("arbitrary", "arbitrary")),
        name="unsort_ln2",
    )(pos, pos, x1, g2, ln_g, ln_b, y_sorted)


def _pair_layout(a, axis):
    shp = a.shape
    a = a.reshape(shp[:axis] + (N_KV_HEADS, N_HEADS // N_KV_HEADS, HEAD_DIM) + shp[axis + 1:])
    return jnp.swapaxes(a, axis, axis + 1).reshape(shp)


def _tile_tables(counts, n_tiles_max):
    n_tile_b = jnp.floor((counts + (TMM - 1.0)) * (1.0 / TMM))
    b_idx = jnp.arange(N_BUCKETS, dtype=jnp.int32)
    tile_end = jnp.sum(jnp.where(b_idx[None, :] <= b_idx[:, None], n_tile_b[None, :], 0.0), axis=1)
    total = jnp.sum(n_tile_b)
    starts = ((tile_end - n_tile_b) * TMM).astype(jnp.int32)
    zrow = jnp.where(n_tile_b > 0, (tile_end - 1.0) * TMM, -1.0).astype(jnp.int32)
    blk = jnp.minimum(jnp.arange(n_tiles_max, dtype=jnp.int32), total.astype(jnp.int32) - 1)
    bucket = jnp.sum(jnp.where(tile_end[None, :] <= blk[:, None].astype(F32), 1, 0), axis=1)
    onehot = bucket[:, None] == b_idx[None, :]
    e_lo = jnp.asarray([(b // N_PAIRS) * EXPERTS_PER_GROUP + PAIRS[b % N_PAIRS][0] for b in range(N_BUCKETS)], jnp.int32)
    e_hi = jnp.asarray([(b // N_PAIRS) * EXPERTS_PER_GROUP + PAIRS[b % N_PAIRS][1] for b in range(N_BUCKETS)], jnp.int32)
    tlo = jnp.sum(jnp.where(onehot, e_lo[None, :], 0), axis=1)
    thi = jnp.sum(jnp.where(onehot, e_hi[None, :], 0), axis=1)
    pad = jnp.zeros((LANES - N_BUCKETS,), jnp.int32)
    return (jnp.concatenate([starts, pad]), jnp.concatenate([zrow, pad - 1]), tlo, thi, blk,
            total.astype(jnp.int32).reshape(1))


def kernel(x, c, w_ada, b_ada, w_in, b_in, conv_w, conv_b, conv_ln_g, conv_ln_b, conv_out_g, sinks,
           attn_out_g, w_out, b_out, ln1_g, ln1_b, w_router_group, b_router_group, w_router_expert,
           b_router_expert, w_gate, w_up, w_down, ln2_g, ln2_b):
    assert w_ada.shape[0] == DEPTH
    bsz, seq, _ = x.shape
    row = lambda v: v[0][None, :]

    mod = _modulation(c, w_ada[0], row(b_ada)).reshape(bsz, 6, 1, D_MODEL)
    sh1, sc1, g1, sh2, sc2, g2 = [mod[:, j:j + 1] for j in range(6)]

    q0 = 2 * D_CONV
    w_q = _pair_layout(w_in[0][:, q0:q0 + D_ATTN], 1)
    b_q = _pair_layout(b_in[0][q0:q0 + D_ATTN], 0)[None, :]
    og_attn = _pair_layout(attn_out_g[0], 0)[None, :]
    w_o = jnp.concatenate([w_out[0][:D_CONV], _pair_layout(w_out[0][D_CONV:], 0)], axis=0)

    y_conv, q, kv = _inproj_conv(x, sh1, sc1, w_in[0], row(b_in), w_q, b_q, conv_w[0], row(conv_b),
                                 row(conv_ln_g), row(conv_ln_b), row(conv_out_g))
    y_attn = _attention(q, kv, sinks[0], og_attn)

    pad = ROUTER_COLS - N_EXPERTS - N_GROUPS
    w_r = jnp.concatenate([w_router_expert[0], w_router_group[0], jnp.zeros((D_MODEL, pad), F32)], axis=1)
    b_r = jnp.concatenate([b_router_expert[0], b_router_group[0], jnp.zeros((pad,), F32)])[None, :]
    x1, code, counts = _outproj_route(x, y_conv, y_attn, w_o, row(b_out), g1, row(ln1_g), row(ln1_b),
                                      sc2, sh2, w_r, b_r)

    n_tiles_max = (bsz * seq) // TMM + N_BUCKETS
    starts, zrow, tlo, thi, blk, nt = _tile_tables(counts[0, :N_BUCKETS], n_tiles_max)
    h_sorted, pos = _scatter_rows(starts, zrow, nt, code[:, :, 0], x1, sc2, sh2, n_tiles_max * TMM)
    y_sorted = _moe_sorted(tlo, thi, blk, nt, h_sorted, w_gate[0], w_up[0], w_down[0], w_r, b_r)
    return _unsort_ln2(pos, x1, g2, row(ln2_g), row(ln2_b), y_sorted)
```

```python
import jax
import jax.numpy as jnp
from jax import lax
from jax.experimental import pallas as pl
from jax.experimental.pallas import tpu as pltpu

F32 = jnp.float32
BF16 = jnp.bfloat16

D_MODEL = 1024
D_CONV = 512
CONV_WIDTH = 31
GROUP_DIM = 64
N_HEADS = 8
N_KV_HEADS = 2
HEAD_DIM = 64
D_ATTN = N_HEADS * HEAD_DIM
D_KV = N_KV_HEADS * HEAD_DIM
BLOCK = 128
D_IN = 2 * D_CONV + D_ATTN + 2 * D_KV
N_GROUPS = 4
EXPERTS_PER_GROUP = 8
N_EXPERTS = N_GROUPS * EXPERTS_PER_GROUP
D_EXPERT = D_MODEL // 4
DEPTH = 1
ALPHA = (2.0 * DEPTH) ** 0.25
EPS = 1e-5
NEG = -1e30

LANES = 128
SUBLANES = 8
TS = 512
HALO = 32
CR = 128
AB = 2
PAIR_SLOTS = [j + (N_HEADS // N_KV_HEADS) * g for j in range(N_HEADS // N_KV_HEADS) for g in range(N_KV_HEADS)]
ROUTER_COLS = LANES
VMEM_LIMIT = 56 * 1024 * 1024

PAIRS = [(lo, hi) for lo in range(EXPERTS_PER_GROUP) for hi in range(lo + 1, EXPERTS_PER_GROUP)]
N_PAIRS = len(PAIRS)
N_BUCKETS = N_GROUPS * N_PAIRS
TMM = 128
CODE_SHIFT = 16


def _dot(a, b):
    return jnp.dot(a, b, preferred_element_type=F32)


def _dot_nt(a, b):
    return lax.dot_general(a, b, (((1,), (1,)), ((), ())), preferred_element_type=F32)


def _fill_group_maps(bsel_ref, bexp_ref):
    c = bsel_ref.shape[0]
    ch = lax.broadcasted_iota(jnp.int32, (c, LANES), 0) // GROUP_DIM
    gi = lax.broadcasted_iota(jnp.int32, (c, LANES), 1)
    bsel_ref[...] = jnp.where(ch == gi, 1.0 / GROUP_DIM, 0.0).astype(BF16)
    gi2 = lax.broadcasted_iota(jnp.int32, (LANES, c), 0)
    ch2 = lax.broadcasted_iota(jnp.int32, (LANES, c), 1) // GROUP_DIM
    bexp_ref[...] = jnp.where(ch2 == gi2, 1.0, 0.0).astype(BF16)


def _group_rms(y, bsel, bexp):
    ms = _dot((y * y).astype(BF16), bsel)
    r = lax.rsqrt(ms + EPS)
    r_hi = r.astype(BF16)
    r_lo = (r - r_hi.astype(F32)).astype(BF16)
    return y * (_dot(r_hi, bexp) + _dot(r_lo, bexp))


def _layer_norm(y, g, b):
    mu = jnp.mean(y, axis=-1, keepdims=True)
    d = y - mu
    var = jnp.mean(d * d, axis=-1, keepdims=True)
    return d * lax.rsqrt(var + EPS) * g + b


def _mod_kernel(c_ref, w_ref, b_ref, o_ref):
    c = c_ref[...]
    c_act = c * jax.nn.sigmoid(c)
    o_ref[...] = jnp.dot(c_act, w_ref[...], preferred_element_type=F32,
                         precision=lax.Precision.HIGHEST) + b_ref[...]


def _modulation(c, w_ada, b_ada):
    bsz = c.shape[0]
    n = w_ada.shape[1]
    return pl.pallas_call(
        _mod_kernel,
        grid=(n // D_MODEL,),
        in_specs=[pl.BlockSpec((bsz, D_MODEL), lambda j: (0, 0)),
                  pl.BlockSpec((D_MODEL, D_MODEL), lambda j: (0, j)),
                  pl.BlockSpec((1, D_MODEL), lambda j: (0, j))],
        out_specs=pl.BlockSpec((bsz, D_MODEL), lambda j: (0, j)),
        out_shape=jax.ShapeDtypeStruct((bsz, n), F32),
        name="adaln_mod",
    )(c, w_ada, b_ada)


def _inproj_conv_kernel(x_ref, sh_ref, sc_ref, w_ref, b_ref, wq_ref, bq_ref, cw_ref, cb_ref, lg_ref, lb_ref, og_ref,
                        yc_ref, q_ref, kv_ref, wbf, gs, cacc, wb, bsel, bexp):
    first = (pl.program_id(0) == 0) & (pl.program_id(1) == 0)
    s = pl.program_id(1)

    @pl.when(first)
    def _():
        wbf[...] = w_ref[...].astype(BF16)
        wbf[:, 2 * D_CONV:2 * D_CONV + D_ATTN] = wq_ref[...].astype(BF16)
        wb[...] = jnp.broadcast_to(cw_ref[...][:, None, :], wb.shape)
        _fill_group_maps(bsel, bexp)

    @pl.when(s == 0)
    def _():
        gs[0, 0:HALO, :] = jnp.zeros((HALO, D_CONV), F32)

    @pl.when(s > 0)
    def _():
        gs[0, 0:HALO, :] = gs[0, TS:TS + HALO, :]

    h = (x_ref[...] * (1.0 + sc_ref[...]) + sh_ref[...]).astype(BF16)
    u_a = _dot(h, wbf[:, 0:D_CONV]) + b_ref[:, 0:D_CONV]
    u_b = _dot(h, wbf[:, D_CONV:2 * D_CONV]) + b_ref[:, D_CONV:2 * D_CONV]
    gs[0, HALO:HALO + TS, :] = u_a * jax.nn.sigmoid(u_b)
    q0 = 2 * D_CONV
    q = _dot(h, wbf[:, q0:q0 + D_ATTN]) + bq_ref[...]
    q_ref[...] = (q * (HEAD_DIM ** -0.5)).astype(BF16)
    k0 = q0 + D_ATTN
    kv_ref[...] = (_dot(h, wbf[:, k0:k0 + 2 * D_KV]) + b_ref[:, k0:k0 + 2 * D_KV]).astype(BF16)

    n_sh = TS + HALO - SUBLANES
    for j in range(1, SUBLANES):
        gs[j, 0:n_sh, :] = gs[0, j:j + n_sh, :]

    cb = cb_ref[...]
    lg = lg_ref[...]
    lb = lb_ref[...]
    og = og_ref[...]

    for c in range(D_CONV // LANES):
        ls = slice(c * LANES, (c + 1) * LANES)

        def conv_rows(i, carry, ls=ls):
            r0 = pl.multiple_of(i * CR, CR)
            acc = jnp.broadcast_to(cb[:, ls], (CR, LANES)).reshape(CR // SUBLANES, SUBLANES, LANES)
            for k in range(CONV_WIDTH):
                off = HALO - (CONV_WIDTH - 1) + k
                seg = gs[off % SUBLANES, pl.ds(r0 + (off // SUBLANES) * SUBLANES, CR), ls]
                acc = acc + wb[k, :, ls][None] * seg.reshape(CR // SUBLANES, SUBLANES, LANES)
            cacc[pl.ds(r0, CR), ls] = acc.reshape(CR, LANES)
            return carry

        lax.fori_loop(0, TS // CR, conv_rows, 0)
    y = _layer_norm(cacc[...], lg, lb)
    y = y * jax.nn.sigmoid(y)
    yc_ref[...] = (_group_rms(y, bsel[...], bexp[...]) * og).astype(BF16)


def _inproj_conv(x, sh1, sc1, w_in, b_in, w_q, b_q, conv_w, conv_b, ln_g, ln_b, out_g):
    bsz, seq, _ = x.shape
    vec = lambda n: pl.BlockSpec((1, n), lambda b, s: (0, 0))
    mod = pl.BlockSpec((None, None, 1, D_MODEL), lambda b, s: (b, 0, 0, 0))
    tile = lambda n: pl.BlockSpec((None, TS, n), lambda b, s: (b, s, 0))
    return pl.pallas_call(
        _inproj_conv_kernel,
        grid=(bsz, seq // TS),
        in_specs=[tile(D_MODEL), mod, mod,
                  pl.BlockSpec((D_MODEL, D_IN), lambda b, s: (0, 0)), vec(D_IN),
                  pl.BlockSpec((D_MODEL, D_ATTN), lambda b, s: (0, 0)), vec(D_ATTN),
                  pl.BlockSpec((CONV_WIDTH, D_CONV), lambda b, s: (0, 0)),
                  vec(D_CONV), vec(D_CONV), vec(D_CONV), vec(D_CONV)],
        out_specs=[tile(D_CONV), tile(D_ATTN), tile(2 * D_KV)],
        out_shape=[jax.ShapeDtypeStruct((bsz, seq, D_CONV), BF16),
                   jax.ShapeDtypeStruct((bsz, seq, D_ATTN), BF16),
                   jax.ShapeDtypeStruct((bsz, seq, 2 * D_KV), BF16)],
        scratch_shapes=[pltpu.VMEM((D_MODEL, D_IN), BF16),
                        pltpu.VMEM((SUBLANES, TS + HALO, D_CONV), F32),
                        pltpu.VMEM((TS, D_CONV), F32),
                        pltpu.VMEM((CONV_WIDTH, SUBLANES, D_CONV), F32),
                        pltpu.VMEM((D_CONV, LANES), BF16),
                        pltpu.VMEM((LANES, D_CONV), BF16)],
        compiler_params=pltpu.CompilerParams(
            dimension_semantics=("arbitrary", "arbitrary"), vmem_limit_bytes=VMEM_LIMIT),
        name="inproj_conv",
    )(x, sh1, sc1, w_in, b_in, w_q, b_q, conv_w, conv_b, ln_g, ln_b, out_g)


def _attn_kernel(sinks_ref, q_ref, kvc_ref, kvp_ref, og_ref, o_ref, bsel, bexp):
    n = pl.program_id(1)

    @pl.when((pl.program_id(0) == 0) & (n == 0))
    def _():
        _fill_group_maps(bsel, bexp)

    qr = lax.broadcasted_iota(jnp.int32, (BLOCK, 2 * BLOCK), 0)
    kc = lax.broadcasted_iota(jnp.int32, (BLOCK, 2 * BLOCK), 1)
    band = (kc > qr) & (kc <= qr + BLOCK)
    lane = lax.broadcasted_iota(jnp.int32, (BLOCK, LANES), 1)
    lower = lane < HEAD_DIM
    lower_bf = jnp.where(lower, 1.0, 0.0).astype(BF16)
    upper_bf = jnp.where(lower, 0.0, 1.0).astype(BF16)
    og = og_ref[...]
    for blk in range(AB):
        r0 = blk * BLOCK
        if blk == 0:
            kv = jnp.concatenate([kvp_ref[...], kvc_ref[0:BLOCK, :]], axis=0)
            mask = band & ((n > 0) | (kc >= BLOCK))
        else:
            kv = kvc_ref[r0 - BLOCK:r0 + BLOCK, :]
            mask = band
        keys = kv[:, 0:D_KV]
        vals = kv[:, D_KV:2 * D_KV]
        parts = []
        for j in range(D_ATTN // LANES):
            col = q_ref[r0:r0 + BLOCK, j * LANES:(j + 1) * LANES]
            parts += [col * lower_bf, col * upper_bf]
        s_all = _dot_nt(jnp.concatenate(parts, axis=0), keys)
        probs = []
        den = jnp.ones((BLOCK, LANES), F32)
        for slot in range(N_HEADS):
            sc = jnp.where(mask, s_all[slot * BLOCK:(slot + 1) * BLOCK], NEG)
            sink = sinks_ref[PAIR_SLOTS[slot]]
            m = jnp.maximum(jnp.max(sc, axis=-1, keepdims=True), sink)
            p = jnp.exp(sc - m)
            den = jnp.where(lane == slot, jnp.sum(p, axis=-1, keepdims=True) + jnp.exp(sink - m), den)
            probs.append(p.astype(BF16))
        pv = _dot(jnp.concatenate(probs, axis=0), vals)
        o = jnp.concatenate(
            [jnp.where(lower, pv[(2 * j) * BLOCK:(2 * j + 1) * BLOCK], pv[(2 * j + 1) * BLOCK:(2 * j + 2) * BLOCK])
             for j in range(D_ATTN // LANES)], axis=1)
        ms = _dot((o * o).astype(BF16), bsel[...])
        r = lax.rsqrt(ms + EPS * den * den)
        r_hi = r.astype(BF16)
        r_lo = (r - r_hi.astype(F32)).astype(BF16)
        scale = _dot(r_hi, bexp[...]) + _dot(r_lo, bexp[...])
        o_ref[r0:r0 + BLOCK, :] = (o * scale * og).astype(BF16)


def _attention(q, kv, sinks, out_g):
    bsz, seq, _ = q.shape
    rows = AB * BLOCK
    grid_spec = pltpu.PrefetchScalarGridSpec(
        num_scalar_prefetch=1,
        grid=(bsz, seq // rows),
        in_specs=[pl.BlockSpec((None, rows, D_ATTN), lambda b, n, sk: (b, n, 0)),
                  pl.BlockSpec((None, rows, 2 * D_KV), lambda b, n, sk: (b, n, 0)),
                  pl.BlockSpec((None, BLOCK, 2 * D_KV), lambda b, n, sk: (b, jnp.maximum(AB * n - 1, 0), 0)),
                  pl.BlockSpec((1, D_ATTN), lambda b, n, sk: (0, 0))],
        out_specs=pl.BlockSpec((None, rows, D_ATTN), lambda b, n, sk: (b, n, 0)),
        scratch_shapes=[pltpu.VMEM((D_ATTN, LANES), BF16), pltpu.VMEM((LANES, D_ATTN), BF16)],
    )
    return pl.pallas_call(
        _attn_kernel,
        grid_spec=grid_spec,
        out_shape=jax.ShapeDtypeStruct((bsz, seq, D_ATTN), BF16),
        compiler_params=pltpu.CompilerParams(dimension_semantics=("arbitrary", "arbitrary")),
        name="swa_attention",
    )(sinks, q, kv, kv, out_g)


def _group_softmax_top(logits, lane):
    is_g = (lane >= N_EXPERTS) & (lane < N_EXPERTS + N_GROUPS)
    gl = jnp.where(is_g, logits, NEG)
    gmax = jnp.max(gl, axis=-1, keepdims=True)
    p_top = 1.0 / jnp.sum(jnp.where(is_g, jnp.exp(gl - gmax), 0.0), axis=-1, keepdims=True)
    return is_g, gl, gmax, p_top


def _route_bucket(logits):
    rows = logits.shape[0]
    lane = lax.broadcasted_iota(jnp.int32, (rows, ROUTER_COLS), 1)
    lanef = lane.astype(F32)
    is_g, gl, gmax, _ = _group_softmax_top(logits, lane)
    gidx = jnp.min(jnp.where(is_g & (gl == gmax), lanef - N_EXPERTS, 99.0), axis=-1, keepdims=True)
    in_grp = (lane < N_EXPERTS) & ((lane // EXPERTS_PER_GROUP) == gidx.astype(jnp.int32))
    el = jnp.where(in_grp, logits, NEG)
    m1 = jnp.max(el, axis=-1, keepdims=True)
    i1 = jnp.min(jnp.where(in_grp & (el == m1), lanef, 999.0), axis=-1, keepdims=True)
    rest = in_grp & (lanef != i1)
    el2 = jnp.where(rest, logits, NEG)
    m2 = jnp.max(el2, axis=-1, keepdims=True)
    i2 = jnp.min(jnp.where(rest & (el2 == m2), lanef, 999.0), axis=-1, keepdims=True)
    lo = jnp.minimum(i1, i2) - EXPERTS_PER_GROUP * gidx
    hi = jnp.maximum(i1, i2) - EXPERTS_PER_GROUP * gidx
    pair = lo * (EXPERTS_PER_GROUP - 1) - lo * (lo - 1.0) * 0.5 + (hi - lo - 1.0)
    return gidx * N_PAIRS + pair, lanef


def _outproj_kernel(x_ref, yc_ref, ya_ref, w_ref, b_ref, g1_ref, lg_ref, lb_ref, sc2_ref, sh2_ref,
                    wr_ref, br_ref, x1_ref, code_ref, counts_ref, wbf, wrbf, ltri, running):
    @pl.when((pl.program_id(0) == 0) & (pl.program_id(1) == 0))
    def _():
        wbf[...] = w_ref[...].astype(BF16)
        wrbf[...] = wr_ref[...].astype(BF16)
        r = lax.broadcasted_iota(jnp.int32, (TS, TS), 0)
        c = lax.broadcasted_iota(jnp.int32, (TS, TS), 1)
        ltri[...] = jnp.where(c < r, 1.0, 0.0).astype(BF16)
        running[...] = jnp.zeros_like(running)

    mix = _dot(yc_ref[...], wbf[0:D_CONV, :]) + _dot(ya_ref[...], wbf[D_CONV:, :]) + b_ref[...]
    x1 = _layer_norm(ALPHA * x_ref[...] + g1_ref[...] * mix, lg_ref[...], lb_ref[...])
    x1_ref[...] = x1
    h2 = (x1 * (1.0 + sc2_ref[...]) + sh2_ref[...]).astype(BF16)
    bucket, lanef = _route_bucket(_dot(h2, wrbf[...]) + br_ref[...])
    mine = lanef == bucket
    onehot = jnp.where(mine, 1.0, 0.0)
    earlier = _dot(ltri[...], onehot.astype(BF16)) + running[...]
    rank = jnp.sum(jnp.where(mine, earlier, 0.0), axis=-1, keepdims=True)
    running[...] += jnp.sum(onehot, axis=0, keepdims=True)
    code = bucket.astype(jnp.int32) * (1 << CODE_SHIFT) + rank.astype(jnp.int32)
    code_ref[...] = jnp.broadcast_to(code, code_ref.shape)
    counts_ref[...] = running[...]


def _outproj_route(x, y_conv, y_attn, w_out, b_out, g1, ln_g, ln_b, sc2, sh2, w_r, b_r):
    bsz, seq, _ = x.shape
    vec = lambda n: pl.BlockSpec((1, n), lambda b, s: (0, 0))
    mod = pl.BlockSpec((None, None, 1, D_MODEL), lambda b, s: (b, 0, 0, 0))
    tile = lambda n: pl.BlockSpec((None, TS, n), lambda b, s: (b, s, 0))
    return pl.pallas_call(
        _outproj_kernel,
        grid=(bsz, seq // TS),
        in_specs=[tile(D_MODEL), tile(D_CONV), tile(D_ATTN),
                  pl.BlockSpec((D_MODEL, D_MODEL), lambda b, s: (0, 0)), vec(D_MODEL),
                  mod, vec(D_MODEL), vec(D_MODEL), mod, mod,
                  pl.BlockSpec((D_MODEL, ROUTER_COLS), lambda b, s: (0, 0)), vec(ROUTER_COLS)],
        out_specs=[tile(D_MODEL), tile(LANES), vec(LANES)],
        out_shape=[jax.ShapeDtypeStruct((bsz, seq, D_MODEL), F32),
                   jax.ShapeDtypeStruct((bsz, seq, LANES), jnp.int32),
                   jax.ShapeDtypeStruct((1, LANES), F32)],
        scratch_shapes=[pltpu.VMEM((D_MODEL, D_MODEL), BF16), pltpu.VMEM((D_MODEL, ROUTER_COLS), BF16),
                        pltpu.VMEM((TS, TS), BF16), pltpu.VMEM((1, LANES), F32)],
        compiler_params=pltpu.CompilerParams(
            dimension_semantics=("arbitrary", "arbitrary"), vmem_limit_bytes=VMEM_LIMIT),
        name="outproj_route",
    )(x, y_conv, y_attn, w_out, b_out, g1, ln_g, ln_b, sc2, sh2, w_r, b_r)


def _row_of(starts_ref, code):
    return (starts_ref[lax.shift_right_logical(code, jnp.int32(CODE_SHIFT))]
            + (code & ((1 << CODE_SHIFT) - 1)))


CHUNKS = D_MODEL // LANES
assert CHUNKS == SUBLANES


def _tok_rows(p):
    return pl.ds(pl.multiple_of(p * CHUNKS, CHUNKS), CHUNKS)


def _to_tiles(ref, x):
    for c in range(CHUNKS):
        ref[pl.ds(c, x.shape[0], stride=CHUNKS), :] = x[:, c * LANES:(c + 1) * LANES]


def _from_tiles(ref, n):
    return jnp.concatenate([ref[pl.ds(c, n, stride=CHUNKS), :] for c in range(CHUNKS)], axis=1)


def _wait_tokens(hbm_ref, n, sem):
    pltpu.make_async_copy(hbm_ref.at[pl.ds(0, n * CHUNKS)], hbm_ref.at[pl.ds(0, n * CHUNKS)], sem).wait()


def _scatter_kernel(starts_ref, zrow_ref, nt_ref, code_ref, x1_ref, sc2_ref, sh2_ref, hs_ref, pos_ref,
                    hbuf, zeros, sems, zsem):
    nsteps = pl.num_programs(0) * pl.num_programs(1)
    step = pl.program_id(0) * pl.num_programs(1) + pl.program_id(1)
    slot = step % 2

    def _tile_rows(tok):
        return pl.ds(pl.multiple_of(tok * CHUNKS, TMM * CHUNKS), TMM * CHUNKS)

    def zero_copy(b):
        return pltpu.make_async_copy(zeros, hs_ref.at[_tile_rows(zrow_ref[b])], zsem)

    def tail_copy(t):
        return pltpu.make_async_copy(zeros, hs_ref.at[_tile_rows(t * TMM)], zsem)

    @pl.when(step == 0)
    def _():
        zeros[...] = jnp.zeros_like(zeros)
        n_tiles = hs_ref.shape[0] // (TMM * CHUNKS)

        def start(b, c):
            @pl.when(zrow_ref[b] >= 0)
            def _():
                zero_copy(b).start()
            return c

        def wait(b, c):
            @pl.when(zrow_ref[b] >= 0)
            def _():
                zero_copy(b).wait()
            return c

        def start_tail(t, c):
            tail_copy(t).start()
            return c

        def wait_tail(t, c):
            tail_copy(t).wait()
            return c

        lax.fori_loop(0, N_BUCKETS, start, 0)
        lax.fori_loop(nt_ref[0], n_tiles, start_tail, 0)
        lax.fori_loop(0, N_BUCKETS, wait, 0)
        lax.fori_loop(nt_ref[0], n_tiles, wait_tail, 0)

    @pl.when(step >= 2)
    def _():
        _wait_tokens(hs_ref, TS, sems.at[slot])

    _to_tiles(hbuf.at[slot], x1_ref[...] * (1.0 + sc2_ref[...]) + sh2_ref[...])

    def issue(i, c):
        for u in range(SUBLANES):
            r = i * SUBLANES + u
            pos = _row_of(starts_ref, code_ref[0, r])
            pos_ref[0, r] = pos
            pltpu.make_async_copy(hbuf.at[slot, _tok_rows(r)], hs_ref.at[_tok_rows(pos)],
                                  sems.at[slot]).start(priority=u % 2)
        return c

    lax.fori_loop(0, TS // SUBLANES, issue, 0)

    @pl.when(step == nsteps - 1)
    def _():
        _wait_tokens(hs_ref, TS, sems.at[slot])
        _wait_tokens(hs_ref, TS, sems.at[1 - slot])


def _scatter_rows(starts, zrow, nt, code, x1, sc2, sh2, n_rows):
    bsz, seq, _ = x1.shape
    nst = seq // TS
    grid_spec = pltpu.PrefetchScalarGridSpec(
        num_scalar_prefetch=3,
        grid=(bsz, nst),
        in_specs=[pl.BlockSpec((None, 1, TS), lambda b, s, *_: (b * nst + s, 0, 0), memory_space=pltpu.SMEM),
                  pl.BlockSpec((None, TS, D_MODEL), lambda b, s, *_: (b, s, 0)),
                  pl.BlockSpec((None, None, 1, D_MODEL), lambda b, s, *_: (b, 0, 0, 0)),
                  pl.BlockSpec((None, None, 1, D_MODEL), lambda b, s, *_: (b, 0, 0, 0))],
        out_specs=[pl.BlockSpec(memory_space=pl.ANY),
                   pl.BlockSpec((None, 1, TS), lambda b, s, *_: (b * nst + s, 0, 0), memory_space=pltpu.SMEM)],
        scratch_shapes=[pltpu.VMEM((2, TS * CHUNKS, LANES), F32), pltpu.VMEM((TMM * CHUNKS, LANES), F32),
                        pltpu.SemaphoreType.DMA((2,)), pltpu.SemaphoreType.DMA],
    )
    return pl.pallas_call(
        _scatter_kernel,
        grid_spec=grid_spec,
        out_shape=[jax.ShapeDtypeStruct((n_rows * CHUNKS, LANES), F32),
                   jax.ShapeDtypeStruct((bsz * nst, 1, TS), jnp.int32)],
        compiler_params=pltpu.CompilerParams(dimension_semantics=("arbitrary", "arbitrary")),
        name="moe_scatter",
    )(starts, zrow, nt, code.reshape(bsz * nst, 1, TS), x1, sc2, sh2)


def _moe_kernel(tlo_ref, thi_ref, blk_ref, nt_ref, h_ref, wg_hbm, wu_hbm, wd_hbm, wr_ref, br_ref, y_ref,
                wrbf, wg, wu, wd, sg, su, sd, sems):
    i = pl.program_id(0)
    group = lax.shift_right_logical(tlo_ref[i], jnp.int32(3))
    prev_group = lax.shift_right_logical(tlo_ref[jnp.maximum(i - 1, 0)], jnp.int32(3))

    @pl.when(i == 0)
    def _():
        wrbf[...] = wr_ref[...].astype(BF16)

    @pl.when((i == 0) | (group != prev_group))
    def _():
        def copies(e, slot):
            ex = group * EXPERTS_PER_GROUP + e
            return (pltpu.make_async_copy(wg_hbm.at[ex], sg.at[slot], sems.at[slot, 0]),
                    pltpu.make_async_copy(wu_hbm.at[ex], su.at[slot], sems.at[slot, 1]),
                    pltpu.make_async_copy(wd_hbm.at[ex], sd.at[slot], sems.at[slot, 2]))

        for cp in copies(0, 0):
            cp.start()

        def land(e, c):
            slot = e % 2

            @pl.when(e + 1 < EXPERTS_PER_GROUP)
            def _():
                for cp in copies(e + 1, 1 - slot):
                    cp.start()

            for cp in copies(e, slot):
                cp.wait()
            wg[e] = sg[slot].astype(BF16)
            wu[e] = su[slot].astype(BF16)
            wd[e] = sd[slot].astype(BF16)
            return c

        lax.fori_loop(0, EXPERTS_PER_GROUP, land, 0)

    @pl.when(i < nt_ref[0])
    def _():
        h = _from_tiles(h_ref, TMM).astype(BF16)
        logits = _dot(h, wrbf[...]) + br_ref[...]
        lane = lax.broadcasted_iota(jnp.int32, logits.shape, 1)
        p_top = _group_softmax_top(logits, lane)[3]
        l_lo = jnp.sum(jnp.where(lane == tlo_ref[i], logits, 0.0), axis=-1, keepdims=True)
        l_hi = jnp.sum(jnp.where(lane == thi_ref[i], logits, 0.0), axis=-1, keepdims=True)
        w_lo = p_top / (1.0 + jnp.exp(l_hi - l_lo))
        w_hi = p_top / (1.0 + jnp.exp(l_lo - l_hi))

        def ffn(e):
            a = _dot(h, wg[e])
            a = a * jax.nn.sigmoid(a) * _dot(h, wu[e])
            return _dot(a.astype(BF16), wd[e])

        local = EXPERTS_PER_GROUP - 1
        _to_tiles(y_ref, w_lo * ffn(tlo_ref[i] & local) + w_hi * ffn(thi_ref[i] & local))

    @pl.when(i >= nt_ref[0])
    def _():
        y_ref[...] = jnp.zeros_like(y_ref)


def _moe_sorted(tlo, thi, blk, nt, h_sorted, w_gate, w_up, w_down, w_r, b_r):
    n_rows = h_sorted.shape[0] // CHUNKS
    rows = pl.BlockSpec((TMM * CHUNKS, LANES), lambda i, tlo, thi, blk, nt: (blk[i], 0))
    hbm = pl.BlockSpec(memory_space=pl.ANY)
    w_in_shape, w_out_shape = (D_MODEL, D_EXPERT), (D_EXPERT, D_MODEL)
    grid_spec = pltpu.PrefetchScalarGridSpec(
        num_scalar_prefetch=4,
        grid=(n_rows // TMM,),
        in_specs=[rows, hbm, hbm, hbm,
                  pl.BlockSpec((D_MODEL, ROUTER_COLS), lambda i, *_: (0, 0)),
                  pl.BlockSpec((1, ROUTER_COLS), lambda i, *_: (0, 0))],
        out_specs=pl.BlockSpec((TMM * CHUNKS, LANES), lambda i, *_: (i, 0)),
        scratch_shapes=[pltpu.VMEM((D_MODEL, ROUTER_COLS), BF16),
                        pltpu.VMEM((EXPERTS_PER_GROUP,) + w_in_shape, BF16),
                        pltpu.VMEM((EXPERTS_PER_GROUP,) + w_in_shape, BF16),
                        pltpu.VMEM((EXPERTS_PER_GROUP,) + w_out_shape, BF16),
                        pltpu.VMEM((2,) + w_in_shape, F32), pltpu.VMEM((2,) + w_in_shape, F32),
                        pltpu.VMEM((2,) + w_out_shape, F32), pltpu.SemaphoreType.DMA((2, 3))],
    )
    return pl.pallas_call(
        _moe_kernel,
        grid_spec=grid_spec,
        out_shape=jax.ShapeDtypeStruct((n_rows * CHUNKS, LANES), F32),
        compiler_params=pltpu.CompilerParams(dimension_semantics=("arbitrary",), vmem_limit_bytes=VMEM_LIMIT),
        name="moe_sorted",
    )(tlo, thi, blk, nt, h_sorted, w_gate, w_up, w_down, w_r, b_r)


def _final_kernel(pos_ref, posn_ref, x1_ref, g2_ref, lg_ref, lb_ref, ys_ref, o_ref, ybuf, sems):
    nsteps = pl.num_programs(0) * pl.num_programs(1)
    step = pl.program_id(0) * pl.num_programs(1) + pl.program_id(1)
    slot = step % 2

    def gather(rows_ref, sl):
        def issue(i, c):
            for u in range(SUBLANES):
                r = i * SUBLANES + u
                pltpu.make_async_copy(ys_ref.at[_tok_rows(rows_ref[0, r])], ybuf.at[sl, _tok_rows(r)],
                                      sems.at[sl]).start(priority=u % 2)
            return c

        lax.fori_loop(0, TS // SUBLANES, issue, 0)

    @pl.when(step == 0)
    def _():
        gather(pos_ref, slot)

    @pl.when(step + 1 < nsteps)
    def _():
        gather(posn_ref, 1 - slot)

    _wait_tokens(ys_ref, TS, sems.at[slot])
    y = _from_tiles(ybuf.at[slot], TS)
    o_ref[...] = _layer_norm(ALPHA * x1_ref[...] + g2_ref[...] * y, lg_ref[...], lb_ref[...])


def _unsort_ln2(pos, x1, g2, ln_g, ln_b, y_sorted):
    bsz, seq, _ = x1.shape
    nst = seq // TS
    last = bsz * nst - 1
    return pl.pallas_call(
        _final_kernel,
        grid=(bsz, nst),
        in_specs=[pl.BlockSpec((None, 1, TS), lambda b, s: (b * nst + s, 0, 0), memory_space=pltpu.SMEM),
                  pl.BlockSpec((None, 1, TS), lambda b, s: (jnp.minimum(b * nst + s + 1, last), 0, 0),
                               memory_space=pltpu.SMEM),
                  pl.BlockSpec((None, TS, D_MODEL), lambda b, s: (b, s, 0)),
                  pl.BlockSpec((None, None, 1, D_MODEL), lambda b, s: (b, 0, 0, 0)),
                  pl.BlockSpec((1, D_MODEL), lambda b, s: (0, 0)),
                  pl.BlockSpec((1, D_MODEL), lambda b, s: (0, 0)),
                  pl.BlockSpec(memory_space=pl.ANY)],
        out_specs=pl.BlockSpec((None, TS, D_MODEL), lambda b, s: (b, s, 0)),
        out_shape=jax.ShapeDtypeStruct((bsz, seq, D_MODEL), F32),
        scratch_shapes=[pltpu.VMEM((2, TS * CHUNKS, LANES), F32), pltpu.SemaphoreType.DMA((2,))],
        compiler_params=pltpu.CompilerParams(dimension_semantics=("arbitrary", "arbitrary")),
        name="unsort_ln2",
    )(pos, pos, x1, g2, ln_g, ln_b, y_sorted)


def _pair_layout(a, axis):
    shp = a.shape
    a = a.reshape(shp[:axis] + (N_KV_HEADS, N_HEADS // N_KV_HEADS, HEAD_DIM) + shp[axis + 1:])
    return jnp.swapaxes(a, axis, axis + 1).reshape(shp)


def _tile_tables(counts, n_tiles_max):
    n_tile_b = jnp.floor((counts + (TMM - 1.0)) * (1.0 / TMM))
    b_idx = jnp.arange(N_BUCKETS, dtype=jnp.int32)
    tile_end = jnp.sum(jnp.where(b_idx[None, :] <= b_idx[:, None], n_tile_b[None, :], 0.0), axis=1)
    total = jnp.sum(n_tile_b)
    starts = ((tile_end - n_tile_b) * TMM).astype(jnp.int32)
    zrow = jnp.where(n_tile_b > 0, (tile_end - 1.0) * TMM, -1.0).astype(jnp.int32)
    blk = jnp.minimum(jnp.arange(n_tiles_max, dtype=jnp.int32), total.astype(jnp.int32) - 1)
    bucket = jnp.sum(jnp.where(tile_end[None, :] <= blk[:, None].astype(F32), 1, 0), axis=1)
    onehot = bucket[:, None] == b_idx[None, :]
    e_lo = jnp.asarray([(b // N_PAIRS) * EXPERTS_PER_GROUP + PAIRS[b % N_PAIRS][0] for b in range(N_BUCKETS)], jnp.int32)
    e_hi = jnp.asarray([(b // N_PAIRS) * EXPERTS_PER_GROUP + PAIRS[b % N_PAIRS][1] for b in range(N_BUCKETS)], jnp.int32)
    tlo = jnp.sum(jnp.where(onehot, e_lo[None, :], 0), axis=1)
    thi = jnp.sum(jnp.where(onehot, e_hi[None, :], 0), axis=1)
    pad = jnp.zeros((LANES - N_BUCKETS,), jnp.int32)
    return (jnp.concatenate([starts, pad]), jnp.concatenate([zrow, pad - 1]), tlo, thi, blk,
            total.astype(jnp.int32).reshape(1))


def kernel(x, c, w_ada, b_ada, w_in, b_in, conv_w, conv_b, conv_ln_g, conv_ln_b, conv_out_g, sinks,
           attn_out_g, w_out, b_out, ln1_g, ln1_b, w_router_group, b_router_group, w_router_expert,
           b_router_expert, w_gate, w_up, w_down, ln2_g, ln2_b):
    assert w_ada.shape[0] == DEPTH
    bsz, seq, _ = x.shape
    row = lambda v: v[0][None, :]

    mod = _modulation(c, w_ada[0], row(b_ada)).reshape(bsz, 6, 1, D_MODEL)
    sh1, sc1, g1, sh2, sc2, g2 = [mod[:, j:j + 1] for j in range(6)]

    q0 = 2 * D_CONV
    w_q = _pair_layout(w_in[0][:, q0:q0 + D_ATTN], 1)
    b_q = _pair_layout(b_in[0][q0:q0 + D_ATTN], 0)[None, :]
    og_attn = _pair_layout(attn_out_g[0], 0)[None, :]
    w_o = jnp.concatenate([w_out[0][:D_CONV], _pair_layout(w_out[0][D_CONV:], 0)], axis=0)

    y_conv, q, kv = _inproj_conv(x, sh1, sc1, w_in[0], row(b_in), w_q, b_q, conv_w[0], row(conv_b),
                                 row(conv_ln_g), row(conv_ln_b), row(conv_out_g))
    y_attn = _attention(q, kv, sinks[0], og_attn)

    pad = ROUTER_COLS - N_EXPERTS - N_GROUPS
    w_r = jnp.concatenate([w_router_expert[0], w_router_group[0], jnp.zeros((D_MODEL, pad), F32)], axis=1)
    b_r = jnp.concatenate([b_router_expert[0], b_router_group[0], jnp.zeros((pad,), F32)])[None, :]
    x1, code, counts = _outproj_route(x, y_conv, y_attn, w_o, row(b_out), g1, row(ln1_g), row(ln1_b),
                                      sc2, sh2, w_r, b_r)

    n_tiles_max = (bsz * seq) // TMM + N_BUCKETS
    starts, zrow, tlo, thi, blk, nt = _tile_tables(counts[0, :N_BUCKETS], n_tiles_max)
    h_sorted, pos = _scatter_rows(starts, zrow, nt, code[:, :, 0], x1, sc2, sh2, n_tiles_max * TMM)
    y_sorted = _moe_sorted(tlo, thi, blk, nt, h_sorted, w_gate[0], w_up[0], w_down[0], w_r, b_r)
    return _unsort_ln2(pos, x1, g2, row(ln2_g), row(ln2_b), y_sorted)
```

```python
import jax
import jax.numpy as jnp
from jax import lax
from jax.experimental import pallas as pl
from jax.experimental.pallas import tpu as pltpu

F32 = jnp.float32
BF16 = jnp.bfloat16

D_MODEL = 1024
D_CONV = 512
CONV_WIDTH = 31
GROUP_DIM = 64
N_HEADS = 8
N_KV_HEADS = 2
HEAD_DIM = 64
D_ATTN = N_HEADS * HEAD_DIM
D_KV = N_KV_HEADS * HEAD_DIM
BLOCK = 128
D_IN = 2 * D_CONV + D_ATTN + 2 * D_KV
N_GROUPS = 4
EXPERTS_PER_GROUP = 8
N_EXPERTS = N_GROUPS * EXPERTS_PER_GROUP
D_EXPERT = D_MODEL // 4
DEPTH = 1
ALPHA = (2.0 * DEPTH) ** 0.25
EPS = 1e-5
NEG = -1e30

LANES = 128
SUBLANES = 8
TS = 512
HALO = 32
CR = 128
AB = 2
PAIR_SLOTS = [j + (N_HEADS // N_KV_HEADS) * g for j in range(N_HEADS // N_KV_HEADS) for g in range(N_KV_HEADS)]
ROUTER_COLS = LANES
VMEM_LIMIT = 56 * 1024 * 1024

PAIRS = [(lo, hi) for lo in range(EXPERTS_PER_GROUP) for hi in range(lo + 1, EXPERTS_PER_GROUP)]
N_PAIRS = len(PAIRS)
N_BUCKETS = N_GROUPS * N_PAIRS
TMM = 128
CODE_SHIFT = 16


def _dot(a, b):
    return jnp.dot(a, b, preferred_element_type=F32)


def _dot_nt(a, b):
    return lax.dot_general(a, b, (((1,), (1,)), ((), ())), preferred_element_type=F32)


def _fill_group_maps(bsel_ref, bexp_ref):
    c = bsel_ref.shape[0]
    ch = lax.broadcasted_iota(jnp.int32, (c, LANES), 0) // GROUP_DIM
    gi = lax.broadcasted_iota(jnp.int32, (c, LANES), 1)
    bsel_ref[...] = jnp.where(ch == gi, 1.0 / GROUP_DIM, 0.0).astype(BF16)
    gi2 = lax.broadcasted_iota(jnp.int32, (LANES, c), 0)
    ch2 = lax.broadcasted_iota(jnp.int32, (LANES, c), 1) // GROUP_DIM
    bexp_ref[...] = jnp.where(ch2 == gi2, 1.0, 0.0).astype(BF16)


def _group_rms(y, bsel, bexp):
    ms = _dot((y * y).astype(BF16), bsel)
    r = lax.rsqrt(ms + EPS)
    r_hi = r.astype(BF16)
    r_lo = (r - r_hi.astype(F32)).astype(BF16)
    return y * (_dot(r_hi, bexp) + _dot(r_lo, bexp))


def _layer_norm(y, g, b):
    mu = jnp.mean(y, axis=-1, keepdims=True)
    d = y - mu
    var = jnp.mean(d * d, axis=-1, keepdims=True)
    return d * lax.rsqrt(var + EPS) * g + b


def _mod_kernel(c_ref, w_ref, b_ref, o_ref):
    c = c_ref[...]
    c_act = c * jax.nn.sigmoid(c)
    o_ref[...] = jnp.dot(c_act, w_ref[...], preferred_element_type=F32,
                         precision=lax.Precision.HIGHEST) + b_ref[...]


def _modulation(c, w_ada, b_ada):
    bsz = c.shape[0]
    n = w_ada.shape[1]
    return pl.pallas_call(
        _mod_kernel,
        grid=(n // D_MODEL,),
        in_specs=[pl.BlockSpec((bsz, D_MODEL), lambda j: (0, 0)),
                  pl.BlockSpec((D_MODEL, D_MODEL), lambda j: (0, j)),
                  pl.BlockSpec((1, D_MODEL), lambda j: (0, j))],
        out_specs=pl.BlockSpec((bsz, D_MODEL), lambda j: (0, j)),
        out_shape=jax.ShapeDtypeStruct((bsz, n), F32),
        name="adaln_mod",
    )(c, w_ada, b_ada)


def _inproj_conv_kernel(x_ref, sh_ref, sc_ref, w_ref, b_ref, wq_ref, bq_ref, cw_ref, cb_ref, lg_ref, lb_ref, og_ref,
                        yc_ref, q_ref, kv_ref, wbf, gs, cacc, wb, bsel, bexp):
    first = (pl.program_id(0) == 0) & (pl.program_id(1) == 0)
    s = pl.program_id(1)

    @pl.when(first)
    def _():
        wbf[...] = w_ref[...].astype(BF16)
        wbf[:, 2 * D_CONV:2 * D_CONV + D_ATTN] = wq_ref[...].astype(BF16)
        wb[...] = jnp.broadcast_to(cw_ref[...][:, None, :], wb.shape)
        _fill_group_maps(bsel, bexp)

    n_ct = D_CONV // LANES

    @pl.when(s == 0)
    def _():
        gs[0, :, 0:HALO, :] = jnp.zeros((n_ct, HALO, LANES), F32)

    @pl.when(s > 0)
    def _():
        gs[0, :, 0:HALO, :] = gs[0, :, TS:TS + HALO, :]

    h = (x_ref[...] * (1.0 + sc_ref[...]) + sh_ref[...]).astype(BF16)
    u_a = _dot(h, wbf[:, 0:D_CONV]) + b_ref[:, 0:D_CONV]
    u_b = _dot(h, wbf[:, D_CONV:2 * D_CONV]) + b_ref[:, D_CONV:2 * D_CONV]
    glu = u_a * jax.nn.sigmoid(u_b)
    for c in range(n_ct):
        gs[0, c, HALO:HALO + TS, :] = glu[:, c * LANES:(c + 1) * LANES]
    q0 = 2 * D_CONV
    q = _dot(h, wbf[:, q0:q0 + D_ATTN]) + bq_ref[...]
    q_ref[...] = (q * (HEAD_DIM ** -0.5)).astype(BF16)
    k0 = q0 + D_ATTN
    kv_ref[...] = (_dot(h, wbf[:, k0:k0 + 2 * D_KV]) + b_ref[:, k0:k0 + 2 * D_KV]).astype(BF16)

    n_sh = TS + HALO - SUBLANES
    for j in range(1, SUBLANES):
        gs[j, :, 0:n_sh, :] = gs[0, :, j:j + n_sh, :]

    cb = cb_ref[...]
    lg = lg_ref[...]
    lb = lb_ref[...]
    og = og_ref[...]

    for c in range(n_ct):
        ls = slice(c * LANES, (c + 1) * LANES)

        def conv_rows(i, carry, c=c, ls=ls):
            r0 = pl.multiple_of(i * CR, CR)
            acc = jnp.broadcast_to(cb[:, ls], (CR, LANES)).reshape(CR // SUBLANES, SUBLANES, LANES)
            for k in range(CONV_WIDTH):
                off = HALO - (CONV_WIDTH - 1) + k
                seg = gs[off % SUBLANES, c, pl.ds(r0 + (off // SUBLANES) * SUBLANES, CR), :]
                acc = acc + wb[k, :, ls][None] * seg.reshape(CR // SUBLANES, SUBLANES, LANES)
            cacc[c, pl.ds(r0, CR), :] = acc.reshape(CR, LANES)
            return carry

        lax.fori_loop(0, TS // CR, conv_rows, 0)
    y = _layer_norm(jnp.concatenate([cacc[c] for c in range(n_ct)], axis=1), lg, lb)
    y = y * jax.nn.sigmoid(y)
    yc_ref[...] = (_group_rms(y, bsel[...], bexp[...]) * og).astype(BF16)


def _inproj_conv(x, sh1, sc1, w_in, b_in, w_q, b_q, conv_w, conv_b, ln_g, ln_b, out_g):
    bsz, seq, _ = x.shape
    vec = lambda n: pl.BlockSpec((1, n), lambda b, s: (0, 0))
    mod = pl.BlockSpec((None, None, 1, D_MODEL), lambda b, s: (b, 0, 0, 0))
    tile = lambda n: pl.BlockSpec((None, TS, n), lambda b, s: (b, s, 0))
    return pl.pallas_call(
        _inproj_conv_kernel,
        grid=(bsz, seq // TS),
        in_specs=[tile(D_MODEL), mod, mod,
                  pl.BlockSpec((D_MODEL, D_IN), lambda b, s: (0, 0)), vec(D_IN),
                  pl.BlockSpec((D_MODEL, D_ATTN), lambda b, s: (0, 0)), vec(D_ATTN),
                  pl.BlockSpec((CONV_WIDTH, D_CONV), lambda b, s: (0, 0)),
                  vec(D_CONV), vec(D_CONV), vec(D_CONV), vec(D_CONV)],
        out_specs=[tile(D_CONV), tile(D_ATTN), tile(2 * D_KV)],
        out_shape=[jax.ShapeDtypeStruct((bsz, seq, D_CONV), BF16),
                   jax.ShapeDtypeStruct((bsz, seq, D_ATTN), BF16),
                   jax.ShapeDtypeStruct((bsz, seq, 2 * D_KV), BF16)],
        scratch_shapes=[pltpu.VMEM((D_MODEL, D_IN), BF16),
                        pltpu.VMEM((SUBLANES, D_CONV // LANES, TS + HALO, LANES), F32),
                        pltpu.VMEM((D_CONV // LANES, TS, LANES), F32),
                        pltpu.VMEM((CONV_WIDTH, SUBLANES, D_CONV), F32),
                        pltpu.VMEM((D_CONV, LANES), BF16),
                        pltpu.VMEM((LANES, D_CONV), BF16)],
        compiler_params=pltpu.CompilerParams(
            dimension_semantics=("arbitrary", "arbitrary"), vmem_limit_bytes=VMEM_LIMIT),
        name="inproj_conv",
    )(x, sh1, sc1, w_in, b_in, w_q, b_q, conv_w, conv_b, ln_g, ln_b, out_g)


def _attn_kernel(sinks_ref, q_ref, kvc_ref, kvp_ref, og_ref, o_ref, bsel, bexp):
    n = pl.program_id(1)

    @pl.when((pl.program_id(0) == 0) & (n == 0))
    def _():
        _fill_group_maps(bsel, bexp)

    qr = lax.broadcasted_iota(jnp.int32, (BLOCK, 2 * BLOCK), 0)
    kc = lax.broadcasted_iota(jnp.int32, (BLOCK, 2 * BLOCK), 1)
    band = (kc > qr) & (kc <= qr + BLOCK)
    lane = lax.broadcasted_iota(jnp.int32, (BLOCK, LANES), 1)
    lower = lane < HEAD_DIM
    lower_bf = jnp.where(lower, 1.0, 0.0).astype(BF16)
    upper_bf = jnp.where(lower, 0.0, 1.0).astype(BF16)
    og = og_ref[...]
    for blk in range(AB):
        r0 = blk * BLOCK
        if blk == 0:
            kv = jnp.concatenate([kvp_ref[...], kvc_ref[0:BLOCK, :]], axis=0)
            mask = band & ((n > 0) | (kc >= BLOCK))
        else:
            kv = kvc_ref[r0 - BLOCK:r0 + BLOCK, :]
            mask = band
        keys = kv[:, 0:D_KV]
        vals = kv[:, D_KV:2 * D_KV]
        parts = []
        for j in range(D_ATTN // LANES):
            col = q_ref[r0:r0 + BLOCK, j * LANES:(j + 1) * LANES]
            parts += [col * lower_bf, col * upper_bf]
        s_all = _dot_nt(jnp.concatenate(parts, axis=0), keys)
        probs = []
        den = jnp.ones((BLOCK, LANES), F32)
        for slot in range(N_HEADS):
            sc = jnp.where(mask, s_all[slot * BLOCK:(slot + 1) * BLOCK], NEG)
            sink = sinks_ref[PAIR_SLOTS[slot]]
            m = jnp.maximum(jnp.max(sc, axis=-1, keepdims=True), sink)
            p = jnp.exp(sc - m)
            den = jnp.where(lane == slot, jnp.sum(p, axis=-1, keepdims=True) + jnp.exp(sink - m), den)
            probs.append(p.astype(BF16))
        pv = _dot(jnp.concatenate(probs, axis=0), vals)
        o = jnp.concatenate(
            [jnp.where(lower, pv[(2 * j) * BLOCK:(2 * j + 1) * BLOCK], pv[(2 * j + 1) * BLOCK:(2 * j + 2) * BLOCK])
             for j in range(D_ATTN // LANES)], axis=1)
        ms = _dot((o * o).astype(BF16), bsel[...])
        r = lax.rsqrt(ms + EPS * den * den)
        r_hi = r.astype(BF16)
        r_lo = (r - r_hi.astype(F32)).astype(BF16)
        scale = _dot(r_hi, bexp[...]) + _dot(r_lo, bexp[...])
        o_ref[r0:r0 + BLOCK, :] = (o * scale * og).astype(BF16)


def _attention(q, kv, sinks, out_g):
    bsz, seq, _ = q.shape
    rows = AB * BLOCK
    grid_spec = pltpu.PrefetchScalarGridSpec(
        num_scalar_prefetch=1,
        grid=(bsz, seq // rows),
        in_specs=[pl.BlockSpec((None, rows, D_ATTN), lambda b, n, sk: (b, n, 0)),
                  pl.BlockSpec((None, rows, 2 * D_KV), lambda b, n, sk: (b, n, 0)),
                  pl.BlockSpec((None, BLOCK, 2 * D_KV), lambda b, n, sk: (b, jnp.maximum(AB * n - 1, 0), 0)),
                  pl.BlockSpec((1, D_ATTN), lambda b, n, sk: (0, 0))],
        out_specs=pl.BlockSpec((None, rows, D_ATTN), lambda b, n, sk: (b, n, 0)),
        scratch_shapes=[pltpu.VMEM((D_ATTN, LANES), BF16), pltpu.VMEM((LANES, D_ATTN), BF16)],
    )
    return pl.pallas_call(
        _attn_kernel,
        grid_spec=grid_spec,
        out_shape=jax.ShapeDtypeStruct((bsz, seq, D_ATTN), BF16),
        compiler_params=pltpu.CompilerParams(dimension_semantics=("arbitrary", "arbitrary")),
        name="swa_attention",
    )(sinks, q, kv, kv, out_g)


def _group_softmax_top(logits, lane):
    is_g = (lane >= N_EXPERTS) & (lane < N_EXPERTS + N_GROUPS)
    gl = jnp.where(is_g, logits, NEG)
    gmax = jnp.max(gl, axis=-1, keepdims=True)
    p_top = 1.0 / jnp.sum(jnp.where(is_g, jnp.exp(gl - gmax), 0.0), axis=-1, keepdims=True)
    return is_g, gl, gmax, p_top


def _route_bucket(logits):
    rows = logits.shape[0]
    lane = lax.broadcasted_iota(jnp.int32, (rows, ROUTER_COLS), 1)
    lanef = lane.astype(F32)
    is_g, gl, gmax, _ = _group_softmax_top(logits, lane)
    gidx = jnp.min(jnp.where(is_g & (gl == gmax), lanef - N_EXPERTS, 99.0), axis=-1, keepdims=True)
    in_grp = (lane < N_EXPERTS) & ((lane // EXPERTS_PER_GROUP) == gidx.astype(jnp.int32))
    el = jnp.where(in_grp, logits, NEG)
    m1 = jnp.max(el, axis=-1, keepdims=True)
    i1 = jnp.min(jnp.where(in_grp & (el == m1), lanef, 999.0), axis=-1, keepdims=True)
    rest = in_grp & (lanef != i1)
    el2 = jnp.where(rest, logits, NEG)
    m2 = jnp.max(el2, axis=-1, keepdims=True)
    i2 = jnp.min(jnp.where(rest & (el2 == m2), lanef, 999.0), axis=-1, keepdims=True)
    lo = jnp.minimum(i1, i2) - EXPERTS_PER_GROUP * gidx
    hi = jnp.maximum(i1, i2) - EXPERTS_PER_GROUP * gidx
    pair = lo * (EXPERTS_PER_GROUP - 1) - lo * (lo - 1.0) * 0.5 + (hi - lo - 1.0)
    return gidx * N_PAIRS + pair, lanef


def _outproj_kernel(x_ref, yc_ref, ya_ref, w_ref, b_ref, g1_ref, lg_ref, lb_ref, sc2_ref, sh2_ref,
                    wr_ref, br_ref, x1_ref, h2t_ref, code_ref, counts_ref, wbf, wrbf, ltri, running):
    @pl.when((pl.program_id(0) == 0) & (pl.program_id(1) == 0))
    def _():
        wbf[...] = w_ref[...].astype(BF16)
        wrbf[...] = wr_ref[...].astype(BF16)
        r = lax.broadcasted_iota(jnp.int32, (TS, TS), 0)
        c = lax.broadcasted_iota(jnp.int32, (TS, TS), 1)
        ltri[...] = jnp.where(c < r, 1.0, 0.0).astype(BF16)
        running[...] = jnp.zeros_like(running)

    mix = _dot(yc_ref[...], wbf[0:D_CONV, :]) + _dot(ya_ref[...], wbf[D_CONV:, :]) + b_ref[...]
    x1 = _layer_norm(ALPHA * x_ref[...] + g1_ref[...] * mix, lg_ref[...], lb_ref[...])
    x1_ref[...] = x1
    h2 = x1 * (1.0 + sc2_ref[...]) + sh2_ref[...]
    _to_tiles(h2t_ref, h2)
    h2 = h2.astype(BF16)
    bucket, lanef = _route_bucket(_dot(h2, wrbf[...]) + br_ref[...])
    mine = lanef == bucket
    onehot = jnp.where(mine, 1.0, 0.0)
    earlier = _dot(ltri[...], onehot.astype(BF16)) + running[...]
    rank = jnp.sum(jnp.where(mine, earlier, 0.0), axis=-1, keepdims=True)
    running[...] += jnp.sum(onehot, axis=0, keepdims=True)
    code = bucket.astype(jnp.int32) * (1 << CODE_SHIFT) + rank.astype(jnp.int32)
    code_ref[...] = jnp.broadcast_to(code, code_ref.shape)
    counts_ref[...] = running[...]


def _outproj_route(x, y_conv, y_attn, w_out, b_out, g1, ln_g, ln_b, sc2, sh2, w_r, b_r):
    bsz, seq, _ = x.shape
    nst = seq // TS
    vec = lambda n: pl.BlockSpec((1, n), lambda b, s: (0, 0))
    mod = pl.BlockSpec((None, None, 1, D_MODEL), lambda b, s: (b, 0, 0, 0))
    tile = lambda n: pl.BlockSpec((None, TS, n), lambda b, s: (b, s, 0))
    return pl.pallas_call(
        _outproj_kernel,
        grid=(bsz, seq // TS),
        in_specs=[tile(D_MODEL), tile(D_CONV), tile(D_ATTN),
                  pl.BlockSpec((D_MODEL, D_MODEL), lambda b, s: (0, 0)), vec(D_MODEL),
                  mod, vec(D_MODEL), vec(D_MODEL), mod, mod,
                  pl.BlockSpec((D_MODEL, ROUTER_COLS), lambda b, s: (0, 0)), vec(ROUTER_COLS)],
        out_specs=[tile(D_MODEL), pl.BlockSpec((TS * CHUNKS, LANES), lambda b, s: (b * nst + s, 0)),
                   tile(LANES), vec(LANES)],
        out_shape=[jax.ShapeDtypeStruct((bsz, seq, D_MODEL), F32),
                   jax.ShapeDtypeStruct((bsz * seq * CHUNKS, LANES), F32),
                   jax.ShapeDtypeStruct((bsz, seq, LANES), jnp.int32),
                   jax.ShapeDtypeStruct((1, LANES), F32)],
        scratch_shapes=[pltpu.VMEM((D_MODEL, D_MODEL), BF16), pltpu.VMEM((D_MODEL, ROUTER_COLS), BF16),
                        pltpu.VMEM((TS, TS), BF16), pltpu.VMEM((1, LANES), F32)],
        compiler_params=pltpu.CompilerParams(
            dimension_semantics=("arbitrary", "arbitrary"), vmem_limit_bytes=VMEM_LIMIT),
        name="outproj_route",
    )(x, y_conv, y_attn, w_out, b_out, g1, ln_g, ln_b, sc2, sh2, w_r, b_r)


def _row_of(starts_ref, code):
    return (starts_ref[lax.shift_right_logical(code, jnp.int32(CODE_SHIFT))]
            + (code & ((1 << CODE_SHIFT) - 1)))


CHUNKS = D_MODEL // LANES
assert CHUNKS == SUBLANES


def _tok_rows(p):
    return pl.ds(pl.multiple_of(p * CHUNKS, CHUNKS), CHUNKS)


def _to_tiles(ref, x):
    for c in range(CHUNKS):
        ref[pl.ds(c, x.shape[0], stride=CHUNKS), :] = x[:, c * LANES:(c + 1) * LANES]


def _from_tiles(ref, n):
    return jnp.concatenate([ref[pl.ds(c, n, stride=CHUNKS), :] for c in range(CHUNKS)], axis=1)


def _wait_tokens(hbm_ref, n, sem):
    pltpu.make_async_copy(hbm_ref.at[pl.ds(0, n * CHUNKS)], hbm_ref.at[pl.ds(0, n * CHUNKS)], sem).wait()


def _scatter_kernel(starts_ref, zrow_ref, nt_ref, code_ref, h2t_ref, hs_ref, pos_ref, zeros, sems, zsem):
    nsteps = pl.num_programs(0) * pl.num_programs(1)
    step = pl.program_id(0) * pl.num_programs(1) + pl.program_id(1)
    slot = step % 2

    def _tile_rows(tok):
        return pl.ds(pl.multiple_of(tok * CHUNKS, TMM * CHUNKS), TMM * CHUNKS)

    def zero_copy(b):
        return pltpu.make_async_copy(zeros, hs_ref.at[_tile_rows(zrow_ref[b])], zsem)

    def tail_copy(t):
        return pltpu.make_async_copy(zeros, hs_ref.at[_tile_rows(t * TMM)], zsem)

    @pl.when(step == 0)
    def _():
        zeros[...] = jnp.zeros_like(zeros)
        n_tiles = hs_ref.shape[0] // (TMM * CHUNKS)

        def start(b, c):
            @pl.when(zrow_ref[b] >= 0)
            def _():
                zero_copy(b).start()
            return c

        def wait(b, c):
            @pl.when(zrow_ref[b] >= 0)
            def _():
                zero_copy(b).wait()
            return c

        def start_tail(t, c):
            tail_copy(t).start()
            return c

        def wait_tail(t, c):
            tail_copy(t).wait()
            return c

        lax.fori_loop(0, N_BUCKETS, start, 0)
        lax.fori_loop(nt_ref[0], n_tiles, start_tail, 0)
        lax.fori_loop(0, N_BUCKETS, wait, 0)
        lax.fori_loop(nt_ref[0], n_tiles, wait_tail, 0)

    @pl.when(step >= 2)
    def _():
        _wait_tokens(hs_ref, TS, sems.at[slot])

    def issue(i, c):
        for u in range(SUBLANES):
            r = i * SUBLANES + u
            pos = _row_of(starts_ref, code_ref[0, r])
            pos_ref[0, r] = pos
            pltpu.make_async_copy(h2t_ref.at[_tok_rows(step * TS + r)], hs_ref.at[_tok_rows(pos)],
                                  sems.at[slot]).start(priority=u % 2)
        return c

    lax.fori_loop(0, TS // SUBLANES, issue, 0)

    @pl.when(step == nsteps - 1)
    def _():
        _wait_tokens(hs_ref, TS, sems.at[slot])
        _wait_tokens(hs_ref, TS, sems.at[1 - slot])


def _scatter_rows(starts, zrow, nt, code, h2t, n_rows):
    bsz, seq = code.shape
    nst = seq // TS
    grid_spec = pltpu.PrefetchScalarGridSpec(
        num_scalar_prefetch=3,
        grid=(bsz, nst),
        in_specs=[pl.BlockSpec((None, 1, TS), lambda b, s, *_: (b * nst + s, 0, 0), memory_space=pltpu.SMEM),
                  pl.BlockSpec(memory_space=pl.ANY)],
        out_specs=[pl.BlockSpec(memory_space=pl.ANY),
                   pl.BlockSpec((None, 1, TS), lambda b, s, *_: (b * nst + s, 0, 0), memory_space=pltpu.SMEM)],
        scratch_shapes=[pltpu.VMEM((TMM * CHUNKS, LANES), F32),
                        pltpu.SemaphoreType.DMA((2,)), pltpu.SemaphoreType.DMA],
    )
    return pl.pallas_call(
        _scatter_kernel,
        grid_spec=grid_spec,
        out_shape=[jax.ShapeDtypeStruct((n_rows * CHUNKS, LANES), F32),
                   jax.ShapeDtypeStruct((bsz * nst, 1, TS), jnp.int32)],
        compiler_params=pltpu.CompilerParams(dimension_semantics=("arbitrary", "arbitrary")),
        name="moe_scatter",
    )(starts, zrow, nt, code.reshape(bsz * nst, 1, TS), h2t)


def _moe_kernel(tlo_ref, thi_ref, blk_ref, nt_ref, h_ref, wg_hbm, wu_hbm, wd_hbm, wr_ref, br_ref, y_ref,
                wrbf, wg, wu, wd, sg, su, sd, sems):
    i = pl.program_id(0)
    group = lax.shift_right_logical(tlo_ref[i], jnp.int32(3))
    prev_group = lax.shift_right_logical(tlo_ref[jnp.maximum(i - 1, 0)], jnp.int32(3))

    @pl.when(i == 0)
    def _():
        wrbf[...] = wr_ref[...].astype(BF16)

    @pl.when((i == 0) | (group != prev_group))
    def _():
        def copies(e, slot):
            ex = group * EXPERTS_PER_GROUP + e
            return (pltpu.make_async_copy(wg_hbm.at[ex], sg.at[slot], sems.at[slot, 0]),
                    pltpu.make_async_copy(wu_hbm.at[ex], su.at[slot], sems.at[slot, 1]),
                    pltpu.make_async_copy(wd_hbm.at[ex], sd.at[slot], sems.at[slot, 2]))

        for cp in copies(0, 0):
            cp.start()

        def land(e, c):
            slot = e % 2

            @pl.when(e + 1 < EXPERTS_PER_GROUP)
            def _():
                for cp in copies(e + 1, 1 - slot):
                    cp.start()

            for cp in copies(e, slot):
                cp.wait()
            wg[e] = sg[slot].astype(BF16)
            wu[e] = su[slot].astype(BF16)
            wd[e] = sd[slot].astype(BF16)
            return c

        lax.fori_loop(0, EXPERTS_PER_GROUP, land, 0)

    @pl.when(i < nt_ref[0])
    def _():
        h = _from_tiles(h_ref, TMM).astype(BF16)
        logits = _dot(h, wrbf[...]) + br_ref[...]
        lane = lax.broadcasted_iota(jnp.int32, logits.shape, 1)
        p_top = _group_softmax_top(logits, lane)[3]
        l_lo = jnp.sum(jnp.where(lane == tlo_ref[i], logits, 0.0), axis=-1, keepdims=True)
        l_hi = jnp.sum(jnp.where(lane == thi_ref[i], logits, 0.0), axis=-1, keepdims=True)
        w_lo = p_top / (1.0 + jnp.exp(l_hi - l_lo))
        w_hi = p_top / (1.0 + jnp.exp(l_lo - l_hi))

        def ffn(e):
            a = _dot(h, wg[e])
            a = a * jax.nn.sigmoid(a) * _dot(h, wu[e])
            return _dot(a.astype(BF16), wd[e])

        local = EXPERTS_PER_GROUP - 1
        _to_tiles(y_ref, w_lo * ffn(tlo_ref[i] & local) + w_hi * ffn(thi_ref[i] & local))

    @pl.when(i >= nt_ref[0])
    def _():
        y_ref[...] = jnp.zeros_like(y_ref)


def _moe_sorted(tlo, thi, blk, nt, h_sorted, w_gate, w_up, w_down, w_r, b_r):
    n_rows = h_sorted.shape[0] // CHUNKS
    rows = pl.BlockSpec((TMM * CHUNKS, LANES), lambda i, tlo, thi, blk, nt: (blk[i], 0))
    hbm = pl.BlockSpec(memory_space=pl.ANY)
    w_in_shape, w_out_shape = (D_MODEL, D_EXPERT), (D_EXPERT, D_MODEL)
    grid_spec = pltpu.PrefetchScalarGridSpec(
        num_scalar_prefetch=4,
        grid=(n_rows // TMM,),
        in_specs=[rows, hbm, hbm, hbm,
                  pl.BlockSpec((D_MODEL, ROUTER_COLS), lambda i, *_: (0, 0)),
                  pl.BlockSpec((1, ROUTER_COLS), lambda i, *_: (0, 0))],
        out_specs=pl.BlockSpec((TMM * CHUNKS, LANES), lambda i, *_: (i, 0)),
        scratch_shapes=[pltpu.VMEM((D_MODEL, ROUTER_COLS), BF16),
                        pltpu.VMEM((EXPERTS_PER_GROUP,) + w_in_shape, BF16),
                        pltpu.VMEM((EXPERTS_PER_GROUP,) + w_in_shape, BF16),
                        pltpu.VMEM((EXPERTS_PER_GROUP,) + w_out_shape, BF16),
                        pltpu.VMEM((2,) + w_in_shape, F32), pltpu.VMEM((2,) + w_in_shape, F32),
                        pltpu.VMEM((2,) + w_out_shape, F32), pltpu.SemaphoreType.DMA((2, 3))],
    )
    return pl.pallas_call(
        _moe_kernel,
        grid_spec=grid_spec,
        out_shape=jax.ShapeDtypeStruct((n_rows * CHUNKS, LANES), F32),
        compiler_params=pltpu.CompilerParams(dimension_semantics=("arbitrary",), vmem_limit_bytes=VMEM_LIMIT),
        name="moe_sorted",
    )(tlo, thi, blk, nt, h_sorted, w_gate, w_up, w_down, w_r, b_r)


def _final_kernel(pos_ref, posn_ref, x1_ref, g2_ref, lg_ref, lb_ref, ys_ref, o_ref, ybuf, sems):
    nsteps = pl.num_programs(0) * pl.num_programs(1)
    step = pl.program_id(0) * pl.num_programs(1) + pl.program_id(1)
    slot = step % 2

    def gather(rows_ref, sl):
        def issue(i, c):
            for u in range(SUBLANES):
                r = i * SUBLANES + u
                pltpu.make_async_copy(ys_ref.at[_tok_rows(rows_ref[0, r])], ybuf.at[sl, _tok_rows(r)],
                                      sems.at[sl]).start(priority=u % 2)
            return c

        lax.fori_loop(0, TS // SUBLANES, issue, 0)

    @pl.when(step == 0)
    def _():
        gather(pos_ref, slot)

    @pl.when(step + 1 < nsteps)
    def _():
        gather(posn_ref, 1 - slot)

    _wait_tokens(ys_ref, TS, sems.at[slot])
    y = _from_tiles(ybuf.at[slot], TS)
    o_ref[...] = _layer_norm(ALPHA * x1_ref[...] + g2_ref[...] * y, lg_ref[...], lb_ref[...])


def _unsort_ln2(pos, x1, g2, ln_g, ln_b, y_sorted):
    bsz, seq, _ = x1.shape
    nst = seq // TS
    last = bsz * nst - 1
    return pl.pallas_call(
        _final_kernel,
        grid=(bsz, nst),
        in_specs=[pl.BlockSpec((None, 1, TS), lambda b, s: (b * nst + s, 0, 0), memory_space=pltpu.SMEM),
                  pl.BlockSpec((None, 1, TS), lambda b, s: (jnp.minimum(b * nst + s + 1, last), 0, 0),
                               memory_space=pltpu.SMEM),
                  pl.BlockSpec((None, TS, D_MODEL), lambda b, s: (b, s, 0)),
                  pl.BlockSpec((None, None, 1, D_MODEL), lambda b, s: (b, 0, 0, 0)),
                  pl.BlockSpec((1, D_MODEL), lambda b, s: (0, 0)),
                  pl.BlockSpec((1, D_MODEL), lambda b, s: (0, 0)),
                  pl.BlockSpec(memory_space=pl.ANY)],
        out_specs=pl.BlockSpec((None, TS, D_MODEL), lambda b, s: (b, s, 0)),
        out_shape=jax.ShapeDtypeStruct((bsz, seq, D_MODEL), F32),
        scratch_shapes=[pltpu.VMEM((2, TS * CHUNKS, LANES), F32), pltpu.SemaphoreType.DMA((2,))],
        compiler_params=pltpu.CompilerParams(dimension_semantics=("arbitrary", "arbitrary")),
        name="unsort_ln2",
    )(pos, pos, x1, g2, ln_g, ln_b, y_sorted)


def _pair_layout(a, axis):
    shp = a.shape
    a = a.reshape(shp[:axis] + (N_KV_HEADS, N_HEADS // N_KV_HEADS, HEAD_DIM) + shp[axis + 1:])
    return jnp.swapaxes(a, axis, axis + 1).reshape(shp)


def _tile_tables(counts, n_tiles_max):
    n_tile_b = jnp.floor((counts + (TMM - 1.0)) * (1.0 / TMM))
    b_idx = jnp.arange(N_BUCKETS, dtype=jnp.int32)
    tile_end = jnp.sum(jnp.where(b_idx[None, :] <= b_idx[:, None], n_tile_b[None, :], 0.0), axis=1)
    total = jnp.sum(n_tile_b)
    starts = ((tile_end - n_tile_b) * TMM).astype(jnp.int32)
    zrow = jnp.where(n_tile_b > 0, (tile_end - 1.0) * TMM, -1.0).astype(jnp.int32)
    blk = jnp.minimum(jnp.arange(n_tiles_max, dtype=jnp.int32), total.astype(jnp.int32) - 1)
    bucket = jnp.sum(jnp.where(tile_end[None, :] <= blk[:, None].astype(F32), 1, 0), axis=1)
    onehot = bucket[:, None] == b_idx[None, :]
    e_lo = jnp.asarray([(b // N_PAIRS) * EXPERTS_PER_GROUP + PAIRS[b % N_PAIRS][0] for b in range(N_BUCKETS)], jnp.int32)
    e_hi = jnp.asarray([(b // N_PAIRS) * EXPERTS_PER_GROUP + PAIRS[b % N_PAIRS][1] for b in range(N_BUCKETS)], jnp.int32)
    tlo = jnp.sum(jnp.where(onehot, e_lo[None, :], 0), axis=1)
    thi = jnp.sum(jnp.where(onehot, e_hi[None, :], 0), axis=1)
    pad = jnp.zeros((LANES - N_BUCKETS,), jnp.int32)
    return (jnp.concatenate([starts, pad]), jnp.concatenate([zrow, pad - 1]), tlo, thi, blk,
            total.astype(jnp.int32).reshape(1))


def kernel(x, c, w_ada, b_ada, w_in, b_in, conv_w, conv_b, conv_ln_g, conv_ln_b, conv_out_g, sinks,
           attn_out_g, w_out, b_out, ln1_g, ln1_b, w_router_group, b_router_group, w_router_expert,
           b_router_expert, w_gate, w_up, w_down, ln2_g, ln2_b):
    assert w_ada.shape[0] == DEPTH
    bsz, seq, _ = x.shape
    row = lambda v: v[0][None, :]

    mod = _modulation(c, w_ada[0], row(b_ada)).reshape(bsz, 6, 1, D_MODEL)
    sh1, sc1, g1, sh2, sc2, g2 = [mod[:, j:j + 1] for j in range(6)]

    q0 = 2 * D_CONV
    w_q = _pair_layout(w_in[0][:, q0:q0 + D_ATTN], 1)
    b_q = _pair_layout(b_in[0][q0:q0 + D_ATTN], 0)[None, :]
    og_attn = _pair_layout(attn_out_g[0], 0)[None, :]
    w_o = jnp.concatenate([w_out[0][:D_CONV], _pair_layout(w_out[0][D_CONV:], 0)], axis=0)

    y_conv, q, kv = _inproj_conv(x, sh1, sc1, w_in[0], row(b_in), w_q, b_q, conv_w[0], row(conv_b),
                                 row(conv_ln_g), row(conv_ln_b), row(conv_out_g))
    y_attn = _attention(q, kv, sinks[0], og_attn)

    pad = ROUTER_COLS - N_EXPERTS - N_GROUPS
    w_r = jnp.concatenate([w_router_expert[0], w_router_group[0], jnp.zeros((D_MODEL, pad), F32)], axis=1)
    b_r = jnp.concatenate([b_router_expert[0], b_router_group[0], jnp.zeros((pad,), F32)])[None, :]
    x1, h2t, code, counts = _outproj_route(x, y_conv, y_attn, w_o, row(b_out), g1, row(ln1_g), row(ln1_b),
                                      sc2, sh2, w_r, b_r)

    n_tiles_max = (bsz * seq) // TMM + N_BUCKETS
    starts, zrow, tlo, thi, blk, nt = _tile_tables(counts[0, :N_BUCKETS], n_tiles_max)
    h_sorted, pos = _scatter_rows(starts, zrow, nt, code[:, :, 0], h2t, n_tiles_max * TMM)
    y_sorted = _moe_sorted(tlo, thi, blk, nt, h_sorted, w_gate[0], w_up[0], w_down[0], w_r, b_r)
    return _unsort_ln2(pos, x1, g2, row(ln2_g), row(ln2_b), y_sorted)
```

```python
import jax
import jax.numpy as jnp
from jax import lax
from jax.experimental import pallas as pl
from jax.experimental.pallas import tpu as pltpu

F32 = jnp.float32
BF16 = jnp.bfloat16

D_MODEL = 1024
D_CONV = 512
CONV_WIDTH = 31
GROUP_DIM = 64
N_HEADS = 8
N_KV_HEADS = 2
HEAD_DIM = 64
D_ATTN = N_HEADS * HEAD_DIM
D_KV = N_KV_HEADS * HEAD_DIM
BLOCK = 128
D_IN = 2 * D_CONV + D_ATTN + 2 * D_KV
N_GROUPS = 4
EXPERTS_PER_GROUP = 8
N_EXPERTS = N_GROUPS * EXPERTS_PER_GROUP
D_EXPERT = D_MODEL // 4
DEPTH = 1
ALPHA = (2.0 * DEPTH) ** 0.25
EPS = 1e-5
NEG = -1e30

LANES = 128
SUBLANES = 8
TS = 512
HALO = 32
CR = 128
AB = 2
PAIR_SLOTS = [j + (N_HEADS // N_KV_HEADS) * g for j in range(N_HEADS // N_KV_HEADS) for g in range(N_KV_HEADS)]
ROUTER_COLS = LANES
VMEM_LIMIT = 56 * 1024 * 1024

PAIRS = [(lo, hi) for lo in range(EXPERTS_PER_GROUP) for hi in range(lo + 1, EXPERTS_PER_GROUP)]
N_PAIRS = len(PAIRS)
N_BUCKETS = N_GROUPS * N_PAIRS
TMM = 128
CODE_SHIFT = 16


def _dot(a, b):
    return jnp.dot(a, b, preferred_element_type=F32)


def _dot_nt(a, b):
    return lax.dot_general(a, b, (((1,), (1,)), ((), ())), preferred_element_type=F32)


def _fill_group_maps(bsel_ref, bexp_ref):
    c = bsel_ref.shape[0]
    ch = lax.broadcasted_iota(jnp.int32, (c, LANES), 0) // GROUP_DIM
    gi = lax.broadcasted_iota(jnp.int32, (c, LANES), 1)
    bsel_ref[...] = jnp.where(ch == gi, 1.0 / GROUP_DIM, 0.0).astype(BF16)
    gi2 = lax.broadcasted_iota(jnp.int32, (LANES, c), 0)
    ch2 = lax.broadcasted_iota(jnp.int32, (LANES, c), 1) // GROUP_DIM
    bexp_ref[...] = jnp.where(ch2 == gi2, 1.0, 0.0).astype(BF16)


def _group_rms(y, bsel, bexp):
    ms = _dot((y * y).astype(BF16), bsel)
    r = lax.rsqrt(ms + EPS)
    r_hi = r.astype(BF16)
    r_lo = (r - r_hi.astype(F32)).astype(BF16)
    return y * (_dot(r_hi, bexp) + _dot(r_lo, bexp))


def _layer_norm(y, g, b):
    mu = jnp.mean(y, axis=-1, keepdims=True)
    d = y - mu
    var = jnp.mean(d * d, axis=-1, keepdims=True)
    return d * lax.rsqrt(var + EPS) * g + b


def _mod_kernel(c_ref, w_ref, b_ref, o_ref):
    c = c_ref[...]
    c_act = c * jax.nn.sigmoid(c)
    o_ref[...] = jnp.dot(c_act, w_ref[...], preferred_element_type=F32,
                         precision=lax.Precision.HIGHEST) + b_ref[...]


def _modulation(c, w_ada, b_ada):
    bsz = c.shape[0]
    n = w_ada.shape[1]
    return pl.pallas_call(
        _mod_kernel,
        grid=(n // D_MODEL,),
        in_specs=[pl.BlockSpec((bsz, D_MODEL), lambda j: (0, 0)),
                  pl.BlockSpec((D_MODEL, D_MODEL), lambda j: (0, j)),
                  pl.BlockSpec((1, D_MODEL), lambda j: (0, j))],
        out_specs=pl.BlockSpec((bsz, D_MODEL), lambda j: (0, j)),
        out_shape=jax.ShapeDtypeStruct((bsz, n), F32),
        name="adaln_mod",
    )(c, w_ada, b_ada)


def _inproj_conv_kernel(x_ref, sh_ref, sc_ref, w_ref, b_ref, wq_ref, bq_ref, cw_ref, cb_ref, lg_ref, lb_ref, og_ref,
                        yc_ref, q_ref, kv_ref, wbf, gs, cacc, wb, bsel, bexp):
    first = (pl.program_id(0) == 0) & (pl.program_id(1) == 0)
    s = pl.program_id(1)

    @pl.when(first)
    def _():
        wbf[...] = w_ref[...].astype(BF16)
        wbf[:, 2 * D_CONV:2 * D_CONV + D_ATTN] = wq_ref[...].astype(BF16)
        wb[...] = jnp.broadcast_to(cw_ref[...][:, None, :], wb.shape)
        _fill_group_maps(bsel, bexp)

    n_ct = D_CONV // LANES

    @pl.when(s == 0)
    def _():
        gs[0, :, 0:HALO, :] = jnp.zeros((n_ct, HALO, LANES), F32)

    @pl.when(s > 0)
    def _():
        gs[0, :, 0:HALO, :] = gs[0, :, TS:TS + HALO, :]

    h = (x_ref[...] * (1.0 + sc_ref[...]) + sh_ref[...]).astype(BF16)
    u_a = _dot(h, wbf[:, 0:D_CONV]) + b_ref[:, 0:D_CONV]
    u_b = _dot(h, wbf[:, D_CONV:2 * D_CONV]) + b_ref[:, D_CONV:2 * D_CONV]
    glu = u_a * jax.nn.sigmoid(u_b)
    for c in range(n_ct):
        gs[0, c, HALO:HALO + TS, :] = glu[:, c * LANES:(c + 1) * LANES]
    q0 = 2 * D_CONV
    q = _dot(h, wbf[:, q0:q0 + D_ATTN]) + bq_ref[...]
    q_ref[...] = (q * (HEAD_DIM ** -0.5)).astype(BF16)
    k0 = q0 + D_ATTN
    kv_ref[...] = (_dot(h, wbf[:, k0:k0 + 2 * D_KV]) + b_ref[:, k0:k0 + 2 * D_KV]).astype(BF16)

    n_sh = TS + HALO - SUBLANES
    for j in range(1, SUBLANES):
        gs[j, :, 0:n_sh, :] = gs[0, :, j:j + n_sh, :]

    cb = cb_ref[...]
    lg = lg_ref[...]
    lb = lb_ref[...]
    og = og_ref[...]

    for c in range(n_ct):
        ls = slice(c * LANES, (c + 1) * LANES)

        def conv_rows(i, carry, c=c, ls=ls):
            r0 = pl.multiple_of(i * CR, CR)
            acc = jnp.broadcast_to(cb[:, ls], (CR, LANES)).reshape(CR // SUBLANES, SUBLANES, LANES)
            for k in range(CONV_WIDTH):
                off = HALO - (CONV_WIDTH - 1) + k
                seg = gs[off % SUBLANES, c, pl.ds(r0 + (off // SUBLANES) * SUBLANES, CR), :]
                acc = acc + wb[k, :, ls][None] * seg.reshape(CR // SUBLANES, SUBLANES, LANES)
            cacc[c, pl.ds(r0, CR), :] = acc.reshape(CR, LANES)
            return carry

        lax.fori_loop(0, TS // CR, conv_rows, 0)
    y = _layer_norm(jnp.concatenate([cacc[c] for c in range(n_ct)], axis=1), lg, lb)
    y = y * jax.nn.sigmoid(y)
    yc_ref[...] = (_group_rms(y, bsel[...], bexp[...]) * og).astype(BF16)


def _inproj_conv(x, sh1, sc1, w_in, b_in, w_q, b_q, conv_w, conv_b, ln_g, ln_b, out_g):
    bsz, seq, _ = x.shape
    vec = lambda n: pl.BlockSpec((1, n), lambda b, s: (0, 0))
    mod = pl.BlockSpec((None, None, 1, D_MODEL), lambda b, s: (b, 0, 0, 0))
    tile = lambda n: pl.BlockSpec((None, TS, n), lambda b, s: (b, s, 0))
    return pl.pallas_call(
        _inproj_conv_kernel,
        grid=(bsz, seq // TS),
        in_specs=[tile(D_MODEL), mod, mod,
                  pl.BlockSpec((D_MODEL, D_IN), lambda b, s: (0, 0)), vec(D_IN),
                  pl.BlockSpec((D_MODEL, D_ATTN), lambda b, s: (0, 0)), vec(D_ATTN),
                  pl.BlockSpec((CONV_WIDTH, D_CONV), lambda b, s: (0, 0)),
                  vec(D_CONV), vec(D_CONV), vec(D_CONV), vec(D_CONV)],
        out_specs=[tile(D_CONV), tile(D_ATTN), tile(2 * D_KV)],
        out_shape=[jax.ShapeDtypeStruct((bsz, seq, D_CONV), BF16),
                   jax.ShapeDtypeStruct((bsz, seq, D_ATTN), BF16),
                   jax.ShapeDtypeStruct((bsz, seq, 2 * D_KV), BF16)],
        scratch_shapes=[pltpu.VMEM((D_MODEL, D_IN), BF16),
                        pltpu.VMEM((SUBLANES, D_CONV // LANES, TS + HALO, LANES), F32),
                        pltpu.VMEM((D_CONV // LANES, TS, LANES), F32),
                        pltpu.VMEM((CONV_WIDTH, SUBLANES, D_CONV), F32),
                        pltpu.VMEM((D_CONV, LANES), BF16),
                        pltpu.VMEM((LANES, D_CONV), BF16)],
        compiler_params=pltpu.CompilerParams(
            dimension_semantics=("arbitrary", "arbitrary"), vmem_limit_bytes=VMEM_LIMIT),
        name="inproj_conv",
    )(x, sh1, sc1, w_in, b_in, w_q, b_q, conv_w, conv_b, ln_g, ln_b, out_g)


def _attn_kernel(sinks_ref, q_ref, kvc_ref, kvp_ref, og_ref, o_ref, bsel, bexp):
    n = pl.program_id(1)

    @pl.when((pl.program_id(0) == 0) & (n == 0))
    def _():
        _fill_group_maps(bsel, bexp)

    qr = lax.broadcasted_iota(jnp.int32, (BLOCK, 2 * BLOCK), 0)
    kc = lax.broadcasted_iota(jnp.int32, (BLOCK, 2 * BLOCK), 1)
    band = (kc > qr) & (kc <= qr + BLOCK)
    lane = lax.broadcasted_iota(jnp.int32, (BLOCK, LANES), 1)
    lower = lane < HEAD_DIM
    lower_bf = jnp.where(lower, 1.0, 0.0).astype(BF16)
    upper_bf = jnp.where(lower, 0.0, 1.0).astype(BF16)
    og = og_ref[...]
    for blk in range(AB):
        r0 = blk * BLOCK
        if blk == 0:
            kv = jnp.concatenate([kvp_ref[...], kvc_ref[0:BLOCK, :]], axis=0)
            mask = band & ((n > 0) | (kc >= BLOCK))
        else:
            kv = kvc_ref[r0 - BLOCK:r0 + BLOCK, :]
            mask = band
        keys = kv[:, 0:D_KV]
        vals = kv[:, D_KV:2 * D_KV]
        parts = []
        for j in range(D_ATTN // LANES):
            col = q_ref[r0:r0 + BLOCK, j * LANES:(j + 1) * LANES]
            parts += [col * lower_bf, col * upper_bf]
        s_all = _dot_nt(jnp.concatenate(parts, axis=0), keys)
        probs = []
        den = jnp.ones((BLOCK, LANES), F32)
        for slot in range(N_HEADS):
            sc = jnp.where(mask, s_all[slot * BLOCK:(slot + 1) * BLOCK], NEG)
            sink = sinks_ref[PAIR_SLOTS[slot]]
            m = jnp.maximum(jnp.max(sc, axis=-1, keepdims=True), sink)
            p = jnp.exp(sc - m)
            den = jnp.where(lane == slot, jnp.sum(p, axis=-1, keepdims=True) + jnp.exp(sink - m), den)
            probs.append(p.astype(BF16))
        pv = _dot(jnp.concatenate(probs, axis=0), vals)
        o = jnp.concatenate(
            [jnp.where(lower, pv[(2 * j) * BLOCK:(2 * j + 1) * BLOCK], pv[(2 * j + 1) * BLOCK:(2 * j + 2) * BLOCK])
             for j in range(D_ATTN // LANES)], axis=1)
        ms = _dot((o * o).astype(BF16), bsel[...])
        r = lax.rsqrt(ms + EPS * den * den)
        r_hi = r.astype(BF16)
        r_lo = (r - r_hi.astype(F32)).astype(BF16)
        scale = _dot(r_hi, bexp[...]) + _dot(r_lo, bexp[...])
        o_ref[r0:r0 + BLOCK, :] = (o * scale * og).astype(BF16)


def _attention(q, kv, sinks, out_g):
    bsz, seq, _ = q.shape
    rows = AB * BLOCK
    grid_spec = pltpu.PrefetchScalarGridSpec(
        num_scalar_prefetch=1,
        grid=(bsz, seq // rows),
        in_specs=[pl.BlockSpec((None, rows, D_ATTN), lambda b, n, sk: (b, n, 0)),
                  pl.BlockSpec((None, rows, 2 * D_KV), lambda b, n, sk: (b, n, 0)),
                  pl.BlockSpec((None, BLOCK, 2 * D_KV), lambda b, n, sk: (b, jnp.maximum(AB * n - 1, 0), 0)),
                  pl.BlockSpec((1, D_ATTN), lambda b, n, sk: (0, 0))],
        out_specs=pl.BlockSpec((None, rows, D_ATTN), lambda b, n, sk: (b, n, 0)),
        scratch_shapes=[pltpu.VMEM((D_ATTN, LANES), BF16), pltpu.VMEM((LANES, D_ATTN), BF16)],
    )
    return pl.pallas_call(
        _attn_kernel,
        grid_spec=grid_spec,
        out_shape=jax.ShapeDtypeStruct((bsz, seq, D_ATTN), BF16),
        compiler_params=pltpu.CompilerParams(dimension_semantics=("arbitrary", "arbitrary")),
        name="swa_attention",
    )(sinks, q, kv, kv, out_g)


def _group_softmax_top(logits, lane):
    is_g = (lane >= N_EXPERTS) & (lane < N_EXPERTS + N_GROUPS)
    gl = jnp.where(is_g, logits, NEG)
    gmax = jnp.max(gl, axis=-1, keepdims=True)
    p_top = 1.0 / jnp.sum(jnp.where(is_g, jnp.exp(gl - gmax), 0.0), axis=-1, keepdims=True)
    return is_g, gl, gmax, p_top


def _route_bucket(logits):
    rows = logits.shape[0]
    lane = lax.broadcasted_iota(jnp.int32, (rows, ROUTER_COLS), 1)
    lanef = lane.astype(F32)
    is_g, gl, gmax, _ = _group_softmax_top(logits, lane)
    gidx = jnp.min(jnp.where(is_g & (gl == gmax), lanef - N_EXPERTS, 99.0), axis=-1, keepdims=True)
    in_grp = (lane < N_EXPERTS) & ((lane // EXPERTS_PER_GROUP) == gidx.astype(jnp.int32))
    el = jnp.where(in_grp, logits, NEG)
    m1 = jnp.max(el, axis=-1, keepdims=True)
    i1 = jnp.min(jnp.where(in_grp & (el == m1), lanef, 999.0), axis=-1, keepdims=True)
    rest = in_grp & (lanef != i1)
    el2 = jnp.where(rest, logits, NEG)
    m2 = jnp.max(el2, axis=-1, keepdims=True)
    i2 = jnp.min(jnp.where(rest & (el2 == m2), lanef, 999.0), axis=-1, keepdims=True)
    lo = jnp.minimum(i1, i2) - EXPERTS_PER_GROUP * gidx
    hi = jnp.maximum(i1, i2) - EXPERTS_PER_GROUP * gidx
    pair = lo * (EXPERTS_PER_GROUP - 1) - lo * (lo - 1.0) * 0.5 + (hi - lo - 1.0)
    return gidx * N_PAIRS + pair, lanef


def _outproj_kernel(x_ref, yc_ref, ya_ref, w_ref, b_ref, g1_ref, lg_ref, lb_ref, sc2_ref, sh2_ref,
                    wr_ref, br_ref, x1_ref, h2t_ref, code_ref, counts_ref, wbf, wrbf, ltri, running):
    @pl.when((pl.program_id(0) == 0) & (pl.program_id(1) == 0))
    def _():
        wbf[...] = w_ref[...].astype(BF16)
        wrbf[...] = wr_ref[...].astype(BF16)
        r = lax.broadcasted_iota(jnp.int32, (TS, TS), 0)
        c = lax.broadcasted_iota(jnp.int32, (TS, TS), 1)
        ltri[...] = jnp.where(c < r, 1.0, 0.0).astype(BF16)
        running[...] = jnp.zeros_like(running)

    mix = _dot(yc_ref[...], wbf[0:D_CONV, :]) + _dot(ya_ref[...], wbf[D_CONV:, :]) + b_ref[...]
    x1 = _layer_norm(ALPHA * x_ref[...] + g1_ref[...] * mix, lg_ref[...], lb_ref[...])
    x1_ref[...] = x1
    h2 = x1 * (1.0 + sc2_ref[...]) + sh2_ref[...]
    _to_tiles(h2t_ref, h2)
    h2 = h2.astype(BF16)
    bucket, lanef = _route_bucket(_dot(h2, wrbf[...]) + br_ref[...])
    mine = lanef == bucket
    onehot = jnp.where(mine, 1.0, 0.0)
    earlier = _dot(ltri[...], onehot.astype(BF16)) + running[...]
    rank = jnp.sum(jnp.where(mine, earlier, 0.0), axis=-1, keepdims=True)
    running[...] += jnp.sum(onehot, axis=0, keepdims=True)
    code = bucket.astype(jnp.int32) * (1 << CODE_SHIFT) + rank.astype(jnp.int32)
    code_ref[...] = jnp.broadcast_to(code, code_ref.shape)
    counts_ref[...] = running[...]


def _outproj_route(x, y_conv, y_attn, w_out, b_out, g1, ln_g, ln_b, sc2, sh2, w_r, b_r):
    bsz, seq, _ = x.shape
    nst = seq // TS
    vec = lambda n: pl.BlockSpec((1, n), lambda b, s: (0, 0))
    mod = pl.BlockSpec((None, None, 1, D_MODEL), lambda b, s: (b, 0, 0, 0))
    tile = lambda n: pl.BlockSpec((None, TS, n), lambda b, s: (b, s, 0))
    return pl.pallas_call(
        _outproj_kernel,
        grid=(bsz, seq // TS),
        in_specs=[tile(D_MODEL), tile(D_CONV), tile(D_ATTN),
                  pl.BlockSpec((D_MODEL, D_MODEL), lambda b, s: (0, 0)), vec(D_MODEL),
                  mod, vec(D_MODEL), vec(D_MODEL), mod, mod,
                  pl.BlockSpec((D_MODEL, ROUTER_COLS), lambda b, s: (0, 0)), vec(ROUTER_COLS)],
        out_specs=[tile(D_MODEL), pl.BlockSpec((TS * CHUNKS, LANES), lambda b, s: (b * nst + s, 0)),
                   tile(LANES), vec(LANES)],
        out_shape=[jax.ShapeDtypeStruct((bsz, seq, D_MODEL), F32),
                   jax.ShapeDtypeStruct((bsz * seq * CHUNKS, LANES), F32),
                   jax.ShapeDtypeStruct((bsz, seq, LANES), jnp.int32),
                   jax.ShapeDtypeStruct((1, LANES), F32)],
        scratch_shapes=[pltpu.VMEM((D_MODEL, D_MODEL), BF16), pltpu.VMEM((D_MODEL, ROUTER_COLS), BF16),
                        pltpu.VMEM((TS, TS), BF16), pltpu.VMEM((1, LANES), F32)],
        compiler_params=pltpu.CompilerParams(
            dimension_semantics=("arbitrary", "arbitrary"), vmem_limit_bytes=VMEM_LIMIT),
        name="outproj_route",
    )(x, y_conv, y_attn, w_out, b_out, g1, ln_g, ln_b, sc2, sh2, w_r, b_r)


def _row_of(starts_ref, code):
    return (starts_ref[lax.shift_right_logical(code, jnp.int32(CODE_SHIFT))]
            + (code & ((1 << CODE_SHIFT) - 1)))


CHUNKS = D_MODEL // LANES
assert CHUNKS == SUBLANES


def _tok_rows(p):
    return pl.ds(pl.multiple_of(p * CHUNKS, CHUNKS), CHUNKS)


def _to_tiles(ref, x):
    for c in range(CHUNKS):
        ref[pl.ds(c, x.shape[0], stride=CHUNKS), :] = x[:, c * LANES:(c + 1) * LANES]


def _from_tiles(ref, n):
    return jnp.concatenate([ref[pl.ds(c, n, stride=CHUNKS), :] for c in range(CHUNKS)], axis=1)


def _wait_tokens(hbm_ref, n, sem):
    pltpu.make_async_copy(hbm_ref.at[pl.ds(0, n * CHUNKS)], hbm_ref.at[pl.ds(0, n * CHUNKS)], sem).wait()


def _scatter_kernel(starts_ref, zrow_ref, nt_ref, code_ref, h2t_ref, hs_ref, pos_ref, hbuf, zeros, sems, zsem):
    nsteps = pl.num_programs(0) * pl.num_programs(1)
    step = pl.program_id(0) * pl.num_programs(1) + pl.program_id(1)
    slot = step % 2

    def _tile_rows(tok):
        return pl.ds(pl.multiple_of(tok * CHUNKS, TMM * CHUNKS), TMM * CHUNKS)

    def zero_copy(b):
        return pltpu.make_async_copy(zeros, hs_ref.at[_tile_rows(zrow_ref[b])], zsem)

    def tail_copy(t):
        return pltpu.make_async_copy(zeros, hs_ref.at[_tile_rows(t * TMM)], zsem)

    @pl.when(step == 0)
    def _():
        zeros[...] = jnp.zeros_like(zeros)
        n_tiles = hs_ref.shape[0] // (TMM * CHUNKS)

        def start(b, c):
            @pl.when(zrow_ref[b] >= 0)
            def _():
                zero_copy(b).start()
            return c

        def wait(b, c):
            @pl.when(zrow_ref[b] >= 0)
            def _():
                zero_copy(b).wait()
            return c

        def start_tail(t, c):
            tail_copy(t).start()
            return c

        def wait_tail(t, c):
            tail_copy(t).wait()
            return c

        lax.fori_loop(0, N_BUCKETS, start, 0)
        lax.fori_loop(nt_ref[0], n_tiles, start_tail, 0)
        lax.fori_loop(0, N_BUCKETS, wait, 0)
        lax.fori_loop(nt_ref[0], n_tiles, wait_tail, 0)

    @pl.when(step >= 2)
    def _():
        _wait_tokens(hs_ref, TS, sems.at[slot])

    hbuf[slot] = h2t_ref[...]

    def issue(i, c):
        for u in range(SUBLANES):
            r = i * SUBLANES + u
            pos = _row_of(starts_ref, code_ref[0, r])
            pos_ref[0, r] = pos
            pltpu.make_async_copy(hbuf.at[slot, _tok_rows(r)], hs_ref.at[_tok_rows(pos)],
                                  sems.at[slot]).start(priority=u % 2)
        return c

    lax.fori_loop(0, TS // SUBLANES, issue, 0)

    @pl.when(step == nsteps - 1)
    def _():
        _wait_tokens(hs_ref, TS, sems.at[slot])
        _wait_tokens(hs_ref, TS, sems.at[1 - slot])


def _scatter_rows(starts, zrow, nt, code, h2t, n_rows):
    bsz, seq = code.shape
    nst = seq // TS
    grid_spec = pltpu.PrefetchScalarGridSpec(
        num_scalar_prefetch=3,
        grid=(bsz, nst),
        in_specs=[pl.BlockSpec((None, 1, TS), lambda b, s, *_: (b * nst + s, 0, 0), memory_space=pltpu.SMEM),
                  pl.BlockSpec((TS * CHUNKS, LANES), lambda b, s, *_: (b * nst + s, 0))],
        out_specs=[pl.BlockSpec(memory_space=pl.ANY),
                   pl.BlockSpec((None, 1, TS), lambda b, s, *_: (b * nst + s, 0, 0), memory_space=pltpu.SMEM)],
        scratch_shapes=[pltpu.VMEM((2, TS * CHUNKS, LANES), F32), pltpu.VMEM((TMM * CHUNKS, LANES), F32),
                        pltpu.SemaphoreType.DMA((2,)), pltpu.SemaphoreType.DMA],
    )
    return pl.pallas_call(
        _scatter_kernel,
        grid_spec=grid_spec,
        out_shape=[jax.ShapeDtypeStruct((n_rows * CHUNKS, LANES), F32),
                   jax.ShapeDtypeStruct((bsz * nst, 1, TS), jnp.int32)],
        compiler_params=pltpu.CompilerParams(dimension_semantics=("arbitrary", "arbitrary")),
        name="moe_scatter",
    )(starts, zrow, nt, code.reshape(bsz * nst, 1, TS), h2t)


def _moe_kernel(tlo_ref, thi_ref, blk_ref, nt_ref, h_ref, wg_hbm, wu_hbm, wd_hbm, wr_ref, br_ref, y_ref,
                wrbf, wg, wu, wd, sg, su, sd, sems):
    i = pl.program_id(0)
    group = lax.shift_right_logical(tlo_ref[i], jnp.int32(3))
    prev_group = lax.shift_right_logical(tlo_ref[jnp.maximum(i - 1, 0)], jnp.int32(3))

    @pl.when(i == 0)
    def _():
        wrbf[...] = wr_ref[...].astype(BF16)

    @pl.when((i == 0) | (group != prev_group))
    def _():
        def copies(e, slot):
            ex = group * EXPERTS_PER_GROUP + e
            return (pltpu.make_async_copy(wg_hbm.at[ex], sg.at[slot], sems.at[slot, 0]),
                    pltpu.make_async_copy(wu_hbm.at[ex], su.at[slot], sems.at[slot, 1]),
                    pltpu.make_async_copy(wd_hbm.at[ex], sd.at[slot], sems.at[slot, 2]))

        for cp in copies(0, 0):
            cp.start()

        def land(e, c):
            slot = e % 2

            @pl.when(e + 1 < EXPERTS_PER_GROUP)
            def _():
                for cp in copies(e + 1, 1 - slot):
                    cp.start()

            for cp in copies(e, slot):
                cp.wait()
            wg[e] = sg[slot].astype(BF16)
            wu[e] = su[slot].astype(BF16)
            wd[e] = sd[slot].astype(BF16)
            return c

        lax.fori_loop(0, EXPERTS_PER_GROUP, land, 0)

    @pl.when(i < nt_ref[0])
    def _():
        h = _from_tiles(h_ref, TMM).astype(BF16)
        logits = _dot(h, wrbf[...]) + br_ref[...]
        lane = lax.broadcasted_iota(jnp.int32, logits.shape, 1)
        p_top = _group_softmax_top(logits, lane)[3]
        l_lo = jnp.sum(jnp.where(lane == tlo_ref[i], logits, 0.0), axis=-1, keepdims=True)
        l_hi = jnp.sum(jnp.where(lane == thi_ref[i], logits, 0.0), axis=-1, keepdims=True)
        w_lo = p_top / (1.0 + jnp.exp(l_hi - l_lo))
        w_hi = p_top / (1.0 + jnp.exp(l_lo - l_hi))

        def ffn(e):
            a = _dot(h, wg[e])
            a = a * jax.nn.sigmoid(a) * _dot(h, wu[e])
            return _dot(a.astype(BF16), wd[e])

        local = EXPERTS_PER_GROUP - 1
        _to_tiles(y_ref, w_lo * ffn(tlo_ref[i] & local) + w_hi * ffn(thi_ref[i] & local))

    @pl.when(i >= nt_ref[0])
    def _():
        y_ref[...] = jnp.zeros_like(y_ref)


def _moe_sorted(tlo, thi, blk, nt, h_sorted, w_gate, w_up, w_down, w_r, b_r):
    n_rows = h_sorted.shape[0] // CHUNKS
    rows = pl.BlockSpec((TMM * CHUNKS, LANES), lambda i, tlo, thi, blk, nt: (blk[i], 0))
    hbm = pl.BlockSpec(memory_space=pl.ANY)
    w_in_shape, w_out_shape = (D_MODEL, D_EXPERT), (D_EXPERT, D_MODEL)
    grid_spec = pltpu.PrefetchScalarGridSpec(
        num_scalar_prefetch=4,
        grid=(n_rows // TMM,),
        in_specs=[rows, hbm, hbm, hbm,
                  pl.BlockSpec((D_MODEL, ROUTER_COLS), lambda i, *_: (0, 0)),
                  pl.BlockSpec((1, ROUTER_COLS), lambda i, *_: (0, 0))],
        out_specs=pl.BlockSpec((TMM * CHUNKS, LANES), lambda i, *_: (i, 0)),
        scratch_shapes=[pltpu.VMEM((D_MODEL, ROUTER_COLS), BF16),
                        pltpu.VMEM((EXPERTS_PER_GROUP,) + w_in_shape, BF16),
                        pltpu.VMEM((EXPERTS_PER_GROUP,) + w_in_shape, BF16),
                        pltpu.VMEM((EXPERTS_PER_GROUP,) + w_out_shape, BF16),
                        pltpu.VMEM((2,) + w_in_shape, F32), pltpu.VMEM((2,) + w_in_shape, F32),
                        pltpu.VMEM((2,) + w_out_shape, F32), pltpu.SemaphoreType.DMA((2, 3))],
    )
    return pl.pallas_call(
        _moe_kernel,
        grid_spec=grid_spec,
        out_shape=jax.ShapeDtypeStruct((n_rows * CHUNKS, LANES), F32),
        compiler_params=pltpu.CompilerParams(dimension_semantics=("arbitrary",), vmem_limit_bytes=VMEM_LIMIT),
        name="moe_sorted",
    )(tlo, thi, blk, nt, h_sorted, w_gate, w_up, w_down, w_r, b_r)


def _final_kernel(pos_ref, posn_ref, x1_ref, g2_ref, lg_ref, lb_ref, ys_ref, o_ref, ybuf, sems):
    nsteps = pl.num_programs(0) * pl.num_programs(1)
    step = pl.program_id(0) * pl.num_programs(1) + pl.program_id(1)
    slot = step % 2

    def gather(rows_ref, sl):
        def issue(i, c):
            for u in range(SUBLANES):
                r = i * SUBLANES + u
                pltpu.make_async_copy(ys_ref.at[_tok_rows(rows_ref[0, r])], ybuf.at[sl, _tok_rows(r)],
                                      sems.at[sl]).start(priority=u % 2)
            return c

        lax.fori_loop(0, TS // SUBLANES, issue, 0)

    @pl.when(step == 0)
    def _():
        gather(pos_ref, slot)

    @pl.when(step + 1 < nsteps)
    def _():
        gather(posn_ref, 1 - slot)

    _wait_tokens(ys_ref, TS, sems.at[slot])
    y = _from_tiles(ybuf.at[slot], TS)
    o_ref[...] = _layer_norm(ALPHA * x1_ref[...] + g2_ref[...] * y, lg_ref[...], lb_ref[...])


def _unsort_ln2(pos, x1, g2, ln_g, ln_b, y_sorted):
    bsz, seq, _ = x1.shape
    nst = seq // TS
    last = bsz * nst - 1
    return pl.pallas_call(
        _final_kernel,
        grid=(bsz, nst),
        in_specs=[pl.BlockSpec((None, 1, TS), lambda b, s: (b * nst + s, 0, 0), memory_space=pltpu.SMEM),
                  pl.BlockSpec((None, 1, TS), lambda b, s: (jnp.minimum(b * nst + s + 1, last), 0, 0),
                               memory_space=pltpu.SMEM),
                  pl.BlockSpec((None, TS, D_MODEL), lambda b, s: (b, s, 0)),
                  pl.BlockSpec((None, None, 1, D_MODEL), lambda b, s: (b, 0, 0, 0)),
                  pl.BlockSpec((1, D_MODEL), lambda b, s: (0, 0)),
                  pl.BlockSpec((1, D_MODEL), lambda b, s: (0, 0)),
                  pl.BlockSpec(memory_space=pl.ANY)],
        out_specs=pl.BlockSpec((None, TS, D_MODEL), lambda b, s: (b, s, 0)),
        out_shape=jax.ShapeDtypeStruct((bsz, seq, D_MODEL), F32),
        scratch_shapes=[pltpu.VMEM((2, TS * CHUNKS, LANES), F32), pltpu.SemaphoreType.DMA((2,))],
        compiler_params=pltpu.CompilerParams(dimension_semantics=("arbitrary", "arbitrary")),
        name="unsort_ln2",
    )(pos, pos, x1, g2, ln_g, ln_b, y_sorted)


def _pair_layout(a, axis):
    shp = a.shape
    a = a.reshape(shp[:axis] + (N_KV_HEADS, N_HEADS // N_KV_HEADS, HEAD_DIM) + shp[axis + 1:])
    return jnp.swapaxes(a, axis, axis + 1).reshape(shp)


def _tile_tables(counts, n_tiles_max):
    n_tile_b = jnp.floor((counts + (TMM - 1.0)) * (1.0 / TMM))
    b_idx = jnp.arange(N_BUCKETS, dtype=jnp.int32)
    tile_end = jnp.sum(jnp.where(b_idx[None, :] <= b_idx[:, None], n_tile_b[None, :], 0.0), axis=1)
    total = jnp.sum(n_tile_b)
    starts = ((tile_end - n_tile_b) * TMM).astype(jnp.int32)
    zrow = jnp.where(n_tile_b > 0, (tile_end - 1.0) * TMM, -1.0).astype(jnp.int32)
    blk = jnp.minimum(jnp.arange(n_tiles_max, dtype=jnp.int32), total.astype(jnp.int32) - 1)
    bucket = jnp.sum(jnp.where(tile_end[None, :] <= blk[:, None].astype(F32), 1, 0), axis=1)
    onehot = bucket[:, None] == b_idx[None, :]
    e_lo = jnp.asarray([(b // N_PAIRS) * EXPERTS_PER_GROUP + PAIRS[b % N_PAIRS][0] for b in range(N_BUCKETS)], jnp.int32)
    e_hi = jnp.asarray([(b // N_PAIRS) * EXPERTS_PER_GROUP + PAIRS[b % N_PAIRS][1] for b in range(N_BUCKETS)], jnp.int32)
    tlo = jnp.sum(jnp.where(onehot, e_lo[None, :], 0), axis=1)
    thi = jnp.sum(jnp.where(onehot, e_hi[None, :], 0), axis=1)
    pad = jnp.zeros((LANES - N_BUCKETS,), jnp.int32)
    return (jnp.concatenate([starts, pad]), jnp.concatenate([zrow, pad - 1]), tlo, thi, blk,
            total.astype(jnp.int32).reshape(1))


def kernel(x, c, w_ada, b_ada, w_in, b_in, conv_w, conv_b, conv_ln_g, conv_ln_b, conv_out_g, sinks,
           attn_out_g, w_out, b_out, ln1_g, ln1_b, w_router_group, b_router_group, w_router_expert,
           b_router_expert, w_gate, w_up, w_down, ln2_g, ln2_b):
    assert w_ada.shape[0] == DEPTH
    bsz, seq, _ = x.shape
    row = lambda v: v[0][None, :]

    mod = _modulation(c, w_ada[0], row(b_ada)).reshape(bsz, 6, 1, D_MODEL)
    sh1, sc1, g1, sh2, sc2, g2 = [mod[:, j:j + 1] for j in range(6)]

    q0 = 2 * D_CONV
    w_q = _pair_layout(w_in[0][:, q0:q0 + D_ATTN], 1)
    b_q = _pair_layout(b_in[0][q0:q0 + D_ATTN], 0)[None, :]
    og_attn = _pair_layout(attn_out_g[0], 0)[None, :]
    w_o = jnp.concatenate([w_out[0][:D_CONV], _pair_layout(w_out[0][D_CONV:], 0)], axis=0)

    y_conv, q, kv = _inproj_conv(x, sh1, sc1, w_in[0], row(b_in), w_q, b_q, conv_w[0], row(conv_b),
                                 row(conv_ln_g), row(conv_ln_b), row(conv_out_g))
    y_attn = _attention(q, kv, sinks[0], og_attn)

    pad = ROUTER_COLS - N_EXPERTS - N_GROUPS
    w_r = jnp.concatenate([w_router_expert[0], w_router_group[0], jnp.zeros((D_MODEL, pad), F32)], axis=1)
    b_r = jnp.concatenate([b_router_expert[0], b_router_group[0], jnp.zeros((pad,), F32)])[None, :]
    x1, h2t, code, counts = _outproj_route(x, y_conv, y_attn, w_o, row(b_out), g1, row(ln1_g), row(ln1_b),
                                      sc2, sh2, w_r, b_r)

    n_tiles_max = (bsz * seq) // TMM + N_BUCKETS
    starts, zrow, tlo, thi, blk, nt = _tile_tables(counts[0, :N_BUCKETS], n_tiles_max)
    h_sorted, pos = _scatter_rows(starts, zrow, nt, code[:, :, 0], h2t, n_tiles_max * TMM)
    y_sorted = _moe_sorted(tlo, thi, blk, nt, h_sorted, w_gate[0], w_up[0], w_down[0], w_r, b_r)
    return _unsort_ln2(pos, x1, g2, row(ln2_g), row(ln2_b), y_sorted)
```

```python
import jax
import jax.numpy as jnp
from jax import lax
from jax.experimental import pallas as pl
from jax.experimental.pallas import tpu as pltpu

F32 = jnp.float32
BF16 = jnp.bfloat16

D_MODEL = 1024
D_CONV = 512
CONV_WIDTH = 31
GROUP_DIM = 64
N_HEADS = 8
N_KV_HEADS = 2
HEAD_DIM = 64
D_ATTN = N_HEADS * HEAD_DIM
D_KV = N_KV_HEADS * HEAD_DIM
BLOCK = 128
D_IN = 2 * D_CONV + D_ATTN + 2 * D_KV
N_GROUPS = 4
EXPERTS_PER_GROUP = 8
N_EXPERTS = N_GROUPS * EXPERTS_PER_GROUP
D_EXPERT = D_MODEL // 4
DEPTH = 1
ALPHA = (2.0 * DEPTH) ** 0.25
EPS = 1e-5
NEG = -1e30

LANES = 128
SUBLANES = 8
TS = 512
HALO = 32
CR = 64
AB = 4
PAIR_SLOTS = [j + (N_HEADS // N_KV_HEADS) * g for j in range(N_HEADS // N_KV_HEADS) for g in range(N_KV_HEADS)]
ROUTER_COLS = LANES
VMEM_LIMIT = 56 * 1024 * 1024

PAIRS = [(lo, hi) for lo in range(EXPERTS_PER_GROUP) for hi in range(lo + 1, EXPERTS_PER_GROUP)]
N_PAIRS = len(PAIRS)
N_BUCKETS = N_GROUPS * N_PAIRS
TMM = 128
CODE_SHIFT = 16


def _dot(a, b):
    return jnp.dot(a, b, preferred_element_type=F32)


def _dot_nt(a, b):
    return lax.dot_general(a, b, (((1,), (1,)), ((), ())), preferred_element_type=F32)


def _fill_group_maps(bsel_ref, bexp_ref):
    c = bsel_ref.shape[0]
    ch = lax.broadcasted_iota(jnp.int32, (c, LANES), 0) // GROUP_DIM
    gi = lax.broadcasted_iota(jnp.int32, (c, LANES), 1)
    bsel_ref[...] = jnp.where(ch == gi, 1.0 / GROUP_DIM, 0.0).astype(BF16)
    gi2 = lax.broadcasted_iota(jnp.int32, (LANES, c), 0)
    ch2 = lax.broadcasted_iota(jnp.int32, (LANES, c), 1) // GROUP_DIM
    bexp_ref[...] = jnp.where(ch2 == gi2, 1.0, 0.0).astype(BF16)


def _group_rms(y, bsel, bexp):
    ms = _dot((y * y).astype(BF16), bsel)
    r = lax.rsqrt(ms + EPS)
    r_hi = r.astype(BF16)
    r_lo = (r - r_hi.astype(F32)).astype(BF16)
    return y * (_dot(r_hi, bexp) + _dot(r_lo, bexp))


def _layer_norm(y, g, b):
    mu = jnp.mean(y, axis=-1, keepdims=True)
    d = y - mu
    var = jnp.mean(d * d, axis=-1, keepdims=True)
    return d * lax.rsqrt(var + EPS) * g + b


def _mod_kernel(c_ref, w_ref, b_ref, o_ref):
    c = c_ref[...]
    c_act = c * jax.nn.sigmoid(c)
    o_ref[...] = jnp.dot(c_act, w_ref[...], preferred_element_type=F32,
                         precision=lax.Precision.HIGHEST) + b_ref[...]


def _modulation(c, w_ada, b_ada):
    bsz = c.shape[0]
    n = w_ada.shape[1]
    return pl.pallas_call(
        _mod_kernel,
        grid=(n // D_MODEL,),
        in_specs=[pl.BlockSpec((bsz, D_MODEL), lambda j: (0, 0)),
                  pl.BlockSpec((D_MODEL, D_MODEL), lambda j: (0, j)),
                  pl.BlockSpec((1, D_MODEL), lambda j: (0, j))],
        out_specs=pl.BlockSpec((bsz, D_MODEL), lambda j: (0, j)),
        out_shape=jax.ShapeDtypeStruct((bsz, n), F32),
        name="adaln_mod",
    )(c, w_ada, b_ada)


def _inproj_conv_kernel(x_ref, sh_ref, sc_ref, w_ref, b_ref, wq_ref, bq_ref, cw_ref, cb_ref, lg_ref, lb_ref, og_ref,
                        yc_ref, q_ref, kv_ref, wbf, gs, cacc, wb, bsel, bexp):
    first = (pl.program_id(0) == 0) & (pl.program_id(1) == 0)
    s = pl.program_id(1)

    @pl.when(first)
    def _():
        wbf[...] = w_ref[...].astype(BF16)
        wbf[:, 2 * D_CONV:2 * D_CONV + D_ATTN] = wq_ref[...].astype(BF16)
        wb[...] = jnp.broadcast_to(cw_ref[...][:, None, :], wb.shape)
        _fill_group_maps(bsel, bexp)

    n_ct = D_CONV // LANES

    @pl.when(s == 0)
    def _():
        gs[0, :, 0:HALO, :] = jnp.zeros((n_ct, HALO, LANES), F32)

    @pl.when(s > 0)
    def _():
        gs[0, :, 0:HALO, :] = gs[0, :, TS:TS + HALO, :]

    h = (x_ref[...] * (1.0 + sc_ref[...]) + sh_ref[...]).astype(BF16)
    u_a = _dot(h, wbf[:, 0:D_CONV]) + b_ref[:, 0:D_CONV]
    u_b = _dot(h, wbf[:, D_CONV:2 * D_CONV]) + b_ref[:, D_CONV:2 * D_CONV]
    glu = u_a * jax.nn.sigmoid(u_b)
    for c in range(n_ct):
        gs[0, c, HALO:HALO + TS, :] = glu[:, c * LANES:(c + 1) * LANES]
    q0 = 2 * D_CONV
    k0 = q0 + D_ATTN

    n_sh = TS + HALO - SUBLANES
    for j in range(1, SUBLANES):
        gs[j, :, 0:n_sh, :] = gs[0, :, j:j + n_sh, :]

    cb = cb_ref[...]
    lg = lg_ref[...]
    lb = lb_ref[...]
    og = og_ref[...]

    late_cols = 2 * LANES
    n_late = (D_ATTN + 2 * D_KV) // late_cols

    def late_proj(j):
        c0 = j * late_cols
        if c0 < D_ATTN:
            q = _dot(h, wbf[:, q0 + c0:q0 + c0 + late_cols]) + bq_ref[:, c0:c0 + late_cols]
            q_ref[:, c0:c0 + late_cols] = (q * (HEAD_DIM ** -0.5)).astype(BF16)
        else:
            kv_ref[...] = (_dot(h, wbf[:, k0:k0 + 2 * D_KV]) + b_ref[:, k0:k0 + 2 * D_KV]).astype(BF16)

    n_chunks = n_ct * (TS // CR)
    every = n_chunks // n_late
    for n in range(n_chunks):
        c, r0 = n // (TS // CR), (n % (TS // CR)) * CR
        ls = slice(c * LANES, (c + 1) * LANES)
        acc = jnp.broadcast_to(cb[:, ls], (CR, LANES)).reshape(CR // SUBLANES, SUBLANES, LANES)
        for k in range(CONV_WIDTH):
            off = HALO - (CONV_WIDTH - 1) + k
            a0 = r0 + (off // SUBLANES) * SUBLANES
            seg = gs[off % SUBLANES, c, a0:a0 + CR, :]
            acc = acc + wb[k, :, ls][None] * seg.reshape(CR // SUBLANES, SUBLANES, LANES)
        cacc[c, r0:r0 + CR, :] = acc.reshape(CR, LANES)
        if n % every == 0 and n // every < n_late:
            late_proj(n // every)
    y = _layer_norm(jnp.concatenate([cacc[c] for c in range(n_ct)], axis=1), lg, lb)
    y = y * jax.nn.sigmoid(y)
    yc_ref[...] = (_group_rms(y, bsel[...], bexp[...]) * og).astype(BF16)


def _inproj_conv(x, sh1, sc1, w_in, b_in, w_q, b_q, conv_w, conv_b, ln_g, ln_b, out_g):
    bsz, seq, _ = x.shape
    vec = lambda n: pl.BlockSpec((1, n), lambda b, s: (0, 0))
    mod = pl.BlockSpec((None, None, 1, D_MODEL), lambda b, s: (b, 0, 0, 0))
    tile = lambda n: pl.BlockSpec((None, TS, n), lambda b, s: (b, s, 0))
    return pl.pallas_call(
        _inproj_conv_kernel,
        grid=(bsz, seq // TS),
        in_specs=[tile(D_MODEL), mod, mod,
                  pl.BlockSpec((D_MODEL, D_IN), lambda b, s: (0, 0)), vec(D_IN),
                  pl.BlockSpec((D_MODEL, D_ATTN), lambda b, s: (0, 0)), vec(D_ATTN),
                  pl.BlockSpec((CONV_WIDTH, D_CONV), lambda b, s: (0, 0)),
                  vec(D_CONV), vec(D_CONV), vec(D_CONV), vec(D_CONV)],
        out_specs=[tile(D_CONV), tile(D_ATTN), tile(2 * D_KV)],
        out_shape=[jax.ShapeDtypeStruct((bsz, seq, D_CONV), BF16),
                   jax.ShapeDtypeStruct((bsz, seq, D_ATTN), BF16),
                   jax.ShapeDtypeStruct((bsz, seq, 2 * D_KV), BF16)],
        scratch_shapes=[pltpu.VMEM((D_MODEL, D_IN), BF16),
                        pltpu.VMEM((SUBLANES, D_CONV // LANES, TS + HALO, LANES), F32),
                        pltpu.VMEM((D_CONV // LANES, TS, LANES), F32),
                        pltpu.VMEM((CONV_WIDTH, SUBLANES, D_CONV), F32),
                        pltpu.VMEM((D_CONV, LANES), BF16),
                        pltpu.VMEM((LANES, D_CONV), BF16)],
        compiler_params=pltpu.CompilerParams(
            dimension_semantics=("arbitrary", "arbitrary"), vmem_limit_bytes=VMEM_LIMIT),
        name="inproj_conv",
    )(x, sh1, sc1, w_in, b_in, w_q, b_q, conv_w, conv_b, ln_g, ln_b, out_g)


def _attn_kernel(sinks_ref, q_ref, kvc_ref, kvp_ref, og_ref, o_ref, bsel, bexp):
    n = pl.program_id(1)

    @pl.when((pl.program_id(0) == 0) & (n == 0))
    def _():
        _fill_group_maps(bsel, bexp)

    qr = lax.broadcasted_iota(jnp.int32, (BLOCK, 2 * BLOCK), 0)
    kc = lax.broadcasted_iota(jnp.int32, (BLOCK, 2 * BLOCK), 1)
    band = (kc > qr) & (kc <= qr + BLOCK)
    lane = lax.broadcasted_iota(jnp.int32, (BLOCK, LANES), 1)
    lower = lane < HEAD_DIM
    lower_bf = jnp.where(lower, 1.0, 0.0).astype(BF16)
    upper_bf = jnp.where(lower, 0.0, 1.0).astype(BF16)
    og = og_ref[...]
    for blk in range(AB):
        r0 = blk * BLOCK
        if blk == 0:
            kv = jnp.concatenate([kvp_ref[...], kvc_ref[0:BLOCK, :]], axis=0)
            mask = band & ((n > 0) | (kc >= BLOCK))
        else:
            kv = kvc_ref[r0 - BLOCK:r0 + BLOCK, :]
            mask = band
        keys = kv[:, 0:D_KV]
        vals = kv[:, D_KV:2 * D_KV]
        parts = []
        for j in range(D_ATTN // LANES):
            col = q_ref[r0:r0 + BLOCK, j * LANES:(j + 1) * LANES]
            parts += [col * lower_bf, col * upper_bf]
        s_all = _dot_nt(jnp.concatenate(parts, axis=0), keys)
        probs = []
        den = jnp.ones((BLOCK, LANES), F32)
        for slot in range(N_HEADS):
            sc = jnp.where(mask, s_all[slot * BLOCK:(slot + 1) * BLOCK], NEG)
            sink = sinks_ref[PAIR_SLOTS[slot]]
            m = jnp.maximum(jnp.max(sc, axis=-1, keepdims=True), sink)
            p = jnp.exp(sc - m)
            den = jnp.where(lane == slot, jnp.sum(p, axis=-1, keepdims=True) + jnp.exp(sink - m), den)
            probs.append(p.astype(BF16))
        pv = _dot(jnp.concatenate(probs, axis=0), vals)
        o = jnp.concatenate(
            [jnp.where(lower, pv[(2 * j) * BLOCK:(2 * j + 1) * BLOCK], pv[(2 * j + 1) * BLOCK:(2 * j + 2) * BLOCK])
             for j in range(D_ATTN // LANES)], axis=1)
        ms = _dot((o * o).astype(BF16), bsel[...])
        r = lax.rsqrt(ms + EPS * den * den)
        r_hi = r.astype(BF16)
        r_lo = (r - r_hi.astype(F32)).astype(BF16)
        scale = _dot(r_hi, bexp[...]) + _dot(r_lo, bexp[...])
        o_ref[r0:r0 + BLOCK, :] = (o * scale * og).astype(BF16)


def _attention(q, kv, sinks, out_g):
    bsz, seq, _ = q.shape
    rows = AB * BLOCK
    grid_spec = pltpu.PrefetchScalarGridSpec(
        num_scalar_prefetch=1,
        grid=(bsz, seq // rows),
        in_specs=[pl.BlockSpec((None, rows, D_ATTN), lambda b, n, sk: (b, n, 0)),
                  pl.BlockSpec((None, rows, 2 * D_KV), lambda b, n, sk: (b, n, 0)),
                  pl.BlockSpec((None, BLOCK, 2 * D_KV), lambda b, n, sk: (b, jnp.maximum(AB * n - 1, 0), 0)),
                  pl.BlockSpec((1, D_ATTN), lambda b, n, sk: (0, 0))],
        out_specs=pl.BlockSpec((None, rows, D_ATTN), lambda b, n, sk: (b, n, 0)),
        scratch_shapes=[pltpu.VMEM((D_ATTN, LANES), BF16), pltpu.VMEM((LANES, D_ATTN), BF16)],
    )
    return pl.pallas_call(
        _attn_kernel,
        grid_spec=grid_spec,
        out_shape=jax.ShapeDtypeStruct((bsz, seq, D_ATTN), BF16),
        compiler_params=pltpu.CompilerParams(dimension_semantics=("arbitrary", "arbitrary")),
        name="swa_attention",
    )(sinks, q, kv, kv, out_g)


def _group_softmax_top(logits, lane):
    is_g = (lane >= N_EXPERTS) & (lane < N_EXPERTS + N_GROUPS)
    gl = jnp.where(is_g, logits, NEG)
    gmax = jnp.max(gl, axis=-1, keepdims=True)
    p_top = 1.0 / jnp.sum(jnp.where(is_g, jnp.exp(gl - gmax), 0.0), axis=-1, keepdims=True)
    return is_g, gl, gmax, p_top


def _route_bucket(logits):
    rows = logits.shape[0]
    lane = lax.broadcasted_iota(jnp.int32, (rows, ROUTER_COLS), 1)
    lanef = lane.astype(F32)
    is_g, gl, gmax, _ = _group_softmax_top(logits, lane)
    gidx = jnp.min(jnp.where(is_g & (gl == gmax), lanef - N_EXPERTS, 99.0), axis=-1, keepdims=True)
    in_grp = (lane < N_EXPERTS) & ((lane // EXPERTS_PER_GROUP) == gidx.astype(jnp.int32))
    el = jnp.where(in_grp, logits, NEG)
    m1 = jnp.max(el, axis=-1, keepdims=True)
    i1 = jnp.min(jnp.where(in_grp & (el == m1), lanef, 999.0), axis=-1, keepdims=True)
    rest = in_grp & (lanef != i1)
    el2 = jnp.where(rest, logits, NEG)
    m2 = jnp.max(el2, axis=-1, keepdims=True)
    i2 = jnp.min(jnp.where(rest & (el2 == m2), lanef, 999.0), axis=-1, keepdims=True)
    lo = jnp.minimum(i1, i2) - EXPERTS_PER_GROUP * gidx
    hi = jnp.maximum(i1, i2) - EXPERTS_PER_GROUP * gidx
    pair = lo * (EXPERTS_PER_GROUP - 1) - lo * (lo - 1.0) * 0.5 + (hi - lo - 1.0)
    return gidx * N_PAIRS + pair, lanef


def _outproj_kernel(x_ref, yc_ref, ya_ref, w_ref, b_ref, g1_ref, lg_ref, lb_ref, sc2_ref, sh2_ref,
                    wr_ref, br_ref, x1_ref, h2t_ref, code_ref, counts_ref, wbf, wrbf, ltri, running):
    @pl.when((pl.program_id(0) == 0) & (pl.program_id(1) == 0))
    def _():
        wbf[...] = w_ref[...].astype(BF16)
        wrbf[...] = wr_ref[...].astype(BF16)
        r = lax.broadcasted_iota(jnp.int32, (TS, TS), 0)
        c = lax.broadcasted_iota(jnp.int32, (TS, TS), 1)
        ltri[...] = jnp.where(c < r, 1.0, 0.0).astype(BF16)
        running[...] = jnp.zeros_like(running)

    mix = _dot(yc_ref[...], wbf[0:D_CONV, :]) + _dot(ya_ref[...], wbf[D_CONV:, :]) + b_ref[...]
    x1 = _layer_norm(ALPHA * x_ref[...] + g1_ref[...] * mix, lg_ref[...], lb_ref[...])
    x1_ref[...] = x1
    h2 = x1 * (1.0 + sc2_ref[...]) + sh2_ref[...]
    _to_tiles(h2t_ref, h2)
    h2 = h2.astype(BF16)
    bucket, lanef = _route_bucket(_dot(h2, wrbf[...]) + br_ref[...])
    mine = lanef == bucket
    onehot = jnp.where(mine, 1.0, 0.0)
    earlier = _dot(ltri[...], onehot.astype(BF16)) + running[...]
    rank = jnp.sum(jnp.where(mine, earlier, 0.0), axis=-1, keepdims=True)
    running[...] += jnp.sum(onehot, axis=0, keepdims=True)
    code = bucket.astype(jnp.int32) * (1 << CODE_SHIFT) + rank.astype(jnp.int32)
    code_ref[...] = jnp.broadcast_to(code, code_ref.shape)
    counts_ref[...] = running[...]


def _outproj_route(x, y_conv, y_attn, w_out, b_out, g1, ln_g, ln_b, sc2, sh2, w_r, b_r):
    bsz, seq, _ = x.shape
    nst = seq // TS
    vec = lambda n: pl.BlockSpec((1, n), lambda b, s: (0, 0))
    mod = pl.BlockSpec((None, None, 1, D_MODEL), lambda b, s: (b, 0, 0, 0))
    tile = lambda n: pl.BlockSpec((None, TS, n), lambda b, s: (b, s, 0))
    return pl.pallas_call(
        _outproj_kernel,
        grid=(bsz, seq // TS),
        in_specs=[tile(D_MODEL), tile(D_CONV), tile(D_ATTN),
                  pl.BlockSpec((D_MODEL, D_MODEL), lambda b, s: (0, 0)), vec(D_MODEL),
                  mod, vec(D_MODEL), vec(D_MODEL), mod, mod,
                  pl.BlockSpec((D_MODEL, ROUTER_COLS), lambda b, s: (0, 0)), vec(ROUTER_COLS)],
        out_specs=[tile(D_MODEL), pl.BlockSpec((TS * CHUNKS, LANES), lambda b, s: (b * nst + s, 0)),
                   tile(LANES), vec(LANES)],
        out_shape=[jax.ShapeDtypeStruct((bsz, seq, D_MODEL), F32),
                   jax.ShapeDtypeStruct((bsz * seq * CHUNKS, LANES), F32),
                   jax.ShapeDtypeStruct((bsz, seq, LANES), jnp.int32),
                   jax.ShapeDtypeStruct((1, LANES), F32)],
        scratch_shapes=[pltpu.VMEM((D_MODEL, D_MODEL), BF16), pltpu.VMEM((D_MODEL, ROUTER_COLS), BF16),
                        pltpu.VMEM((TS, TS), BF16), pltpu.VMEM((1, LANES), F32)],
        compiler_params=pltpu.CompilerParams(
            dimension_semantics=("arbitrary", "arbitrary"), vmem_limit_bytes=VMEM_LIMIT),
        name="outproj_route",
    )(x, y_conv, y_attn, w_out, b_out, g1, ln_g, ln_b, sc2, sh2, w_r, b_r)


def _row_of(starts_ref, code):
    return (starts_ref[lax.shift_right_logical(code, jnp.int32(CODE_SHIFT))]
            + (code & ((1 << CODE_SHIFT) - 1)))


CHUNKS = D_MODEL // LANES
assert CHUNKS == SUBLANES


def _tok_rows(p):
    return pl.ds(pl.multiple_of(p * CHUNKS, CHUNKS), CHUNKS)


def _to_tiles(ref, x):
    for c in range(CHUNKS):
        ref[pl.ds(c, x.shape[0], stride=CHUNKS), :] = x[:, c * LANES:(c + 1) * LANES]


def _from_tiles(ref, n):
    return jnp.concatenate([ref[pl.ds(c, n, stride=CHUNKS), :] for c in range(CHUNKS)], axis=1)


def _wait_tokens(hbm_ref, n, sem):
    pltpu.make_async_copy(hbm_ref.at[pl.ds(0, n * CHUNKS)], hbm_ref.at[pl.ds(0, n * CHUNKS)], sem).wait()


def _scatter_kernel(starts_ref, zrow_ref, nt_ref, code_ref, h2t_ref, hs_ref, pos_ref, hbuf, zeros, sems, zsem):
    nsteps = pl.num_programs(0) * pl.num_programs(1)
    step = pl.program_id(0) * pl.num_programs(1) + pl.program_id(1)
    slot = step % 2

    def _tile_rows(tok):
        return pl.ds(pl.multiple_of(tok * CHUNKS, TMM * CHUNKS), TMM * CHUNKS)

    def zero_copy(b):
        return pltpu.make_async_copy(zeros, hs_ref.at[_tile_rows(zrow_ref[b])], zsem)

    def tail_copy(t):
        return pltpu.make_async_copy(zeros, hs_ref.at[_tile_rows(t * TMM)], zsem)

    @pl.when(step == 0)
    def _():
        zeros[...] = jnp.zeros_like(zeros)
        n_tiles = hs_ref.shape[0] // (TMM * CHUNKS)

        def start(b, c):
            @pl.when(zrow_ref[b] >= 0)
            def _():
                zero_copy(b).start()
            return c

        def wait(b, c):
            @pl.when(zrow_ref[b] >= 0)
            def _():
                zero_copy(b).wait()
            return c

        def start_tail(t, c):
            tail_copy(t).start()
            return c

        def wait_tail(t, c):
            tail_copy(t).wait()
            return c

        lax.fori_loop(0, N_BUCKETS, start, 0)
        lax.fori_loop(nt_ref[0], n_tiles, start_tail, 0)
        lax.fori_loop(0, N_BUCKETS, wait, 0)
        lax.fori_loop(nt_ref[0], n_tiles, wait_tail, 0)

    @pl.when(step >= 2)
    def _():
        _wait_tokens(hs_ref, TS, sems.at[slot])

    hbuf[slot] = h2t_ref[...]

    def issue(i, c):
        for u in range(SUBLANES):
            r = i * SUBLANES + u
            pos = _row_of(starts_ref, code_ref[0, r])
            pos_ref[0, r] = pos
            pltpu.make_async_copy(hbuf.at[slot, _tok_rows(r)], hs_ref.at[_tok_rows(pos)],
                                  sems.at[slot]).start(priority=u % 2)
        return c

    lax.fori_loop(0, TS // SUBLANES, issue, 0)

    @pl.when(step == nsteps - 1)
    def _():
        _wait_tokens(hs_ref, TS, sems.at[slot])
        _wait_tokens(hs_ref, TS, sems.at[1 - slot])


def _scatter_rows(starts, zrow, nt, code, h2t, n_rows):
    bsz, seq = code.shape
    nst = seq // TS
    grid_spec = pltpu.PrefetchScalarGridSpec(
        num_scalar_prefetch=3,
        grid=(bsz, nst),
        in_specs=[pl.BlockSpec((None, 1, TS), lambda b, s, *_: (b * nst + s, 0, 0), memory_space=pltpu.SMEM),
                  pl.BlockSpec((TS * CHUNKS, LANES), lambda b, s, *_: (b * nst + s, 0))],
        out_specs=[pl.BlockSpec(memory_space=pl.ANY),
                   pl.BlockSpec((None, 1, TS), lambda b, s, *_: (b * nst + s, 0, 0), memory_space=pltpu.SMEM)],
        scratch_shapes=[pltpu.VMEM((2, TS * CHUNKS, LANES), F32), pltpu.VMEM((TMM * CHUNKS, LANES), F32),
                        pltpu.SemaphoreType.DMA((2,)), pltpu.SemaphoreType.DMA],
    )
    return pl.pallas_call(
        _scatter_kernel,
        grid_spec=grid_spec,
        out_shape=[jax.ShapeDtypeStruct((n_rows * CHUNKS, LANES), F32),
                   jax.ShapeDtypeStruct((bsz * nst, 1, TS), jnp.int32)],
        compiler_params=pltpu.CompilerParams(dimension_semantics=("arbitrary", "arbitrary")),
        name="moe_scatter",
    )(starts, zrow, nt, code.reshape(bsz * nst, 1, TS), h2t)


def _moe_kernel(tlo_ref, thi_ref, blk_ref, nt_ref, h_ref, wg_hbm, wu_hbm, wd_hbm, wr_ref, br_ref, y_ref,
                wrbf, wg, wu, wd, sg, su, sd, sems):
    i = pl.program_id(0)
    group = lax.shift_right_logical(tlo_ref[i], jnp.int32(3))
    prev_group = lax.shift_right_logical(tlo_ref[jnp.maximum(i - 1, 0)], jnp.int32(3))

    @pl.when(i == 0)
    def _():
        wrbf[...] = wr_ref[...].astype(BF16)

    @pl.when((i == 0) | (group != prev_group))
    def _():
        def copies(e, slot):
            ex = group * EXPERTS_PER_GROUP + e
            return (pltpu.make_async_copy(wg_hbm.at[ex], sg.at[slot], sems.at[slot, 0]),
                    pltpu.make_async_copy(wu_hbm.at[ex], su.at[slot], sems.at[slot, 1]),
                    pltpu.make_async_copy(wd_hbm.at[ex], sd.at[slot], sems.at[slot, 2]))

        for cp in copies(0, 0):
            cp.start()

        def land(e, c):
            slot = e % 2

            @pl.when(e + 1 < EXPERTS_PER_GROUP)
            def _():
                for cp in copies(e + 1, 1 - slot):
                    cp.start()

            for cp in copies(e, slot):
                cp.wait()
            wg[e] = sg[slot].astype(BF16)
            wu[e] = su[slot].astype(BF16)
            wd[e] = sd[slot].astype(BF16)
            return c

        lax.fori_loop(0, EXPERTS_PER_GROUP, land, 0)

    @pl.when(i < nt_ref[0])
    def _():
        h = _from_tiles(h_ref, TMM).astype(BF16)
        logits = _dot(h, wrbf[...]) + br_ref[...]
        lane = lax.broadcasted_iota(jnp.int32, logits.shape, 1)
        p_top = _group_softmax_top(logits, lane)[3]
        l_lo = jnp.sum(jnp.where(lane == tlo_ref[i], logits, 0.0), axis=-1, keepdims=True)
        l_hi = jnp.sum(jnp.where(lane == thi_ref[i], logits, 0.0), axis=-1, keepdims=True)
        w_lo = p_top / (1.0 + jnp.exp(l_hi - l_lo))
        w_hi = p_top / (1.0 + jnp.exp(l_lo - l_hi))

        def ffn(e):
            a = _dot(h, wg[e])
            a = a * jax.nn.sigmoid(a) * _dot(h, wu[e])
            return _dot(a.astype(BF16), wd[e])

        local = EXPERTS_PER_GROUP - 1
        _to_tiles(y_ref, w_lo * ffn(tlo_ref[i] & local) + w_hi * ffn(thi_ref[i] & local))

    @pl.when(i >= nt_ref[0])
    def _():
        y_ref[...] = jnp.zeros_like(y_ref)


def _moe_sorted(tlo, thi, blk, nt, h_sorted, w_gate, w_up, w_down, w_r, b_r):
    n_rows = h_sorted.shape[0] // CHUNKS
    rows = pl.BlockSpec((TMM * CHUNKS, LANES), lambda i, tlo, thi, blk, nt: (blk[i], 0))
    hbm = pl.BlockSpec(memory_space=pl.ANY)
    w_in_shape, w_out_shape = (D_MODEL, D_EXPERT), (D_EXPERT, D_MODEL)
    grid_spec = pltpu.PrefetchScalarGridSpec(
        num_scalar_prefetch=4,
        grid=(n_rows // TMM,),
        in_specs=[rows, hbm, hbm, hbm,
                  pl.BlockSpec((D_MODEL, ROUTER_COLS), lambda i, *_: (0, 0)),
                  pl.BlockSpec((1, ROUTER_COLS), lambda i, *_: (0, 0))],
        out_specs=pl.BlockSpec((TMM * CHUNKS, LANES), lambda i, *_: (i, 0)),
        scratch_shapes=[pltpu.VMEM((D_MODEL, ROUTER_COLS), BF16),
                        pltpu.VMEM((EXPERTS_PER_GROUP,) + w_in_shape, BF16),
                        pltpu.VMEM((EXPERTS_PER_GROUP,) + w_in_shape, BF16),
                        pltpu.VMEM((EXPERTS_PER_GROUP,) + w_out_shape, BF16),
                        pltpu.VMEM((2,) + w_in_shape, F32), pltpu.VMEM((2,) + w_in_shape, F32),
                        pltpu.VMEM((2,) + w_out_shape, F32), pltpu.SemaphoreType.DMA((2, 3))],
    )
    return pl.pallas_call(
        _moe_kernel,
        grid_spec=grid_spec,
        out_shape=jax.ShapeDtypeStruct((n_rows * CHUNKS, LANES), F32),
        compiler_params=pltpu.CompilerParams(dimension_semantics=("arbitrary",), vmem_limit_bytes=VMEM_LIMIT),
        name="moe_sorted",
    )(tlo, thi, blk, nt, h_sorted, w_gate, w_up, w_down, w_r, b_r)


def _final_kernel(pos_ref, posn_ref, x1_ref, g2_ref, lg_ref, lb_ref, ys_ref, o_ref, ybuf, sems):
    nsteps = pl.num_programs(0) * pl.num_programs(1)
    step = pl.program_id(0) * pl.num_programs(1) + pl.program_id(1)
    slot = step % 2

    def gather(rows_ref, sl):
        def issue(i, c):
            for u in range(SUBLANES):
                r = i * SUBLANES + u
                pltpu.make_async_copy(ys_ref.at[_tok_rows(rows_ref[0, r])], ybuf.at[sl, _tok_rows(r)],
                                      sems.at[sl]).start(priority=u % 2)
            return c

        lax.fori_loop(0, TS // SUBLANES, issue, 0)

    @pl.when(step == 0)
    def _():
        gather(pos_ref, slot)

    @pl.when(step + 1 < nsteps)
    def _():
        gather(posn_ref, 1 - slot)

    _wait_tokens(ys_ref, TS, sems.at[slot])
    y = _from_tiles(ybuf.at[slot], TS)
    o_ref[...] = _layer_norm(ALPHA * x1_ref[...] + g2_ref[...] * y, lg_ref[...], lb_ref[...])


def _unsort_ln2(pos, x1, g2, ln_g, ln_b, y_sorted):
    bsz, seq, _ = x1.shape
    nst = seq // TS
    last = bsz * nst - 1
    return pl.pallas_call(
        _final_kernel,
        grid=(bsz, nst),
        in_specs=[pl.BlockSpec((None, 1, TS), lambda b, s: (b * nst + s, 0, 0), memory_space=pltpu.SMEM),
                  pl.BlockSpec((None, 1, TS), lambda b, s: (jnp.minimum(b * nst + s + 1, last), 0, 0),
                               memory_space=pltpu.SMEM),
                  pl.BlockSpec((None, TS, D_MODEL), lambda b, s: (b, s, 0)),
                  pl.BlockSpec((None, None, 1, D_MODEL), lambda b, s: (b, 0, 0, 0)),
                  pl.BlockSpec((1, D_MODEL), lambda b, s: (0, 0)),
                  pl.BlockSpec((1, D_MODEL), lambda b, s: (0, 0)),
                  pl.BlockSpec(memory_space=pl.ANY)],
        out_specs=pl.BlockSpec((None, TS, D_MODEL), lambda b, s: (b, s, 0)),
        out_shape=jax.ShapeDtypeStruct((bsz, seq, D_MODEL), F32),
        scratch_shapes=[pltpu.VMEM((2, TS * CHUNKS, LANES), F32), pltpu.SemaphoreType.DMA((2,))],
        compiler_params=pltpu.CompilerParams(dimension_semantics=("arbitrary", "arbitrary")),
        name="unsort_ln2",
    )(pos, pos, x1, g2, ln_g, ln_b, y_sorted)


def _pair_layout(a, axis):
    shp = a.shape
    a = a.reshape(shp[:axis] + (N_KV_HEADS, N_HEADS // N_KV_HEADS, HEAD_DIM) + shp[axis + 1:])
    return jnp.swapaxes(a, axis, axis + 1).reshape(shp)


def _tile_tables(counts, n_tiles_max):
    n_tile_b = jnp.floor((counts + (TMM - 1.0)) * (1.0 / TMM))
    b_idx = jnp.arange(N_BUCKETS, dtype=jnp.int32)
    tile_end = jnp.sum(jnp.where(b_idx[None, :] <= b_idx[:, None], n_tile_b[None, :], 0.0), axis=1)
    total = jnp.sum(n_tile_b)
    starts = ((tile_end - n_tile_b) * TMM).astype(jnp.int32)
    zrow = jnp.where(n_tile_b > 0, (tile_end - 1.0) * TMM, -1.0).astype(jnp.int32)
    blk = jnp.minimum(jnp.arange(n_tiles_max, dtype=jnp.int32), total.astype(jnp.int32) - 1)
    bucket = jnp.sum(jnp.where(tile_end[None, :] <= blk[:, None].astype(F32), 1, 0), axis=1)
    onehot = bucket[:, None] == b_idx[None, :]
    e_lo = jnp.asarray([(b // N_PAIRS) * EXPERTS_PER_GROUP + PAIRS[b % N_PAIRS][0] for b in range(N_BUCKETS)], jnp.int32)
    e_hi = jnp.asarray([(b // N_PAIRS) * EXPERTS_PER_GROUP + PAIRS[b % N_PAIRS][1] for b in range(N_BUCKETS)], jnp.int32)
    tlo = jnp.sum(jnp.where(onehot, e_lo[None, :], 0), axis=1)
    thi = jnp.sum(jnp.where(onehot, e_hi[None, :], 0), axis=1)
    pad = jnp.zeros((LANES - N_BUCKETS,), jnp.int32)
    return (jnp.concatenate([starts, pad]), jnp.concatenate([zrow, pad - 1]), tlo, thi, blk,
            total.astype(jnp.int32).reshape(1))


def kernel(x, c, w_ada, b_ada, w_in, b_in, conv_w, conv_b, conv_ln_g, conv_ln_b, conv_out_g, sinks,
           attn_out_g, w_out, b_out, ln1_g, ln1_b, w_router_group, b_router_group, w_router_expert,
           b_router_expert, w_gate, w_up, w_down, ln2_g, ln2_b):
    assert w_ada.shape[0] == DEPTH
    bsz, seq, _ = x.shape
    row = lambda v: v[0][None, :]

    mod = _modulation(c, w_ada[0], row(b_ada)).reshape(bsz, 6, 1, D_MODEL)
    sh1, sc1, g1, sh2, sc2, g2 = [mod[:, j:j + 1] for j in range(6)]

    q0 = 2 * D_CONV
    w_q = _pair_layout(w_in[0][:, q0:q0 + D_ATTN], 1)
    b_q = _pair_layout(b_in[0][q0:q0 + D_ATTN], 0)[None, :]
    og_attn = _pair_layout(attn_out_g[0], 0)[None, :]
    w_o = jnp.concatenate([w_out[0][:D_CONV], _pair_layout(w_out[0][D_CONV:], 0)], axis=0)

    y_conv, q, kv = _inproj_conv(x, sh1, sc1, w_in[0], row(b_in), w_q, b_q, conv_w[0], row(conv_b),
                                 row(conv_ln_g), row(conv_ln_b), row(conv_out_g))
    y_attn = _attention(q, kv, sinks[0], og_attn)

    pad = ROUTER_COLS - N_EXPERTS - N_GROUPS
    w_r = jnp.concatenate([w_router_expert[0], w_router_group[0], jnp.zeros((D_MODEL, pad), F32)], axis=1)
    b_r = jnp.concatenate([b_router_expert[0], b_router_group[0], jnp.zeros((pad,), F32)])[None, :]
    x1, h2t, code, counts = _outproj_route(x, y_conv, y_attn, w_o, row(b_out), g1, row(ln1_g), row(ln1_b),
                                      sc2, sh2, w_r, b_r)

    n_tiles_max = (bsz * seq) // TMM + N_BUCKETS
    starts, zrow, tlo, thi, blk, nt = _tile_tables(counts[0, :N_BUCKETS], n_tiles_max)
    h_sorted, pos = _scatter_rows(starts, zrow, nt, code[:, :, 0], h2t, n_tiles_max * TMM)
    y_sorted = _moe_sorted(tlo, thi, blk, nt, h_sorted, w_gate[0], w_up[0], w_down[0], w_r, b_r)
    return _unsort_ln2(pos, x1, g2, row(ln2_g), row(ln2_b), y_sorted)
```

```python
import jax
import jax.numpy as jnp
import numpy as np
from jax import lax
from jax.experimental import pallas as pl
from jax.experimental.pallas import tpu as pltpu

F32 = jnp.float32
BF16 = jnp.bfloat16

D_MODEL = 1024
D_CONV = 512
CONV_WIDTH = 31
GROUP_DIM = 64
N_HEADS = 8
N_KV_HEADS = 2
HEAD_DIM = 64
D_ATTN = N_HEADS * HEAD_DIM
D_KV = N_KV_HEADS * HEAD_DIM
BLOCK = 128
D_IN = 2 * D_CONV + D_ATTN + 2 * D_KV
N_GROUPS = 4
EXPERTS_PER_GROUP = 8
N_EXPERTS = N_GROUPS * EXPERTS_PER_GROUP
D_EXPERT = D_MODEL // 4
DEPTH = 1
ALPHA = (2.0 * DEPTH) ** 0.25
EPS = 1e-5
NEG = -1e30

LANES = 128
SUBLANES = 8
TS = 512
HALO = 32
CR = 64
AB = 4
PAIR_SLOTS = [j + (N_HEADS // N_KV_HEADS) * g for j in range(N_HEADS // N_KV_HEADS) for g in range(N_KV_HEADS)]
ROUTER_COLS = LANES
VMEM_LIMIT = 56 * 1024 * 1024

PAIRS = [(lo, hi) for lo in range(EXPERTS_PER_GROUP) for hi in range(lo + 1, EXPERTS_PER_GROUP)]
N_PAIRS = len(PAIRS)
N_BUCKETS = N_GROUPS * N_PAIRS
TMM = 128
NSUB = 4
CODE_SHIFT = 16


def _dot(a, b):
    return jnp.dot(a, b, preferred_element_type=F32)


def _dot_nt(a, b):
    return lax.dot_general(a, b, (((1,), (1,)), ((), ())), preferred_element_type=F32)


def _fill_group_maps(bsel_ref, bexp_ref):
    c = bsel_ref.shape[0]
    ch = lax.broadcasted_iota(jnp.int32, (c, LANES), 0) // GROUP_DIM
    gi = lax.broadcasted_iota(jnp.int32, (c, LANES), 1)
    bsel_ref[...] = jnp.where(ch == gi, 1.0 / GROUP_DIM, 0.0).astype(BF16)
    gi2 = lax.broadcasted_iota(jnp.int32, (LANES, c), 0)
    ch2 = lax.broadcasted_iota(jnp.int32, (LANES, c), 1) // GROUP_DIM
    bexp_ref[...] = jnp.where(ch2 == gi2, 1.0, 0.0).astype(BF16)


def _group_rms(y, bsel, bexp):
    ms = _dot((y * y).astype(BF16), bsel)
    r = lax.rsqrt(ms + EPS)
    r_hi = r.astype(BF16)
    r_lo = (r - r_hi.astype(F32)).astype(BF16)
    return y * (_dot(r_hi, bexp) + _dot(r_lo, bexp))


def _layer_norm(y, g, b):
    mu = jnp.mean(y, axis=-1, keepdims=True)
    d = y - mu
    var = jnp.mean(d * d, axis=-1, keepdims=True)
    return d * lax.rsqrt(var + EPS) * g + b


def _mod_kernel(c_ref, w_ref, b_ref, o_ref):
    c = c_ref[...]
    c_act = c * jax.nn.sigmoid(c)
    o_ref[...] = jnp.dot(c_act, w_ref[...], preferred_element_type=F32,
                         precision=lax.Precision.HIGHEST) + b_ref[...]


def _modulation(c, w_ada, b_ada):
    bsz = c.shape[0]
    n = w_ada.shape[1]
    return pl.pallas_call(
        _mod_kernel,
        grid=(n // D_MODEL,),
        in_specs=[pl.BlockSpec((bsz, D_MODEL), lambda j: (0, 0)),
                  pl.BlockSpec((D_MODEL, D_MODEL), lambda j: (0, j)),
                  pl.BlockSpec((1, D_MODEL), lambda j: (0, j))],
        out_specs=pl.BlockSpec((bsz, D_MODEL), lambda j: (0, j)),
        out_shape=jax.ShapeDtypeStruct((bsz, n), F32),
        name="adaln_mod",
    )(c, w_ada, b_ada)


def _inproj_conv_kernel(x_ref, sh_ref, sc_ref, w_ref, b_ref, wq_ref, bq_ref, cw_ref, cb_ref, lg_ref, lb_ref, og_ref,
                        yc_ref, q_ref, kv_ref, wbf, gs, cacc, wb, bsel, bexp):
    first = (pl.program_id(0) == 0) & (pl.program_id(1) == 0)
    s = pl.program_id(1)

    @pl.when(first)
    def _():
        wbf[...] = w_ref[...].astype(BF16)
        wbf[:, 2 * D_CONV:2 * D_CONV + D_ATTN] = wq_ref[...].astype(BF16)
        wb[...] = jnp.broadcast_to(cw_ref[...][:, None, :], wb.shape)
        _fill_group_maps(bsel, bexp)

    n_ct = D_CONV // LANES

    @pl.when(s == 0)
    def _():
        gs[0, :, 0:HALO, :] = jnp.zeros((n_ct, HALO, LANES), F32)

    @pl.when(s > 0)
    def _():
        gs[0, :, 0:HALO, :] = gs[0, :, TS:TS + HALO, :]

    h = (x_ref[...] * (1.0 + sc_ref[...]) + sh_ref[...]).astype(BF16)
    u_a = _dot(h, wbf[:, 0:D_CONV]) + b_ref[:, 0:D_CONV]
    u_b = _dot(h, wbf[:, D_CONV:2 * D_CONV]) + b_ref[:, D_CONV:2 * D_CONV]
    glu = u_a * jax.nn.sigmoid(u_b)
    for c in range(n_ct):
        gs[0, c, HALO:HALO + TS, :] = glu[:, c * LANES:(c + 1) * LANES]
    q0 = 2 * D_CONV
    k0 = q0 + D_ATTN

    n_sh = TS + HALO - SUBLANES
    for j in range(1, SUBLANES):
        gs[j, :, 0:n_sh, :] = gs[0, :, j:j + n_sh, :]

    cb = cb_ref[...]
    lg = lg_ref[...]
    lb = lb_ref[...]
    og = og_ref[...]

    late_cols = 2 * LANES
    n_late = (D_ATTN + 2 * D_KV) // late_cols

    def late_proj(j):
        c0 = j * late_cols
        if c0 < D_ATTN:
            q = _dot(h, wbf[:, q0 + c0:q0 + c0 + late_cols]) + bq_ref[:, c0:c0 + late_cols]
            q_ref[:, c0:c0 + late_cols] = (q * (HEAD_DIM ** -0.5)).astype(BF16)
        else:
            kv_ref[...] = (_dot(h, wbf[:, k0:k0 + 2 * D_KV]) + b_ref[:, k0:k0 + 2 * D_KV]).astype(BF16)

    n_chunks = n_ct * (TS // CR)
    every = n_chunks // n_late
    for n in range(n_chunks):
        c, r0 = n // (TS // CR), (n % (TS // CR)) * CR
        ls = slice(c * LANES, (c + 1) * LANES)
        acc = jnp.broadcast_to(cb[:, ls], (CR, LANES)).reshape(CR // SUBLANES, SUBLANES, LANES)
        for k in range(CONV_WIDTH):
            off = HALO - (CONV_WIDTH - 1) + k
            a0 = r0 + (off // SUBLANES) * SUBLANES
            seg = gs[off % SUBLANES, c, a0:a0 + CR, :]
            acc = acc + wb[k, :, ls][None] * seg.reshape(CR // SUBLANES, SUBLANES, LANES)
        cacc[c, r0:r0 + CR, :] = acc.reshape(CR, LANES)
        if n % every == 0 and n // every < n_late:
            late_proj(n // every)
    y = _layer_norm(jnp.concatenate([cacc[c] for c in range(n_ct)], axis=1), lg, lb)
    y = y * jax.nn.sigmoid(y)
    yc_ref[...] = (_group_rms(y, bsel[...], bexp[...]) * og).astype(BF16)


def _inproj_conv(x, sh1, sc1, w_in, b_in, w_q, b_q, conv_w, conv_b, ln_g, ln_b, out_g):
    bsz, seq, _ = x.shape
    vec = lambda n: pl.BlockSpec((1, n), lambda b, s: (0, 0))
    mod = pl.BlockSpec((None, None, 1, D_MODEL), lambda b, s: (b, 0, 0, 0))
    tile = lambda n: pl.BlockSpec((None, TS, n), lambda b, s: (b, s, 0))
    return pl.pallas_call(
        _inproj_conv_kernel,
        grid=(bsz, seq // TS),
        in_specs=[tile(D_MODEL), mod, mod,
                  pl.BlockSpec((D_MODEL, D_IN), lambda b, s: (0, 0)), vec(D_IN),
                  pl.BlockSpec((D_MODEL, D_ATTN), lambda b, s: (0, 0)), vec(D_ATTN),
                  pl.BlockSpec((CONV_WIDTH, D_CONV), lambda b, s: (0, 0)),
                  vec(D_CONV), vec(D_CONV), vec(D_CONV), vec(D_CONV)],
        out_specs=[tile(D_CONV), tile(D_ATTN), tile(2 * D_KV)],
        out_shape=[jax.ShapeDtypeStruct((bsz, seq, D_CONV), BF16),
                   jax.ShapeDtypeStruct((bsz, seq, D_ATTN), BF16),
                   jax.ShapeDtypeStruct((bsz, seq, 2 * D_KV), BF16)],
        scratch_shapes=[pltpu.VMEM((D_MODEL, D_IN), BF16),
                        pltpu.VMEM((SUBLANES, D_CONV // LANES, TS + HALO, LANES), F32),
                        pltpu.VMEM((D_CONV // LANES, TS, LANES), F32),
                        pltpu.VMEM((CONV_WIDTH, SUBLANES, D_CONV), F32),
                        pltpu.VMEM((D_CONV, LANES), BF16),
                        pltpu.VMEM((LANES, D_CONV), BF16)],
        compiler_params=pltpu.CompilerParams(
            dimension_semantics=("arbitrary", "arbitrary"), vmem_limit_bytes=VMEM_LIMIT),
        name="inproj_conv",
    )(x, sh1, sc1, w_in, b_in, w_q, b_q, conv_w, conv_b, ln_g, ln_b, out_g)


def _attn_kernel(sinks_ref, q_ref, kvc_ref, kvp_ref, og_ref, o_ref, bsel, bexp):
    n = pl.program_id(1)

    @pl.when((pl.program_id(0) == 0) & (n == 0))
    def _():
        _fill_group_maps(bsel, bexp)

    qr = lax.broadcasted_iota(jnp.int32, (BLOCK, 2 * BLOCK), 0)
    kc = lax.broadcasted_iota(jnp.int32, (BLOCK, 2 * BLOCK), 1)
    band = (kc > qr) & (kc <= qr + BLOCK)
    lane = lax.broadcasted_iota(jnp.int32, (BLOCK, LANES), 1)
    lower = lane < HEAD_DIM
    lower_bf = jnp.where(lower, 1.0, 0.0).astype(BF16)
    upper_bf = jnp.where(lower, 0.0, 1.0).astype(BF16)
    og = og_ref[...]
    for blk in range(AB):
        r0 = blk * BLOCK
        if blk == 0:
            kv = jnp.concatenate([kvp_ref[...], kvc_ref[0:BLOCK, :]], axis=0)
            mask = band & ((n > 0) | (kc >= BLOCK))
        else:
            kv = kvc_ref[r0 - BLOCK:r0 + BLOCK, :]
            mask = band
        keys = kv[:, 0:D_KV]
        vals = kv[:, D_KV:2 * D_KV]
        parts = []
        for j in range(D_ATTN // LANES):
            col = q_ref[r0:r0 + BLOCK, j * LANES:(j + 1) * LANES]
            parts += [col * lower_bf, col * upper_bf]
        s_all = _dot_nt(jnp.concatenate(parts, axis=0), keys)
        probs = []
        den = jnp.ones((BLOCK, LANES), F32)
        for slot in range(N_HEADS):
            sc = jnp.where(mask, s_all[slot * BLOCK:(slot + 1) * BLOCK], NEG)
            sink = sinks_ref[PAIR_SLOTS[slot]]
            m = jnp.maximum(jnp.max(sc, axis=-1, keepdims=True), sink)
            p = jnp.exp(sc - m)
            den = jnp.where(lane == slot, jnp.sum(p, axis=-1, keepdims=True) + jnp.exp(sink - m), den)
            probs.append(p.astype(BF16))
        pv = _dot(jnp.concatenate(probs, axis=0), vals)
        o = jnp.concatenate(
            [jnp.where(lower, pv[(2 * j) * BLOCK:(2 * j + 1) * BLOCK], pv[(2 * j + 1) * BLOCK:(2 * j + 2) * BLOCK])
             for j in range(D_ATTN // LANES)], axis=1)
        ms = _dot((o * o).astype(BF16), bsel[...])
        r = lax.rsqrt(ms + EPS * den * den)
        r_hi = r.astype(BF16)
        r_lo = (r - r_hi.astype(F32)).astype(BF16)
        scale = _dot(r_hi, bexp[...]) + _dot(r_lo, bexp[...])
        o_ref[r0:r0 + BLOCK, :] = (o * scale * og).astype(BF16)


def _attention(q, kv, sinks, out_g):
    bsz, seq, _ = q.shape
    rows = AB * BLOCK
    grid_spec = pltpu.PrefetchScalarGridSpec(
        num_scalar_prefetch=1,
        grid=(bsz, seq // rows),
        in_specs=[pl.BlockSpec((None, rows, D_ATTN), lambda b, n, sk: (b, n, 0)),
                  pl.BlockSpec((None, rows, 2 * D_KV), lambda b, n, sk: (b, n, 0)),
                  pl.BlockSpec((None, BLOCK, 2 * D_KV), lambda b, n, sk: (b, jnp.maximum(AB * n - 1, 0), 0)),
                  pl.BlockSpec((1, D_ATTN), lambda b, n, sk: (0, 0))],
        out_specs=pl.BlockSpec((None, rows, D_ATTN), lambda b, n, sk: (b, n, 0)),
        scratch_shapes=[pltpu.VMEM((D_ATTN, LANES), BF16), pltpu.VMEM((LANES, D_ATTN), BF16)],
    )
    return pl.pallas_call(
        _attn_kernel,
        grid_spec=grid_spec,
        out_shape=jax.ShapeDtypeStruct((bsz, seq, D_ATTN), BF16),
        compiler_params=pltpu.CompilerParams(dimension_semantics=("arbitrary", "arbitrary")),
        name="swa_attention",
    )(sinks, q, kv, kv, out_g)


def _group_softmax_top(logits, lane):
    is_g = (lane >= N_EXPERTS) & (lane < N_EXPERTS + N_GROUPS)
    gl = jnp.where(is_g, logits, NEG)
    gmax = jnp.max(gl, axis=-1, keepdims=True)
    p_top = 1.0 / jnp.sum(jnp.where(is_g, jnp.exp(gl - gmax), 0.0), axis=-1, keepdims=True)
    return is_g, gl, gmax, p_top


def _route_bucket(logits):
    rows = logits.shape[0]
    lane = lax.broadcasted_iota(jnp.int32, (rows, ROUTER_COLS), 1)
    lanef = lane.astype(F32)
    is_g, gl, gmax, _ = _group_softmax_top(logits, lane)
    gidx = jnp.min(jnp.where(is_g & (gl == gmax), lanef - N_EXPERTS, 99.0), axis=-1, keepdims=True)
    in_grp = (lane < N_EXPERTS) & ((lane // EXPERTS_PER_GROUP) == gidx.astype(jnp.int32))
    el = jnp.where(in_grp, logits, NEG)
    m1 = jnp.max(el, axis=-1, keepdims=True)
    i1 = jnp.min(jnp.where(in_grp & (el == m1), lanef, 999.0), axis=-1, keepdims=True)
    rest = in_grp & (lanef != i1)
    el2 = jnp.where(rest, logits, NEG)
    m2 = jnp.max(el2, axis=-1, keepdims=True)
    i2 = jnp.min(jnp.where(rest & (el2 == m2), lanef, 999.0), axis=-1, keepdims=True)
    lo = jnp.minimum(i1, i2) - EXPERTS_PER_GROUP * gidx
    hi = jnp.maximum(i1, i2) - EXPERTS_PER_GROUP * gidx
    pair = lo * (EXPERTS_PER_GROUP - 1) - lo * (lo - 1.0) * 0.5 + (hi - lo - 1.0)
    return gidx * N_PAIRS + pair, lanef


def _outproj_kernel(x_ref, yc_ref, ya_ref, w_ref, b_ref, g1_ref, lg_ref, lb_ref, sc2_ref, sh2_ref,
                    wr_ref, br_ref, x1_ref, h2t_ref, code_ref, counts_ref, wbf, wrbf, ltri, running):
    @pl.when((pl.program_id(0) == 0) & (pl.program_id(1) == 0))
    def _():
        wbf[...] = w_ref[...].astype(BF16)
        wrbf[...] = wr_ref[...].astype(BF16)
        r = lax.broadcasted_iota(jnp.int32, (TS, TS), 0)
        c = lax.broadcasted_iota(jnp.int32, (TS, TS), 1)
        ltri[...] = jnp.where(c < r, 1.0, 0.0).astype(BF16)
        running[...] = jnp.zeros_like(running)

    mix = _dot(yc_ref[...], wbf[0:D_CONV, :]) + _dot(ya_ref[...], wbf[D_CONV:, :]) + b_ref[...]
    x1 = _layer_norm(ALPHA * x_ref[...] + g1_ref[...] * mix, lg_ref[...], lb_ref[...])
    x1_ref[...] = x1
    h2 = x1 * (1.0 + sc2_ref[...]) + sh2_ref[...]
    _to_tiles(h2t_ref, h2)
    h2 = h2.astype(BF16)
    bucket, lanef = _route_bucket(_dot(h2, wrbf[...]) + br_ref[...])
    mine = lanef == bucket
    onehot = jnp.where(mine, 1.0, 0.0)
    earlier = _dot(ltri[...], onehot.astype(BF16)) + running[...]
    rank = jnp.sum(jnp.where(mine, earlier, 0.0), axis=-1, keepdims=True)
    running[...] += jnp.sum(onehot, axis=0, keepdims=True)
    code = bucket.astype(jnp.int32) * (1 << CODE_SHIFT) + rank.astype(jnp.int32)
    code_ref[...] = jnp.broadcast_to(code, code_ref.shape)
    counts_ref[...] = running[...]


def _outproj_route(x, y_conv, y_attn, w_out, b_out, g1, ln_g, ln_b, sc2, sh2, w_r, b_r):
    bsz, seq, _ = x.shape
    nst = seq // TS
    vec = lambda n: pl.BlockSpec((1, n), lambda b, s: (0, 0))
    mod = pl.BlockSpec((None, None, 1, D_MODEL), lambda b, s: (b, 0, 0, 0))
    tile = lambda n: pl.BlockSpec((None, TS, n), lambda b, s: (b, s, 0))
    return pl.pallas_call(
        _outproj_kernel,
        grid=(bsz, seq // TS),
        in_specs=[tile(D_MODEL), tile(D_CONV), tile(D_ATTN),
                  pl.BlockSpec((D_MODEL, D_MODEL), lambda b, s: (0, 0)), vec(D_MODEL),
                  mod, vec(D_MODEL), vec(D_MODEL), mod, mod,
                  pl.BlockSpec((D_MODEL, ROUTER_COLS), lambda b, s: (0, 0)), vec(ROUTER_COLS)],
        out_specs=[tile(D_MODEL), pl.BlockSpec((TS * CHUNKS, LANES), lambda b, s: (b * nst + s, 0)),
                   tile(LANES), vec(LANES)],
        out_shape=[jax.ShapeDtypeStruct((bsz, seq, D_MODEL), F32),
                   jax.ShapeDtypeStruct((bsz * seq * CHUNKS, LANES), F32),
                   jax.ShapeDtypeStruct((bsz, seq, LANES), jnp.int32),
                   jax.ShapeDtypeStruct((1, LANES), F32)],
        scratch_shapes=[pltpu.VMEM((D_MODEL, D_MODEL), BF16), pltpu.VMEM((D_MODEL, ROUTER_COLS), BF16),
                        pltpu.VMEM((TS, TS), BF16), pltpu.VMEM((1, LANES), F32)],
        compiler_params=pltpu.CompilerParams(
            dimension_semantics=("arbitrary", "arbitrary"), vmem_limit_bytes=VMEM_LIMIT),
        name="outproj_route",
    )(x, y_conv, y_attn, w_out, b_out, g1, ln_g, ln_b, sc2, sh2, w_r, b_r)


def _row_of(starts_ref, code):
    return (starts_ref[lax.shift_right_logical(code, jnp.int32(CODE_SHIFT))]
            + (code & ((1 << CODE_SHIFT) - 1)))


CHUNKS = D_MODEL // LANES
assert CHUNKS == SUBLANES


def _tok_rows(p):
    return pl.ds(pl.multiple_of(p * CHUNKS, CHUNKS), CHUNKS)


def _to_tiles(ref, x):
    for c in range(CHUNKS):
        ref[pl.ds(c, x.shape[0], stride=CHUNKS), :] = x[:, c * LANES:(c + 1) * LANES]


def _from_tiles(ref, n):
    return jnp.concatenate([ref[pl.ds(c, n, stride=CHUNKS), :] for c in range(CHUNKS)], axis=1)


def _wait_tokens(hbm_ref, n, sem):
    pltpu.make_async_copy(hbm_ref.at[pl.ds(0, n * CHUNKS)], hbm_ref.at[pl.ds(0, n * CHUNKS)], sem).wait()


def _scatter_kernel(starts_ref, zrow_ref, valid_ref, code_ref, h2t_ref, hs_ref, pos_ref, hbuf, zeros, sems, zsem):
    nsteps = pl.num_programs(0) * pl.num_programs(1)
    step = pl.program_id(0) * pl.num_programs(1) + pl.program_id(1)
    slot = step % 2

    def _tile_rows(tok):
        return pl.ds(pl.multiple_of(tok * CHUNKS, TMM * CHUNKS), TMM * CHUNKS)

    def zero_copy(b):
        return pltpu.make_async_copy(zeros, hs_ref.at[_tile_rows(zrow_ref[b])], zsem)

    def tail_copy(t):
        return pltpu.make_async_copy(zeros, hs_ref.at[_tile_rows(t * TMM)], zsem)

    @pl.when(step == 0)
    def _():
        zeros[...] = jnp.zeros_like(zeros)
        n_tiles = hs_ref.shape[0] // (TMM * CHUNKS)

        def start(b, c):
            @pl.when(zrow_ref[b] >= 0)
            def _():
                zero_copy(b).start()
            return c

        def wait(b, c):
            @pl.when(zrow_ref[b] >= 0)
            def _():
                zero_copy(b).wait()
            return c

        def start_tail(t, c):
            @pl.when(valid_ref[t] == 0)
            def _():
                tail_copy(t).start()
            return c

        def wait_tail(t, c):
            @pl.when(valid_ref[t] == 0)
            def _():
                tail_copy(t).wait()
            return c

        lax.fori_loop(0, N_BUCKETS, start, 0)
        lax.fori_loop(0, n_tiles, start_tail, 0)
        lax.fori_loop(0, N_BUCKETS, wait, 0)
        lax.fori_loop(0, n_tiles, wait_tail, 0)

    @pl.when(step >= 2)
    def _():
        _wait_tokens(hs_ref, TS, sems.at[slot])

    hbuf[slot] = h2t_ref[...]

    def issue(i, c):
        for u in range(SUBLANES):
            r = i * SUBLANES + u
            pos = _row_of(starts_ref, code_ref[0, r])
            pos_ref[0, r] = pos
            pltpu.make_async_copy(hbuf.at[slot, _tok_rows(r)], hs_ref.at[_tok_rows(pos)],
                                  sems.at[slot]).start(priority=u % 2)
        return c

    lax.fori_loop(0, TS // SUBLANES, issue, 0)

    @pl.when(step == nsteps - 1)
    def _():
        _wait_tokens(hs_ref, TS, sems.at[slot])
        _wait_tokens(hs_ref, TS, sems.at[1 - slot])


def _scatter_rows(starts, zrow, valid, code, h2t, n_rows):
    bsz, seq = code.shape
    nst = seq // TS
    grid_spec = pltpu.PrefetchScalarGridSpec(
        num_scalar_prefetch=3,
        grid=(bsz, nst),
        in_specs=[pl.BlockSpec((None, 1, TS), lambda b, s, *_: (b * nst + s, 0, 0), memory_space=pltpu.SMEM),
                  pl.BlockSpec((TS * CHUNKS, LANES), lambda b, s, *_: (b * nst + s, 0))],
        out_specs=[pl.BlockSpec(memory_space=pl.ANY),
                   pl.BlockSpec((None, 1, TS), lambda b, s, *_: (b * nst + s, 0, 0), memory_space=pltpu.SMEM)],
        scratch_shapes=[pltpu.VMEM((2, TS * CHUNKS, LANES), F32), pltpu.VMEM((TMM * CHUNKS, LANES), F32),
                        pltpu.SemaphoreType.DMA((2,)), pltpu.SemaphoreType.DMA],
    )
    return pl.pallas_call(
        _scatter_kernel,
        grid_spec=grid_spec,
        out_shape=[jax.ShapeDtypeStruct((n_rows * CHUNKS, LANES), F32),
                   jax.ShapeDtypeStruct((bsz * nst, 1, TS), jnp.int32)],
        compiler_params=pltpu.CompilerParams(dimension_semantics=("arbitrary", "arbitrary")),
        name="moe_scatter",
    )(starts, zrow, valid, code.reshape(bsz * nst, 1, TS), h2t)


def _moe_kernel(tlo_ref, thi_ref, valid_ref, h_ref, wg_hbm, wu_hbm, wd_hbm, wr_ref, br_ref, y_ref,
                wrbf, wg, wu, wd, sg, su, sd, sems):
    i = pl.program_id(0)
    t0 = i * NSUB
    group = lax.shift_right_logical(tlo_ref[t0], jnp.int32(3))
    prev_group = lax.shift_right_logical(tlo_ref[jnp.maximum(t0 - NSUB, 0)], jnp.int32(3))

    @pl.when(i == 0)
    def _():
        wrbf[...] = wr_ref[...].astype(BF16)

    @pl.when((i == 0) | (group != prev_group))
    def _():
        def copies(e, slot):
            ex = group * EXPERTS_PER_GROUP + e
            return (pltpu.make_async_copy(wg_hbm.at[ex], sg.at[slot], sems.at[slot, 0]),
                    pltpu.make_async_copy(wu_hbm.at[ex], su.at[slot], sems.at[slot, 1]),
                    pltpu.make_async_copy(wd_hbm.at[ex], sd.at[slot], sems.at[slot, 2]))

        for cp in copies(0, 0):
            cp.start()

        def land(e, c):
            slot = e % 2

            @pl.when(e + 1 < EXPERTS_PER_GROUP)
            def _():
                for cp in copies(e + 1, 1 - slot):
                    cp.start()

            for cp in copies(e, slot):
                cp.wait()
            wg[e] = sg[slot].astype(BF16)
            wu[e] = su[slot].astype(BF16)
            wd[e] = sd[slot].astype(BF16)
            return c

        lax.fori_loop(0, EXPERTS_PER_GROUP, land, 0)

    @pl.when(valid_ref[t0] == 1)
    def _():
        lane = lax.broadcasted_iota(jnp.int32, (TMM, ROUTER_COLS), 1)
        local = EXPERTS_PER_GROUP - 1
        subs = range(NSUB)
        tile = lambda ref, j: ref.at[pl.ds(j * TMM * CHUNKS, TMM * CHUNKS)]
        hs = [_from_tiles(tile(h_ref, j), TMM).astype(BF16) for j in subs]
        es = [(tlo_ref[t0 + j] & local, thi_ref[t0 + j] & local) for j in subs]
        pre = [[(_dot(hs[j], wg[e]), _dot(hs[j], wu[e])) for e in es[j]] for j in subs]
        act = [[(a * jax.nn.sigmoid(a) * u).astype(BF16) for a, u in pre[j]] for j in subs]
        ys = [[_dot(act[j][k], wd[es[j][k]]) for k in (0, 1)] for j in subs]
        for j in subs:
            logits = _dot(hs[j], wrbf[...]) + br_ref[...]
            p_top = _group_softmax_top(logits, lane)[3]
            l_lo = jnp.sum(jnp.where(lane == tlo_ref[t0 + j], logits, 0.0), axis=-1, keepdims=True)
            l_hi = jnp.sum(jnp.where(lane == thi_ref[t0 + j], logits, 0.0), axis=-1, keepdims=True)
            w_lo = p_top / (1.0 + jnp.exp(l_hi - l_lo))
            w_hi = p_top / (1.0 + jnp.exp(l_lo - l_hi))
            _to_tiles(tile(y_ref, j), w_lo * ys[j][0] + w_hi * ys[j][1])

    @pl.when(valid_ref[t0] == 0)
    def _():
        y_ref[...] = jnp.zeros_like(y_ref)


def _moe_sorted(tlo, thi, valid, h_sorted, w_gate, w_up, w_down, w_r, b_r):
    n_rows = h_sorted.shape[0] // CHUNKS
    rows = pl.BlockSpec((NSUB * TMM * CHUNKS, LANES), lambda i, *_: (i, 0))
    hbm = pl.BlockSpec(memory_space=pl.ANY)
    w_in_shape, w_out_shape = (D_MODEL, D_EXPERT), (D_EXPERT, D_MODEL)
    grid_spec = pltpu.PrefetchScalarGridSpec(
        num_scalar_prefetch=3,
        grid=(n_rows // (NSUB * TMM),),
        in_specs=[rows, hbm, hbm, hbm,
                  pl.BlockSpec((D_MODEL, ROUTER_COLS), lambda i, *_: (0, 0)),
                  pl.BlockSpec((1, ROUTER_COLS), lambda i, *_: (0, 0))],
        out_specs=rows,
        scratch_shapes=[pltpu.VMEM((D_MODEL, ROUTER_COLS), BF16),
                        pltpu.VMEM((EXPERTS_PER_GROUP,) + w_in_shape, BF16),
                        pltpu.VMEM((EXPERTS_PER_GROUP,) + w_in_shape, BF16),
                        pltpu.VMEM((EXPERTS_PER_GROUP,) + w_out_shape, BF16),
                        pltpu.VMEM((2,) + w_in_shape, F32), pltpu.VMEM((2,) + w_in_shape, F32),
                        pltpu.VMEM((2,) + w_out_shape, F32), pltpu.SemaphoreType.DMA((2, 3))],
    )
    return pl.pallas_call(
        _moe_kernel,
        grid_spec=grid_spec,
        out_shape=jax.ShapeDtypeStruct((n_rows * CHUNKS, LANES), F32),
        compiler_params=pltpu.CompilerParams(dimension_semantics=("arbitrary",), vmem_limit_bytes=VMEM_LIMIT),
        name="moe_sorted",
    )(tlo, thi, valid, h_sorted, w_gate, w_up, w_down, w_r, b_r)


def _final_kernel(pos_ref, posn_ref, x1_ref, g2_ref, lg_ref, lb_ref, ys_ref, o_ref, ybuf, sems):
    nsteps = pl.num_programs(0) * pl.num_programs(1)
    step = pl.program_id(0) * pl.num_programs(1) + pl.program_id(1)
    slot = step % 2

    def gather(rows_ref, sl):
        def issue(i, c):
            for u in range(SUBLANES):
                r = i * SUBLANES + u
                pltpu.make_async_copy(ys_ref.at[_tok_rows(rows_ref[0, r])], ybuf.at[sl, _tok_rows(r)],
                                      sems.at[sl]).start(priority=u % 2)
            return c

        lax.fori_loop(0, TS // SUBLANES, issue, 0)

    @pl.when(step == 0)
    def _():
        gather(pos_ref, slot)

    @pl.when(step + 1 < nsteps)
    def _():
        gather(posn_ref, 1 - slot)

    _wait_tokens(ys_ref, TS, sems.at[slot])
    y = _from_tiles(ybuf.at[slot], TS)
    o_ref[...] = _layer_norm(ALPHA * x1_ref[...] + g2_ref[...] * y, lg_ref[...], lb_ref[...])


def _unsort_ln2(pos, x1, g2, ln_g, ln_b, y_sorted):
    bsz, seq, _ = x1.shape
    nst = seq // TS
    last = bsz * nst - 1
    return pl.pallas_call(
        _final_kernel,
        grid=(bsz, nst),
        in_specs=[pl.BlockSpec((None, 1, TS), lambda b, s: (b * nst + s, 0, 0), memory_space=pltpu.SMEM),
                  pl.BlockSpec((None, 1, TS), lambda b, s: (jnp.minimum(b * nst + s + 1, last), 0, 0),
                               memory_space=pltpu.SMEM),
                  pl.BlockSpec((None, TS, D_MODEL), lambda b, s: (b, s, 0)),
                  pl.BlockSpec((None, None, 1, D_MODEL), lambda b, s: (b, 0, 0, 0)),
                  pl.BlockSpec((1, D_MODEL), lambda b, s: (0, 0)),
                  pl.BlockSpec((1, D_MODEL), lambda b, s: (0, 0)),
                  pl.BlockSpec(memory_space=pl.ANY)],
        out_specs=pl.BlockSpec((None, TS, D_MODEL), lambda b, s: (b, s, 0)),
        out_shape=jax.ShapeDtypeStruct((bsz, seq, D_MODEL), F32),
        scratch_shapes=[pltpu.VMEM((2, TS * CHUNKS, LANES), F32), pltpu.SemaphoreType.DMA((2,))],
        compiler_params=pltpu.CompilerParams(dimension_semantics=("arbitrary", "arbitrary")),
        name="unsort_ln2",
    )(pos, pos, x1, g2, ln_g, ln_b, y_sorted)


def _pair_layout(a, axis):
    shp = a.shape
    a = a.reshape(shp[:axis] + (N_KV_HEADS, N_HEADS // N_KV_HEADS, HEAD_DIM) + shp[axis + 1:])
    return jnp.swapaxes(a, axis, axis + 1).reshape(shp)


def _tile_tables(counts, n_tiles_max):
    b_np = np.arange(N_BUCKETS)
    g_np = b_np // N_PAIRS
    in_group_before = jnp.asarray((g_np[None, :] == g_np[:, None]) & (b_np[None, :] < b_np[:, None]))
    member = jnp.asarray(g_np[None, :] == np.arange(N_GROUPS)[:, None])
    earlier_group = jnp.asarray(np.arange(N_GROUPS)[None, :] < np.arange(N_GROUPS)[:, None])
    e_lo = jnp.asarray(g_np * EXPERTS_PER_GROUP + np.asarray(PAIRS)[b_np % N_PAIRS, 0], jnp.int32)
    e_hi = jnp.asarray(g_np * EXPERTS_PER_GROUP + np.asarray(PAIRS)[b_np % N_PAIRS, 1], jnp.int32)

    n_tile_b = jnp.floor((counts + (TMM - 1.0)) * (1.0 / TMM))
    g_tiles = jnp.sum(jnp.where(member, n_tile_b[None, :], 0.0), axis=1)
    g_padded = jnp.floor((g_tiles + (NSUB - 1.0)) * (1.0 / NSUB)) * NSUB
    g_start = jnp.sum(jnp.where(earlier_group, g_padded[None, :], 0.0), axis=1)
    g_end = g_start + g_padded
    b_start = (jnp.sum(jnp.where(member.T, g_start[None, :], 0.0), axis=1)
               + jnp.sum(jnp.where(in_group_before, n_tile_b[None, :], 0.0), axis=1))
    b_end = b_start + n_tile_b
    starts = (b_start * TMM).astype(jnp.int32)
    zrow = jnp.where(n_tile_b > 0, (b_end - 1.0) * TMM, -1.0).astype(jnp.int32)

    t = jnp.arange(n_tiles_max, dtype=jnp.int32).astype(F32)
    g_of_t = jnp.minimum(jnp.sum(jnp.where(g_end[None, :] <= t[:, None], 1, 0), axis=1), N_GROUPS - 1)
    g_hot = g_of_t[:, None] == jnp.arange(N_GROUPS, dtype=jnp.int32)[None, :]
    local_t = t - jnp.sum(jnp.where(g_hot, g_start[None, :], 0.0), axis=1)
    valid = (local_t < jnp.sum(jnp.where(g_hot, g_tiles[None, :], 0.0), axis=1)) & (t < jnp.sum(g_padded))
    bucket = jnp.sum(jnp.where(b_end[None, :] <= t[:, None], 1, 0), axis=1)
    bucket = jnp.where(valid, bucket, g_of_t * N_PAIRS + (N_PAIRS - 1))
    b_hot = bucket[:, None] == jnp.asarray(b_np, jnp.int32)[None, :]
    tlo = jnp.sum(jnp.where(b_hot, e_lo[None, :], 0), axis=1)
    thi = jnp.sum(jnp.where(b_hot, e_hi[None, :], 0), axis=1)
    pad = jnp.zeros((LANES - N_BUCKETS,), jnp.int32)
    return (jnp.concatenate([starts, pad]), jnp.concatenate([zrow, pad - 1]), tlo, thi,
            jnp.where(valid, 1, 0).astype(jnp.int32))


def kernel(x, c, w_ada, b_ada, w_in, b_in, conv_w, conv_b, conv_ln_g, conv_ln_b, conv_out_g, sinks,
           attn_out_g, w_out, b_out, ln1_g, ln1_b, w_router_group, b_router_group, w_router_expert,
           b_router_expert, w_gate, w_up, w_down, ln2_g, ln2_b):
    assert w_ada.shape[0] == DEPTH
    bsz, seq, _ = x.shape
    row = lambda v: v[0][None, :]

    mod = _modulation(c, w_ada[0], row(b_ada)).reshape(bsz, 6, 1, D_MODEL)
    sh1, sc1, g1, sh2, sc2, g2 = [mod[:, j:j + 1] for j in range(6)]

    q0 = 2 * D_CONV
    w_q = _pair_layout(w_in[0][:, q0:q0 + D_ATTN], 1)
    b_q = _pair_layout(b_in[0][q0:q0 + D_ATTN], 0)[None, :]
    og_attn = _pair_layout(attn_out_g[0], 0)[None, :]
    w_o = jnp.concatenate([w_out[0][:D_CONV], _pair_layout(w_out[0][D_CONV:], 0)], axis=0)

    y_conv, q, kv = _inproj_conv(x, sh1, sc1, w_in[0], row(b_in), w_q, b_q, conv_w[0], row(conv_b),
                                 row(conv_ln_g), row(conv_ln_b), row(conv_out_g))
    y_attn = _attention(q, kv, sinks[0], og_attn)

    pad = ROUTER_COLS - N_EXPERTS - N_GROUPS
    w_r = jnp.concatenate([w_router_expert[0], w_router_group[0], jnp.zeros((D_MODEL, pad), F32)], axis=1)
    b_r = jnp.concatenate([b_router_expert[0], b_router_group[0], jnp.zeros((pad,), F32)])[None, :]
    x1, h2t, code, counts = _outproj_route(x, y_conv, y_attn, w_o, row(b_out), g1, row(ln1_g), row(ln1_b),
                                      sc2, sh2, w_r, b_r)

    n_tiles_max = (bsz * seq) // TMM + N_BUCKETS + N_GROUPS * (NSUB - 1)
    n_tiles_max = -(-n_tiles_max // NSUB) * NSUB
    starts, zrow, tlo, thi, valid = _tile_tables(counts[0, :N_BUCKETS], n_tiles_max)
    h_sorted, pos = _scatter_rows(starts, zrow, valid, code[:, :, 0], h2t, n_tiles_max * TMM)
    y_sorted = _moe_sorted(tlo, thi, valid, h_sorted, w_gate[0], w_up[0], w_down[0], w_r, b_r)
    return _unsort_ln2(pos, x1, g2, row(ln2_g), row(ln2_b), y_sorted)
```

```python
import jax
import jax.numpy as jnp
import numpy as np
from jax import lax
from jax.experimental import pallas as pl
from jax.experimental.pallas import tpu as pltpu

F32 = jnp.float32
BF16 = jnp.bfloat16

D_MODEL = 1024
D_CONV = 512
CONV_WIDTH = 31
GROUP_DIM = 64
N_HEADS = 8
N_KV_HEADS = 2
HEAD_DIM = 64
D_ATTN = N_HEADS * HEAD_DIM
D_KV = N_KV_HEADS * HEAD_DIM
BLOCK = 128
D_IN = 2 * D_CONV + D_ATTN + 2 * D_KV
N_GROUPS = 4
EXPERTS_PER_GROUP = 8
N_EXPERTS = N_GROUPS * EXPERTS_PER_GROUP
D_EXPERT = D_MODEL // 4
DEPTH = 1
ALPHA = (2.0 * DEPTH) ** 0.25
EPS = 1e-5
NEG = -1e30

LANES = 128
SUBLANES = 8
TS = 512
HALO = 32
CR = 64
AB = 4
PAIR_SLOTS = [j + (N_HEADS // N_KV_HEADS) * g for j in range(N_HEADS // N_KV_HEADS) for g in range(N_KV_HEADS)]
ROUTER_COLS = LANES
VMEM_LIMIT = 56 * 1024 * 1024

PAIRS = [(lo, hi) for lo in range(EXPERTS_PER_GROUP) for hi in range(lo + 1, EXPERTS_PER_GROUP)]
N_PAIRS = len(PAIRS)
N_BUCKETS = N_GROUPS * N_PAIRS
TMM = 128
OUT_SLICES = 2
NSUB = 4
CODE_SHIFT = 16


def _dot(a, b):
    return jnp.dot(a, b, preferred_element_type=F32)


def _dot_nt(a, b):
    return lax.dot_general(a, b, (((1,), (1,)), ((), ())), preferred_element_type=F32)


def _fill_group_maps(bsel_ref, bexp_ref):
    c = bsel_ref.shape[0]
    ch = lax.broadcasted_iota(jnp.int32, (c, LANES), 0) // GROUP_DIM
    gi = lax.broadcasted_iota(jnp.int32, (c, LANES), 1)
    bsel_ref[...] = jnp.where(ch == gi, 1.0 / GROUP_DIM, 0.0).astype(BF16)
    gi2 = lax.broadcasted_iota(jnp.int32, (LANES, c), 0)
    ch2 = lax.broadcasted_iota(jnp.int32, (LANES, c), 1) // GROUP_DIM
    bexp_ref[...] = jnp.where(ch2 == gi2, 1.0, 0.0).astype(BF16)


def _group_rms(y, bsel, bexp):
    ms = _dot((y * y).astype(BF16), bsel)
    r = lax.rsqrt(ms + EPS)
    r_hi = r.astype(BF16)
    r_lo = (r - r_hi.astype(F32)).astype(BF16)
    return y * (_dot(r_hi, bexp) + _dot(r_lo, bexp))


def _layer_norm(y, g, b):
    mu = jnp.mean(y, axis=-1, keepdims=True)
    d = y - mu
    var = jnp.mean(d * d, axis=-1, keepdims=True)
    return d * lax.rsqrt(var + EPS) * g + b


def _mod_kernel(c_ref, w_ref, b_ref, o_ref):
    c = c_ref[...]
    c_act = c * jax.nn.sigmoid(c)
    o_ref[...] = jnp.dot(c_act, w_ref[...], preferred_element_type=F32,
                         precision=lax.Precision.HIGHEST) + b_ref[...]


def _modulation(c, w_ada, b_ada):
    bsz = c.shape[0]
    n = w_ada.shape[1]
    return pl.pallas_call(
        _mod_kernel,
        grid=(n // D_MODEL,),
        in_specs=[pl.BlockSpec((bsz, D_MODEL), lambda j: (0, 0)),
                  pl.BlockSpec((D_MODEL, D_MODEL), lambda j: (0, j)),
                  pl.BlockSpec((1, D_MODEL), lambda j: (0, j))],
        out_specs=pl.BlockSpec((bsz, D_MODEL), lambda j: (0, j)),
        out_shape=jax.ShapeDtypeStruct((bsz, n), F32),
        name="adaln_mod",
    )(c, w_ada, b_ada)


def _inproj_conv_kernel(x_ref, sh_ref, sc_ref, w_ref, b_ref, wq_ref, bq_ref, cw_ref, cb_ref, lg_ref, lb_ref, og_ref,
                        yc_ref, q_ref, kv_ref, wbf, gs, cacc, wb, bsel, bexp):
    first = (pl.program_id(0) == 0) & (pl.program_id(1) == 0)
    s = pl.program_id(1)

    @pl.when(first)
    def _():
        wbf[...] = w_ref[...].astype(BF16)
        wbf[:, 2 * D_CONV:2 * D_CONV + D_ATTN] = wq_ref[...].astype(BF16)
        wb[...] = jnp.broadcast_to(cw_ref[...][:, None, :], wb.shape)
        _fill_group_maps(bsel, bexp)

    n_ct = D_CONV // LANES

    @pl.when(s == 0)
    def _():
        gs[0, :, 0:HALO, :] = jnp.zeros((n_ct, HALO, LANES), F32)

    @pl.when(s > 0)
    def _():
        gs[0, :, 0:HALO, :] = gs[0, :, TS:TS + HALO, :]

    h = (x_ref[...] * (1.0 + sc_ref[...]) + sh_ref[...]).astype(BF16)
    u_a = _dot(h, wbf[:, 0:D_CONV]) + b_ref[:, 0:D_CONV]
    u_b = _dot(h, wbf[:, D_CONV:2 * D_CONV]) + b_ref[:, D_CONV:2 * D_CONV]
    glu = u_a * jax.nn.sigmoid(u_b)
    for c in range(n_ct):
        gs[0, c, HALO:HALO + TS, :] = glu[:, c * LANES:(c + 1) * LANES]
    q0 = 2 * D_CONV
    k0 = q0 + D_ATTN

    n_sh = TS + HALO - SUBLANES
    for j in range(1, SUBLANES):
        gs[j, :, 0:n_sh, :] = gs[0, :, j:j + n_sh, :]

    cb = cb_ref[...]
    lg = lg_ref[...]
    lb = lb_ref[...]
    og = og_ref[...]

    late_cols = 2 * LANES
    n_late = (D_ATTN + 2 * D_KV) // late_cols

    def late_proj(j):
        c0 = j * late_cols
        if c0 < D_ATTN:
            q = _dot(h, wbf[:, q0 + c0:q0 + c0 + late_cols]) + bq_ref[:, c0:c0 + late_cols]
            q_ref[:, c0:c0 + late_cols] = (q * (HEAD_DIM ** -0.5)).astype(BF16)
        else:
            kv_ref[...] = (_dot(h, wbf[:, k0:k0 + 2 * D_KV]) + b_ref[:, k0:k0 + 2 * D_KV]).astype(BF16)

    n_chunks = n_ct * (TS // CR)
    every = n_chunks // n_late
    for n in range(n_chunks):
        c, r0 = n // (TS // CR), (n % (TS // CR)) * CR
        ls = slice(c * LANES, (c + 1) * LANES)
        acc = jnp.broadcast_to(cb[:, ls], (CR, LANES)).reshape(CR // SUBLANES, SUBLANES, LANES)
        for k in range(CONV_WIDTH):
            off = HALO - (CONV_WIDTH - 1) + k
            a0 = r0 + (off // SUBLANES) * SUBLANES
            seg = gs[off % SUBLANES, c, a0:a0 + CR, :]
            acc = acc + wb[k, :, ls][None] * seg.reshape(CR // SUBLANES, SUBLANES, LANES)
        cacc[c, r0:r0 + CR, :] = acc.reshape(CR, LANES)
        if n % every == 0 and n // every < n_late:
            late_proj(n // every)
    y = _layer_norm(jnp.concatenate([cacc[c] for c in range(n_ct)], axis=1), lg, lb)
    y = y * jax.nn.sigmoid(y)
    yc_ref[...] = (_group_rms(y, bsel[...], bexp[...]) * og).astype(BF16)


def _inproj_conv(x, sh1, sc1, w_in, b_in, w_q, b_q, conv_w, conv_b, ln_g, ln_b, out_g):
    bsz, seq, _ = x.shape
    vec = lambda n: pl.BlockSpec((1, n), lambda b, s: (0, 0))
    mod = pl.BlockSpec((None, None, 1, D_MODEL), lambda b, s: (b, 0, 0, 0))
    tile = lambda n: pl.BlockSpec((None, TS, n), lambda b, s: (b, s, 0))
    return pl.pallas_call(
        _inproj_conv_kernel,
        grid=(bsz, seq // TS),
        in_specs=[tile(D_MODEL), mod, mod,
                  pl.BlockSpec((D_MODEL, D_IN), lambda b, s: (0, 0)), vec(D_IN),
                  pl.BlockSpec((D_MODEL, D_ATTN), lambda b, s: (0, 0)), vec(D_ATTN),
                  pl.BlockSpec((CONV_WIDTH, D_CONV), lambda b, s: (0, 0)),
                  vec(D_CONV), vec(D_CONV), vec(D_CONV), vec(D_CONV)],
        out_specs=[tile(D_CONV), tile(D_ATTN), tile(2 * D_KV)],
        out_shape=[jax.ShapeDtypeStruct((bsz, seq, D_CONV), BF16),
                   jax.ShapeDtypeStruct((bsz, seq, D_ATTN), BF16),
                   jax.ShapeDtypeStruct((bsz, seq, 2 * D_KV), BF16)],
        scratch_shapes=[pltpu.VMEM((D_MODEL, D_IN), BF16),
                        pltpu.VMEM((SUBLANES, D_CONV // LANES, TS + HALO, LANES), F32),
                        pltpu.VMEM((D_CONV // LANES, TS, LANES), F32),
                        pltpu.VMEM((CONV_WIDTH, SUBLANES, D_CONV), F32),
                        pltpu.VMEM((D_CONV, LANES), BF16),
                        pltpu.VMEM((LANES, D_CONV), BF16)],
        compiler_params=pltpu.CompilerParams(
            dimension_semantics=("arbitrary", "arbitrary"), vmem_limit_bytes=VMEM_LIMIT),
        name="inproj_conv",
    )(x, sh1, sc1, w_in, b_in, w_q, b_q, conv_w, conv_b, ln_g, ln_b, out_g)


def _attn_kernel(sinks_ref, q_ref, kvc_ref, kvp_ref, og_ref, o_ref, bsel, bexp):
    n = pl.program_id(1)

    @pl.when((pl.program_id(0) == 0) & (n == 0))
    def _():
        _fill_group_maps(bsel, bexp)

    qr = lax.broadcasted_iota(jnp.int32, (BLOCK, 2 * BLOCK), 0)
    kc = lax.broadcasted_iota(jnp.int32, (BLOCK, 2 * BLOCK), 1)
    band = (kc > qr) & (kc <= qr + BLOCK)
    lane = lax.broadcasted_iota(jnp.int32, (BLOCK, LANES), 1)
    lower = lane < HEAD_DIM
    lower_bf = jnp.where(lower, 1.0, 0.0).astype(BF16)
    upper_bf = jnp.where(lower, 0.0, 1.0).astype(BF16)
    og = og_ref[...]
    blocks = range(AB)
    kvs, masks = [], []
    for blk in blocks:
        r0 = blk * BLOCK
        if blk == 0:
            kvs.append(jnp.concatenate([kvp_ref[...], kvc_ref[0:BLOCK, :]], axis=0))
            masks.append(band & ((n > 0) | (kc >= BLOCK)))
        else:
            kvs.append(kvc_ref[r0 - BLOCK:r0 + BLOCK, :])
            masks.append(band)
    s_all = []
    for blk in blocks:
        r0 = blk * BLOCK
        parts = []
        for j in range(D_ATTN // LANES):
            col = q_ref[r0:r0 + BLOCK, j * LANES:(j + 1) * LANES]
            parts += [col * lower_bf, col * upper_bf]
        s_all.append(_dot_nt(jnp.concatenate(parts, axis=0), kvs[blk][:, 0:D_KV]))
    probs, dens = [], []
    for blk in blocks:
        pb = []
        den = jnp.ones((BLOCK, LANES), F32)
        for slot in range(N_HEADS):
            sc = jnp.where(masks[blk], s_all[blk][slot * BLOCK:(slot + 1) * BLOCK], NEG)
            sink = sinks_ref[PAIR_SLOTS[slot]]
            m = jnp.maximum(jnp.max(sc, axis=-1, keepdims=True), sink)
            p = jnp.exp(sc - m)
            den = jnp.where(lane == slot, jnp.sum(p, axis=-1, keepdims=True) + jnp.exp(sink - m), den)
            pb.append(p.astype(BF16))
        probs.append(jnp.concatenate(pb, axis=0))
        dens.append(den)
    pvs = [_dot(probs[blk], kvs[blk][:, D_KV:2 * D_KV]) for blk in blocks]
    for blk in blocks:
        pv, den = pvs[blk], dens[blk]
        o = jnp.concatenate(
            [jnp.where(lower, pv[(2 * j) * BLOCK:(2 * j + 1) * BLOCK], pv[(2 * j + 1) * BLOCK:(2 * j + 2) * BLOCK])
             for j in range(D_ATTN // LANES)], axis=1)
        ms = _dot((o * o).astype(BF16), bsel[...])
        r = lax.rsqrt(ms + EPS * den * den)
        r_hi = r.astype(BF16)
        r_lo = (r - r_hi.astype(F32)).astype(BF16)
        scale = _dot(r_hi, bexp[...]) + _dot(r_lo, bexp[...])
        o_ref[blk * BLOCK:(blk + 1) * BLOCK, :] = (o * scale * og).astype(BF16)


def _attention(q, kv, sinks, out_g):
    bsz, seq, _ = q.shape
    rows = AB * BLOCK
    grid_spec = pltpu.PrefetchScalarGridSpec(
        num_scalar_prefetch=1,
        grid=(bsz, seq // rows),
        in_specs=[pl.BlockSpec((None, rows, D_ATTN), lambda b, n, sk: (b, n, 0)),
                  pl.BlockSpec((None, rows, 2 * D_KV), lambda b, n, sk: (b, n, 0)),
                  pl.BlockSpec((None, BLOCK, 2 * D_KV), lambda b, n, sk: (b, jnp.maximum(AB * n - 1, 0), 0)),
                  pl.BlockSpec((1, D_ATTN), lambda b, n, sk: (0, 0))],
        out_specs=pl.BlockSpec((None, rows, D_ATTN), lambda b, n, sk: (b, n, 0)),
        scratch_shapes=[pltpu.VMEM((D_ATTN, LANES), BF16), pltpu.VMEM((LANES, D_ATTN), BF16)],
    )
    return pl.pallas_call(
        _attn_kernel,
        grid_spec=grid_spec,
        out_shape=jax.ShapeDtypeStruct((bsz, seq, D_ATTN), BF16),
        compiler_params=pltpu.CompilerParams(dimension_semantics=("arbitrary", "arbitrary")),
        name="swa_attention",
    )(sinks, q, kv, kv, out_g)


def _group_softmax_top(logits, lane):
    is_g = (lane >= N_EXPERTS) & (lane < N_EXPERTS + N_GROUPS)
    gl = jnp.where(is_g, logits, NEG)
    gmax = jnp.max(gl, axis=-1, keepdims=True)
    p_top = 1.0 / jnp.sum(jnp.where(is_g, jnp.exp(gl - gmax), 0.0), axis=-1, keepdims=True)
    return is_g, gl, gmax, p_top


def _route_bucket(logits):
    rows = logits.shape[0]
    lane = lax.broadcasted_iota(jnp.int32, (rows, ROUTER_COLS), 1)
    lanef = lane.astype(F32)
    is_g, gl, gmax, _ = _group_softmax_top(logits, lane)
    gidx = jnp.min(jnp.where(is_g & (gl == gmax), lanef - N_EXPERTS, 99.0), axis=-1, keepdims=True)
    in_grp = (lane < N_EXPERTS) & ((lane // EXPERTS_PER_GROUP) == gidx.astype(jnp.int32))
    el = jnp.where(in_grp, logits, NEG)
    m1 = jnp.max(el, axis=-1, keepdims=True)
    i1 = jnp.min(jnp.where(in_grp & (el == m1), lanef, 999.0), axis=-1, keepdims=True)
    rest = in_grp & (lanef != i1)
    el2 = jnp.where(rest, logits, NEG)
    m2 = jnp.max(el2, axis=-1, keepdims=True)
    i2 = jnp.min(jnp.where(rest & (el2 == m2), lanef, 999.0), axis=-1, keepdims=True)
    lo = jnp.minimum(i1, i2) - EXPERTS_PER_GROUP * gidx
    hi = jnp.maximum(i1, i2) - EXPERTS_PER_GROUP * gidx
    pair = lo * (EXPERTS_PER_GROUP - 1) - lo * (lo - 1.0) * 0.5 + (hi - lo - 1.0)
    return gidx * N_PAIRS + pair, lanef


def _outproj_kernel(x_ref, yc_ref, ya_ref, w_ref, b_ref, g1_ref, lg_ref, lb_ref, sc2_ref, sh2_ref,
                    wr_ref, br_ref, x1_ref, h2t_ref, code_ref, counts_ref, wbf, wrbf, ltri, running):
    @pl.when((pl.program_id(0) == 0) & (pl.program_id(1) == 0))
    def _():
        wbf[...] = w_ref[...].astype(BF16)
        wrbf[...] = wr_ref[...].astype(BF16)
        r = lax.broadcasted_iota(jnp.int32, (TS, TS), 0)
        c = lax.broadcasted_iota(jnp.int32, (TS, TS), 1)
        ltri[...] = jnp.where(c < r, 1.0, 0.0).astype(BF16)
        running[...] = jnp.zeros_like(running)

    nsl = OUT_SLICES
    rows = TS // nsl
    sl = [slice(k * rows, (k + 1) * rows) for k in range(nsl)]
    mix = [_dot(yc_ref[r, :], wbf[0:D_CONV, :]) + _dot(ya_ref[r, :], wbf[D_CONV:, :]) + b_ref[...] for r in sl]
    x1 = [_layer_norm(ALPHA * x_ref[r, :] + g1_ref[...] * m, lg_ref[...], lb_ref[...]) for r, m in zip(sl, mix)]
    h2 = [v * (1.0 + sc2_ref[...]) + sh2_ref[...] for v in x1]
    for k in range(nsl):
        x1_ref[sl[k], :] = x1[k]
        _to_tiles(h2t_ref.at[pl.ds(k * rows * CHUNKS, rows * CHUNKS)], h2[k])
    logits = [_dot(v.astype(BF16), wrbf[...]) + br_ref[...] for v in h2]
    routed = [_route_bucket(lg) for lg in logits]
    mine = [lanef == bucket for bucket, lanef in routed]
    onehot = [jnp.where(mk, 1.0, 0.0) for mk in mine]
    within = [_dot(ltri[0:rows, 0:rows], oh.astype(BF16)) for oh in onehot]
    base = running[...]
    for k in range(nsl):
        rank = jnp.sum(jnp.where(mine[k], within[k] + base, 0.0), axis=-1, keepdims=True)
        base = base + jnp.sum(onehot[k], axis=0, keepdims=True)
        code = routed[k][0].astype(jnp.int32) * (1 << CODE_SHIFT) + rank.astype(jnp.int32)
        code_ref[sl[k], :] = jnp.broadcast_to(code, (rows, LANES))
    running[...] = base
    counts_ref[...] = running[...]


def _outproj_route(x, y_conv, y_attn, w_out, b_out, g1, ln_g, ln_b, sc2, sh2, w_r, b_r):
    bsz, seq, _ = x.shape
    nst = seq // TS
    vec = lambda n: pl.BlockSpec((1, n), lambda b, s: (0, 0))
    mod = pl.BlockSpec((None, None, 1, D_MODEL), lambda b, s: (b, 0, 0, 0))
    tile = lambda n: pl.BlockSpec((None, TS, n), lambda b, s: (b, s, 0))
    return pl.pallas_call(
        _outproj_kernel,
        grid=(bsz, seq // TS),
        in_specs=[tile(D_MODEL), tile(D_CONV), tile(D_ATTN),
                  pl.BlockSpec((D_MODEL, D_MODEL), lambda b, s: (0, 0)), vec(D_MODEL),
                  mod, vec(D_MODEL), vec(D_MODEL), mod, mod,
                  pl.BlockSpec((D_MODEL, ROUTER_COLS), lambda b, s: (0, 0)), vec(ROUTER_COLS)],
        out_specs=[tile(D_MODEL), pl.BlockSpec((TS * CHUNKS, LANES), lambda b, s: (b * nst + s, 0)),
                   tile(LANES), vec(LANES)],
        out_shape=[jax.ShapeDtypeStruct((bsz, seq, D_MODEL), F32),
                   jax.ShapeDtypeStruct((bsz * seq * CHUNKS, LANES), F32),
                   jax.ShapeDtypeStruct((bsz, seq, LANES), jnp.int32),
                   jax.ShapeDtypeStruct((1, LANES), F32)],
        scratch_shapes=[pltpu.VMEM((D_MODEL, D_MODEL), BF16), pltpu.VMEM((D_MODEL, ROUTER_COLS), BF16),
                        pltpu.VMEM((TS, TS), BF16), pltpu.VMEM((1, LANES), F32)],
        compiler_params=pltpu.CompilerParams(
            dimension_semantics=("arbitrary", "arbitrary"), vmem_limit_bytes=VMEM_LIMIT),
        name="outproj_route",
    )(x, y_conv, y_attn, w_out, b_out, g1, ln_g, ln_b, sc2, sh2, w_r, b_r)


def _row_of(starts_ref, code):
    return (starts_ref[lax.shift_right_logical(code, jnp.int32(CODE_SHIFT))]
            + (code & ((1 << CODE_SHIFT) - 1)))


CHUNKS = D_MODEL // LANES
assert CHUNKS == SUBLANES


def _tok_rows(p):
    return pl.ds(pl.multiple_of(p * CHUNKS, CHUNKS), CHUNKS)


def _to_tiles(ref, x):
    for c in range(CHUNKS):
        ref[pl.ds(c, x.shape[0], stride=CHUNKS), :] = x[:, c * LANES:(c + 1) * LANES]


def _from_tiles(ref, n):
    return jnp.concatenate([ref[pl.ds(c, n, stride=CHUNKS), :] for c in range(CHUNKS)], axis=1)


def _wait_tokens(hbm_ref, n, sem):
    pltpu.make_async_copy(hbm_ref.at[pl.ds(0, n * CHUNKS)], hbm_ref.at[pl.ds(0, n * CHUNKS)], sem).wait()


def _scatter_kernel(starts_ref, zrow_ref, valid_ref, code_ref, h2t_ref, hs_ref, pos_ref, hbuf, zeros, sems, zsem):
    nsteps = pl.num_programs(0) * pl.num_programs(1)
    step = pl.program_id(0) * pl.num_programs(1) + pl.program_id(1)
    slot = step % 2

    def _tile_rows(tok):
        return pl.ds(pl.multiple_of(tok * CHUNKS, TMM * CHUNKS), TMM * CHUNKS)

    def zero_copy(b):
        return pltpu.make_async_copy(zeros, hs_ref.at[_tile_rows(zrow_ref[b])], zsem)

    def tail_copy(t):
        return pltpu.make_async_copy(zeros, hs_ref.at[_tile_rows(t * TMM)], zsem)

    @pl.when(step == 0)
    def _():
        zeros[...] = jnp.zeros_like(zeros)
        n_tiles = hs_ref.shape[0] // (TMM * CHUNKS)

        def start(b, c):
            @pl.when(zrow_ref[b] >= 0)
            def _():
                zero_copy(b).start()
            return c

        def wait(b, c):
            @pl.when(zrow_ref[b] >= 0)
            def _():
                zero_copy(b).wait()
            return c

        def start_tail(t, c):
            @pl.when(valid_ref[t] == 0)
            def _():
                tail_copy(t).start()
            return c

        def wait_tail(t, c):
            @pl.when(valid_ref[t] == 0)
            def _():
                tail_copy(t).wait()
            return c

        lax.fori_loop(0, N_BUCKETS, start, 0)
        lax.fori_loop(0, n_tiles, start_tail, 0)
        lax.fori_loop(0, N_BUCKETS, wait, 0)
        lax.fori_loop(0, n_tiles, wait_tail, 0)

    @pl.when(step >= 2)
    def _():
        _wait_tokens(hs_ref, TS, sems.at[slot])

    hbuf[slot] = h2t_ref[...]

    def issue(i, c):
        for u in range(SUBLANES):
            r = i * SUBLANES + u
            pos = _row_of(starts_ref, code_ref[0, r])
            pos_ref[0, r] = pos
            pltpu.make_async_copy(hbuf.at[slot, _tok_rows(r)], hs_ref.at[_tok_rows(pos)],
                                  sems.at[slot]).start(priority=u % 2)
        return c

    lax.fori_loop(0, TS // SUBLANES, issue, 0)

    @pl.when(step == nsteps - 1)
    def _():
        _wait_tokens(hs_ref, TS, sems.at[slot])
        _wait_tokens(hs_ref, TS, sems.at[1 - slot])


def _scatter_rows(starts, zrow, valid, code, h2t, n_rows):
    bsz, seq = code.shape
    nst = seq // TS
    grid_spec = pltpu.PrefetchScalarGridSpec(
        num_scalar_prefetch=3,
        grid=(bsz, nst),
        in_specs=[pl.BlockSpec((None, 1, TS), lambda b, s, *_: (b * nst + s, 0, 0), memory_space=pltpu.SMEM),
                  pl.BlockSpec((TS * CHUNKS, LANES), lambda b, s, *_: (b * nst + s, 0))],
        out_specs=[pl.BlockSpec(memory_space=pl.ANY),
                   pl.BlockSpec((None, 1, TS), lambda b, s, *_: (b * nst + s, 0, 0), memory_space=pltpu.SMEM)],
        scratch_shapes=[pltpu.VMEM((2, TS * CHUNKS, LANES), F32), pltpu.VMEM((TMM * CHUNKS, LANES), F32),
                        pltpu.SemaphoreType.DMA((2,)), pltpu.SemaphoreType.DMA],
    )
    return pl.pallas_call(
        _scatter_kernel,
        grid_spec=grid_spec,
        out_shape=[jax.ShapeDtypeStruct((n_rows * CHUNKS, LANES), F32),
                   jax.ShapeDtypeStruct((bsz * nst, 1, TS), jnp.int32)],
        compiler_params=pltpu.CompilerParams(dimension_semantics=("arbitrary", "arbitrary")),
        name="moe_scatter",
    )(starts, zrow, valid, code.reshape(bsz * nst, 1, TS), h2t)


def _moe_kernel(tlo_ref, thi_ref, valid_ref, h_ref, wg_hbm, wu_hbm, wd_hbm, wr_ref, br_ref, y_ref,
                wrbf, wg, wu, wd, sg, su, sd, sems):
    i = pl.program_id(0)
    t0 = i * NSUB
    group = lax.shift_right_logical(tlo_ref[t0], jnp.int32(3))
    prev_group = lax.shift_right_logical(tlo_ref[jnp.maximum(t0 - NSUB, 0)], jnp.int32(3))

    @pl.when(i == 0)
    def _():
        wrbf[...] = wr_ref[...].astype(BF16)

    @pl.when((i == 0) | (group != prev_group))
    def _():
        def copies(e, slot):
            ex = group * EXPERTS_PER_GROUP + e
            return (pltpu.make_async_copy(wg_hbm.at[ex], sg.at[slot], sems.at[slot, 0]),
                    pltpu.make_async_copy(wu_hbm.at[ex], su.at[slot], sems.at[slot, 1]),
                    pltpu.make_async_copy(wd_hbm.at[ex], sd.at[slot], sems.at[slot, 2]))

        for cp in copies(0, 0):
            cp.start()

        def land(e, c):
            slot = e % 2

            @pl.when(e + 1 < EXPERTS_PER_GROUP)
            def _():
                for cp in copies(e + 1, 1 - slot):
                    cp.start()

            for cp in copies(e, slot):
                cp.wait()
            wg[e] = sg[slot].astype(BF16)
            wu[e] = su[slot].astype(BF16)
            wd[e] = sd[slot].astype(BF16)
            return c

        lax.fori_loop(0, EXPERTS_PER_GROUP, land, 0)

    @pl.when(valid_ref[t0] == 1)
    def _():
        lane = lax.broadcasted_iota(jnp.int32, (TMM, ROUTER_COLS), 1)
        local = EXPERTS_PER_GROUP - 1
        subs = range(NSUB)
        tile = lambda ref, j: ref.at[pl.ds(j * TMM * CHUNKS, TMM * CHUNKS)]
        hs = [_from_tiles(tile(h_ref, j), TMM).astype(BF16) for j in subs]
        es = [(tlo_ref[t0 + j] & local, thi_ref[t0 + j] & local) for j in subs]
        pre = [[(_dot(hs[j], wg[e]), _dot(hs[j], wu[e])) for e in es[j]] for j in subs]
        act = [[(a * jax.nn.sigmoid(a) * u).astype(BF16) for a, u in pre[j]] for j in subs]
        ys = [[_dot(act[j][k], wd[es[j][k]]) for k in (0, 1)] for j in subs]
        for j in subs:
            logits = _dot(hs[j], wrbf[...]) + br_ref[...]
            p_top = _group_softmax_top(logits, lane)[3]
            l_lo = jnp.sum(jnp.where(lane == tlo_ref[t0 + j], logits, 0.0), axis=-1, keepdims=True)
            l_hi = jnp.sum(jnp.where(lane == thi_ref[t0 + j], logits, 0.0), axis=-1, keepdims=True)
            w_lo = p_top / (1.0 + jnp.exp(l_hi - l_lo))
            w_hi = p_top / (1.0 + jnp.exp(l_lo - l_hi))
            _to_tiles(tile(y_ref, j), w_lo * ys[j][0] + w_hi * ys[j][1])

    @pl.when(valid_ref[t0] == 0)
    def _():
        y_ref[...] = jnp.zeros_like(y_ref)


def _moe_sorted(tlo, thi, valid, h_sorted, w_gate, w_up, w_down, w_r, b_r):
    n_rows = h_sorted.shape[0] // CHUNKS
    rows = pl.BlockSpec((NSUB * TMM * CHUNKS, LANES), lambda i, *_: (i, 0))
    hbm = pl.BlockSpec(memory_space=pl.ANY)
    w_in_shape, w_out_shape = (D_MODEL, D_EXPERT), (D_EXPERT, D_MODEL)
    grid_spec = pltpu.PrefetchScalarGridSpec(
        num_scalar_prefetch=3,
        grid=(n_rows // (NSUB * TMM),),
        in_specs=[rows, hbm, hbm, hbm,
                  pl.BlockSpec((D_MODEL, ROUTER_COLS), lambda i, *_: (0, 0)),
                  pl.BlockSpec((1, ROUTER_COLS), lambda i, *_: (0, 0))],
        out_specs=rows,
        scratch_shapes=[pltpu.VMEM((D_MODEL, ROUTER_COLS), BF16),
                        pltpu.VMEM((EXPERTS_PER_GROUP,) + w_in_shape, BF16),
                        pltpu.VMEM((EXPERTS_PER_GROUP,) + w_in_shape, BF16),
                        pltpu.VMEM((EXPERTS_PER_GROUP,) + w_out_shape, BF16),
                        pltpu.VMEM((2,) + w_in_shape, F32), pltpu.VMEM((2,) + w_in_shape, F32),
                        pltpu.VMEM((2,) + w_out_shape, F32), pltpu.SemaphoreType.DMA((2, 3))],
    )
    return pl.pallas_call(
        _moe_kernel,
        grid_spec=grid_spec,
        out_shape=jax.ShapeDtypeStruct((n_rows * CHUNKS, LANES), F32),
        compiler_params=pltpu.CompilerParams(dimension_semantics=("arbitrary",), vmem_limit_bytes=VMEM_LIMIT),
        name="moe_sorted",
    )(tlo, thi, valid, h_sorted, w_gate, w_up, w_down, w_r, b_r)


def _final_kernel(pos_ref, posn_ref, x1_ref, g2_ref, lg_ref, lb_ref, ys_ref, o_ref, ybuf, sems):
    nsteps = pl.num_programs(0) * pl.num_programs(1)
    step = pl.program_id(0) * pl.num_programs(1) + pl.program_id(1)
    slot = step % 2

    def gather(rows_ref, sl):
        def issue(i, c):
            for u in range(SUBLANES):
                r = i * SUBLANES + u
                pltpu.make_async_copy(ys_ref.at[_tok_rows(rows_ref[0, r])], ybuf.at[sl, _tok_rows(r)],
                                      sems.at[sl]).start(priority=u % 2)
            return c

        lax.fori_loop(0, TS // SUBLANES, issue, 0)

    @pl.when(step == 0)
    def _():
        gather(pos_ref, slot)

    @pl.when(step + 1 < nsteps)
    def _():
        gather(posn_ref, 1 - slot)

    _wait_tokens(ys_ref, TS, sems.at[slot])
    y = _from_tiles(ybuf.at[slot], TS)
    o_ref[...] = _layer_norm(ALPHA * x1_ref[...] + g2_ref[...] * y, lg_ref[...], lb_ref[...])


def _unsort_ln2(pos, x1, g2, ln_g, ln_b, y_sorted):
    bsz, seq, _ = x1.shape
    nst = seq // TS
    last = bsz * nst - 1
    return pl.pallas_call(
        _final_kernel,
        grid=(bsz, nst),
        in_specs=[pl.BlockSpec((None, 1, TS), lambda b, s: (b * nst + s, 0, 0), memory_space=pltpu.SMEM),
                  pl.BlockSpec((None, 1, TS), lambda b, s: (jnp.minimum(b * nst + s + 1, last), 0, 0),
                               memory_space=pltpu.SMEM),
                  pl.BlockSpec((None, TS, D_MODEL), lambda b, s: (b, s, 0)),
                  pl.BlockSpec((None, None, 1, D_MODEL), lambda b, s: (b, 0, 0, 0)),
                  pl.BlockSpec((1, D_MODEL), lambda b, s: (0, 0)),
                  pl.BlockSpec((1, D_MODEL), lambda b, s: (0, 0)),
                  pl.BlockSpec(memory_space=pl.ANY)],
        out_specs=pl.BlockSpec((None, TS, D_MODEL), lambda b, s: (b, s, 0)),
        out_shape=jax.ShapeDtypeStruct((bsz, seq, D_MODEL), F32),
        scratch_shapes=[pltpu.VMEM((2, TS * CHUNKS, LANES), F32), pltpu.SemaphoreType.DMA((2,))],
        compiler_params=pltpu.CompilerParams(dimension_semantics=("arbitrary", "arbitrary")),
        name="unsort_ln2",
    )(pos, pos, x1, g2, ln_g, ln_b, y_sorted)


def _pair_layout(a, axis):
    shp = a.shape
    a = a.reshape(shp[:axis] + (N_KV_HEADS, N_HEADS // N_KV_HEADS, HEAD_DIM) + shp[axis + 1:])
    return jnp.swapaxes(a, axis, axis + 1).reshape(shp)


def _tile_tables(counts, n_tiles_max):
    b_np = np.arange(N_BUCKETS)
    g_np = b_np // N_PAIRS
    in_group_before = jnp.asarray((g_np[None, :] == g_np[:, None]) & (b_np[None, :] < b_np[:, None]))
    member = jnp.asarray(g_np[None, :] == np.arange(N_GROUPS)[:, None])
    earlier_group = jnp.asarray(np.arange(N_GROUPS)[None, :] < np.arange(N_GROUPS)[:, None])
    e_lo = jnp.asarray(g_np * EXPERTS_PER_GROUP + np.asarray(PAIRS)[b_np % N_PAIRS, 0], jnp.int32)
    e_hi = jnp.asarray(g_np * EXPERTS_PER_GROUP + np.asarray(PAIRS)[b_np % N_PAIRS, 1], jnp.int32)

    n_tile_b = jnp.floor((counts + (TMM - 1.0)) * (1.0 / TMM))
    g_tiles = jnp.sum(jnp.where(member, n_tile_b[None, :], 0.0), axis=1)
    g_padded = jnp.floor((g_tiles + (NSUB - 1.0)) * (1.0 / NSUB)) * NSUB
    g_start = jnp.sum(jnp.where(earlier_group, g_padded[None, :], 0.0), axis=1)
    g_end = g_start + g_padded
    b_start = (jnp.sum(jnp.where(member.T, g_start[None, :], 0.0), axis=1)
               + jnp.sum(jnp.where(in_group_before, n_tile_b[None, :], 0.0), axis=1))
    b_end = b_start + n_tile_b
    starts = (b_start * TMM).astype(jnp.int32)
    zrow = jnp.where(n_tile_b > 0, (b_end - 1.0) * TMM, -1.0).astype(jnp.int32)

    t = jnp.arange(n_tiles_max, dtype=jnp.int32).astype(F32)
    g_of_t = jnp.minimum(jnp.sum(jnp.where(g_end[None, :] <= t[:, None], 1, 0), axis=1), N_GROUPS - 1)
    g_hot = g_of_t[:, None] == jnp.arange(N_GROUPS, dtype=jnp.int32)[None, :]
    local_t = t - jnp.sum(jnp.where(g_hot, g_start[None, :], 0.0), axis=1)
    valid = (local_t < jnp.sum(jnp.where(g_hot, g_tiles[None, :], 0.0), axis=1)) & (t < jnp.sum(g_padded))
    bucket = jnp.sum(jnp.where(b_end[None, :] <= t[:, None], 1, 0), axis=1)
    bucket = jnp.where(valid, bucket, g_of_t * N_PAIRS + (N_PAIRS - 1))
    b_hot = bucket[:, None] == jnp.asarray(b_np, jnp.int32)[None, :]
    tlo = jnp.sum(jnp.where(b_hot, e_lo[None, :], 0), axis=1)
    thi = jnp.sum(jnp.where(b_hot, e_hi[None, :], 0), axis=1)
    pad = jnp.zeros((LANES - N_BUCKETS,), jnp.int32)
    return (jnp.concatenate([starts, pad]), jnp.concatenate([zrow, pad - 1]), tlo, thi,
            jnp.where(valid, 1, 0).astype(jnp.int32))


def kernel(x, c, w_ada, b_ada, w_in, b_in, conv_w, conv_b, conv_ln_g, conv_ln_b, conv_out_g, sinks,
           attn_out_g, w_out, b_out, ln1_g, ln1_b, w_router_group, b_router_group, w_router_expert,
           b_router_expert, w_gate, w_up, w_down, ln2_g, ln2_b):
    assert w_ada.shape[0] == DEPTH
    bsz, seq, _ = x.shape
    row = lambda v: v[0][None, :]

    mod = _modulation(c, w_ada[0], row(b_ada)).reshape(bsz, 6, 1, D_MODEL)
    sh1, sc1, g1, sh2, sc2, g2 = [mod[:, j:j + 1] for j in range(6)]

    q0 = 2 * D_CONV
    w_q = _pair_layout(w_in[0][:, q0:q0 + D_ATTN], 1)
    b_q = _pair_layout(b_in[0][q0:q0 + D_ATTN], 0)[None, :]
    og_attn = _pair_layout(attn_out_g[0], 0)[None, :]
    w_o = jnp.concatenate([w_out[0][:D_CONV], _pair_layout(w_out[0][D_CONV:], 0)], axis=0)

    y_conv, q, kv = _inproj_conv(x, sh1, sc1, w_in[0], row(b_in), w_q, b_q, conv_w[0], row(conv_b),
                                 row(conv_ln_g), row(conv_ln_b), row(conv_out_g))
    y_attn = _attention(q, kv, sinks[0], og_attn)

    pad = ROUTER_COLS - N_EXPERTS - N_GROUPS
    w_r = jnp.concatenate([w_router_expert[0], w_router_group[0], jnp.zeros((D_MODEL, pad), F32)], axis=1)
    b_r = jnp.concatenate([b_router_expert[0], b_router_group[0], jnp.zeros((pad,), F32)])[None, :]
    x1, h2t, code, counts = _outproj_route(x, y_conv, y_attn, w_o, row(b_out), g1, row(ln1_g), row(ln1_b),
                                      sc2, sh2, w_r, b_r)

    n_tiles_max = (bsz * seq) // TMM + N_BUCKETS + N_GROUPS * (NSUB - 1)
    n_tiles_max = -(-n_tiles_max // NSUB) * NSUB
    starts, zrow, tlo, thi, valid = _tile_tables(counts[0, :N_BUCKETS], n_tiles_max)
    h_sorted, pos = _scatter_rows(starts, zrow, valid, code[:, :, 0], h2t, n_tiles_max * TMM)
    y_sorted = _moe_sorted(tlo, thi, valid, h_sorted, w_gate[0], w_up[0], w_down[0], w_r, b_r)
    return _unsort_ln2(pos, x1, g2, row(ln2_g), row(ln2_b), y_sorted)
```

```python
import jax
import jax.numpy as jnp
import numpy as np
from jax import lax
from jax.experimental import pallas as pl
from jax.experimental.pallas import tpu as pltpu

F32 = jnp.float32
BF16 = jnp.bfloat16

D_MODEL = 1024
D_CONV = 512
CONV_WIDTH = 31
GROUP_DIM = 64
N_HEADS = 8
N_KV_HEADS = 2
HEAD_DIM = 64
D_ATTN = N_HEADS * HEAD_DIM
D_KV = N_KV_HEADS * HEAD_DIM
BLOCK = 128
D_IN = 2 * D_CONV + D_ATTN + 2 * D_KV
N_GROUPS = 4
EXPERTS_PER_GROUP = 8
N_EXPERTS = N_GROUPS * EXPERTS_PER_GROUP
D_EXPERT = D_MODEL // 4
DEPTH = 1
ALPHA = (2.0 * DEPTH) ** 0.25
EPS = 1e-5
NEG = -1e30

LANES = 128
SUBLANES = 8
TS = 512
HALO = 32
CR = 64
AB = 8
PAIR_SLOTS = [j + (N_HEADS // N_KV_HEADS) * g for j in range(N_HEADS // N_KV_HEADS) for g in range(N_KV_HEADS)]
ROUTER_COLS = LANES
VMEM_LIMIT = 56 * 1024 * 1024

PAIRS = [(lo, hi) for lo in range(EXPERTS_PER_GROUP) for hi in range(lo + 1, EXPERTS_PER_GROUP)]
N_PAIRS = len(PAIRS)
N_BUCKETS = N_GROUPS * N_PAIRS
TMM = 128
OUT_SLICES = 2
FINAL_CHUNKS = 8
NSUB = 4
CODE_SHIFT = 16


def _dot(a, b):
    return jnp.dot(a, b, preferred_element_type=F32)


def _dot_nt(a, b):
    return lax.dot_general(a, b, (((1,), (1,)), ((), ())), preferred_element_type=F32)


def _fill_group_maps(bsel_ref, bexp_ref):
    c = bsel_ref.shape[0]
    ch = lax.broadcasted_iota(jnp.int32, (c, LANES), 0) // GROUP_DIM
    gi = lax.broadcasted_iota(jnp.int32, (c, LANES), 1)
    bsel_ref[...] = jnp.where(ch == gi, 1.0 / GROUP_DIM, 0.0).astype(BF16)
    gi2 = lax.broadcasted_iota(jnp.int32, (LANES, c), 0)
    ch2 = lax.broadcasted_iota(jnp.int32, (LANES, c), 1) // GROUP_DIM
    bexp_ref[...] = jnp.where(ch2 == gi2, 1.0, 0.0).astype(BF16)


def _group_rms(y, bsel, bexp):
    ms = _dot((y * y).astype(BF16), bsel)
    r = lax.rsqrt(ms + EPS)
    r_hi = r.astype(BF16)
    r_lo = (r - r_hi.astype(F32)).astype(BF16)
    return y * (_dot(r_hi, bexp) + _dot(r_lo, bexp))


def _layer_norm(y, g, b):
    mu = jnp.mean(y, axis=-1, keepdims=True)
    d = y - mu
    var = jnp.mean(d * d, axis=-1, keepdims=True)
    return d * lax.rsqrt(var + EPS) * g + b


def _mod_kernel(c_ref, w_ref, b_ref, o_ref):
    c = c_ref[...]
    c_act = c * jax.nn.sigmoid(c)
    o_ref[...] = jnp.dot(c_act, w_ref[...], preferred_element_type=F32,
                         precision=lax.Precision.HIGHEST) + b_ref[...]


def _modulation(c, w_ada, b_ada):
    bsz = c.shape[0]
    n = w_ada.shape[1]
    return pl.pallas_call(
        _mod_kernel,
        grid=(n // D_MODEL,),
        in_specs=[pl.BlockSpec((bsz, D_MODEL), lambda j: (0, 0)),
                  pl.BlockSpec((D_MODEL, D_MODEL), lambda j: (0, j)),
                  pl.BlockSpec((1, D_MODEL), lambda j: (0, j))],
        out_specs=pl.BlockSpec((bsz, D_MODEL), lambda j: (0, j)),
        out_shape=jax.ShapeDtypeStruct((bsz, n), F32),
        name="adaln_mod",
    )(c, w_ada, b_ada)


def _inproj_conv_kernel(x_ref, sh_ref, sc_ref, w_ref, b_ref, wq_ref, bq_ref, cw_ref, cb_ref, lg_ref, lb_ref, og_ref,
                        yc_ref, q_ref, kv_ref, wbf, gs, cacc, wb, bsel, bexp):
    first = (pl.program_id(0) == 0) & (pl.program_id(1) == 0)
    s = pl.program_id(1)

    @pl.when(first)
    def _():
        wbf[...] = w_ref[...].astype(BF16)
        wbf[:, 2 * D_CONV:2 * D_CONV + D_ATTN] = wq_ref[...].astype(BF16)
        wb[...] = jnp.broadcast_to(cw_ref[...][:, None, :], wb.shape)
        _fill_group_maps(bsel, bexp)

    n_ct = D_CONV // LANES

    @pl.when(s == 0)
    def _():
        gs[0, :, 0:HALO, :] = jnp.zeros((n_ct, HALO, LANES), F32)

    @pl.when(s > 0)
    def _():
        gs[0, :, 0:HALO, :] = gs[0, :, TS:TS + HALO, :]

    h = (x_ref[...] * (1.0 + sc_ref[...]) + sh_ref[...]).astype(BF16)
    u_a = _dot(h, wbf[:, 0:D_CONV]) + b_ref[:, 0:D_CONV]
    u_b = _dot(h, wbf[:, D_CONV:2 * D_CONV]) + b_ref[:, D_CONV:2 * D_CONV]
    glu = u_a * jax.nn.sigmoid(u_b)
    for c in range(n_ct):
        gs[0, c, HALO:HALO + TS, :] = glu[:, c * LANES:(c + 1) * LANES]
    q0 = 2 * D_CONV
    k0 = q0 + D_ATTN

    n_sh = TS + HALO - SUBLANES
    for j in range(1, SUBLANES):
        gs[j, :, 0:n_sh, :] = gs[0, :, j:j + n_sh, :]

    cb = cb_ref[...]
    lg = lg_ref[...]
    lb = lb_ref[...]
    og = og_ref[...]

    late_cols = 2 * LANES
    n_late = (D_ATTN + 2 * D_KV) // late_cols

    def late_proj(j):
        c0 = j * late_cols
        if c0 < D_ATTN:
            q = _dot(h, wbf[:, q0 + c0:q0 + c0 + late_cols]) + bq_ref[:, c0:c0 + late_cols]
            q_ref[:, c0:c0 + late_cols] = (q * (HEAD_DIM ** -0.5)).astype(BF16)
        else:
            kv_ref[...] = (_dot(h, wbf[:, k0:k0 + 2 * D_KV]) + b_ref[:, k0:k0 + 2 * D_KV]).astype(BF16)

    n_chunks = n_ct * (TS // CR)
    every = n_chunks // n_late
    for n in range(n_chunks):
        c, r0 = n // (TS // CR), (n % (TS // CR)) * CR
        ls = slice(c * LANES, (c + 1) * LANES)
        acc = jnp.broadcast_to(cb[:, ls], (CR, LANES)).reshape(CR // SUBLANES, SUBLANES, LANES)
        for k in range(CONV_WIDTH):
            off = HALO - (CONV_WIDTH - 1) + k
            a0 = r0 + (off // SUBLANES) * SUBLANES
            seg = gs[off % SUBLANES, c, a0:a0 + CR, :]
            acc = acc + wb[k, :, ls][None] * seg.reshape(CR // SUBLANES, SUBLANES, LANES)
        cacc[c, r0:r0 + CR, :] = acc.reshape(CR, LANES)
        if n % every == 0 and n // every < n_late:
            late_proj(n // every)
    y = _layer_norm(jnp.concatenate([cacc[c] for c in range(n_ct)], axis=1), lg, lb)
    y = y * jax.nn.sigmoid(y)
    yc_ref[...] = (_group_rms(y, bsel[...], bexp[...]) * og).astype(BF16)


def _inproj_conv(x, sh1, sc1, w_in, b_in, w_q, b_q, conv_w, conv_b, ln_g, ln_b, out_g):
    bsz, seq, _ = x.shape
    vec = lambda n: pl.BlockSpec((1, n), lambda b, s: (0, 0))
    mod = pl.BlockSpec((None, None, 1, D_MODEL), lambda b, s: (b, 0, 0, 0))
    tile = lambda n: pl.BlockSpec((None, TS, n), lambda b, s: (b, s, 0))
    return pl.pallas_call(
        _inproj_conv_kernel,
        grid=(bsz, seq // TS),
        in_specs=[tile(D_MODEL), mod, mod,
                  pl.BlockSpec((D_MODEL, D_IN), lambda b, s: (0, 0)), vec(D_IN),
                  pl.BlockSpec((D_MODEL, D_ATTN), lambda b, s: (0, 0)), vec(D_ATTN),
                  pl.BlockSpec((CONV_WIDTH, D_CONV), lambda b, s: (0, 0)),
                  vec(D_CONV), vec(D_CONV), vec(D_CONV), vec(D_CONV)],
        out_specs=[tile(D_CONV), tile(D_ATTN), tile(2 * D_KV)],
        out_shape=[jax.ShapeDtypeStruct((bsz, seq, D_CONV), BF16),
                   jax.ShapeDtypeStruct((bsz, seq, D_ATTN), BF16),
                   jax.ShapeDtypeStruct((bsz, seq, 2 * D_KV), BF16)],
        scratch_shapes=[pltpu.VMEM((D_MODEL, D_IN), BF16),
                        pltpu.VMEM((SUBLANES, D_CONV // LANES, TS + HALO, LANES), F32),
                        pltpu.VMEM((D_CONV // LANES, TS, LANES), F32),
                        pltpu.VMEM((CONV_WIDTH, SUBLANES, D_CONV), F32),
                        pltpu.VMEM((D_CONV, LANES), BF16),
                        pltpu.VMEM((LANES, D_CONV), BF16)],
        compiler_params=pltpu.CompilerParams(
            dimension_semantics=("arbitrary", "arbitrary"), vmem_limit_bytes=VMEM_LIMIT),
        name="inproj_conv",
    )(x, sh1, sc1, w_in, b_in, w_q, b_q, conv_w, conv_b, ln_g, ln_b, out_g)


def _attn_kernel(sinks_ref, q_ref, kvc_ref, kvp_ref, og_ref, o_ref, bsel, bexp):
    n = pl.program_id(1)

    @pl.when((pl.program_id(0) == 0) & (n == 0))
    def _():
        _fill_group_maps(bsel, bexp)

    qr = lax.broadcasted_iota(jnp.int32, (BLOCK, 2 * BLOCK), 0)
    kc = lax.broadcasted_iota(jnp.int32, (BLOCK, 2 * BLOCK), 1)
    band = (kc > qr) & (kc <= qr + BLOCK)
    lane = lax.broadcasted_iota(jnp.int32, (BLOCK, LANES), 1)
    lower = lane < HEAD_DIM
    lower_bf = jnp.where(lower, 1.0, 0.0).astype(BF16)
    upper_bf = jnp.where(lower, 0.0, 1.0).astype(BF16)
    og = og_ref[...]
    blocks = range(AB)
    kvs, masks = [], []
    for blk in blocks:
        r0 = blk * BLOCK
        if blk == 0:
            kvs.append(jnp.concatenate([kvp_ref[...], kvc_ref[0:BLOCK, :]], axis=0))
            masks.append(band & ((n > 0) | (kc >= BLOCK)))
        else:
            kvs.append(kvc_ref[r0 - BLOCK:r0 + BLOCK, :])
            masks.append(band)
    s_all = []
    for blk in blocks:
        r0 = blk * BLOCK
        parts = []
        for j in range(D_ATTN // LANES):
            col = q_ref[r0:r0 + BLOCK, j * LANES:(j + 1) * LANES]
            parts += [col * lower_bf, col * upper_bf]
        s_all.append(_dot_nt(jnp.concatenate(parts, axis=0), kvs[blk][:, 0:D_KV]))
    probs, dens = [], []
    for blk in blocks:
        pb = []
        den = jnp.ones((BLOCK, LANES), F32)
        for slot in range(N_HEADS):
            sc = jnp.where(masks[blk], s_all[blk][slot * BLOCK:(slot + 1) * BLOCK], NEG)
            sink = sinks_ref[PAIR_SLOTS[slot]]
            m = jnp.maximum(jnp.max(sc, axis=-1, keepdims=True), sink)
            p = jnp.exp(sc - m)
            den = jnp.where(lane == slot, jnp.sum(p, axis=-1, keepdims=True) + jnp.exp(sink - m), den)
            pb.append(p.astype(BF16))
        probs.append(jnp.concatenate(pb, axis=0))
        dens.append(den)
    pvs = [_dot(probs[blk], kvs[blk][:, D_KV:2 * D_KV]) for blk in blocks]
    for blk in blocks:
        pv, den = pvs[blk], dens[blk]
        o = jnp.concatenate(
            [jnp.where(lower, pv[(2 * j) * BLOCK:(2 * j + 1) * BLOCK], pv[(2 * j + 1) * BLOCK:(2 * j + 2) * BLOCK])
             for j in range(D_ATTN // LANES)], axis=1)
        ms = _dot((o * o).astype(BF16), bsel[...])
        r = lax.rsqrt(ms + EPS * den * den)
        r_hi = r.astype(BF16)
        r_lo = (r - r_hi.astype(F32)).astype(BF16)
        scale = _dot(r_hi, bexp[...]) + _dot(r_lo, bexp[...])
        o_ref[blk * BLOCK:(blk + 1) * BLOCK, :] = (o * scale * og).astype(BF16)


def _attention(q, kv, sinks, out_g):
    bsz, seq, _ = q.shape
    rows = AB * BLOCK
    grid_spec = pltpu.PrefetchScalarGridSpec(
        num_scalar_prefetch=1,
        grid=(bsz, seq // rows),
        in_specs=[pl.BlockSpec((None, rows, D_ATTN), lambda b, n, sk: (b, n, 0)),
                  pl.BlockSpec((None, rows, 2 * D_KV), lambda b, n, sk: (b, n, 0)),
                  pl.BlockSpec((None, BLOCK, 2 * D_KV), lambda b, n, sk: (b, jnp.maximum(AB * n - 1, 0), 0)),
                  pl.BlockSpec((1, D_ATTN), lambda b, n, sk: (0, 0))],
        out_specs=pl.BlockSpec((None, rows, D_ATTN), lambda b, n, sk: (b, n, 0)),
        scratch_shapes=[pltpu.VMEM((D_ATTN, LANES), BF16), pltpu.VMEM((LANES, D_ATTN), BF16)],
    )
    return pl.pallas_call(
        _attn_kernel,
        grid_spec=grid_spec,
        out_shape=jax.ShapeDtypeStruct((bsz, seq, D_ATTN), BF16),
        compiler_params=pltpu.CompilerParams(dimension_semantics=("arbitrary", "arbitrary")),
        name="swa_attention",
    )(sinks, q, kv, kv, out_g)


def _group_softmax_top(logits, lane):
    is_g = (lane >= N_EXPERTS) & (lane < N_EXPERTS + N_GROUPS)
    gl = jnp.where(is_g, logits, NEG)
    gmax = jnp.max(gl, axis=-1, keepdims=True)
    p_top = 1.0 / jnp.sum(jnp.where(is_g, jnp.exp(gl - gmax), 0.0), axis=-1, keepdims=True)
    return is_g, gl, gmax, p_top


def _route_bucket(logits):
    rows = logits.shape[0]
    lane = lax.broadcasted_iota(jnp.int32, (rows, ROUTER_COLS), 1)
    lanef = lane.astype(F32)
    is_g, gl, gmax, _ = _group_softmax_top(logits, lane)
    gidx = jnp.min(jnp.where(is_g & (gl == gmax), lanef - N_EXPERTS, 99.0), axis=-1, keepdims=True)
    in_grp = (lane < N_EXPERTS) & ((lane // EXPERTS_PER_GROUP) == gidx.astype(jnp.int32))
    el = jnp.where(in_grp, logits, NEG)
    m1 = jnp.max(el, axis=-1, keepdims=True)
    i1 = jnp.min(jnp.where(in_grp & (el == m1), lanef, 999.0), axis=-1, keepdims=True)
    rest = in_grp & (lanef != i1)
    el2 = jnp.where(rest, logits, NEG)
    m2 = jnp.max(el2, axis=-1, keepdims=True)
    i2 = jnp.min(jnp.where(rest & (el2 == m2), lanef, 999.0), axis=-1, keepdims=True)
    lo = jnp.minimum(i1, i2) - EXPERTS_PER_GROUP * gidx
    hi = jnp.maximum(i1, i2) - EXPERTS_PER_GROUP * gidx
    pair = lo * (EXPERTS_PER_GROUP - 1) - lo * (lo - 1.0) * 0.5 + (hi - lo - 1.0)
    return gidx * N_PAIRS + pair, lanef


def _outproj_kernel(x_ref, yc_ref, ya_ref, w_ref, b_ref, g1_ref, lg_ref, lb_ref, sc2_ref, sh2_ref,
                    wr_ref, br_ref, x1_ref, h2t_ref, code_ref, counts_ref, wbf, wrbf, ltri, running):
    @pl.when((pl.program_id(0) == 0) & (pl.program_id(1) == 0))
    def _():
        wbf[...] = w_ref[...].astype(BF16)
        wrbf[...] = wr_ref[...].astype(BF16)
        r = lax.broadcasted_iota(jnp.int32, (TS, TS), 0)
        c = lax.broadcasted_iota(jnp.int32, (TS, TS), 1)
        ltri[...] = jnp.where(c < r, 1.0, 0.0).astype(BF16)
        running[...] = jnp.zeros_like(running)

    nsl = OUT_SLICES
    rows = TS // nsl
    sl = [slice(k * rows, (k + 1) * rows) for k in range(nsl)]
    mix = [_dot(yc_ref[r, :], wbf[0:D_CONV, :]) + _dot(ya_ref[r, :], wbf[D_CONV:, :]) + b_ref[...] for r in sl]
    x1 = [_layer_norm(ALPHA * x_ref[r, :] + g1_ref[...] * m, lg_ref[...], lb_ref[...]) for r, m in zip(sl, mix)]
    h2 = [v * (1.0 + sc2_ref[...]) + sh2_ref[...] for v in x1]
    for k in range(nsl):
        x1_ref[sl[k], :] = x1[k]
        _to_tiles(h2t_ref.at[pl.ds(k * rows * CHUNKS, rows * CHUNKS)], h2[k])
    logits = [_dot(v.astype(BF16), wrbf[...]) + br_ref[...] for v in h2]
    routed = [_route_bucket(lg) for lg in logits]
    mine = [lanef == bucket for bucket, lanef in routed]
    onehot = [jnp.where(mk, 1.0, 0.0) for mk in mine]
    within = [_dot(ltri[0:rows, 0:rows], oh.astype(BF16)) for oh in onehot]
    base = running[...]
    for k in range(nsl):
        rank = jnp.sum(jnp.where(mine[k], within[k] + base, 0.0), axis=-1, keepdims=True)
        base = base + jnp.sum(onehot[k], axis=0, keepdims=True)
        code = routed[k][0].astype(jnp.int32) * (1 << CODE_SHIFT) + rank.astype(jnp.int32)
        code_ref[sl[k], :] = jnp.broadcast_to(code, (rows, LANES))
    running[...] = base
    counts_ref[...] = running[...]


def _outproj_route(x, y_conv, y_attn, w_out, b_out, g1, ln_g, ln_b, sc2, sh2, w_r, b_r):
    bsz, seq, _ = x.shape
    nst = seq // TS
    vec = lambda n: pl.BlockSpec((1, n), lambda b, s: (0, 0))
    mod = pl.BlockSpec((None, None, 1, D_MODEL), lambda b, s: (b, 0, 0, 0))
    tile = lambda n: pl.BlockSpec((None, TS, n), lambda b, s: (b, s, 0))
    return pl.pallas_call(
        _outproj_kernel,
        grid=(bsz, seq // TS),
        in_specs=[tile(D_MODEL), tile(D_CONV), tile(D_ATTN),
                  pl.BlockSpec((D_MODEL, D_MODEL), lambda b, s: (0, 0)), vec(D_MODEL),
                  mod, vec(D_MODEL), vec(D_MODEL), mod, mod,
                  pl.BlockSpec((D_MODEL, ROUTER_COLS), lambda b, s: (0, 0)), vec(ROUTER_COLS)],
        out_specs=[tile(D_MODEL), pl.BlockSpec((TS * CHUNKS, LANES), lambda b, s: (b * nst + s, 0)),
                   tile(LANES), vec(LANES)],
        out_shape=[jax.ShapeDtypeStruct((bsz, seq, D_MODEL), F32),
                   jax.ShapeDtypeStruct((bsz * seq * CHUNKS, LANES), F32),
                   jax.ShapeDtypeStruct((bsz, seq, LANES), jnp.int32),
                   jax.ShapeDtypeStruct((1, LANES), F32)],
        scratch_shapes=[pltpu.VMEM((D_MODEL, D_MODEL), BF16), pltpu.VMEM((D_MODEL, ROUTER_COLS), BF16),
                        pltpu.VMEM((TS, TS), BF16), pltpu.VMEM((1, LANES), F32)],
        compiler_params=pltpu.CompilerParams(
            dimension_semantics=("arbitrary", "arbitrary"), vmem_limit_bytes=VMEM_LIMIT),
        name="outproj_route",
    )(x, y_conv, y_attn, w_out, b_out, g1, ln_g, ln_b, sc2, sh2, w_r, b_r)


def _row_of(starts_ref, code):
    return (starts_ref[lax.shift_right_logical(code, jnp.int32(CODE_SHIFT))]
            + (code & ((1 << CODE_SHIFT) - 1)))


CHUNKS = D_MODEL // LANES
assert CHUNKS == SUBLANES


def _tok_rows(p):
    return pl.ds(pl.multiple_of(p * CHUNKS, CHUNKS), CHUNKS)


def _to_tiles(ref, x):
    for c in range(CHUNKS):
        ref[pl.ds(c, x.shape[0], stride=CHUNKS), :] = x[:, c * LANES:(c + 1) * LANES]


def _from_tiles(ref, n):
    return jnp.concatenate([ref[pl.ds(c, n, stride=CHUNKS), :] for c in range(CHUNKS)], axis=1)


def _wait_tokens(hbm_ref, n, sem):
    pltpu.make_async_copy(hbm_ref.at[pl.ds(0, n * CHUNKS)], hbm_ref.at[pl.ds(0, n * CHUNKS)], sem).wait()


def _scatter_kernel(starts_ref, zrow_ref, valid_ref, code_ref, h2t_ref, hs_ref, pos_ref, hbuf, zeros, sems, zsem):
    nsteps = pl.num_programs(0) * pl.num_programs(1)
    step = pl.program_id(0) * pl.num_programs(1) + pl.program_id(1)
    slot = step % 2

    def _tile_rows(tok):
        return pl.ds(pl.multiple_of(tok * CHUNKS, TMM * CHUNKS), TMM * CHUNKS)

    def zero_copy(b):
        return pltpu.make_async_copy(zeros, hs_ref.at[_tile_rows(zrow_ref[b])], zsem)

    def tail_copy(t):
        return pltpu.make_async_copy(zeros, hs_ref.at[_tile_rows(t * TMM)], zsem)

    @pl.when(step == 0)
    def _():
        zeros[...] = jnp.zeros_like(zeros)
        n_tiles = hs_ref.shape[0] // (TMM * CHUNKS)

        def start(b, c):
            @pl.when(zrow_ref[b] >= 0)
            def _():
                zero_copy(b).start()
            return c

        def wait(b, c):
            @pl.when(zrow_ref[b] >= 0)
            def _():
                zero_copy(b).wait()
            return c

        def start_tail(t, c):
            @pl.when(valid_ref[t] == 0)
            def _():
                tail_copy(t).start()
            return c

        def wait_tail(t, c):
            @pl.when(valid_ref[t] == 0)
            def _():
                tail_copy(t).wait()
            return c

        lax.fori_loop(0, N_BUCKETS, start, 0)
        lax.fori_loop(0, n_tiles, start_tail, 0)
        lax.fori_loop(0, N_BUCKETS, wait, 0)
        lax.fori_loop(0, n_tiles, wait_tail, 0)

    @pl.when(step >= 2)
    def _():
        _wait_tokens(hs_ref, TS, sems.at[slot])

    hbuf[slot] = h2t_ref[...]

    for r in range(TS):
        pos = _row_of(starts_ref, code_ref[0, r])
        pos_ref[0, r] = pos
        pltpu.make_async_copy(hbuf.at[slot, _tok_rows(r)], hs_ref.at[_tok_rows(pos)],
                              sems.at[slot]).start(priority=r % 2)

    @pl.when(step == nsteps - 1)
    def _():
        _wait_tokens(hs_ref, TS, sems.at[slot])
        _wait_tokens(hs_ref, TS, sems.at[1 - slot])


def _scatter_rows(starts, zrow, valid, code, h2t, n_rows):
    bsz, seq = code.shape
    nst = seq // TS
    grid_spec = pltpu.PrefetchScalarGridSpec(
        num_scalar_prefetch=3,
        grid=(bsz, nst),
        in_specs=[pl.BlockSpec((None, 1, TS), lambda b, s, *_: (b * nst + s, 0, 0), memory_space=pltpu.SMEM),
                  pl.BlockSpec((TS * CHUNKS, LANES), lambda b, s, *_: (b * nst + s, 0))],
        out_specs=[pl.BlockSpec(memory_space=pl.ANY),
                   pl.BlockSpec((None, 1, TS), lambda b, s, *_: (b * nst + s, 0, 0), memory_space=pltpu.SMEM)],
        scratch_shapes=[pltpu.VMEM((2, TS * CHUNKS, LANES), F32), pltpu.VMEM((TMM * CHUNKS, LANES), F32),
                        pltpu.SemaphoreType.DMA((2,)), pltpu.SemaphoreType.DMA],
    )
    return pl.pallas_call(
        _scatter_kernel,
        grid_spec=grid_spec,
        out_shape=[jax.ShapeDtypeStruct((n_rows * CHUNKS, LANES), F32),
                   jax.ShapeDtypeStruct((bsz * nst, 1, TS), jnp.int32)],
        compiler_params=pltpu.CompilerParams(dimension_semantics=("arbitrary", "arbitrary")),
        name="moe_scatter",
    )(starts, zrow, valid, code.reshape(bsz * nst, 1, TS), h2t)


def _moe_kernel(tlo_ref, thi_ref, valid_ref, h_ref, wg_hbm, wu_hbm, wd_hbm, wr_ref, br_ref, y_ref,
                wrbf, wg, wu, wd, sg, su, sd, sems):
    i = pl.program_id(0)
    t0 = i * NSUB
    group = lax.shift_right_logical(tlo_ref[t0], jnp.int32(3))
    prev_group = lax.shift_right_logical(tlo_ref[jnp.maximum(t0 - NSUB, 0)], jnp.int32(3))

    @pl.when(i == 0)
    def _():
        wrbf[...] = wr_ref[...].astype(BF16)

    @pl.when((i == 0) | (group != prev_group))
    def _():
        def copies(e, slot):
            ex = group * EXPERTS_PER_GROUP + e
            return (pltpu.make_async_copy(wg_hbm.at[ex], sg.at[slot], sems.at[slot, 0]),
                    pltpu.make_async_copy(wu_hbm.at[ex], su.at[slot], sems.at[slot, 1]),
                    pltpu.make_async_copy(wd_hbm.at[ex], sd.at[slot], sems.at[slot, 2]))

        for cp in copies(0, 0):
            cp.start()

        def land(e, c):
            slot = e % 2

            @pl.when(e + 1 < EXPERTS_PER_GROUP)
            def _():
                for cp in copies(e + 1, 1 - slot):
                    cp.start()

            for cp in copies(e, slot):
                cp.wait()
            wg[e] = sg[slot].astype(BF16)
            wu[e] = su[slot].astype(BF16)
            wd[e] = sd[slot].astype(BF16)
            return c

        lax.fori_loop(0, EXPERTS_PER_GROUP, land, 0)

    @pl.when(valid_ref[t0] == 1)
    def _():
        lane = lax.broadcasted_iota(jnp.int32, (TMM, ROUTER_COLS), 1)
        local = EXPERTS_PER_GROUP - 1
        subs = range(NSUB)
        tile = lambda ref, j: ref.at[pl.ds(j * TMM * CHUNKS, TMM * CHUNKS)]
        hs = [_from_tiles(tile(h_ref, j), TMM).astype(BF16) for j in subs]
        es = [(tlo_ref[t0 + j] & local, thi_ref[t0 + j] & local) for j in subs]
        pre = [[(_dot(hs[j], wg[e]), _dot(hs[j], wu[e])) for e in es[j]] for j in subs]
        act = [[(a * jax.nn.sigmoid(a) * u).astype(BF16) for a, u in pre[j]] for j in subs]
        ys = [[_dot(act[j][k], wd[es[j][k]]) for k in (0, 1)] for j in subs]
        for j in subs:
            logits = _dot(hs[j], wrbf[...]) + br_ref[...]
            p_top = _group_softmax_top(logits, lane)[3]
            l_lo = jnp.sum(jnp.where(lane == tlo_ref[t0 + j], logits, 0.0), axis=-1, keepdims=True)
            l_hi = jnp.sum(jnp.where(lane == thi_ref[t0 + j], logits, 0.0), axis=-1, keepdims=True)
            w_lo = p_top / (1.0 + jnp.exp(l_hi - l_lo))
            w_hi = p_top / (1.0 + jnp.exp(l_lo - l_hi))
            _to_tiles(tile(y_ref, j), w_lo * ys[j][0] + w_hi * ys[j][1])

    @pl.when(valid_ref[t0] == 0)
    def _():
        y_ref[...] = jnp.zeros_like(y_ref)


def _moe_sorted(tlo, thi, valid, h_sorted, w_gate, w_up, w_down, w_r, b_r):
    n_rows = h_sorted.shape[0] // CHUNKS
    rows = pl.BlockSpec((NSUB * TMM * CHUNKS, LANES), lambda i, *_: (i, 0))
    hbm = pl.BlockSpec(memory_space=pl.ANY)
    w_in_shape, w_out_shape = (D_MODEL, D_EXPERT), (D_EXPERT, D_MODEL)
    grid_spec = pltpu.PrefetchScalarGridSpec(
        num_scalar_prefetch=3,
        grid=(n_rows // (NSUB * TMM),),
        in_specs=[rows, hbm, hbm, hbm,
                  pl.BlockSpec((D_MODEL, ROUTER_COLS), lambda i, *_: (0, 0)),
                  pl.BlockSpec((1, ROUTER_COLS), lambda i, *_: (0, 0))],
        out_specs=rows,
        scratch_shapes=[pltpu.VMEM((D_MODEL, ROUTER_COLS), BF16),
                        pltpu.VMEM((EXPERTS_PER_GROUP,) + w_in_shape, BF16),
                        pltpu.VMEM((EXPERTS_PER_GROUP,) + w_in_shape, BF16),
                        pltpu.VMEM((EXPERTS_PER_GROUP,) + w_out_shape, BF16),
                        pltpu.VMEM((2,) + w_in_shape, F32), pltpu.VMEM((2,) + w_in_shape, F32),
                        pltpu.VMEM((2,) + w_out_shape, F32), pltpu.SemaphoreType.DMA((2, 3))],
    )
    return pl.pallas_call(
        _moe_kernel,
        grid_spec=grid_spec,
        out_shape=jax.ShapeDtypeStruct((n_rows * CHUNKS, LANES), F32),
        compiler_params=pltpu.CompilerParams(dimension_semantics=("arbitrary",), vmem_limit_bytes=VMEM_LIMIT),
        name="moe_sorted",
    )(tlo, thi, valid, h_sorted, w_gate, w_up, w_down, w_r, b_r)


def _final_kernel(pos_ref, posn_ref, x1_ref, g2_ref, lg_ref, lb_ref, ys_ref, o_ref, ybuf, sems):
    nsteps = pl.num_programs(0) * pl.num_programs(1)
    step = pl.program_id(0) * pl.num_programs(1) + pl.program_id(1)

    def fetch(rows_ref, r, sl):
        return pltpu.make_async_copy(ys_ref.at[_tok_rows(rows_ref[0, r])], ybuf.at[sl, _tok_rows(r)], sems.at[sl])

    @pl.when(step == 0)
    def _():
        def issue(i, c):
            for u in range(SUBLANES):
                fetch(pos_ref, i * SUBLANES + u, 0).start(priority=u % 2)
            return c

        lax.fori_loop(0, TS // SUBLANES, issue, 0)

    def body(slot):
        _wait_tokens(ys_ref, TS, sems.at[slot])
        ck = TS // FINAL_CHUNKS
        for k in range(FINAL_CHUNKS):
            for r in range(k * ck, (k + 1) * ck):
                fetch(posn_ref, r, 1 - slot).start(priority=r % 2)
            rs = slice(k * ck, (k + 1) * ck)
            y = _from_tiles(ybuf.at[slot, pl.ds(k * ck * CHUNKS, ck * CHUNKS)], ck)
            o_ref[rs, :] = _layer_norm(ALPHA * x1_ref[rs, :] + g2_ref[...] * y, lg_ref[...], lb_ref[...])

    for slot in range(2):
        pl.when(step % 2 == slot)(lambda slot=slot: body(slot))

    @pl.when(step == nsteps - 1)
    def _():
        _wait_tokens(ys_ref, TS, sems.at[1 - step % 2])


def _unsort_ln2(pos, x1, g2, ln_g, ln_b, y_sorted):
    bsz, seq, _ = x1.shape
    nst = seq // TS
    last = bsz * nst - 1
    return pl.pallas_call(
        _final_kernel,
        grid=(bsz, nst),
        in_specs=[pl.BlockSpec((None, 1, TS), lambda b, s: (b * nst + s, 0, 0), memory_space=pltpu.SMEM),
                  pl.BlockSpec((None, 1, TS), lambda b, s: (jnp.minimum(b * nst + s + 1, last), 0, 0),
                               memory_space=pltpu.SMEM),
                  pl.BlockSpec((None, TS, D_MODEL), lambda b, s: (b, s, 0)),
                  pl.BlockSpec((None, None, 1, D_MODEL), lambda b, s: (b, 0, 0, 0)),
                  pl.BlockSpec((1, D_MODEL), lambda b, s: (0, 0)),
                  pl.BlockSpec((1, D_MODEL), lambda b, s: (0, 0)),
                  pl.BlockSpec(memory_space=pl.ANY)],
        out_specs=pl.BlockSpec((None, TS, D_MODEL), lambda b, s: (b, s, 0)),
        out_shape=jax.ShapeDtypeStruct((bsz, seq, D_MODEL), F32),
        scratch_shapes=[pltpu.VMEM((2, TS * CHUNKS, LANES), F32), pltpu.SemaphoreType.DMA((2,))],
        compiler_params=pltpu.CompilerParams(dimension_semantics=("arbitrary", "arbitrary")),
        name="unsort_ln2",
    )(pos, pos, x1, g2, ln_g, ln_b, y_sorted)


def _pair_layout(a, axis):
    shp = a.shape
    a = a.reshape(shp[:axis] + (N_KV_HEADS, N_HEADS // N_KV_HEADS, HEAD_DIM) + shp[axis + 1:])
    return jnp.swapaxes(a, axis, axis + 1).reshape(shp)


def _tile_tables(counts, n_tiles_max):
    b_np = np.arange(N_BUCKETS)
    g_np = b_np // N_PAIRS
    in_group_before = jnp.asarray((g_np[None, :] == g_np[:, None]) & (b_np[None, :] < b_np[:, None]))
    member = jnp.asarray(g_np[None, :] == np.arange(N_GROUPS)[:, None])
    earlier_group = jnp.asarray(np.arange(N_GROUPS)[None, :] < np.arange(N_GROUPS)[:, None])
    e_lo = jnp.asarray(g_np * EXPERTS_PER_GROUP + np.asarray(PAIRS)[b_np % N_PAIRS, 0], jnp.int32)
    e_hi = jnp.asarray(g_np * EXPERTS_PER_GROUP + np.asarray(PAIRS)[b_np % N_PAIRS, 1], jnp.int32)

    n_tile_b = jnp.floor((counts + (TMM - 1.0)) * (1.0 / TMM))
    g_tiles = jnp.sum(jnp.where(member, n_tile_b[None, :], 0.0), axis=1)
    g_padded = jnp.floor((g_tiles + (NSUB - 1.0)) * (1.0 / NSUB)) * NSUB
    g_start = jnp.sum(jnp.where(earlier_group, g_padded[None, :], 0.0), axis=1)
    g_end = g_start + g_padded
    b_start = (jnp.sum(jnp.where(member.T, g_start[None, :], 0.0), axis=1)
               + jnp.sum(jnp.where(in_group_before, n_tile_b[None, :], 0.0), axis=1))
    b_end = b_start + n_tile_b
    starts = (b_start * TMM).astype(jnp.int32)
    zrow = jnp.where(n_tile_b > 0, (b_end - 1.0) * TMM, -1.0).astype(jnp.int32)

    t = jnp.arange(n_tiles_max, dtype=jnp.int32).astype(F32)
    g_of_t = jnp.minimum(jnp.sum(jnp.where(g_end[None, :] <= t[:, None], 1, 0), axis=1), N_GROUPS - 1)
    g_hot = g_of_t[:, None] == jnp.arange(N_GROUPS, dtype=jnp.int32)[None, :]
    local_t = t - jnp.sum(jnp.where(g_hot, g_start[None, :], 0.0), axis=1)
    valid = (local_t < jnp.sum(jnp.where(g_hot, g_tiles[None, :], 0.0), axis=1)) & (t < jnp.sum(g_padded))
    bucket = jnp.sum(jnp.where(b_end[None, :] <= t[:, None], 1, 0), axis=1)
    bucket = jnp.where(valid, bucket, g_of_t * N_PAIRS + (N_PAIRS - 1))
    b_hot = bucket[:, None] == jnp.asarray(b_np, jnp.int32)[None, :]
    tlo = jnp.sum(jnp.where(b_hot, e_lo[None, :], 0), axis=1)
    thi = jnp.sum(jnp.where(b_hot, e_hi[None, :], 0), axis=1)
    pad = jnp.zeros((LANES - N_BUCKETS,), jnp.int32)
    return (jnp.concatenate([starts, pad]), jnp.concatenate([zrow, pad - 1]), tlo, thi,
            jnp.where(valid, 1, 0).astype(jnp.int32))


def kernel(x, c, w_ada, b_ada, w_in, b_in, conv_w, conv_b, conv_ln_g, conv_ln_b, conv_out_g, sinks,
           attn_out_g, w_out, b_out, ln1_g, ln1_b, w_router_group, b_router_group, w_router_expert,
           b_router_expert, w_gate, w_up, w_down, ln2_g, ln2_b):
    assert w_ada.shape[0] == DEPTH
    bsz, seq, _ = x.shape
    row = lambda v: v[0][None, :]

    mod = _modulation(c, w_ada[0], row(b_ada)).reshape(bsz, 6, 1, D_MODEL)
    sh1, sc1, g1, sh2, sc2, g2 = [mod[:, j:j + 1] for j in range(6)]

    q0 = 2 * D_CONV
    w_q = _pair_layout(w_in[0][:, q0:q0 + D_ATTN], 1)
    b_q = _pair_layout(b_in[0][q0:q0 + D_ATTN], 0)[None, :]
    og_attn = _pair_layout(attn_out_g[0], 0)[None, :]
    w_o = jnp.concatenate([w_out[0][:D_CONV], _pair_layout(w_out[0][D_CONV:], 0)], axis=0)

    y_conv, q, kv = _inproj_conv(x, sh1, sc1, w_in[0], row(b_in), w_q, b_q, conv_w[0], row(conv_b),
                                 row(conv_ln_g), row(conv_ln_b), row(conv_out_g))
    y_attn = _attention(q, kv, sinks[0], og_attn)

    pad = ROUTER_COLS - N_EXPERTS - N_GROUPS
    w_r = jnp.concatenate([w_router_expert[0], w_router_group[0], jnp.zeros((D_MODEL, pad), F32)], axis=1)
    b_r = jnp.concatenate([b_router_expert[0], b_router_group[0], jnp.zeros((pad,), F32)])[None, :]
    x1, h2t, code, counts = _outproj_route(x, y_conv, y_attn, w_o, row(b_out), g1, row(ln1_g), row(ln1_b),
                                      sc2, sh2, w_r, b_r)

    n_tiles_max = (bsz * seq) // TMM + N_BUCKETS + N_GROUPS * (NSUB - 1)
    n_tiles_max = -(-n_tiles_max // NSUB) * NSUB
    starts, zrow, tlo, thi, valid = _tile_tables(counts[0, :N_BUCKETS], n_tiles_max)
    h_sorted, pos = _scatter_rows(starts, zrow, valid, code[:, :, 0], h2t, n_tiles_max * TMM)
    y_sorted = _moe_sorted(tlo, thi, valid, h_sorted, w_gate[0], w_up[0], w_down[0], w_r, b_r)
    return _unsort_ln2(pos, x1, g2, row(ln2_g), row(ln2_b), y_sorted)
```

```python
import jax
import jax.numpy as jnp
import numpy as np
from jax import lax
from jax.experimental import pallas as pl
from jax.experimental.pallas import tpu as pltpu

F32 = jnp.float32
BF16 = jnp.bfloat16

D_MODEL = 1024
D_CONV = 512
CONV_WIDTH = 31
GROUP_DIM = 64
N_HEADS = 8
N_KV_HEADS = 2
HEAD_DIM = 64
D_ATTN = N_HEADS * HEAD_DIM
D_KV = N_KV_HEADS * HEAD_DIM
BLOCK = 128
D_IN = 2 * D_CONV + D_ATTN + 2 * D_KV
N_GROUPS = 4
EXPERTS_PER_GROUP = 8
N_EXPERTS = N_GROUPS * EXPERTS_PER_GROUP
D_EXPERT = D_MODEL // 4
DEPTH = 1
ALPHA = (2.0 * DEPTH) ** 0.25
EPS = 1e-5
NEG = -1e30

LANES = 128
SUBLANES = 8
TS = 512
HALO = 32
CR = 32
AB = 8
PAIR_SLOTS = [j + (N_HEADS // N_KV_HEADS) * g for j in range(N_HEADS // N_KV_HEADS) for g in range(N_KV_HEADS)]
ROUTER_COLS = LANES
VMEM_LIMIT = 56 * 1024 * 1024

PAIRS = [(lo, hi) for lo in range(EXPERTS_PER_GROUP) for hi in range(lo + 1, EXPERTS_PER_GROUP)]
N_PAIRS = len(PAIRS)
N_BUCKETS = N_GROUPS * N_PAIRS
TMM = 128
OUT_SLICES = 2
NSUB = 4
CODE_SHIFT = 16


def _dot(a, b):
    return jnp.dot(a, b, preferred_element_type=F32)


def _dot_nt(a, b):
    return lax.dot_general(a, b, (((1,), (1,)), ((), ())), preferred_element_type=F32)


def _fill_group_maps(bsel_ref, bexp_ref):
    c = bsel_ref.shape[0]
    ch = lax.broadcasted_iota(jnp.int32, (c, LANES), 0) // GROUP_DIM
    gi = lax.broadcasted_iota(jnp.int32, (c, LANES), 1)
    bsel_ref[...] = jnp.where(ch == gi, 1.0 / GROUP_DIM, 0.0).astype(BF16)
    gi2 = lax.broadcasted_iota(jnp.int32, (LANES, c), 0)
    ch2 = lax.broadcasted_iota(jnp.int32, (LANES, c), 1) // GROUP_DIM
    bexp_ref[...] = jnp.where(ch2 == gi2, 1.0, 0.0).astype(BF16)


def _group_rms(y, bsel, bexp):
    ms = _dot((y * y).astype(BF16), bsel)
    r = lax.rsqrt(ms + EPS)
    r_hi = r.astype(BF16)
    r_lo = (r - r_hi.astype(F32)).astype(BF16)
    return y * (_dot(r_hi, bexp) + _dot(r_lo, bexp))


def _layer_norm(y, g, b):
    mu = jnp.mean(y, axis=-1, keepdims=True)
    d = y - mu
    var = jnp.mean(d * d, axis=-1, keepdims=True)
    return d * lax.rsqrt(var + EPS) * g + b


def _mod_kernel(c_ref, w_ref, b_ref, o_ref):
    c = c_ref[...]
    c_act = c * jax.nn.sigmoid(c)
    o_ref[...] = jnp.dot(c_act, w_ref[...], preferred_element_type=F32,
                         precision=lax.Precision.HIGHEST) + b_ref[...]


def _modulation(c, w_ada, b_ada):
    bsz = c.shape[0]
    n = w_ada.shape[1]
    return pl.pallas_call(
        _mod_kernel,
        grid=(n // D_MODEL,),
        in_specs=[pl.BlockSpec((bsz, D_MODEL), lambda j: (0, 0)),
                  pl.BlockSpec((D_MODEL, D_MODEL), lambda j: (0, j)),
                  pl.BlockSpec((1, D_MODEL), lambda j: (0, j))],
        out_specs=pl.BlockSpec((bsz, D_MODEL), lambda j: (0, j)),
        out_shape=jax.ShapeDtypeStruct((bsz, n), F32),
        name="adaln_mod",
    )(c, w_ada, b_ada)


def _inproj_conv_kernel(x_ref, sh_ref, sc_ref, w_ref, b_ref, wq_ref, bq_ref, cw_ref, cb_ref, lg_ref, lb_ref, og_ref,
                        yc_ref, q_ref, kv_ref, wbf, gs, cacc, wb, bsel, bexp):
    first = (pl.program_id(0) == 0) & (pl.program_id(1) == 0)
    s = pl.program_id(1)

    @pl.when(first)
    def _():
        wbf[...] = w_ref[...].astype(BF16)
        wbf[:, 2 * D_CONV:2 * D_CONV + D_ATTN] = wq_ref[...].astype(BF16)
        wb[...] = jnp.broadcast_to(cw_ref[...][:, None, :], wb.shape)
        _fill_group_maps(bsel, bexp)

    n_ct = D_CONV // LANES

    @pl.when(s == 0)
    def _():
        gs[0, :, 0:HALO, :] = jnp.zeros((n_ct, HALO, LANES), F32)

    @pl.when(s > 0)
    def _():
        gs[0, :, 0:HALO, :] = gs[0, :, TS:TS + HALO, :]

    h = (x_ref[...] * (1.0 + sc_ref[...]) + sh_ref[...]).astype(BF16)
    q0 = 2 * D_CONV
    k0 = q0 + D_ATTN
    cb = cb_ref[...]
    lg = lg_ref[...]
    lb = lb_ref[...]
    og = og_ref[...]
    n_sh = TS + HALO - SUBLANES
    half = TS // 2
    late_cols = 2 * LANES

    def glu_unit(rows, j):
        cs = slice(j * late_cols, (j + 1) * late_cols)
        u_a = _dot(h[rows], wbf[:, cs]) + b_ref[:, cs]
        u_b = _dot(h[rows], wbf[:, D_CONV + j * late_cols:D_CONV + (j + 1) * late_cols]) \
            + b_ref[:, D_CONV + j * late_cols:D_CONV + (j + 1) * late_cols]
        glu = u_a * jax.nn.sigmoid(u_b)
        for cc in range(late_cols // LANES):
            c = j * (late_cols // LANES) + cc
            gs[0, c, HALO + rows.start:HALO + rows.stop, :] = glu[:, cc * LANES:(cc + 1) * LANES]

    def late_proj(j):
        c0 = j * late_cols
        if c0 < D_ATTN:
            q = _dot(h, wbf[:, q0 + c0:q0 + c0 + late_cols]) + bq_ref[:, c0:c0 + late_cols]
            q_ref[:, c0:c0 + late_cols] = (q * (HEAD_DIM ** -0.5)).astype(BF16)
        else:
            kv_ref[...] = (_dot(h, wbf[:, k0:k0 + 2 * D_KV]) + b_ref[:, k0:k0 + 2 * D_KV]).astype(BF16)

    def shifted(lo, hi):
        for j in range(1, SUBLANES):
            gs[j, :, lo:hi, :] = gs[0, :, lo + j:hi + j, :]

    def conv_chunk(c, r0):
        ls = slice(c * LANES, (c + 1) * LANES)
        acc = jnp.broadcast_to(cb[:, ls], (CR, LANES)).reshape(CR // SUBLANES, SUBLANES, LANES)
        for k in range(CONV_WIDTH):
            off = HALO - (CONV_WIDTH - 1) + k
            a0 = r0 + (off // SUBLANES) * SUBLANES
            seg = gs[off % SUBLANES, c, a0:a0 + CR, :]
            acc = acc + wb[k, :, ls][None] * seg.reshape(CR // SUBLANES, SUBLANES, LANES)
        cacc[c, r0:r0 + CR, :] = acc.reshape(CR, LANES)

    first, second = slice(0, half), slice(half, TS)
    n_glu = D_CONV // late_cols
    n_late = (D_ATTN + 2 * D_KV) // late_cols
    for j in range(n_glu):
        glu_unit(first, j)
    split = HALO - SUBLANES + half
    shifted(0, split)
    chunks_a = [(c, r0) for c in range(n_ct) for r0 in range(0, half, CR)]
    chunks_b = [(c, r0) for c in range(n_ct) for r0 in range(half, TS, CR)]
    units_a = [lambda j=j: glu_unit(second, j) for j in range(n_glu)] + [lambda: late_proj(0)]
    units_b = [lambda j=j: late_proj(j) for j in range(1, n_late)]
    for chunks, units in ((chunks_a, units_a), (chunks_b, units_b)):
        if chunks is chunks_b:
            shifted(split, n_sh)
        every = len(chunks) // len(units)
        for n, (c, r0) in enumerate(chunks):
            conv_chunk(c, r0)
            if n % every == 0 and n // every < len(units):
                units[n // every]()
    y = _layer_norm(jnp.concatenate([cacc[c] for c in range(n_ct)], axis=1), lg, lb)
    y = y * jax.nn.sigmoid(y)
    yc_ref[...] = (_group_rms(y, bsel[...], bexp[...]) * og).astype(BF16)


def _inproj_conv(x, sh1, sc1, w_in, b_in, w_q, b_q, conv_w, conv_b, ln_g, ln_b, out_g):
    bsz, seq, _ = x.shape
    vec = lambda n: pl.BlockSpec((1, n), lambda b, s: (0, 0))
    mod = pl.BlockSpec((None, None, 1, D_MODEL), lambda b, s: (b, 0, 0, 0))
    tile = lambda n: pl.BlockSpec((None, TS, n), lambda b, s: (b, s, 0))
    return pl.pallas_call(
        _inproj_conv_kernel,
        grid=(bsz, seq // TS),
        in_specs=[tile(D_MODEL), mod, mod,
                  pl.BlockSpec((D_MODEL, D_IN), lambda b, s: (0, 0)), vec(D_IN),
                  pl.BlockSpec((D_MODEL, D_ATTN), lambda b, s: (0, 0)), vec(D_ATTN),
                  pl.BlockSpec((CONV_WIDTH, D_CONV), lambda b, s: (0, 0)),
                  vec(D_CONV), vec(D_CONV), vec(D_CONV), vec(D_CONV)],
        out_specs=[tile(D_CONV), tile(D_ATTN), tile(2 * D_KV)],
        out_shape=[jax.ShapeDtypeStruct((bsz, seq, D_CONV), BF16),
                   jax.ShapeDtypeStruct((bsz, seq, D_ATTN), BF16),
                   jax.ShapeDtypeStruct((bsz, seq, 2 * D_KV), BF16)],
        scratch_shapes=[pltpu.VMEM((D_MODEL, D_IN), BF16),
                        pltpu.VMEM((SUBLANES, D_CONV // LANES, TS + HALO, LANES), F32),
                        pltpu.VMEM((D_CONV // LANES, TS, LANES), F32),
                        pltpu.VMEM((CONV_WIDTH, SUBLANES, D_CONV), F32),
                        pltpu.VMEM((D_CONV, LANES), BF16),
                        pltpu.VMEM((LANES, D_CONV), BF16)],
        compiler_params=pltpu.CompilerParams(
            dimension_semantics=("arbitrary", "arbitrary"), vmem_limit_bytes=VMEM_LIMIT),
        name="inproj_conv",
    )(x, sh1, sc1, w_in, b_in, w_q, b_q, conv_w, conv_b, ln_g, ln_b, out_g)


def _attn_kernel(sinks_ref, q_ref, kvc_ref, kvp_ref, og_ref, o_ref, bsel, bexp):
    n = pl.program_id(1)

    @pl.when((pl.program_id(0) == 0) & (n == 0))
    def _():
        _fill_group_maps(bsel, bexp)

    qr = lax.broadcasted_iota(jnp.int32, (BLOCK, 2 * BLOCK), 0)
    kc = lax.broadcasted_iota(jnp.int32, (BLOCK, 2 * BLOCK), 1)
    band = (kc > qr) & (kc <= qr + BLOCK)
    lane = lax.broadcasted_iota(jnp.int32, (BLOCK, LANES), 1)
    lower = lane < HEAD_DIM
    lower_bf = jnp.where(lower, 1.0, 0.0).astype(BF16)
    upper_bf = jnp.where(lower, 0.0, 1.0).astype(BF16)
    og = og_ref[...]
    blocks = range(AB)
    kvs, masks = [], []
    for blk in blocks:
        r0 = blk * BLOCK
        if blk == 0:
            kvs.append(jnp.concatenate([kvp_ref[...], kvc_ref[0:BLOCK, :]], axis=0))
            masks.append(band & ((n > 0) | (kc >= BLOCK)))
        else:
            kvs.append(kvc_ref[r0 - BLOCK:r0 + BLOCK, :])
            masks.append(band)
    s_all = []
    for blk in blocks:
        r0 = blk * BLOCK
        parts = []
        for j in range(D_ATTN // LANES):
            col = q_ref[r0:r0 + BLOCK, j * LANES:(j + 1) * LANES]
            parts += [col * lower_bf, col * upper_bf]
        s_all.append(_dot_nt(jnp.concatenate(parts, axis=0), kvs[blk][:, 0:D_KV]))
    probs, dens = [], []
    for blk in blocks:
        pb = []
        den = jnp.ones((BLOCK, LANES), F32)
        for slot in range(N_HEADS):
            sc = jnp.where(masks[blk], s_all[blk][slot * BLOCK:(slot + 1) * BLOCK], NEG)
            sink = sinks_ref[PAIR_SLOTS[slot]]
            m = jnp.maximum(jnp.max(sc, axis=-1, keepdims=True), sink)
            p = jnp.exp(sc - m)
            den = jnp.where(lane == slot, jnp.sum(p, axis=-1, keepdims=True) + jnp.exp(sink - m), den)
            pb.append(p.astype(BF16))
        probs.append(jnp.concatenate(pb, axis=0))
        dens.append(den)
    pvs = [_dot(probs[blk], kvs[blk][:, D_KV:2 * D_KV]) for blk in blocks]
    for blk in blocks:
        pv, den = pvs[blk], dens[blk]
        o = jnp.concatenate(
            [jnp.where(lower, pv[(2 * j) * BLOCK:(2 * j + 1) * BLOCK], pv[(2 * j + 1) * BLOCK:(2 * j + 2) * BLOCK])
             for j in range(D_ATTN // LANES)], axis=1)
        ms = _dot((o * o).astype(BF16), bsel[...])
        r = lax.rsqrt(ms + EPS * den * den)
        r_hi = r.astype(BF16)
        r_lo = (r - r_hi.astype(F32)).astype(BF16)
        scale = _dot(r_hi, bexp[...]) + _dot(r_lo, bexp[...])
        o_ref[blk * BLOCK:(blk + 1) * BLOCK, :] = (o * scale * og).astype(BF16)


def _attention(q, kv, sinks, out_g):
    bsz, seq, _ = q.shape
    rows = AB * BLOCK
    grid_spec = pltpu.PrefetchScalarGridSpec(
        num_scalar_prefetch=1,
        grid=(bsz, seq // rows),
        in_specs=[pl.BlockSpec((None, rows, D_ATTN), lambda b, n, sk: (b, n, 0)),
                  pl.BlockSpec((None, rows, 2 * D_KV), lambda b, n, sk: (b, n, 0)),
                  pl.BlockSpec((None, BLOCK, 2 * D_KV), lambda b, n, sk: (b, jnp.maximum(AB * n - 1, 0), 0)),
                  pl.BlockSpec((1, D_ATTN), lambda b, n, sk: (0, 0))],
        out_specs=pl.BlockSpec((None, rows, D_ATTN), lambda b, n, sk: (b, n, 0)),
        scratch_shapes=[pltpu.VMEM((D_ATTN, LANES), BF16), pltpu.VMEM((LANES, D_ATTN), BF16)],
    )
    return pl.pallas_call(
        _attn_kernel,
        grid_spec=grid_spec,
        out_shape=jax.ShapeDtypeStruct((bsz, seq, D_ATTN), BF16),
        compiler_params=pltpu.CompilerParams(dimension_semantics=("arbitrary", "arbitrary")),
        name="swa_attention",
    )(sinks, q, kv, kv, out_g)


def _group_softmax_top(logits, lane):
    is_g = (lane >= N_EXPERTS) & (lane < N_EXPERTS + N_GROUPS)
    gl = jnp.where(is_g, logits, NEG)
    gmax = jnp.max(gl, axis=-1, keepdims=True)
    p_top = 1.0 / jnp.sum(jnp.where(is_g, jnp.exp(gl - gmax), 0.0), axis=-1, keepdims=True)
    return is_g, gl, gmax, p_top


def _route_bucket(logits):
    rows = logits.shape[0]
    lane = lax.broadcasted_iota(jnp.int32, (rows, ROUTER_COLS), 1)
    lanef = lane.astype(F32)
    is_g, gl, gmax, _ = _group_softmax_top(logits, lane)
    gidx = jnp.min(jnp.where(is_g & (gl == gmax), lanef - N_EXPERTS, 99.0), axis=-1, keepdims=True)
    in_grp = (lane < N_EXPERTS) & ((lane // EXPERTS_PER_GROUP) == gidx.astype(jnp.int32))
    el = jnp.where(in_grp, logits, NEG)
    m1 = jnp.max(el, axis=-1, keepdims=True)
    i1 = jnp.min(jnp.where(in_grp & (el == m1), lanef, 999.0), axis=-1, keepdims=True)
    rest = in_grp & (lanef != i1)
    el2 = jnp.where(rest, logits, NEG)
    m2 = jnp.max(el2, axis=-1, keepdims=True)
    i2 = jnp.min(jnp.where(rest & (el2 == m2), lanef, 999.0), axis=-1, keepdims=True)
    lo = jnp.minimum(i1, i2) - EXPERTS_PER_GROUP * gidx
    hi = jnp.maximum(i1, i2) - EXPERTS_PER_GROUP * gidx
    pair = lo * (EXPERTS_PER_GROUP - 1) - lo * (lo - 1.0) * 0.5 + (hi - lo - 1.0)
    return gidx * N_PAIRS + pair, lanef


def _outproj_kernel(x_ref, yc_ref, ya_ref, w_ref, b_ref, g1_ref, lg_ref, lb_ref, sc2_ref, sh2_ref,
                    wr_ref, br_ref, x1_ref, h2t_ref, code_ref, counts_ref, wbf, wrbf, ltri, running):
    @pl.when((pl.program_id(0) == 0) & (pl.program_id(1) == 0))
    def _():
        wbf[...] = w_ref[...].astype(BF16)
        wrbf[...] = wr_ref[...].astype(BF16)
        r = lax.broadcasted_iota(jnp.int32, (TS, TS), 0)
        c = lax.broadcasted_iota(jnp.int32, (TS, TS), 1)
        ltri[...] = jnp.where(c < r, 1.0, 0.0).astype(BF16)
        running[...] = jnp.zeros_like(running)

    nsl = OUT_SLICES
    rows = TS // nsl
    sl = [slice(k * rows, (k + 1) * rows) for k in range(nsl)]
    mix = [_dot(yc_ref[r, :], wbf[0:D_CONV, :]) + _dot(ya_ref[r, :], wbf[D_CONV:, :]) + b_ref[...] for r in sl]
    x1 = [_layer_norm(ALPHA * x_ref[r, :] + g1_ref[...] * m, lg_ref[...], lb_ref[...]) for r, m in zip(sl, mix)]
    h2 = [v * (1.0 + sc2_ref[...]) + sh2_ref[...] for v in x1]
    for k in range(nsl):
        x1_ref[sl[k], :] = x1[k]
        _to_tiles(h2t_ref.at[pl.ds(k * rows * CHUNKS, rows * CHUNKS)], h2[k])
    logits = [_dot(v.astype(BF16), wrbf[...]) + br_ref[...] for v in h2]
    routed = [_route_bucket(lg) for lg in logits]
    mine = [lanef == bucket for bucket, lanef in routed]
    onehot = [jnp.where(mk, 1.0, 0.0) for mk in mine]
    within = [_dot(ltri[0:rows, 0:rows], oh.astype(BF16)) for oh in onehot]
    base = running[...]
    for k in range(nsl):
        rank = jnp.sum(jnp.where(mine[k], within[k] + base, 0.0), axis=-1, keepdims=True)
        base = base + jnp.sum(onehot[k], axis=0, keepdims=True)
        code = routed[k][0].astype(jnp.int32) * (1 << CODE_SHIFT) + rank.astype(jnp.int32)
        code_ref[sl[k], :] = jnp.broadcast_to(code, (rows, LANES))
    running[...] = base
    counts_ref[...] = running[...]


def _outproj_route(x, y_conv, y_attn, w_out, b_out, g1, ln_g, ln_b, sc2, sh2, w_r, b_r):
    bsz, seq, _ = x.shape
    nst = seq // TS
    vec = lambda n: pl.BlockSpec((1, n), lambda b, s: (0, 0))
    mod = pl.BlockSpec((None, None, 1, D_MODEL), lambda b, s: (b, 0, 0, 0))
    tile = lambda n: pl.BlockSpec((None, TS, n), lambda b, s: (b, s, 0))
    return pl.pallas_call(
        _outproj_kernel,
        grid=(bsz, seq // TS),
        in_specs=[tile(D_MODEL), tile(D_CONV), tile(D_ATTN),
                  pl.BlockSpec((D_MODEL, D_MODEL), lambda b, s: (0, 0)), vec(D_MODEL),
                  mod, vec(D_MODEL), vec(D_MODEL), mod, mod,
                  pl.BlockSpec((D_MODEL, ROUTER_COLS), lambda b, s: (0, 0)), vec(ROUTER_COLS)],
        out_specs=[tile(D_MODEL), pl.BlockSpec((TS * CHUNKS, LANES), lambda b, s: (b * nst + s, 0)),
                   tile(LANES), vec(LANES)],
        out_shape=[jax.ShapeDtypeStruct((bsz, seq, D_MODEL), F32),
                   jax.ShapeDtypeStruct((bsz * seq * CHUNKS, LANES), F32),
                   jax.ShapeDtypeStruct((bsz, seq, LANES), jnp.int32),
                   jax.ShapeDtypeStruct((1, LANES), F32)],
        scratch_shapes=[pltpu.VMEM((D_MODEL, D_MODEL), BF16), pltpu.VMEM((D_MODEL, ROUTER_COLS), BF16),
                        pltpu.VMEM((TS, TS), BF16), pltpu.VMEM((1, LANES), F32)],
        compiler_params=pltpu.CompilerParams(
            dimension_semantics=("arbitrary", "arbitrary"), vmem_limit_bytes=VMEM_LIMIT),
        name="outproj_route",
    )(x, y_conv, y_attn, w_out, b_out, g1, ln_g, ln_b, sc2, sh2, w_r, b_r)


def _row_of(starts_ref, code):
    return (starts_ref[lax.shift_right_logical(code, jnp.int32(CODE_SHIFT))]
            + (code & ((1 << CODE_SHIFT) - 1)))


CHUNKS = D_MODEL // LANES
assert CHUNKS == SUBLANES


def _tok_rows(p):
    return pl.ds(pl.multiple_of(p * CHUNKS, CHUNKS), CHUNKS)


def _to_tiles(ref, x):
    for c in range(CHUNKS):
        ref[pl.ds(c, x.shape[0], stride=CHUNKS), :] = x[:, c * LANES:(c + 1) * LANES]


def _from_tiles(ref, n):
    return jnp.concatenate([ref[pl.ds(c, n, stride=CHUNKS), :] for c in range(CHUNKS)], axis=1)


def _wait_tokens(hbm_ref, n, sem):
    pltpu.make_async_copy(hbm_ref.at[pl.ds(0, n * CHUNKS)], hbm_ref.at[pl.ds(0, n * CHUNKS)], sem).wait()


def _scatter_kernel(starts_ref, zrow_ref, valid_ref, code_ref, h2t_ref, hs_ref, pos_ref, hbuf, zeros, sems, zsem):
    nsteps = pl.num_programs(0) * pl.num_programs(1)
    step = pl.program_id(0) * pl.num_programs(1) + pl.program_id(1)
    slot = step % 2

    def _tile_rows(tok):
        return pl.ds(pl.multiple_of(tok * CHUNKS, TMM * CHUNKS), TMM * CHUNKS)

    def zero_copy(b):
        return pltpu.make_async_copy(zeros, hs_ref.at[_tile_rows(zrow_ref[b])], zsem)

    def tail_copy(t):
        return pltpu.make_async_copy(zeros, hs_ref.at[_tile_rows(t * TMM)], zsem)

    @pl.when(step == 0)
    def _():
        zeros[...] = jnp.zeros_like(zeros)
        n_tiles = hs_ref.shape[0] // (TMM * CHUNKS)

        def start(b, c):
            @pl.when(zrow_ref[b] >= 0)
            def _():
                zero_copy(b).start()
            return c

        def wait(b, c):
            @pl.when(zrow_ref[b] >= 0)
            def _():
                zero_copy(b).wait()
            return c

        def start_tail(t, c):
            @pl.when(valid_ref[t] == 0)
            def _():
                tail_copy(t).start()
            return c

        def wait_tail(t, c):
            @pl.when(valid_ref[t] == 0)
            def _():
                tail_copy(t).wait()
            return c

        lax.fori_loop(0, N_BUCKETS, start, 0)
        lax.fori_loop(0, n_tiles, start_tail, 0)
        lax.fori_loop(0, N_BUCKETS, wait, 0)
        lax.fori_loop(0, n_tiles, wait_tail, 0)

    @pl.when(step >= 2)
    def _():
        _wait_tokens(hs_ref, TS, sems.at[slot])

    hbuf[slot] = h2t_ref[...]

    for r in range(TS):
        pos = _row_of(starts_ref, code_ref[0, r])
        pos_ref[0, r] = pos
        pltpu.make_async_copy(hbuf.at[slot, _tok_rows(r)], hs_ref.at[_tok_rows(pos)],
                              sems.at[slot]).start(priority=r % 2)

    @pl.when(step == nsteps - 1)
    def _():
        _wait_tokens(hs_ref, TS, sems.at[slot])
        _wait_tokens(hs_ref, TS, sems.at[1 - slot])


def _scatter_rows(starts, zrow, valid, code, h2t, n_rows):
    bsz, seq = code.shape
    nst = seq // TS
    grid_spec = pltpu.PrefetchScalarGridSpec(
        num_scalar_prefetch=3,
        grid=(bsz, nst),
        in_specs=[pl.BlockSpec((None, 1, TS), lambda b, s, *_: (b * nst + s, 0, 0), memory_space=pltpu.SMEM),
                  pl.BlockSpec((TS * CHUNKS, LANES), lambda b, s, *_: (b * nst + s, 0))],
        out_specs=[pl.BlockSpec(memory_space=pl.ANY),
                   pl.BlockSpec((None, 1, TS), lambda b, s, *_: (b * nst + s, 0, 0), memory_space=pltpu.SMEM)],
        scratch_shapes=[pltpu.VMEM((2, TS * CHUNKS, LANES), F32), pltpu.VMEM((TMM * CHUNKS, LANES), F32),
                        pltpu.SemaphoreType.DMA((2,)), pltpu.SemaphoreType.DMA],
    )
    return pl.pallas_call(
        _scatter_kernel,
        grid_spec=grid_spec,
        out_shape=[jax.ShapeDtypeStruct((n_rows * CHUNKS, LANES), F32),
                   jax.ShapeDtypeStruct((bsz * nst, 1, TS), jnp.int32)],
        compiler_params=pltpu.CompilerParams(dimension_semantics=("arbitrary", "arbitrary")),
        name="moe_scatter",
    )(starts, zrow, valid, code.reshape(bsz * nst, 1, TS), h2t)


def _moe_kernel(tlo_ref, thi_ref, valid_ref, h_ref, wg_hbm, wu_hbm, wd_hbm, wr_ref, br_ref, y_ref,
                wrbf, wg, wu, wd, sg, su, sd, sems):
    i = pl.program_id(0)
    t0 = i * NSUB
    group = lax.shift_right_logical(tlo_ref[t0], jnp.int32(3))
    prev_group = lax.shift_right_logical(tlo_ref[jnp.maximum(t0 - NSUB, 0)], jnp.int32(3))

    @pl.when(i == 0)
    def _():
        wrbf[...] = wr_ref[...].astype(BF16)

    @pl.when((i == 0) | (group != prev_group))
    def _():
        def copies(e, slot):
            ex = group * EXPERTS_PER_GROUP + e
            return (pltpu.make_async_copy(wg_hbm.at[ex], sg.at[slot], sems.at[slot, 0]),
                    pltpu.make_async_copy(wu_hbm.at[ex], su.at[slot], sems.at[slot, 1]),
                    pltpu.make_async_copy(wd_hbm.at[ex], sd.at[slot], sems.at[slot, 2]))

        for cp in copies(0, 0):
            cp.start()

        def land(e, c):
            slot = e % 2

            @pl.when(e + 1 < EXPERTS_PER_GROUP)
            def _():
                for cp in copies(e + 1, 1 - slot):
                    cp.start()

            for cp in copies(e, slot):
                cp.wait()
            wg[e] = sg[slot].astype(BF16)
            wu[e] = su[slot].astype(BF16)
            wd[e] = sd[slot].astype(BF16)
            return c

        lax.fori_loop(0, EXPERTS_PER_GROUP, land, 0)

    @pl.when(valid_ref[t0] == 1)
    def _():
        lane = lax.broadcasted_iota(jnp.int32, (TMM, ROUTER_COLS), 1)
        local = EXPERTS_PER_GROUP - 1
        subs = range(NSUB)
        tile = lambda ref, j: ref.at[pl.ds(j * TMM * CHUNKS, TMM * CHUNKS)]
        hs = [_from_tiles(tile(h_ref, j), TMM).astype(BF16) for j in subs]
        es = [(tlo_ref[t0 + j] & local, thi_ref[t0 + j] & local) for j in subs]
        pre = [[(_dot(hs[j], wg[e]), _dot(hs[j], wu[e])) for e in es[j]] for j in subs]
        act = [[(a * jax.nn.sigmoid(a) * u).astype(BF16) for a, u in pre[j]] for j in subs]
        ys = [[_dot(act[j][k], wd[es[j][k]]) for k in (0, 1)] for j in subs]
        for j in subs:
            logits = _dot(hs[j], wrbf[...]) + br_ref[...]
            p_top = _group_softmax_top(logits, lane)[3]
            l_lo = jnp.sum(jnp.where(lane == tlo_ref[t0 + j], logits, 0.0), axis=-1, keepdims=True)
            l_hi = jnp.sum(jnp.where(lane == thi_ref[t0 + j], logits, 0.0), axis=-1, keepdims=True)
            w_lo = p_top / (1.0 + jnp.exp(l_hi - l_lo))
            w_hi = p_top / (1.0 + jnp.exp(l_lo - l_hi))
            _to_tiles(tile(y_ref, j), w_lo * ys[j][0] + w_hi * ys[j][1])

    @pl.when(valid_ref[t0] == 0)
    def _():
        y_ref[...] = jnp.zeros_like(y_ref)


def _moe_sorted(tlo, thi, valid, h_sorted, w_gate, w_up, w_down, w_r, b_r):
    n_rows = h_sorted.shape[0] // CHUNKS
    rows = pl.BlockSpec((NSUB * TMM * CHUNKS, LANES), lambda i, *_: (i, 0))
    hbm = pl.BlockSpec(memory_space=pl.ANY)
    w_in_shape, w_out_shape = (D_MODEL, D_EXPERT), (D_EXPERT, D_MODEL)
    grid_spec = pltpu.PrefetchScalarGridSpec(
        num_scalar_prefetch=3,
        grid=(n_rows // (NSUB * TMM),),
        in_specs=[rows, hbm, hbm, hbm,
                  pl.BlockSpec((D_MODEL, ROUTER_COLS), lambda i, *_: (0, 0)),
                  pl.BlockSpec((1, ROUTER_COLS), lambda i, *_: (0, 0))],
        out_specs=rows,
        scratch_shapes=[pltpu.VMEM((D_MODEL, ROUTER_COLS), BF16),
                        pltpu.VMEM((EXPERTS_PER_GROUP,) + w_in_shape, BF16),
                        pltpu.VMEM((EXPERTS_PER_GROUP,) + w_in_shape, BF16),
                        pltpu.VMEM((EXPERTS_PER_GROUP,) + w_out_shape, BF16),
                        pltpu.VMEM((2,) + w_in_shape, F32), pltpu.VMEM((2,) + w_in_shape, F32),
                        pltpu.VMEM((2,) + w_out_shape, F32), pltpu.SemaphoreType.DMA((2, 3))],
    )
    return pl.pallas_call(
        _moe_kernel,
        grid_spec=grid_spec,
        out_shape=jax.ShapeDtypeStruct((n_rows * CHUNKS, LANES), F32),
        compiler_params=pltpu.CompilerParams(dimension_semantics=("arbitrary",), vmem_limit_bytes=VMEM_LIMIT),
        name="moe_sorted",
    )(tlo, thi, valid, h_sorted, w_gate, w_up, w_down, w_r, b_r)


def _final_kernel(pos_ref, posn_ref, x1_ref, g2_ref, lg_ref, lb_ref, ys_ref, o_ref, ybuf, sems):
    nsteps = pl.num_programs(0) * pl.num_programs(1)
    step = pl.program_id(0) * pl.num_programs(1) + pl.program_id(1)
    slot = step % 2

    def gather(rows_ref, sl):
        def issue(i, c):
            for u in range(SUBLANES):
                r = i * SUBLANES + u
                pltpu.make_async_copy(ys_ref.at[_tok_rows(rows_ref[0, r])], ybuf.at[sl, _tok_rows(r)],
                                      sems.at[sl]).start(priority=u % 2)
            return c

        lax.fori_loop(0, TS // SUBLANES, issue, 0)

    @pl.when(step == 0)
    def _():
        gather(pos_ref, slot)

    @pl.when(step + 1 < nsteps)
    def _():
        gather(posn_ref, 1 - slot)

    _wait_tokens(ys_ref, TS, sems.at[slot])
    y = _from_tiles(ybuf.at[slot], TS)
    o_ref[...] = _layer_norm(ALPHA * x1_ref[...] + g2_ref[...] * y, lg_ref[...], lb_ref[...])


def _unsort_ln2(pos, x1, g2, ln_g, ln_b, y_sorted):
    bsz, seq, _ = x1.shape
    nst = seq // TS
    last = bsz * nst - 1
    return pl.pallas_call(
        _final_kernel,
        grid=(bsz, nst),
        in_specs=[pl.BlockSpec((None, 1, TS), lambda b, s: (b * nst + s, 0, 0), memory_space=pltpu.SMEM),
                  pl.BlockSpec((None, 1, TS), lambda b, s: (jnp.minimum(b * nst + s + 1, last), 0, 0),
                               memory_space=pltpu.SMEM),
                  pl.BlockSpec((None, TS, D_MODEL), lambda b, s: (b, s, 0)),
                  pl.BlockSpec((None, None, 1, D_MODEL), lambda b, s: (b, 0, 0, 0)),
                  pl.BlockSpec((1, D_MODEL), lambda b, s: (0, 0)),
                  pl.BlockSpec((1, D_MODEL), lambda b, s: (0, 0)),
                  pl.BlockSpec(memory_space=pl.ANY)],
        out_specs=pl.BlockSpec((None, TS, D_MODEL), lambda b, s: (b, s, 0)),
        out_shape=jax.ShapeDtypeStruct((bsz, seq, D_MODEL), F32),
        scratch_shapes=[pltpu.VMEM((2, TS * CHUNKS, LANES), F32), pltpu.SemaphoreType.DMA((2,))],
        compiler_params=pltpu.CompilerParams(dimension_semantics=("arbitrary", "arbitrary")),
        name="unsort_ln2",
    )(pos, pos, x1, g2, ln_g, ln_b, y_sorted)


def _pair_layout(a, axis):
    shp = a.shape
    a = a.reshape(shp[:axis] + (N_KV_HEADS, N_HEADS // N_KV_HEADS, HEAD_DIM) + shp[axis + 1:])
    return jnp.swapaxes(a, axis, axis + 1).reshape(shp)


def _tile_tables(counts, n_tiles_max):
    b_np = np.arange(N_BUCKETS)
    g_np = b_np // N_PAIRS
    in_group_before = jnp.asarray((g_np[None, :] == g_np[:, None]) & (b_np[None, :] < b_np[:, None]))
    member = jnp.asarray(g_np[None, :] == np.arange(N_GROUPS)[:, None])
    earlier_group = jnp.asarray(np.arange(N_GROUPS)[None, :] < np.arange(N_GROUPS)[:, None])
    e_lo = jnp.asarray(g_np * EXPERTS_PER_GROUP + np.asarray(PAIRS)[b_np % N_PAIRS, 0], jnp.int32)
    e_hi = jnp.asarray(g_np * EXPERTS_PER_GROUP + np.asarray(PAIRS)[b_np % N_PAIRS, 1], jnp.int32)

    n_tile_b = jnp.floor((counts + (TMM - 1.0)) * (1.0 / TMM))
    g_tiles = jnp.sum(jnp.where(member, n_tile_b[None, :], 0.0), axis=1)
    g_padded = jnp.floor((g_tiles + (NSUB - 1.0)) * (1.0 / NSUB)) * NSUB
    g_start = jnp.sum(jnp.where(earlier_group, g_padded[None, :], 0.0), axis=1)
    g_end = g_start + g_padded
    b_start = (jnp.sum(jnp.where(member.T, g_start[None, :], 0.0), axis=1)
               + jnp.sum(jnp.where(in_group_before, n_tile_b[None, :], 0.0), axis=1))
    b_end = b_start + n_tile_b
    starts = (b_start * TMM).astype(jnp.int32)
    zrow = jnp.where(n_tile_b > 0, (b_end - 1.0) * TMM, -1.0).astype(jnp.int32)

    t = jnp.arange(n_tiles_max, dtype=jnp.int32).astype(F32)
    g_of_t = jnp.minimum(jnp.sum(jnp.where(g_end[None, :] <= t[:, None], 1, 0), axis=1), N_GROUPS - 1)
    g_hot = g_of_t[:, None] == jnp.arange(N_GROUPS, dtype=jnp.int32)[None, :]
    local_t = t - jnp.sum(jnp.where(g_hot, g_start[None, :], 0.0), axis=1)
    valid = (local_t < jnp.sum(jnp.where(g_hot, g_tiles[None, :], 0.0), axis=1)) & (t < jnp.sum(g_padded))
    bucket = jnp.sum(jnp.where(b_end[None, :] <= t[:, None], 1, 0), axis=1)
    bucket = jnp.where(valid, bucket, g_of_t * N_PAIRS + (N_PAIRS - 1))
    b_hot = bucket[:, None] == jnp.asarray(b_np, jnp.int32)[None, :]
    tlo = jnp.sum(jnp.where(b_hot, e_lo[None, :], 0), axis=1)
    thi = jnp.sum(jnp.where(b_hot, e_hi[None, :], 0), axis=1)
    pad = jnp.zeros((LANES - N_BUCKETS,), jnp.int32)
    return (jnp.concatenate([starts, pad]), jnp.concatenate([zrow, pad - 1]), tlo, thi,
            jnp.where(valid, 1, 0).astype(jnp.int32))


def kernel(x, c, w_ada, b_ada, w_in, b_in, conv_w, conv_b, conv_ln_g, conv_ln_b, conv_out_g, sinks,
           attn_out_g, w_out, b_out, ln1_g, ln1_b, w_router_group, b_router_group, w_router_expert,
           b_router_expert, w_gate, w_up, w_down, ln2_g, ln2_b):
    assert w_ada.shape[0] == DEPTH
    bsz, seq, _ = x.shape
    row = lambda v: v[0][None, :]

    mod = _modulation(c, w_ada[0], row(b_ada)).reshape(bsz, 6, 1, D_MODEL)
    sh1, sc1, g1, sh2, sc2, g2 = [mod[:, j:j + 1] for j in range(6)]

    q0 = 2 * D_CONV
    w_q = _pair_layout(w_in[0][:, q0:q0 + D_ATTN], 1)
    b_q = _pair_layout(b_in[0][q0:q0 + D_ATTN], 0)[None, :]
    og_attn = _pair_layout(attn_out_g[0], 0)[None, :]
    w_o = jnp.concatenate([w_out[0][:D_CONV], _pair_layout(w_out[0][D_CONV:], 0)], axis=0)

    y_conv, q, kv = _inproj_conv(x, sh1, sc1, w_in[0], row(b_in), w_q, b_q, conv_w[0], row(conv_b),
                                 row(conv_ln_g), row(conv_ln_b), row(conv_out_g))
    y_attn = _attention(q, kv, sinks[0], og_attn)

    pad = ROUTER_COLS - N_EXPERTS - N_GROUPS
    w_r = jnp.concatenate([w_router_expert[0], w_router_group[0], jnp.zeros((D_MODEL, pad), F32)], axis=1)
    b_r = jnp.concatenate([b_router_expert[0], b_router_group[0], jnp.zeros((pad,), F32)])[None, :]
    x1, h2t, code, counts = _outproj_route(x, y_conv, y_attn, w_o, row(b_out), g1, row(ln1_g), row(ln1_b),
                                      sc2, sh2, w_r, b_r)

    n_tiles_max = (bsz * seq) // TMM + N_BUCKETS + N_GROUPS * (NSUB - 1)
    n_tiles_max = -(-n_tiles_max // NSUB) * NSUB
    starts, zrow, tlo, thi, valid = _tile_tables(counts[0, :N_BUCKETS], n_tiles_max)
    h_sorted, pos = _scatter_rows(starts, zrow, valid, code[:, :, 0], h2t, n_tiles_max * TMM)
    y_sorted = _moe_sorted(tlo, thi, valid, h_sorted, w_gate[0], w_up[0], w_down[0], w_r, b_r)
    return _unsort_ln2(pos, x1, g2, row(ln2_g), row(ln2_b), y_sorted)
```

```python
import jax
import jax.numpy as jnp
import numpy as np
from jax import lax
from jax.experimental import pallas as pl
from jax.experimental.pallas import tpu as pltpu

F32 = jnp.float32
BF16 = jnp.bfloat16

D_MODEL = 1024
D_CONV = 512
CONV_WIDTH = 31
GROUP_DIM = 64
N_HEADS = 8
N_KV_HEADS = 2
HEAD_DIM = 64
D_ATTN = N_HEADS * HEAD_DIM
D_KV = N_KV_HEADS * HEAD_DIM
BLOCK = 128
D_IN = 2 * D_CONV + D_ATTN + 2 * D_KV
N_GROUPS = 4
EXPERTS_PER_GROUP = 8
N_EXPERTS = N_GROUPS * EXPERTS_PER_GROUP
D_EXPERT = D_MODEL // 4
DEPTH = 1
ALPHA = (2.0 * DEPTH) ** 0.25
EPS = 1e-5
NEG = -1e30

LANES = 128
SUBLANES = 8
TS = 512
HALO = 32
CR = 32
AB = 8
PAIR_SLOTS = [j + (N_HEADS // N_KV_HEADS) * g for j in range(N_HEADS // N_KV_HEADS) for g in range(N_KV_HEADS)]
ROUTER_COLS = LANES
VMEM_LIMIT = 56 * 1024 * 1024

PAIRS = [(lo, hi) for lo in range(EXPERTS_PER_GROUP) for hi in range(lo + 1, EXPERTS_PER_GROUP)]
N_PAIRS = len(PAIRS)
N_BUCKETS = N_GROUPS * N_PAIRS
TMM = 128
OUT_SLICES = 2
NSUB = 4
CODE_SHIFT = 16


def _dot(a, b):
    return jnp.dot(a, b, preferred_element_type=F32)


def _dot_nt(a, b):
    return lax.dot_general(a, b, (((1,), (1,)), ((), ())), preferred_element_type=F32)


def _fill_group_maps(bsel_ref, bexp_ref):
    c = bsel_ref.shape[0]
    ch = lax.broadcasted_iota(jnp.int32, (c, LANES), 0) // GROUP_DIM
    gi = lax.broadcasted_iota(jnp.int32, (c, LANES), 1)
    bsel_ref[...] = jnp.where(ch == gi, 1.0 / GROUP_DIM, 0.0).astype(BF16)
    gi2 = lax.broadcasted_iota(jnp.int32, (LANES, c), 0)
    ch2 = lax.broadcasted_iota(jnp.int32, (LANES, c), 1) // GROUP_DIM
    bexp_ref[...] = jnp.where(ch2 == gi2, 1.0, 0.0).astype(BF16)


def _group_rms(y, bsel, bexp):
    ms = _dot((y * y).astype(BF16), bsel)
    r = lax.rsqrt(ms + EPS)
    r_hi = r.astype(BF16)
    r_lo = (r - r_hi.astype(F32)).astype(BF16)
    return y * (_dot(r_hi, bexp) + _dot(r_lo, bexp))


def _layer_norm(y, g, b):
    mu = jnp.mean(y, axis=-1, keepdims=True)
    d = y - mu
    var = jnp.mean(d * d, axis=-1, keepdims=True)
    return d * lax.rsqrt(var + EPS) * g + b


def _mod_kernel(c_ref, w_ref, b_ref, o_ref):
    c = c_ref[...]
    c_act = c * jax.nn.sigmoid(c)
    o_ref[...] = jnp.dot(c_act, w_ref[...], preferred_element_type=F32,
                         precision=lax.Precision.HIGHEST) + b_ref[...]


def _modulation(c, w_ada, b_ada):
    bsz = c.shape[0]
    n = w_ada.shape[1]
    return pl.pallas_call(
        _mod_kernel,
        grid=(n // D_MODEL,),
        in_specs=[pl.BlockSpec((bsz, D_MODEL), lambda j: (0, 0)),
                  pl.BlockSpec((D_MODEL, D_MODEL), lambda j: (0, j)),
                  pl.BlockSpec((1, D_MODEL), lambda j: (0, j))],
        out_specs=pl.BlockSpec((bsz, D_MODEL), lambda j: (0, j)),
        out_shape=jax.ShapeDtypeStruct((bsz, n), F32),
        name="adaln_mod",
    )(c, w_ada, b_ada)


def _inproj_conv_kernel(x_ref, sh_ref, sc_ref, w_ref, b_ref, wq_ref, bq_ref, cw_ref, cb_ref, lg_ref, lb_ref, og_ref,
                        yc_ref, q_ref, kv_ref, wbf, gs, cacc, wb, bsel, bexp):
    first = (pl.program_id(0) == 0) & (pl.program_id(1) == 0)
    s = pl.program_id(1)

    @pl.when(first)
    def _():
        wbf[...] = w_ref[...].astype(BF16)
        wbf[:, 2 * D_CONV:2 * D_CONV + D_ATTN] = wq_ref[...].astype(BF16)
        wb[...] = jnp.broadcast_to(cw_ref[...][:, None, :], wb.shape)
        _fill_group_maps(bsel, bexp)

    n_ct = D_CONV // LANES

    @pl.when(s == 0)
    def _():
        gs[0, :, 0:HALO, :] = jnp.zeros((n_ct, HALO, LANES), F32)

    @pl.when(s > 0)
    def _():
        gs[0, :, 0:HALO, :] = gs[0, :, TS:TS + HALO, :]

    h = (x_ref[...] * (1.0 + sc_ref[...]) + sh_ref[...]).astype(BF16)
    q0 = 2 * D_CONV
    k0 = q0 + D_ATTN
    cb = cb_ref[...]
    lg = lg_ref[...]
    lb = lb_ref[...]
    og = og_ref[...]
    n_sh = TS + HALO - SUBLANES
    half = TS // 2
    late_cols = 2 * LANES

    def glu_unit(rows, j):
        cs = slice(j * late_cols, (j + 1) * late_cols)
        u_a = _dot(h[rows], wbf[:, cs]) + b_ref[:, cs]
        u_b = _dot(h[rows], wbf[:, D_CONV + j * late_cols:D_CONV + (j + 1) * late_cols]) \
            + b_ref[:, D_CONV + j * late_cols:D_CONV + (j + 1) * late_cols]
        glu = u_a * jax.nn.sigmoid(u_b)
        for cc in range(late_cols // LANES):
            c = j * (late_cols // LANES) + cc
            gs[0, c, HALO + rows.start:HALO + rows.stop, :] = glu[:, cc * LANES:(cc + 1) * LANES]

    def late_proj(j):
        c0 = j * late_cols
        if c0 < D_ATTN:
            q = _dot(h, wbf[:, q0 + c0:q0 + c0 + late_cols]) + bq_ref[:, c0:c0 + late_cols]
            q_ref[:, c0:c0 + late_cols] = (q * (HEAD_DIM ** -0.5)).astype(BF16)
        else:
            kv_ref[...] = (_dot(h, wbf[:, k0:k0 + 2 * D_KV]) + b_ref[:, k0:k0 + 2 * D_KV]).astype(BF16)

    def shifted(lo, hi):
        for j in range(1, SUBLANES):
            gs[j, :, lo:hi, :] = gs[0, :, lo + j:hi + j, :]

    def conv_chunk(c, r0):
        ls = slice(c * LANES, (c + 1) * LANES)
        acc = jnp.broadcast_to(cb[:, ls], (CR, LANES)).reshape(CR // SUBLANES, SUBLANES, LANES)
        for k in range(CONV_WIDTH):
            off = HALO - (CONV_WIDTH - 1) + k
            a0 = r0 + (off // SUBLANES) * SUBLANES
            seg = gs[off % SUBLANES, c, a0:a0 + CR, :]
            acc = acc + wb[k, :, ls][None] * seg.reshape(CR // SUBLANES, SUBLANES, LANES)
        cacc[c, r0:r0 + CR, :] = acc.reshape(CR, LANES)

    first, second = slice(0, half), slice(half, TS)
    n_glu = D_CONV // late_cols
    n_late = (D_ATTN + 2 * D_KV) // late_cols
    for j in range(n_glu):
        glu_unit(first, j)
    split = HALO - SUBLANES + half
    shifted(0, split)
    chunks_a = [(c, r0) for c in range(n_ct) for r0 in range(0, half, CR)]
    chunks_b = [(c, r0) for c in range(n_ct) for r0 in range(half, TS, CR)]
    units_a = [lambda j=j: glu_unit(second, j) for j in range(n_glu)] + [lambda: late_proj(0)]
    units_b = [lambda j=j: late_proj(j) for j in range(1, n_late)]
    for chunks, units in ((chunks_a, units_a), (chunks_b, units_b)):
        if chunks is chunks_b:
            shifted(split, n_sh)
        every = len(chunks) // len(units)
        for n, (c, r0) in enumerate(chunks):
            conv_chunk(c, r0)
            if n % every == 0 and n // every < len(units):
                units[n // every]()
    y = _layer_norm(jnp.concatenate([cacc[c] for c in range(n_ct)], axis=1), lg, lb)
    y = y * jax.nn.sigmoid(y)
    yc_ref[...] = (_group_rms(y, bsel[...], bexp[...]) * og).astype(BF16)


def _inproj_conv(x, sh1, sc1, w_in, b_in, w_q, b_q, conv_w, conv_b, ln_g, ln_b, out_g):
    bsz, seq, _ = x.shape
    vec = lambda n: pl.BlockSpec((1, n), lambda b, s: (0, 0))
    mod = pl.BlockSpec((None, None, 1, D_MODEL), lambda b, s: (b, 0, 0, 0))
    tile = lambda n: pl.BlockSpec((None, TS, n), lambda b, s: (b, s, 0))
    return pl.pallas_call(
        _inproj_conv_kernel,
        grid=(bsz, seq // TS),
        in_specs=[tile(D_MODEL), mod, mod,
                  pl.BlockSpec((D_MODEL, D_IN), lambda b, s: (0, 0)), vec(D_IN),
                  pl.BlockSpec((D_MODEL, D_ATTN), lambda b, s: (0, 0)), vec(D_ATTN),
                  pl.BlockSpec((CONV_WIDTH, D_CONV), lambda b, s: (0, 0)),
                  vec(D_CONV), vec(D_CONV), vec(D_CONV), vec(D_CONV)],
        out_specs=[tile(D_CONV), tile(D_ATTN), tile(2 * D_KV)],
        out_shape=[jax.ShapeDtypeStruct((bsz, seq, D_CONV), BF16),
                   jax.ShapeDtypeStruct((bsz, seq, D_ATTN), BF16),
                   jax.ShapeDtypeStruct((bsz, seq, 2 * D_KV), BF16)],
        scratch_shapes=[pltpu.VMEM((D_MODEL, D_IN), BF16),
                        pltpu.VMEM((SUBLANES, D_CONV // LANES, TS + HALO, LANES), F32),
                        pltpu.VMEM((D_CONV // LANES, TS, LANES), F32),
                        pltpu.VMEM((CONV_WIDTH, SUBLANES, D_CONV), F32),
                        pltpu.VMEM((D_CONV, LANES), BF16),
                        pltpu.VMEM((LANES, D_CONV), BF16)],
        compiler_params=pltpu.CompilerParams(
            dimension_semantics=("arbitrary", "arbitrary"), vmem_limit_bytes=VMEM_LIMIT),
        name="inproj_conv",
    )(x, sh1, sc1, w_in, b_in, w_q, b_q, conv_w, conv_b, ln_g, ln_b, out_g)


def _attn_kernel(sinks_ref, q_ref, kvc_ref, kvp_ref, og_ref, o_ref, bsel, bexp):
    n = pl.program_id(1)

    @pl.when((pl.program_id(0) == 0) & (n == 0))
    def _():
        _fill_group_maps(bsel, bexp)

    qr = lax.broadcasted_iota(jnp.int32, (BLOCK, BLOCK), 0)
    kc = lax.broadcasted_iota(jnp.int32, (BLOCK, BLOCK), 1)
    from_prev = kc > qr
    lane = lax.broadcasted_iota(jnp.int32, (BLOCK, LANES), 1)
    lower = lane < HEAD_DIM
    lower_bf = jnp.where(lower, 1.0, 0.0).astype(BF16)
    upper_bf = jnp.where(lower, 0.0, 1.0).astype(BF16)
    og = og_ref[...]
    blocks = range(AB)
    kvs = []
    for blk in blocks:
        r0 = blk * BLOCK
        if blk == 0:
            kvs.append(jnp.concatenate([kvp_ref[...], kvc_ref[0:BLOCK, :]], axis=0))
        else:
            kvs.append(kvc_ref[r0 - BLOCK:r0 + BLOCK, :])
    s_all = []
    for blk in blocks:
        r0 = blk * BLOCK
        parts = []
        for j in range(D_ATTN // LANES):
            col = q_ref[r0:r0 + BLOCK, j * LANES:(j + 1) * LANES]
            parts += [col * lower_bf, col * upper_bf]
        s_all.append(_dot_nt(jnp.concatenate(parts, axis=0), kvs[blk][:, 0:D_KV]))
    probs, dens = [], []
    for blk in blocks:
        pb = []
        den = jnp.ones((BLOCK, LANES), F32)
        for slot in range(N_HEADS):
            s_prev = s_all[blk][slot * BLOCK:(slot + 1) * BLOCK, 0:BLOCK]
            s_own = s_all[blk][slot * BLOCK:(slot + 1) * BLOCK, BLOCK:2 * BLOCK]
            if blk == 0:
                s_prev = jnp.where(n > 0, s_prev, NEG)
            sc = jnp.where(from_prev, s_prev, s_own)
            sink = sinks_ref[PAIR_SLOTS[slot]]
            m = jnp.maximum(jnp.max(sc, axis=-1, keepdims=True), sink)
            p = jnp.exp(sc - m)
            den = jnp.where(lane == slot, jnp.sum(p, axis=-1, keepdims=True) + jnp.exp(sink - m), den)
            p = p.astype(BF16)
            zero = jnp.zeros_like(p)
            pb.append(jnp.concatenate([jnp.where(from_prev, p, zero), jnp.where(from_prev, zero, p)], axis=1))
        probs.append(jnp.concatenate(pb, axis=0))
        dens.append(den)
    pvs = [_dot(probs[blk], kvs[blk][:, D_KV:2 * D_KV]) for blk in blocks]
    for blk in blocks:
        pv, den = pvs[blk], dens[blk]
        o = jnp.concatenate(
            [jnp.where(lower, pv[(2 * j) * BLOCK:(2 * j + 1) * BLOCK], pv[(2 * j + 1) * BLOCK:(2 * j + 2) * BLOCK])
             for j in range(D_ATTN // LANES)], axis=1)
        ms = _dot((o * o).astype(BF16), bsel[...])
        r = lax.rsqrt(ms + EPS * den * den)
        r_hi = r.astype(BF16)
        r_lo = (r - r_hi.astype(F32)).astype(BF16)
        scale = _dot(r_hi, bexp[...]) + _dot(r_lo, bexp[...])
        o_ref[blk * BLOCK:(blk + 1) * BLOCK, :] = (o * scale * og).astype(BF16)


def _attention(q, kv, sinks, out_g):
    bsz, seq, _ = q.shape
    rows = AB * BLOCK
    grid_spec = pltpu.PrefetchScalarGridSpec(
        num_scalar_prefetch=1,
        grid=(bsz, seq // rows),
        in_specs=[pl.BlockSpec((None, rows, D_ATTN), lambda b, n, sk: (b, n, 0)),
                  pl.BlockSpec((None, rows, 2 * D_KV), lambda b, n, sk: (b, n, 0)),
                  pl.BlockSpec((None, BLOCK, 2 * D_KV), lambda b, n, sk: (b, jnp.maximum(AB * n - 1, 0), 0)),
                  pl.BlockSpec((1, D_ATTN), lambda b, n, sk: (0, 0))],
        out_specs=pl.BlockSpec((None, rows, D_ATTN), lambda b, n, sk: (b, n, 0)),
        scratch_shapes=[pltpu.VMEM((D_ATTN, LANES), BF16), pltpu.VMEM((LANES, D_ATTN), BF16)],
    )
    return pl.pallas_call(
        _attn_kernel,
        grid_spec=grid_spec,
        out_shape=jax.ShapeDtypeStruct((bsz, seq, D_ATTN), BF16),
        compiler_params=pltpu.CompilerParams(dimension_semantics=("arbitrary", "arbitrary")),
        name="swa_attention",
    )(sinks, q, kv, kv, out_g)


def _group_softmax_top(logits, lane):
    is_g = (lane >= N_EXPERTS) & (lane < N_EXPERTS + N_GROUPS)
    gl = jnp.where(is_g, logits, NEG)
    gmax = jnp.max(gl, axis=-1, keepdims=True)
    p_top = 1.0 / jnp.sum(jnp.where(is_g, jnp.exp(gl - gmax), 0.0), axis=-1, keepdims=True)
    return is_g, gl, gmax, p_top


def _route_bucket(logits):
    rows = logits.shape[0]
    lane = lax.broadcasted_iota(jnp.int32, (rows, ROUTER_COLS), 1)
    lanef = lane.astype(F32)
    is_g, gl, gmax, _ = _group_softmax_top(logits, lane)
    gidx = jnp.min(jnp.where(is_g & (gl == gmax), lanef - N_EXPERTS, 99.0), axis=-1, keepdims=True)
    in_grp = (lane < N_EXPERTS) & ((lane // EXPERTS_PER_GROUP) == gidx.astype(jnp.int32))
    el = jnp.where(in_grp, logits, NEG)
    m1 = jnp.max(el, axis=-1, keepdims=True)
    i1 = jnp.min(jnp.where(in_grp & (el == m1), lanef, 999.0), axis=-1, keepdims=True)
    rest = in_grp & (lanef != i1)
    el2 = jnp.where(rest, logits, NEG)
    m2 = jnp.max(el2, axis=-1, keepdims=True)
    i2 = jnp.min(jnp.where(rest & (el2 == m2), lanef, 999.0), axis=-1, keepdims=True)
    lo = jnp.minimum(i1, i2) - EXPERTS_PER_GROUP * gidx
    hi = jnp.maximum(i1, i2) - EXPERTS_PER_GROUP * gidx
    pair = lo * (EXPERTS_PER_GROUP - 1) - lo * (lo - 1.0) * 0.5 + (hi - lo - 1.0)
    return gidx * N_PAIRS + pair, lanef


def _outproj_kernel(x_ref, yc_ref, ya_ref, w_ref, b_ref, g1_ref, lg_ref, lb_ref, sc2_ref, sh2_ref,
                    wr_ref, br_ref, x1_ref, h2t_ref, code_ref, counts_ref, wbf, wrbf, ltri, running):
    @pl.when((pl.program_id(0) == 0) & (pl.program_id(1) == 0))
    def _():
        wbf[...] = w_ref[...].astype(BF16)
        wrbf[...] = wr_ref[...].astype(BF16)
        r = lax.broadcasted_iota(jnp.int32, (TS, TS), 0)
        c = lax.broadcasted_iota(jnp.int32, (TS, TS), 1)
        ltri[...] = jnp.where(c < r, 1.0, 0.0).astype(BF16)
        running[...] = jnp.zeros_like(running)

    nsl = OUT_SLICES
    rows = TS // nsl
    sl = [slice(k * rows, (k + 1) * rows) for k in range(nsl)]
    mix = [_dot(yc_ref[r, :], wbf[0:D_CONV, :]) + _dot(ya_ref[r, :], wbf[D_CONV:, :]) + b_ref[...] for r in sl]
    x1 = [_layer_norm(ALPHA * x_ref[r, :] + g1_ref[...] * m, lg_ref[...], lb_ref[...]) for r, m in zip(sl, mix)]
    h2 = [v * (1.0 + sc2_ref[...]) + sh2_ref[...] for v in x1]
    for k in range(nsl):
        x1_ref[sl[k], :] = x1[k]
        _to_tiles(h2t_ref.at[pl.ds(k * rows * CHUNKS, rows * CHUNKS)], h2[k])
    logits = [_dot(v.astype(BF16), wrbf[...]) + br_ref[...] for v in h2]
    routed = [_route_bucket(lg) for lg in logits]
    mine = [lanef == bucket for bucket, lanef in routed]
    onehot = [jnp.where(mk, 1.0, 0.0) for mk in mine]
    within = [_dot(ltri[0:rows, 0:rows], oh.astype(BF16)) for oh in onehot]
    base = running[...]
    for k in range(nsl):
        rank = jnp.sum(jnp.where(mine[k], within[k] + base, 0.0), axis=-1, keepdims=True)
        base = base + jnp.sum(onehot[k], axis=0, keepdims=True)
        code = routed[k][0].astype(jnp.int32) * (1 << CODE_SHIFT) + rank.astype(jnp.int32)
        code_ref[sl[k], :] = jnp.broadcast_to(code, (rows, LANES))
    running[...] = base
    counts_ref[...] = running[...]


def _outproj_route(x, y_conv, y_attn, w_out, b_out, g1, ln_g, ln_b, sc2, sh2, w_r, b_r):
    bsz, seq, _ = x.shape
    nst = seq // TS
    vec = lambda n: pl.BlockSpec((1, n), lambda b, s: (0, 0))
    mod = pl.BlockSpec((None, None, 1, D_MODEL), lambda b, s: (b, 0, 0, 0))
    tile = lambda n: pl.BlockSpec((None, TS, n), lambda b, s: (b, s, 0))
    return pl.pallas_call(
        _outproj_kernel,
        grid=(bsz, seq // TS),
        in_specs=[tile(D_MODEL), tile(D_CONV), tile(D_ATTN),
                  pl.BlockSpec((D_MODEL, D_MODEL), lambda b, s: (0, 0)), vec(D_MODEL),
                  mod, vec(D_MODEL), vec(D_MODEL), mod, mod,
                  pl.BlockSpec((D_MODEL, ROUTER_COLS), lambda b, s: (0, 0)), vec(ROUTER_COLS)],
        out_specs=[tile(D_MODEL), pl.BlockSpec((TS * CHUNKS, LANES), lambda b, s: (b * nst + s, 0)),
                   tile(LANES), vec(LANES)],
        out_shape=[jax.ShapeDtypeStruct((bsz, seq, D_MODEL), F32),
                   jax.ShapeDtypeStruct((bsz * seq * CHUNKS, LANES), F32),
                   jax.ShapeDtypeStruct((bsz, seq, LANES), jnp.int32),
                   jax.ShapeDtypeStruct((1, LANES), F32)],
        scratch_shapes=[pltpu.VMEM((D_MODEL, D_MODEL), BF16), pltpu.VMEM((D_MODEL, ROUTER_COLS), BF16),
                        pltpu.VMEM((TS, TS), BF16), pltpu.VMEM((1, LANES), F32)],
        compiler_params=pltpu.CompilerParams(
            dimension_semantics=("arbitrary", "arbitrary"), vmem_limit_bytes=VMEM_LIMIT),
        name="outproj_route",
    )(x, y_conv, y_attn, w_out, b_out, g1, ln_g, ln_b, sc2, sh2, w_r, b_r)


def _row_of(starts_ref, code):
    return (starts_ref[lax.shift_right_logical(code, jnp.int32(CODE_SHIFT))]
            + (code & ((1 << CODE_SHIFT) - 1)))


CHUNKS = D_MODEL // LANES
assert CHUNKS == SUBLANES


def _tok_rows(p):
    return pl.ds(pl.multiple_of(p * CHUNKS, CHUNKS), CHUNKS)


def _to_tiles(ref, x):
    for c in range(CHUNKS):
        ref[pl.ds(c, x.shape[0], stride=CHUNKS), :] = x[:, c * LANES:(c + 1) * LANES]


def _from_tiles(ref, n):
    return jnp.concatenate([ref[pl.ds(c, n, stride=CHUNKS), :] for c in range(CHUNKS)], axis=1)


def _wait_tokens(hbm_ref, n, sem):
    pltpu.make_async_copy(hbm_ref.at[pl.ds(0, n * CHUNKS)], hbm_ref.at[pl.ds(0, n * CHUNKS)], sem).wait()


def _scatter_kernel(starts_ref, zrow_ref, valid_ref, code_ref, h2t_ref, hs_ref, pos_ref, hbuf, zeros, sems, zsem):
    nsteps = pl.num_programs(0) * pl.num_programs(1)
    step = pl.program_id(0) * pl.num_programs(1) + pl.program_id(1)
    slot = step % 2

    def _tile_rows(tok):
        return pl.ds(pl.multiple_of(tok * CHUNKS, TMM * CHUNKS), TMM * CHUNKS)

    def zero_copy(b):
        return pltpu.make_async_copy(zeros, hs_ref.at[_tile_rows(zrow_ref[b])], zsem)

    def tail_copy(t):
        return pltpu.make_async_copy(zeros, hs_ref.at[_tile_rows(t * TMM)], zsem)

    @pl.when(step == 0)
    def _():
        zeros[...] = jnp.zeros_like(zeros)
        n_tiles = hs_ref.shape[0] // (TMM * CHUNKS)

        def start(b, c):
            @pl.when(zrow_ref[b] >= 0)
            def _():
                zero_copy(b).start()
            return c

        def wait(b, c):
            @pl.when(zrow_ref[b] >= 0)
            def _():
                zero_copy(b).wait()
            return c

        def start_tail(t, c):
            @pl.when(valid_ref[t] == 0)
            def _():
                tail_copy(t).start()
            return c

        def wait_tail(t, c):
            @pl.when(valid_ref[t] == 0)
            def _():
                tail_copy(t).wait()
            return c

        lax.fori_loop(0, N_BUCKETS, start, 0)
        lax.fori_loop(0, n_tiles, start_tail, 0)
        lax.fori_loop(0, N_BUCKETS, wait, 0)
        lax.fori_loop(0, n_tiles, wait_tail, 0)

    @pl.when(step >= 2)
    def _():
        _wait_tokens(hs_ref, TS, sems.at[slot])

    hbuf[slot] = h2t_ref[...]

    for r in range(TS):
        pos = _row_of(starts_ref, code_ref[0, r])
        pos_ref[0, r] = pos
        pltpu.make_async_copy(hbuf.at[slot, _tok_rows(r)], hs_ref.at[_tok_rows(pos)],
                              sems.at[slot]).start(priority=r % 2)

    @pl.when(step == nsteps - 1)
    def _():
        _wait_tokens(hs_ref, TS, sems.at[slot])
        _wait_tokens(hs_ref, TS, sems.at[1 - slot])


def _scatter_rows(starts, zrow, valid, code, h2t, n_rows):
    bsz, seq = code.shape
    nst = seq // TS
    grid_spec = pltpu.PrefetchScalarGridSpec(
        num_scalar_prefetch=3,
        grid=(bsz, nst),
        in_specs=[pl.BlockSpec((None, 1, TS), lambda b, s, *_: (b * nst + s, 0, 0), memory_space=pltpu.SMEM),
                  pl.BlockSpec((TS * CHUNKS, LANES), lambda b, s, *_: (b * nst + s, 0))],
        out_specs=[pl.BlockSpec(memory_space=pl.ANY),
                   pl.BlockSpec((None, 1, TS), lambda b, s, *_: (b * nst + s, 0, 0), memory_space=pltpu.SMEM)],
        scratch_shapes=[pltpu.VMEM((2, TS * CHUNKS, LANES), F32), pltpu.VMEM((TMM * CHUNKS, LANES), F32),
                        pltpu.SemaphoreType.DMA((2,)), pltpu.SemaphoreType.DMA],
    )
    return pl.pallas_call(
        _scatter_kernel,
        grid_spec=grid_spec,
        out_shape=[jax.ShapeDtypeStruct((n_rows * CHUNKS, LANES), F32),
                   jax.ShapeDtypeStruct((bsz * nst, 1, TS), jnp.int32)],
        compiler_params=pltpu.CompilerParams(dimension_semantics=("arbitrary", "arbitrary")),
        name="moe_scatter",
    )(starts, zrow, valid, code.reshape(bsz * nst, 1, TS), h2t)


def _moe_kernel(tlo_ref, thi_ref, valid_ref, h_ref, wg_hbm, wu_hbm, wd_hbm, wr_ref, br_ref, y_ref,
                wrbf, wg, wu, wd, sg, su, sd, sems):
    i = pl.program_id(0)
    t0 = i * NSUB
    group = lax.shift_right_logical(tlo_ref[t0], jnp.int32(3))
    prev_group = lax.shift_right_logical(tlo_ref[jnp.maximum(t0 - NSUB, 0)], jnp.int32(3))

    @pl.when(i == 0)
    def _():
        wrbf[...] = wr_ref[...].astype(BF16)

    @pl.when((i == 0) | (group != prev_group))
    def _():
        def copies(e, slot):
            ex = group * EXPERTS_PER_GROUP + e
            return (pltpu.make_async_copy(wg_hbm.at[ex], sg.at[slot], sems.at[slot, 0]),
                    pltpu.make_async_copy(wu_hbm.at[ex], su.at[slot], sems.at[slot, 1]),
                    pltpu.make_async_copy(wd_hbm.at[ex], sd.at[slot], sems.at[slot, 2]))

        for cp in copies(0, 0):
            cp.start()

        def land(e, c):
            slot = e % 2

            @pl.when(e + 1 < EXPERTS_PER_GROUP)
            def _():
                for cp in copies(e + 1, 1 - slot):
                    cp.start()

            for cp in copies(e, slot):
                cp.wait()
            wg[e] = sg[slot].astype(BF16)
            wu[e] = su[slot].astype(BF16)
            wd[e] = sd[slot].astype(BF16)
            return c

        lax.fori_loop(0, EXPERTS_PER_GROUP, land, 0)

    @pl.when(valid_ref[t0] == 1)
    def _():
        lane = lax.broadcasted_iota(jnp.int32, (TMM, ROUTER_COLS), 1)
        local = EXPERTS_PER_GROUP - 1
        subs = range(NSUB)
        tile = lambda ref, j: ref.at[pl.ds(j * TMM * CHUNKS, TMM * CHUNKS)]
        hs = [_from_tiles(tile(h_ref, j), TMM).astype(BF16) for j in subs]
        es = [(tlo_ref[t0 + j] & local, thi_ref[t0 + j] & local) for j in subs]
        pre = [[(_dot(hs[j], wg[e]), _dot(hs[j], wu[e])) for e in es[j]] for j in subs]
        act = [[(a * jax.nn.sigmoid(a) * u).astype(BF16) for a, u in pre[j]] for j in subs]
        ys = [[_dot(act[j][k], wd[es[j][k]]) for k in (0, 1)] for j in subs]
        for j in subs:
            logits = _dot(hs[j], wrbf[...]) + br_ref[...]
            p_top = _group_softmax_top(logits, lane)[3]
            l_lo = jnp.sum(jnp.where(lane == tlo_ref[t0 + j], logits, 0.0), axis=-1, keepdims=True)
            l_hi = jnp.sum(jnp.where(lane == thi_ref[t0 + j], logits, 0.0), axis=-1, keepdims=True)
            w_lo = p_top / (1.0 + jnp.exp(l_hi - l_lo))
            w_hi = p_top / (1.0 + jnp.exp(l_lo - l_hi))
            _to_tiles(tile(y_ref, j), w_lo * ys[j][0] + w_hi * ys[j][1])

    @pl.when(valid_ref[t0] == 0)
    def _():
        y_ref[...] = jnp.zeros_like(y_ref)


def _moe_sorted(tlo, thi, valid, h_sorted, w_gate, w_up, w_down, w_r, b_r):
    n_rows = h_sorted.shape[0] // CHUNKS
    rows = pl.BlockSpec((NSUB * TMM * CHUNKS, LANES), lambda i, *_: (i, 0))
    hbm = pl.BlockSpec(memory_space=pl.ANY)
    w_in_shape, w_out_shape = (D_MODEL, D_EXPERT), (D_EXPERT, D_MODEL)
    grid_spec = pltpu.PrefetchScalarGridSpec(
        num_scalar_prefetch=3,
        grid=(n_rows // (NSUB * TMM),),
        in_specs=[rows, hbm, hbm, hbm,
                  pl.BlockSpec((D_MODEL, ROUTER_COLS), lambda i, *_: (0, 0)),
                  pl.BlockSpec((1, ROUTER_COLS), lambda i, *_: (0, 0))],
        out_specs=rows,
        scratch_shapes=[pltpu.VMEM((D_MODEL, ROUTER_COLS), BF16),
                        pltpu.VMEM((EXPERTS_PER_GROUP,) + w_in_shape, BF16),
                        pltpu.VMEM((EXPERTS_PER_GROUP,) + w_in_shape, BF16),
                        pltpu.VMEM((EXPERTS_PER_GROUP,) + w_out_shape, BF16),
                        pltpu.VMEM((2,) + w_in_shape, F32), pltpu.VMEM((2,) + w_in_shape, F32),
                        pltpu.VMEM((2,) + w_out_shape, F32), pltpu.SemaphoreType.DMA((2, 3))],
    )
    return pl.pallas_call(
        _moe_kernel,
        grid_spec=grid_spec,
        out_shape=jax.ShapeDtypeStruct((n_rows * CHUNKS, LANES), F32),
        compiler_params=pltpu.CompilerParams(dimension_semantics=("arbitrary",), vmem_limit_bytes=VMEM_LIMIT),
        name="moe_sorted",
    )(tlo, thi, valid, h_sorted, w_gate, w_up, w_down, w_r, b_r)


def _final_kernel(pos_ref, posn_ref, x1_ref, g2_ref, lg_ref, lb_ref, ys_ref, o_ref, ybuf, sems):
    nsteps = pl.num_programs(0) * pl.num_programs(1)
    step = pl.program_id(0) * pl.num_programs(1) + pl.program_id(1)
    slot = step % 2

    def gather(rows_ref, sl):
        def issue(i, c):
            for u in range(SUBLANES):
                r = i * SUBLANES + u
                pltpu.make_async_copy(ys_ref.at[_tok_rows(rows_ref[0, r])], ybuf.at[sl, _tok_rows(r)],
                                      sems.at[sl]).start(priority=u % 2)
            return c

        lax.fori_loop(0, TS // SUBLANES, issue, 0)

    @pl.when(step == 0)
    def _():
        gather(pos_ref, slot)

    @pl.when(step + 1 < nsteps)
    def _():
        gather(posn_ref, 1 - slot)

    _wait_tokens(ys_ref, TS, sems.at[slot])
    y = _from_tiles(ybuf.at[slot], TS)
    o_ref[...] = _layer_norm(ALPHA * x1_ref[...] + g2_ref[...] * y, lg_ref[...], lb_ref[...])


def _unsort_ln2(pos, x1, g2, ln_g, ln_b, y_sorted):
    bsz, seq, _ = x1.shape
    nst = seq // TS
    last = bsz * nst - 1
    return pl.pallas_call(
        _final_kernel,
        grid=(bsz, nst),
        in_specs=[pl.BlockSpec((None, 1, TS), lambda b, s: (b * nst + s, 0, 0), memory_space=pltpu.SMEM),
                  pl.BlockSpec((None, 1, TS), lambda b, s: (jnp.minimum(b * nst + s + 1, last), 0, 0),
                               memory_space=pltpu.SMEM),
                  pl.BlockSpec((None, TS, D_MODEL), lambda b, s: (b, s, 0)),
                  pl.BlockSpec((None, None, 1, D_MODEL), lambda b, s: (b, 0, 0, 0)),
                  pl.BlockSpec((1, D_MODEL), lambda b, s: (0, 0)),
                  pl.BlockSpec((1, D_MODEL), lambda b, s: (0, 0)),
                  pl.BlockSpec(memory_space=pl.ANY)],
        out_specs=pl.BlockSpec((None, TS, D_MODEL), lambda b, s: (b, s, 0)),
        out_shape=jax.ShapeDtypeStruct((bsz, seq, D_MODEL), F32),
        scratch_shapes=[pltpu.VMEM((2, TS * CHUNKS, LANES), F32), pltpu.SemaphoreType.DMA((2,))],
        compiler_params=pltpu.CompilerParams(dimension_semantics=("arbitrary", "arbitrary")),
        name="unsort_ln2",
    )(pos, pos, x1, g2, ln_g, ln_b, y_sorted)


def _pair_layout(a, axis):
    shp = a.shape
    a = a.reshape(shp[:axis] + (N_KV_HEADS, N_HEADS // N_KV_HEADS, HEAD_DIM) + shp[axis + 1:])
    return jnp.swapaxes(a, axis, axis + 1).reshape(shp)


def _tile_tables(counts, n_tiles_max):
    b_np = np.arange(N_BUCKETS)
    g_np = b_np // N_PAIRS
    in_group_before = jnp.asarray((g_np[None, :] == g_np[:, None]) & (b_np[None, :] < b_np[:, None]))
    member = jnp.asarray(g_np[None, :] == np.arange(N_GROUPS)[:, None])
    earlier_group = jnp.asarray(np.arange(N_GROUPS)[None, :] < np.arange(N_GROUPS)[:, None])
    e_lo = jnp.asarray(g_np * EXPERTS_PER_GROUP + np.asarray(PAIRS)[b_np % N_PAIRS, 0], jnp.int32)
    e_hi = jnp.asarray(g_np * EXPERTS_PER_GROUP + np.asarray(PAIRS)[b_np % N_PAIRS, 1], jnp.int32)

    n_tile_b = jnp.floor((counts + (TMM - 1.0)) * (1.0 / TMM))
    g_tiles = jnp.sum(jnp.where(member, n_tile_b[None, :], 0.0), axis=1)
    g_padded = jnp.floor((g_tiles + (NSUB - 1.0)) * (1.0 / NSUB)) * NSUB
    g_start = jnp.sum(jnp.where(earlier_group, g_padded[None, :], 0.0), axis=1)
    g_end = g_start + g_padded
    b_start = (jnp.sum(jnp.where(member.T, g_start[None, :], 0.0), axis=1)
               + jnp.sum(jnp.where(in_group_before, n_tile_b[None, :], 0.0), axis=1))
    b_end = b_start + n_tile_b
    starts = (b_start * TMM).astype(jnp.int32)
    zrow = jnp.where(n_tile_b > 0, (b_end - 1.0) * TMM, -1.0).astype(jnp.int32)

    t = jnp.arange(n_tiles_max, dtype=jnp.int32).astype(F32)
    g_of_t = jnp.minimum(jnp.sum(jnp.where(g_end[None, :] <= t[:, None], 1, 0), axis=1), N_GROUPS - 1)
    g_hot = g_of_t[:, None] == jnp.arange(N_GROUPS, dtype=jnp.int32)[None, :]
    local_t = t - jnp.sum(jnp.where(g_hot, g_start[None, :], 0.0), axis=1)
    valid = (local_t < jnp.sum(jnp.where(g_hot, g_tiles[None, :], 0.0), axis=1)) & (t < jnp.sum(g_padded))
    bucket = jnp.sum(jnp.where(b_end[None, :] <= t[:, None], 1, 0), axis=1)
    bucket = jnp.where(valid, bucket, g_of_t * N_PAIRS + (N_PAIRS - 1))
    b_hot = bucket[:, None] == jnp.asarray(b_np, jnp.int32)[None, :]
    tlo = jnp.sum(jnp.where(b_hot, e_lo[None, :], 0), axis=1)
    thi = jnp.sum(jnp.where(b_hot, e_hi[None, :], 0), axis=1)
    pad = jnp.zeros((LANES - N_BUCKETS,), jnp.int32)
    return (jnp.concatenate([starts, pad]), jnp.concatenate([zrow, pad - 1]), tlo, thi,
            jnp.where(valid, 1, 0).astype(jnp.int32))


def kernel(x, c, w_ada, b_ada, w_in, b_in, conv_w, conv_b, conv_ln_g, conv_ln_b, conv_out_g, sinks,
           attn_out_g, w_out, b_out, ln1_g, ln1_b, w_router_group, b_router_group, w_router_expert,
           b_router_expert, w_gate, w_up, w_down, ln2_g, ln2_b):
    assert w_ada.shape[0] == DEPTH
    bsz, seq, _ = x.shape
    row = lambda v: v[0][None, :]

    mod = _modulation(c, w_ada[0], row(b_ada)).reshape(bsz, 6, 1, D_MODEL)
    sh1, sc1, g1, sh2, sc2, g2 = [mod[:, j:j + 1] for j in range(6)]

    q0 = 2 * D_CONV
    w_q = _pair_layout(w_in[0][:, q0:q0 + D_ATTN], 1)
    b_q = _pair_layout(b_in[0][q0:q0 + D_ATTN], 0)[None, :]
    og_attn = _pair_layout(attn_out_g[0], 0)[None, :]
    w_o = jnp.concatenate([w_out[0][:D_CONV], _pair_layout(w_out[0][D_CONV:], 0)], axis=0)

    y_conv, q, kv = _inproj_conv(x, sh1, sc1, w_in[0], row(b_in), w_q, b_q, conv_w[0], row(conv_b),
                                 row(conv_ln_g), row(conv_ln_b), row(conv_out_g))
    y_attn = _attention(q, kv, sinks[0], og_attn)

    pad = ROUTER_COLS - N_EXPERTS - N_GROUPS
    w_r = jnp.concatenate([w_router_expert[0], w_router_group[0], jnp.zeros((D_MODEL, pad), F32)], axis=1)
    b_r = jnp.concatenate([b_router_expert[0], b_router_group[0], jnp.zeros((pad,), F32)])[None, :]
    x1, h2t, code, counts = _outproj_route(x, y_conv, y_attn, w_o, row(b_out), g1, row(ln1_g), row(ln1_b),
                                      sc2, sh2, w_r, b_r)

    n_tiles_max = (bsz * seq) // TMM + N_BUCKETS + N_GROUPS * (NSUB - 1)
    n_tiles_max = -(-n_tiles_max // NSUB) * NSUB
    starts, zrow, tlo, thi, valid = _tile_tables(counts[0, :N_BUCKETS], n_tiles_max)
    h_sorted, pos = _scatter_rows(starts, zrow, valid, code[:, :, 0], h2t, n_tiles_max * TMM)
    y_sorted = _moe_sorted(tlo, thi, valid, h_sorted, w_gate[0], w_up[0], w_down[0], w_r, b_r)
    return _unsort_ln2(pos, x1, g2, row(ln2_g), row(ln2_b), y_sorted)
```

```python
import jax
import jax.numpy as jnp
import numpy as np
from jax import lax
from jax.experimental import pallas as pl
from jax.experimental.pallas import tpu as pltpu

F32 = jnp.float32
BF16 = jnp.bfloat16

D_MODEL = 1024
D_CONV = 512
CONV_WIDTH = 31
GROUP_DIM = 64
N_HEADS = 8
N_KV_HEADS = 2
HEAD_DIM = 64
D_ATTN = N_HEADS * HEAD_DIM
D_KV = N_KV_HEADS * HEAD_DIM
BLOCK = 128
D_IN = 2 * D_CONV + D_ATTN + 2 * D_KV
N_GROUPS = 4
EXPERTS_PER_GROUP = 8
N_EXPERTS = N_GROUPS * EXPERTS_PER_GROUP
D_EXPERT = D_MODEL // 4
DEPTH = 1
ALPHA = (2.0 * DEPTH) ** 0.25
EPS = 1e-5
NEG = -1e30

LANES = 128
SUBLANES = 8
TS = 512
HALO = 32
CR = 32
AB = 8
PAIR_SLOTS = [j + (N_HEADS // N_KV_HEADS) * g for j in range(N_HEADS // N_KV_HEADS) for g in range(N_KV_HEADS)]
ROUTER_COLS = LANES
VMEM_LIMIT = 56 * 1024 * 1024

PAIRS = [(lo, hi) for lo in range(EXPERTS_PER_GROUP) for hi in range(lo + 1, EXPERTS_PER_GROUP)]
N_PAIRS = len(PAIRS)
N_BUCKETS = N_GROUPS * N_PAIRS
TMM = 128
OUT_SLICES = 4
NSUB = 4
CODE_SHIFT = 16


def _dot(a, b):
    return jnp.dot(a, b, preferred_element_type=F32)


def _dot_nt(a, b):
    return lax.dot_general(a, b, (((1,), (1,)), ((), ())), preferred_element_type=F32)


def _fill_group_maps(bsel_ref, bexp_ref):
    c = bsel_ref.shape[0]
    ch = lax.broadcasted_iota(jnp.int32, (c, LANES), 0) // GROUP_DIM
    gi = lax.broadcasted_iota(jnp.int32, (c, LANES), 1)
    bsel_ref[...] = jnp.where(ch == gi, 1.0 / GROUP_DIM, 0.0).astype(BF16)
    gi2 = lax.broadcasted_iota(jnp.int32, (LANES, c), 0)
    ch2 = lax.broadcasted_iota(jnp.int32, (LANES, c), 1) // GROUP_DIM
    bexp_ref[...] = jnp.where(ch2 == gi2, 1.0, 0.0).astype(BF16)


def _group_rms(y, bsel, bexp):
    ms = _dot((y * y).astype(BF16), bsel)
    r = lax.rsqrt(ms + EPS)
    r_hi = r.astype(BF16)
    r_lo = (r - r_hi.astype(F32)).astype(BF16)
    return y * (_dot(r_hi, bexp) + _dot(r_lo, bexp))


def _layer_norm(y, g, b):
    mu = jnp.mean(y, axis=-1, keepdims=True)
    d = y - mu
    var = jnp.mean(d * d, axis=-1, keepdims=True)
    return d * lax.rsqrt(var + EPS) * g + b


def _mod_kernel(c_ref, w_ref, b_ref, o_ref):
    c = c_ref[...]
    c_act = c * jax.nn.sigmoid(c)
    o_ref[...] = _dot(c_act.astype(BF16), w_ref[...].astype(BF16)) + b_ref[...]


def _modulation(c, w_ada, b_ada):
    bsz = c.shape[0]
    n = w_ada.shape[1]
    return pl.pallas_call(
        _mod_kernel,
        grid=(n // D_MODEL,),
        in_specs=[pl.BlockSpec((bsz, D_MODEL), lambda j: (0, 0)),
                  pl.BlockSpec((D_MODEL, D_MODEL), lambda j: (0, j)),
                  pl.BlockSpec((1, D_MODEL), lambda j: (0, j))],
        out_specs=pl.BlockSpec((bsz, D_MODEL), lambda j: (0, j)),
        out_shape=jax.ShapeDtypeStruct((bsz, n), F32),
        name="adaln_mod",
    )(c, w_ada, b_ada)


def _inproj_conv_kernel(x_ref, sh_ref, sc_ref, w_ref, b_ref, wq_ref, bq_ref, cw_ref, cb_ref, lg_ref, lb_ref, og_ref,
                        yc_ref, q_ref, kv_ref, wbf, gs, cacc, wb, bsel, bexp):
    first = (pl.program_id(0) == 0) & (pl.program_id(1) == 0)
    s = pl.program_id(1)

    @pl.when(first)
    def _():
        wbf[...] = w_ref[...].astype(BF16)
        wbf[:, 2 * D_CONV:2 * D_CONV + D_ATTN] = wq_ref[...].astype(BF16)
        wb[...] = jnp.broadcast_to(cw_ref[...][:, None, :], wb.shape)
        _fill_group_maps(bsel, bexp)

    n_ct = D_CONV // LANES

    @pl.when(s == 0)
    def _():
        gs[0, :, 0:HALO, :] = jnp.zeros((n_ct, HALO, LANES), F32)

    @pl.when(s > 0)
    def _():
        gs[0, :, 0:HALO, :] = gs[0, :, TS:TS + HALO, :]

    h = (x_ref[...] * (1.0 + sc_ref[...]) + sh_ref[...]).astype(BF16)
    q0 = 2 * D_CONV
    k0 = q0 + D_ATTN
    cb = cb_ref[...]
    lg = lg_ref[...]
    lb = lb_ref[...]
    og = og_ref[...]
    n_sh = TS + HALO - SUBLANES
    half = TS // 2
    late_cols = 2 * LANES

    def glu_unit(rows, j):
        cs = slice(j * late_cols, (j + 1) * late_cols)
        u_a = _dot(h[rows], wbf[:, cs]) + b_ref[:, cs]
        u_b = _dot(h[rows], wbf[:, D_CONV + j * late_cols:D_CONV + (j + 1) * late_cols]) \
            + b_ref[:, D_CONV + j * late_cols:D_CONV + (j + 1) * late_cols]
        glu = u_a * jax.nn.sigmoid(u_b)
        for cc in range(late_cols // LANES):
            c = j * (late_cols // LANES) + cc
            gs[0, c, HALO + rows.start:HALO + rows.stop, :] = glu[:, cc * LANES:(cc + 1) * LANES]

    def late_proj(j):
        c0 = j * late_cols
        if c0 < D_ATTN:
            q = _dot(h, wbf[:, q0 + c0:q0 + c0 + late_cols]) + bq_ref[:, c0:c0 + late_cols]
            q_ref[:, c0:c0 + late_cols] = (q * (HEAD_DIM ** -0.5)).astype(BF16)
        else:
            kv_ref[...] = (_dot(h, wbf[:, k0:k0 + 2 * D_KV]) + b_ref[:, k0:k0 + 2 * D_KV]).astype(BF16)

    def shifted(lo, hi):
        for j in range(1, SUBLANES):
            gs[j, :, lo:hi, :] = gs[0, :, lo + j:hi + j, :]

    def conv_chunk(c, r0):
        ls = slice(c * LANES, (c + 1) * LANES)
        acc = jnp.broadcast_to(cb[:, ls], (CR, LANES)).reshape(CR // SUBLANES, SUBLANES, LANES)
        for k in range(CONV_WIDTH):
            off = HALO - (CONV_WIDTH - 1) + k
            a0 = r0 + (off // SUBLANES) * SUBLANES
            seg = gs[off % SUBLANES, c, a0:a0 + CR, :]
            acc = acc + wb[k, :, ls][None] * seg.reshape(CR // SUBLANES, SUBLANES, LANES)
        cacc[c, r0:r0 + CR, :] = acc.reshape(CR, LANES)

    first, second = slice(0, half), slice(half, TS)
    n_glu = D_CONV // late_cols
    n_late = (D_ATTN + 2 * D_KV) // late_cols
    for j in range(n_glu):
        glu_unit(first, j)
    split = HALO - SUBLANES + half
    shifted(0, split)
    chunks_a = [(c, r0) for c in range(n_ct) for r0 in range(0, half, CR)]
    chunks_b = [(c, r0) for c in range(n_ct) for r0 in range(half, TS, CR)]
    units_a = [lambda j=j: glu_unit(second, j) for j in range(n_glu)] + [lambda: late_proj(0)]
    units_b = [lambda j=j: late_proj(j) for j in range(1, n_late)]
    for chunks, units in ((chunks_a, units_a), (chunks_b, units_b)):
        if chunks is chunks_b:
            shifted(split, n_sh)
        every = len(chunks) // len(units)
        for n, (c, r0) in enumerate(chunks):
            conv_chunk(c, r0)
            if n % every == 0 and n // every < len(units):
                units[n // every]()
    y = _layer_norm(jnp.concatenate([cacc[c] for c in range(n_ct)], axis=1), lg, lb)
    y = y * jax.nn.sigmoid(y)
    yc_ref[...] = (_group_rms(y, bsel[...], bexp[...]) * og).astype(BF16)


def _inproj_conv(x, sh1, sc1, w_in, b_in, w_q, b_q, conv_w, conv_b, ln_g, ln_b, out_g):
    bsz, seq, _ = x.shape
    vec = lambda n: pl.BlockSpec((1, n), lambda b, s: (0, 0))
    mod = pl.BlockSpec((None, None, 1, D_MODEL), lambda b, s: (b, 0, 0, 0))
    tile = lambda n: pl.BlockSpec((None, TS, n), lambda b, s: (b, s, 0))
    return pl.pallas_call(
        _inproj_conv_kernel,
        grid=(bsz, seq // TS),
        in_specs=[tile(D_MODEL), mod, mod,
                  pl.BlockSpec((D_MODEL, D_IN), lambda b, s: (0, 0)), vec(D_IN),
                  pl.BlockSpec((D_MODEL, D_ATTN), lambda b, s: (0, 0)), vec(D_ATTN),
                  pl.BlockSpec((CONV_WIDTH, D_CONV), lambda b, s: (0, 0)),
                  vec(D_CONV), vec(D_CONV), vec(D_CONV), vec(D_CONV)],
        out_specs=[tile(D_CONV), tile(D_ATTN), tile(2 * D_KV)],
        out_shape=[jax.ShapeDtypeStruct((bsz, seq, D_CONV), BF16),
                   jax.ShapeDtypeStruct((bsz, seq, D_ATTN), BF16),
                   jax.ShapeDtypeStruct((bsz, seq, 2 * D_KV), BF16)],
        scratch_shapes=[pltpu.VMEM((D_MODEL, D_IN), BF16),
                        pltpu.VMEM((SUBLANES, D_CONV // LANES, TS + HALO, LANES), F32),
                        pltpu.VMEM((D_CONV // LANES, TS, LANES), F32),
                        pltpu.VMEM((CONV_WIDTH, SUBLANES, D_CONV), F32),
                        pltpu.VMEM((D_CONV, LANES), BF16),
                        pltpu.VMEM((LANES, D_CONV), BF16)],
        compiler_params=pltpu.CompilerParams(
            dimension_semantics=("arbitrary", "arbitrary"), vmem_limit_bytes=VMEM_LIMIT),
        name="inproj_conv",
    )(x, sh1, sc1, w_in, b_in, w_q, b_q, conv_w, conv_b, ln_g, ln_b, out_g)


def _attn_kernel(sinks_ref, q_ref, kvc_ref, kvp_ref, og_ref, o_ref, bsel, bexp):
    n = pl.program_id(1)

    @pl.when((pl.program_id(0) == 0) & (n == 0))
    def _():
        _fill_group_maps(bsel, bexp)

    qr = lax.broadcasted_iota(jnp.int32, (BLOCK, BLOCK), 0)
    kc = lax.broadcasted_iota(jnp.int32, (BLOCK, BLOCK), 1)
    from_prev = kc > qr
    lane = lax.broadcasted_iota(jnp.int32, (BLOCK, LANES), 1)
    lower = lane < HEAD_DIM
    lower_bf = jnp.where(lower, 1.0, 0.0).astype(BF16)
    upper_bf = jnp.where(lower, 0.0, 1.0).astype(BF16)
    og = og_ref[...]
    blocks = range(AB)
    kvs = []
    for blk in blocks:
        r0 = blk * BLOCK
        if blk == 0:
            kvs.append(jnp.concatenate([kvp_ref[...], kvc_ref[0:BLOCK, :]], axis=0))
        else:
            kvs.append(kvc_ref[r0 - BLOCK:r0 + BLOCK, :])
    s_all = []
    for blk in blocks:
        r0 = blk * BLOCK
        parts = []
        for j in range(D_ATTN // LANES):
            col = q_ref[r0:r0 + BLOCK, j * LANES:(j + 1) * LANES]
            parts += [col * lower_bf, col * upper_bf]
        s_all.append(_dot_nt(jnp.concatenate(parts, axis=0), kvs[blk][:, 0:D_KV]))
    probs, dens = [], []
    for blk in blocks:
        pb = []
        den = jnp.ones((BLOCK, LANES), F32)
        for slot in range(N_HEADS):
            s_prev = s_all[blk][slot * BLOCK:(slot + 1) * BLOCK, 0:BLOCK]
            s_own = s_all[blk][slot * BLOCK:(slot + 1) * BLOCK, BLOCK:2 * BLOCK]
            if blk == 0:
                s_prev = jnp.where(n > 0, s_prev, NEG)
            sc = jnp.where(from_prev, s_prev, s_own)
            sink = sinks_ref[PAIR_SLOTS[slot]]
            m = jnp.maximum(jnp.max(sc, axis=-1, keepdims=True), sink)
            p = jnp.exp(sc - m)
            den = jnp.where(lane == slot, jnp.sum(p, axis=-1, keepdims=True) + jnp.exp(sink - m), den)
            p = p.astype(BF16)
            zero = jnp.zeros_like(p)
            pb.append(jnp.concatenate([jnp.where(from_prev, p, zero), jnp.where(from_prev, zero, p)], axis=1))
        probs.append(jnp.concatenate(pb, axis=0))
        dens.append(den)
    pvs = [_dot(probs[blk], kvs[blk][:, D_KV:2 * D_KV]) for blk in blocks]
    for blk in blocks:
        pv, den = pvs[blk], dens[blk]
        o = jnp.concatenate(
            [jnp.where(lower, pv[(2 * j) * BLOCK:(2 * j + 1) * BLOCK], pv[(2 * j + 1) * BLOCK:(2 * j + 2) * BLOCK])
             for j in range(D_ATTN // LANES)], axis=1)
        ms = _dot((o * o).astype(BF16), bsel[...])
        r = lax.rsqrt(ms + EPS * den * den)
        r_hi = r.astype(BF16)
        r_lo = (r - r_hi.astype(F32)).astype(BF16)
        scale = _dot(r_hi, bexp[...]) + _dot(r_lo, bexp[...])
        o_ref[blk * BLOCK:(blk + 1) * BLOCK, :] = (o * scale * og).astype(BF16)


def _attention(q, kv, sinks, out_g):
    bsz, seq, _ = q.shape
    rows = AB * BLOCK
    grid_spec = pltpu.PrefetchScalarGridSpec(
        num_scalar_prefetch=1,
        grid=(bsz, seq // rows),
        in_specs=[pl.BlockSpec((None, rows, D_ATTN), lambda b, n, sk: (b, n, 0)),
                  pl.BlockSpec((None, rows, 2 * D_KV), lambda b, n, sk: (b, n, 0)),
                  pl.BlockSpec((None, BLOCK, 2 * D_KV), lambda b, n, sk: (b, jnp.maximum(AB * n - 1, 0), 0)),
                  pl.BlockSpec((1, D_ATTN), lambda b, n, sk: (0, 0))],
        out_specs=pl.BlockSpec((None, rows, D_ATTN), lambda b, n, sk: (b, n, 0)),
        scratch_shapes=[pltpu.VMEM((D_ATTN, LANES), BF16), pltpu.VMEM((LANES, D_ATTN), BF16)],
    )
    return pl.pallas_call(
        _attn_kernel,
        grid_spec=grid_spec,
        out_shape=jax.ShapeDtypeStruct((bsz, seq, D_ATTN), BF16),
        compiler_params=pltpu.CompilerParams(dimension_semantics=("arbitrary", "arbitrary")),
        name="swa_attention",
    )(sinks, q, kv, kv, out_g)


def _group_softmax_top(logits, lane):
    is_g = (lane >= N_EXPERTS) & (lane < N_EXPERTS + N_GROUPS)
    gl = jnp.where(is_g, logits, NEG)
    gmax = jnp.max(gl, axis=-1, keepdims=True)
    p_top = 1.0 / jnp.sum(jnp.where(is_g, jnp.exp(gl - gmax), 0.0), axis=-1, keepdims=True)
    return is_g, gl, gmax, p_top


def _route_bucket(logits):
    rows = logits.shape[0]
    lane = lax.broadcasted_iota(jnp.int32, (rows, ROUTER_COLS), 1)
    lanef = lane.astype(F32)
    is_g, gl, gmax, _ = _group_softmax_top(logits, lane)
    gidx = jnp.min(jnp.where(is_g & (gl == gmax), lanef - N_EXPERTS, 99.0), axis=-1, keepdims=True)
    in_grp = (lane < N_EXPERTS) & ((lane // EXPERTS_PER_GROUP) == gidx.astype(jnp.int32))
    el = jnp.where(in_grp, logits, NEG)
    m1 = jnp.max(el, axis=-1, keepdims=True)
    i1 = jnp.min(jnp.where(in_grp & (el == m1), lanef, 999.0), axis=-1, keepdims=True)
    rest = in_grp & (lanef != i1)
    el2 = jnp.where(rest, logits, NEG)
    m2 = jnp.max(el2, axis=-1, keepdims=True)
    i2 = jnp.min(jnp.where(rest & (el2 == m2), lanef, 999.0), axis=-1, keepdims=True)
    lo = jnp.minimum(i1, i2) - EXPERTS_PER_GROUP * gidx
    hi = jnp.maximum(i1, i2) - EXPERTS_PER_GROUP * gidx
    pair = lo * (EXPERTS_PER_GROUP - 1) - lo * (lo - 1.0) * 0.5 + (hi - lo - 1.0)
    return gidx * N_PAIRS + pair, lanef


def _outproj_kernel(x_ref, yc_ref, ya_ref, w_ref, b_ref, g1_ref, lg_ref, lb_ref, sc2_ref, sh2_ref,
                    wr_ref, br_ref, x1_ref, h2t_ref, code_ref, counts_ref, wbf, wrbf, ltri, running):
    @pl.when((pl.program_id(0) == 0) & (pl.program_id(1) == 0))
    def _():
        wbf[...] = w_ref[...].astype(BF16)
        wrbf[...] = wr_ref[...].astype(BF16)
        r = lax.broadcasted_iota(jnp.int32, (TS, TS), 0)
        c = lax.broadcasted_iota(jnp.int32, (TS, TS), 1)
        ltri[...] = jnp.where(c < r, 1.0, 0.0).astype(BF16)
        running[...] = jnp.zeros_like(running)

    nsl = OUT_SLICES
    rows = TS // nsl
    sl = [slice(k * rows, (k + 1) * rows) for k in range(nsl)]
    mix = [_dot(yc_ref[r, :], wbf[0:D_CONV, :]) + _dot(ya_ref[r, :], wbf[D_CONV:, :]) + b_ref[...] for r in sl]
    x1 = [_layer_norm(ALPHA * x_ref[r, :] + g1_ref[...] * m, lg_ref[...], lb_ref[...]) for r, m in zip(sl, mix)]
    h2 = [v * (1.0 + sc2_ref[...]) + sh2_ref[...] for v in x1]
    for k in range(nsl):
        x1_ref[sl[k], :] = x1[k]
        _to_tiles(h2t_ref.at[pl.ds(k * rows * CHUNKS, rows * CHUNKS)], h2[k])
    logits = [_dot(v.astype(BF16), wrbf[...]) + br_ref[...] for v in h2]
    routed = [_route_bucket(lg) for lg in logits]
    mine = [lanef == bucket for bucket, lanef in routed]
    onehot = [jnp.where(mk, 1.0, 0.0) for mk in mine]
    within = [_dot(ltri[0:rows, 0:rows], oh.astype(BF16)) for oh in onehot]
    base = running[...]
    for k in range(nsl):
        rank = jnp.sum(jnp.where(mine[k], within[k] + base, 0.0), axis=-1, keepdims=True)
        base = base + jnp.sum(onehot[k], axis=0, keepdims=True)
        code = routed[k][0].astype(jnp.int32) * (1 << CODE_SHIFT) + rank.astype(jnp.int32)
        code_ref[sl[k], :] = jnp.broadcast_to(code, (rows, SUBLANES))
    running[...] = base
    counts_ref[...] = running[...]


def _outproj_route(x, y_conv, y_attn, w_out, b_out, g1, ln_g, ln_b, sc2, sh2, w_r, b_r):
    bsz, seq, _ = x.shape
    nst = seq // TS
    vec = lambda n: pl.BlockSpec((1, n), lambda b, s: (0, 0))
    mod = pl.BlockSpec((None, None, 1, D_MODEL), lambda b, s: (b, 0, 0, 0))
    tile = lambda n: pl.BlockSpec((None, TS, n), lambda b, s: (b, s, 0))
    return pl.pallas_call(
        _outproj_kernel,
        grid=(bsz, seq // TS),
        in_specs=[tile(D_MODEL), tile(D_CONV), tile(D_ATTN),
                  pl.BlockSpec((D_MODEL, D_MODEL), lambda b, s: (0, 0)), vec(D_MODEL),
                  mod, vec(D_MODEL), vec(D_MODEL), mod, mod,
                  pl.BlockSpec((D_MODEL, ROUTER_COLS), lambda b, s: (0, 0)), vec(ROUTER_COLS)],
        out_specs=[tile(D_MODEL), pl.BlockSpec((TS * CHUNKS, LANES), lambda b, s: (b * nst + s, 0)),
                   tile(SUBLANES), vec(LANES)],
        out_shape=[jax.ShapeDtypeStruct((bsz, seq, D_MODEL), F32),
                   jax.ShapeDtypeStruct((bsz * seq * CHUNKS, LANES), F32),
                   jax.ShapeDtypeStruct((bsz, seq, SUBLANES), jnp.int32),
                   jax.ShapeDtypeStruct((1, LANES), F32)],
        scratch_shapes=[pltpu.VMEM((D_MODEL, D_MODEL), BF16), pltpu.VMEM((D_MODEL, ROUTER_COLS), BF16),
                        pltpu.VMEM((TS, TS), BF16), pltpu.VMEM((1, LANES), F32)],
        compiler_params=pltpu.CompilerParams(
            dimension_semantics=("arbitrary", "arbitrary"), vmem_limit_bytes=VMEM_LIMIT),
        name="outproj_route",
    )(x, y_conv, y_attn, w_out, b_out, g1, ln_g, ln_b, sc2, sh2, w_r, b_r)


def _row_of(starts_ref, code):
    return (starts_ref[lax.shift_right_logical(code, jnp.int32(CODE_SHIFT))]
            + (code & ((1 << CODE_SHIFT) - 1)))


CHUNKS = D_MODEL // LANES
assert CHUNKS == SUBLANES


def _tok_rows(p):
    return pl.ds(pl.multiple_of(p * CHUNKS, CHUNKS), CHUNKS)


def _to_tiles(ref, x):
    for c in range(CHUNKS):
        ref[pl.ds(c, x.shape[0], stride=CHUNKS), :] = x[:, c * LANES:(c + 1) * LANES]


def _from_tiles(ref, n):
    return jnp.concatenate([ref[pl.ds(c, n, stride=CHUNKS), :] for c in range(CHUNKS)], axis=1)


def _wait_tokens(hbm_ref, n, sem):
    pltpu.make_async_copy(hbm_ref.at[pl.ds(0, n * CHUNKS)], hbm_ref.at[pl.ds(0, n * CHUNKS)], sem).wait()


def _scatter_kernel(starts_ref, zrow_ref, valid_ref, code_ref, h2t_ref, hs_ref, pos_ref, hbuf, zeros, sems, zsem):
    nsteps = pl.num_programs(0) * pl.num_programs(1)
    step = pl.program_id(0) * pl.num_programs(1) + pl.program_id(1)
    slot = step % 2

    def _tile_rows(tok):
        return pl.ds(pl.multiple_of(tok * CHUNKS, TMM * CHUNKS), TMM * CHUNKS)

    def zero_copy(b):
        return pltpu.make_async_copy(zeros, hs_ref.at[_tile_rows(zrow_ref[b])], zsem)

    def tail_copy(t):
        return pltpu.make_async_copy(zeros, hs_ref.at[_tile_rows(t * TMM)], zsem)

    @pl.when(step == 0)
    def _():
        zeros[...] = jnp.zeros_like(zeros)
        n_tiles = hs_ref.shape[0] // (TMM * CHUNKS)

        def start(b, c):
            @pl.when(zrow_ref[b] >= 0)
            def _():
                zero_copy(b).start()
            return c

        def wait(b, c):
            @pl.when(zrow_ref[b] >= 0)
            def _():
                zero_copy(b).wait()
            return c

        def start_tail(t, c):
            @pl.when(valid_ref[t] == 0)
            def _():
                tail_copy(t).start()
            return c

        def wait_tail(t, c):
            @pl.when(valid_ref[t] == 0)
            def _():
                tail_copy(t).wait()
            return c

        lax.fori_loop(0, N_BUCKETS, start, 0)
        lax.fori_loop(0, n_tiles, start_tail, 0)
        lax.fori_loop(0, N_BUCKETS, wait, 0)
        lax.fori_loop(0, n_tiles, wait_tail, 0)

    @pl.when(step >= 2)
    def _():
        _wait_tokens(hs_ref, TS, sems.at[slot])

    hbuf[slot] = h2t_ref[...]

    for r in range(TS):
        pos = _row_of(starts_ref, code_ref[0, r])
        pos_ref[0, r] = pos
        pltpu.make_async_copy(hbuf.at[slot, _tok_rows(r)], hs_ref.at[_tok_rows(pos)],
                              sems.at[slot]).start(priority=r % 2)

    @pl.when(step == nsteps - 1)
    def _():
        _wait_tokens(hs_ref, TS, sems.at[slot])
        _wait_tokens(hs_ref, TS, sems.at[1 - slot])


def _scatter_rows(starts, zrow, valid, code, h2t, n_rows):
    bsz, seq = code.shape
    nst = seq // TS
    grid_spec = pltpu.PrefetchScalarGridSpec(
        num_scalar_prefetch=3,
        grid=(bsz, nst),
        in_specs=[pl.BlockSpec((None, 1, TS), lambda b, s, *_: (b * nst + s, 0, 0), memory_space=pltpu.SMEM),
                  pl.BlockSpec((TS * CHUNKS, LANES), lambda b, s, *_: (b * nst + s, 0))],
        out_specs=[pl.BlockSpec(memory_space=pl.ANY),
                   pl.BlockSpec((None, 1, TS), lambda b, s, *_: (b * nst + s, 0, 0), memory_space=pltpu.SMEM)],
        scratch_shapes=[pltpu.VMEM((2, TS * CHUNKS, LANES), F32), pltpu.VMEM((TMM * CHUNKS, LANES), F32),
                        pltpu.SemaphoreType.DMA((2,)), pltpu.SemaphoreType.DMA],
    )
    return pl.pallas_call(
        _scatter_kernel,
        grid_spec=grid_spec,
        out_shape=[jax.ShapeDtypeStruct((n_rows * CHUNKS, LANES), F32),
                   jax.ShapeDtypeStruct((bsz * nst, 1, TS), jnp.int32)],
        compiler_params=pltpu.CompilerParams(dimension_semantics=("arbitrary", "arbitrary")),
        name="moe_scatter",
    )(starts, zrow, valid, code.reshape(bsz * nst, 1, TS), h2t)


def _moe_kernel(tlo_ref, thi_ref, valid_ref, h_ref, wg_hbm, wu_hbm, wd_hbm, wr_ref, br_ref, y_ref,
                wrbf, wg, wu, wd, sg, su, sd, sems):
    i = pl.program_id(0)
    t0 = i * NSUB
    group = lax.shift_right_logical(tlo_ref[t0], jnp.int32(3))
    prev_group = lax.shift_right_logical(tlo_ref[jnp.maximum(t0 - NSUB, 0)], jnp.int32(3))

    @pl.when(i == 0)
    def _():
        wrbf[...] = wr_ref[...].astype(BF16)

    @pl.when((i == 0) | (group != prev_group))
    def _():
        def copies(e, slot):
            ex = group * EXPERTS_PER_GROUP + e
            return (pltpu.make_async_copy(wg_hbm.at[ex], sg.at[slot], sems.at[slot, 0]),
                    pltpu.make_async_copy(wu_hbm.at[ex], su.at[slot], sems.at[slot, 1]),
                    pltpu.make_async_copy(wd_hbm.at[ex], sd.at[slot], sems.at[slot, 2]))

        for cp in copies(0, 0):
            cp.start()

        def land(e, c):
            slot = e % 2

            @pl.when(e + 1 < EXPERTS_PER_GROUP)
            def _():
                for cp in copies(e + 1, 1 - slot):
                    cp.start()

            for cp in copies(e, slot):
                cp.wait()
            wg[e] = sg[slot].astype(BF16)
            wu[e] = su[slot].astype(BF16)
            wd[e] = sd[slot].astype(BF16)
            return c

        lax.fori_loop(0, EXPERTS_PER_GROUP, land, 0)

    @pl.when(valid_ref[t0] == 1)
    def _():
        lane = lax.broadcasted_iota(jnp.int32, (TMM, ROUTER_COLS), 1)
        local = EXPERTS_PER_GROUP - 1
        subs = range(NSUB)
        tile = lambda ref, j: ref.at[pl.ds(j * TMM * CHUNKS, TMM * CHUNKS)]
        hs = [_from_tiles(tile(h_ref, j), TMM).astype(BF16) for j in subs]
        es = [(tlo_ref[t0 + j] & local, thi_ref[t0 + j] & local) for j in subs]
        pre = [[(_dot(hs[j], wg[e]), _dot(hs[j], wu[e])) for e in es[j]] for j in subs]
        act = [[(a * jax.nn.sigmoid(a) * u).astype(BF16) for a, u in pre[j]] for j in subs]
        ys = [[_dot(act[j][k], wd[es[j][k]]) for k in (0, 1)] for j in subs]
        for j in subs:
            logits = _dot(hs[j], wrbf[...]) + br_ref[...]
            p_top = _group_softmax_top(logits, lane)[3]
            l_lo = jnp.sum(jnp.where(lane == tlo_ref[t0 + j], logits, 0.0), axis=-1, keepdims=True)
            l_hi = jnp.sum(jnp.where(lane == thi_ref[t0 + j], logits, 0.0), axis=-1, keepdims=True)
            w_lo = p_top / (1.0 + jnp.exp(l_hi - l_lo))
            w_hi = p_top / (1.0 + jnp.exp(l_lo - l_hi))
            _to_tiles(tile(y_ref, j), w_lo * ys[j][0] + w_hi * ys[j][1])

    @pl.when(valid_ref[t0] == 0)
    def _():
        y_ref[...] = jnp.zeros_like(y_ref)


def _moe_sorted(tlo, thi, valid, h_sorted, w_gate, w_up, w_down, w_r, b_r):
    n_rows = h_sorted.shape[0] // CHUNKS
    rows = pl.BlockSpec((NSUB * TMM * CHUNKS, LANES), lambda i, *_: (i, 0))
    hbm = pl.BlockSpec(memory_space=pl.ANY)
    w_in_shape, w_out_shape = (D_MODEL, D_EXPERT), (D_EXPERT, D_MODEL)
    grid_spec = pltpu.PrefetchScalarGridSpec(
        num_scalar_prefetch=3,
        grid=(n_rows // (NSUB * TMM),),
        in_specs=[rows, hbm, hbm, hbm,
                  pl.BlockSpec((D_MODEL, ROUTER_COLS), lambda i, *_: (0, 0)),
                  pl.BlockSpec((1, ROUTER_COLS), lambda i, *_: (0, 0))],
        out_specs=rows,
        scratch_shapes=[pltpu.VMEM((D_MODEL, ROUTER_COLS), BF16),
                        pltpu.VMEM((EXPERTS_PER_GROUP,) + w_in_shape, BF16),
                        pltpu.VMEM((EXPERTS_PER_GROUP,) + w_in_shape, BF16),
                        pltpu.VMEM((EXPERTS_PER_GROUP,) + w_out_shape, BF16),
                        pltpu.VMEM((2,) + w_in_shape, F32), pltpu.VMEM((2,) + w_in_shape, F32),
                        pltpu.VMEM((2,) + w_out_shape, F32), pltpu.SemaphoreType.DMA((2, 3))],
    )
    return pl.pallas_call(
        _moe_kernel,
        grid_spec=grid_spec,
        out_shape=jax.ShapeDtypeStruct((n_rows * CHUNKS, LANES), F32),
        compiler_params=pltpu.CompilerParams(dimension_semantics=("arbitrary",), vmem_limit_bytes=VMEM_LIMIT),
        name="moe_sorted",
    )(tlo, thi, valid, h_sorted, w_gate, w_up, w_down, w_r, b_r)


def _final_kernel(pos_ref, posn_ref, x1_ref, g2_ref, lg_ref, lb_ref, ys_ref, o_ref, ybuf, sems):
    nsteps = pl.num_programs(0) * pl.num_programs(1)
    step = pl.program_id(0) * pl.num_programs(1) + pl.program_id(1)
    slot = step % 2

    def gather(rows_ref, sl):
        def issue(i, c):
            for u in range(SUBLANES):
                r = i * SUBLANES + u
                pltpu.make_async_copy(ys_ref.at[_tok_rows(rows_ref[0, r])], ybuf.at[sl, _tok_rows(r)],
                                      sems.at[sl]).start(priority=u % 2)
            return c

        lax.fori_loop(0, TS // SUBLANES, issue, 0)

    @pl.when(step == 0)
    def _():
        gather(pos_ref, slot)

    @pl.when(step + 1 < nsteps)
    def _():
        gather(posn_ref, 1 - slot)

    _wait_tokens(ys_ref, TS, sems.at[slot])
    y = _from_tiles(ybuf.at[slot], TS)
    o_ref[...] = _layer_norm(ALPHA * x1_ref[...] + g2_ref[...] * y, lg_ref[...], lb_ref[...])


def _unsort_ln2(pos, x1, g2, ln_g, ln_b, y_sorted):
    bsz, seq, _ = x1.shape
    nst = seq // TS
    last = bsz * nst - 1
    return pl.pallas_call(
        _final_kernel,
        grid=(bsz, nst),
        in_specs=[pl.BlockSpec((None, 1, TS), lambda b, s: (b * nst + s, 0, 0), memory_space=pltpu.SMEM),
                  pl.BlockSpec((None, 1, TS), lambda b, s: (jnp.minimum(b * nst + s + 1, last), 0, 0),
                               memory_space=pltpu.SMEM),
                  pl.BlockSpec((None, TS, D_MODEL), lambda b, s: (b, s, 0)),
                  pl.BlockSpec((None, None, 1, D_MODEL), lambda b, s: (b, 0, 0, 0)),
                  pl.BlockSpec((1, D_MODEL), lambda b, s: (0, 0)),
                  pl.BlockSpec((1, D_MODEL), lambda b, s: (0, 0)),
                  pl.BlockSpec(memory_space=pl.ANY)],
        out_specs=pl.BlockSpec((None, TS, D_MODEL), lambda b, s: (b, s, 0)),
        out_shape=jax.ShapeDtypeStruct((bsz, seq, D_MODEL), F32),
        scratch_shapes=[pltpu.VMEM((2, TS * CHUNKS, LANES), F32), pltpu.SemaphoreType.DMA((2,))],
        compiler_params=pltpu.CompilerParams(dimension_semantics=("arbitrary", "arbitrary")),
        name="unsort_ln2",
    )(pos, pos, x1, g2, ln_g, ln_b, y_sorted)


def _pair_layout(a, axis):
    shp = a.shape
    a = a.reshape(shp[:axis] + (N_KV_HEADS, N_HEADS // N_KV_HEADS, HEAD_DIM) + shp[axis + 1:])
    return jnp.swapaxes(a, axis, axis + 1).reshape(shp)


def _tile_tables(counts, n_tiles_max):
    b_np = np.arange(N_BUCKETS)
    g_np = b_np // N_PAIRS
    in_group_before = jnp.asarray((g_np[None, :] == g_np[:, None]) & (b_np[None, :] < b_np[:, None]))
    member = jnp.asarray(g_np[None, :] == np.arange(N_GROUPS)[:, None])
    earlier_group = jnp.asarray(np.arange(N_GROUPS)[None, :] < np.arange(N_GROUPS)[:, None])
    e_lo = jnp.asarray(g_np * EXPERTS_PER_GROUP + np.asarray(PAIRS)[b_np % N_PAIRS, 0], jnp.int32)
    e_hi = jnp.asarray(g_np * EXPERTS_PER_GROUP + np.asarray(PAIRS)[b_np % N_PAIRS, 1], jnp.int32)

    n_tile_b = jnp.floor((counts + (TMM - 1.0)) * (1.0 / TMM))
    g_tiles = jnp.sum(jnp.where(member, n_tile_b[None, :], 0.0), axis=1)
    g_padded = jnp.floor((g_tiles + (NSUB - 1.0)) * (1.0 / NSUB)) * NSUB
    g_start = jnp.sum(jnp.where(earlier_group, g_padded[None, :], 0.0), axis=1)
    g_end = g_start + g_padded
    b_start = (jnp.sum(jnp.where(member.T, g_start[None, :], 0.0), axis=1)
               + jnp.sum(jnp.where(in_group_before, n_tile_b[None, :], 0.0), axis=1))
    b_end = b_start + n_tile_b
    starts = (b_start * TMM).astype(jnp.int32)
    zrow = jnp.where(n_tile_b > 0, (b_end - 1.0) * TMM, -1.0).astype(jnp.int32)

    t = jnp.arange(n_tiles_max, dtype=jnp.int32).astype(F32)
    g_of_t = jnp.minimum(jnp.sum(jnp.where(g_end[None, :] <= t[:, None], 1, 0), axis=1), N_GROUPS - 1)
    g_hot = g_of_t[:, None] == jnp.arange(N_GROUPS, dtype=jnp.int32)[None, :]
    local_t = t - jnp.sum(jnp.where(g_hot, g_start[None, :], 0.0), axis=1)
    valid = (local_t < jnp.sum(jnp.where(g_hot, g_tiles[None, :], 0.0), axis=1)) & (t < jnp.sum(g_padded))
    bucket = jnp.sum(jnp.where(b_end[None, :] <= t[:, None], 1, 0), axis=1)
    bucket = jnp.where(valid, bucket, g_of_t * N_PAIRS + (N_PAIRS - 1))
    b_hot = bucket[:, None] == jnp.asarray(b_np, jnp.int32)[None, :]
    tlo = jnp.sum(jnp.where(b_hot, e_lo[None, :], 0), axis=1)
    thi = jnp.sum(jnp.where(b_hot, e_hi[None, :], 0), axis=1)
    pad = jnp.zeros((LANES - N_BUCKETS,), jnp.int32)
    return (jnp.concatenate([starts, pad]), jnp.concatenate([zrow, pad - 1]), tlo, thi,
            jnp.where(valid, 1, 0).astype(jnp.int32))


def kernel(x, c, w_ada, b_ada, w_in, b_in, conv_w, conv_b, conv_ln_g, conv_ln_b, conv_out_g, sinks,
           attn_out_g, w_out, b_out, ln1_g, ln1_b, w_router_group, b_router_group, w_router_expert,
           b_router_expert, w_gate, w_up, w_down, ln2_g, ln2_b):
    assert w_ada.shape[0] == DEPTH
    bsz, seq, _ = x.shape
    row = lambda v: v[0][None, :]

    mod = _modulation(c, w_ada[0], row(b_ada)).reshape(bsz, 6, 1, D_MODEL)
    sh1, sc1, g1, sh2, sc2, g2 = [mod[:, j:j + 1] for j in range(6)]

    q0 = 2 * D_CONV
    w_q = _pair_layout(w_in[0][:, q0:q0 + D_ATTN], 1)
    b_q = _pair_layout(b_in[0][q0:q0 + D_ATTN], 0)[None, :]
    og_attn = _pair_layout(attn_out_g[0], 0)[None, :]
    w_o = jnp.concatenate([w_out[0][:D_CONV], _pair_layout(w_out[0][D_CONV:], 0)], axis=0)

    y_conv, q, kv = _inproj_conv(x, sh1, sc1, w_in[0], row(b_in), w_q, b_q, conv_w[0], row(conv_b),
                                 row(conv_ln_g), row(conv_ln_b), row(conv_out_g))
    y_attn = _attention(q, kv, sinks[0], og_attn)

    pad = ROUTER_COLS - N_EXPERTS - N_GROUPS
    w_r = jnp.concatenate([w_router_expert[0], w_router_group[0], jnp.zeros((D_MODEL, pad), F32)], axis=1)
    b_r = jnp.concatenate([b_router_expert[0], b_router_group[0], jnp.zeros((pad,), F32)])[None, :]
    x1, h2t, code, counts = _outproj_route(x, y_conv, y_attn, w_o, row(b_out), g1, row(ln1_g), row(ln1_b),
                                      sc2, sh2, w_r, b_r)

    n_tiles_max = (bsz * seq) // TMM + N_BUCKETS + N_GROUPS * (NSUB - 1)
    n_tiles_max = -(-n_tiles_max // NSUB) * NSUB
    starts, zrow, tlo, thi, valid = _tile_tables(counts[0, :N_BUCKETS], n_tiles_max)
    h_sorted, pos = _scatter_rows(starts, zrow, valid, code[:, :, 0], h2t, n_tiles_max * TMM)
    y_sorted = _moe_sorted(tlo, thi, valid, h_sorted, w_gate[0], w_up[0], w_down[0], w_r, b_r)
    return _unsort_ln2(pos, x1, g2, row(ln2_g), row(ln2_b), y_sorted)
```

```python
import jax
import jax.numpy as jnp
import numpy as np
from jax import lax
from jax.experimental import pallas as pl
from jax.experimental.pallas import tpu as pltpu

F32 = jnp.float32
BF16 = jnp.bfloat16

D_MODEL = 1024
D_CONV = 512
CONV_WIDTH = 31
GROUP_DIM = 64
N_HEADS = 8
N_KV_HEADS = 2
HEAD_DIM = 64
D_ATTN = N_HEADS * HEAD_DIM
D_KV = N_KV_HEADS * HEAD_DIM
BLOCK = 128
D_IN = 2 * D_CONV + D_ATTN + 2 * D_KV
N_GROUPS = 4
EXPERTS_PER_GROUP = 8
N_EXPERTS = N_GROUPS * EXPERTS_PER_GROUP
D_EXPERT = D_MODEL // 4
DEPTH = 1
ALPHA = (2.0 * DEPTH) ** 0.25
EPS = 1e-5
NEG = -1e30

LANES = 128
SUBLANES = 8
TS = 512
TD = 1024
HALO = 32
CR = 32
AB = 8
PAIR_SLOTS = [j + (N_HEADS // N_KV_HEADS) * g for j in range(N_HEADS // N_KV_HEADS) for g in range(N_KV_HEADS)]
ROUTER_COLS = LANES
VMEM_LIMIT = 56 * 1024 * 1024

PAIRS = [(lo, hi) for lo in range(EXPERTS_PER_GROUP) for hi in range(lo + 1, EXPERTS_PER_GROUP)]
N_PAIRS = len(PAIRS)
N_BUCKETS = N_GROUPS * N_PAIRS
TMM = 128
OUT_SLICES = 4
NSUB = 4
CODE_SHIFT = 16


def _dot(a, b):
    return jnp.dot(a, b, preferred_element_type=F32)


def _dot_nt(a, b):
    return lax.dot_general(a, b, (((1,), (1,)), ((), ())), preferred_element_type=F32)


def _fill_group_maps(bsel_ref, bexp_ref):
    c = bsel_ref.shape[0]
    ch = lax.broadcasted_iota(jnp.int32, (c, LANES), 0) // GROUP_DIM
    gi = lax.broadcasted_iota(jnp.int32, (c, LANES), 1)
    bsel_ref[...] = jnp.where(ch == gi, 1.0 / GROUP_DIM, 0.0).astype(BF16)
    gi2 = lax.broadcasted_iota(jnp.int32, (LANES, c), 0)
    ch2 = lax.broadcasted_iota(jnp.int32, (LANES, c), 1) // GROUP_DIM
    bexp_ref[...] = jnp.where(ch2 == gi2, 1.0, 0.0).astype(BF16)


def _group_rms(y, bsel, bexp):
    ms = _dot((y * y).astype(BF16), bsel)
    r = lax.rsqrt(ms + EPS)
    r_hi = r.astype(BF16)
    r_lo = (r - r_hi.astype(F32)).astype(BF16)
    return y * (_dot(r_hi, bexp) + _dot(r_lo, bexp))


def _layer_norm(y, g, b):
    mu = jnp.mean(y, axis=-1, keepdims=True)
    d = y - mu
    var = jnp.mean(d * d, axis=-1, keepdims=True)
    return d * lax.rsqrt(var + EPS) * g + b


def _mod_kernel(c_ref, w_ref, b_ref, o_ref):
    c = c_ref[...]
    c_act = c * jax.nn.sigmoid(c)
    o_ref[...] = _dot(c_act.astype(BF16), w_ref[...].astype(BF16)) + b_ref[...]


def _modulation(c, w_ada, b_ada):
    bsz = c.shape[0]
    n = w_ada.shape[1]
    return pl.pallas_call(
        _mod_kernel,
        grid=(n // D_MODEL,),
        in_specs=[pl.BlockSpec((bsz, D_MODEL), lambda j: (0, 0)),
                  pl.BlockSpec((D_MODEL, D_MODEL), lambda j: (0, j)),
                  pl.BlockSpec((1, D_MODEL), lambda j: (0, j))],
        out_specs=pl.BlockSpec((bsz, D_MODEL), lambda j: (0, j)),
        out_shape=jax.ShapeDtypeStruct((bsz, n), F32),
        name="adaln_mod",
    )(c, w_ada, b_ada)


def _inproj_conv_kernel(x_ref, sh_ref, sc_ref, w_ref, b_ref, wq_ref, bq_ref, cw_ref, cb_ref, lg_ref, lb_ref, og_ref,
                        yc_ref, q_ref, kv_ref, wbf, gs, cacc, wb, bsel, bexp):
    first = (pl.program_id(0) == 0) & (pl.program_id(1) == 0)
    s = pl.program_id(1)

    @pl.when(first)
    def _():
        wbf[...] = w_ref[...].astype(BF16)
        wbf[:, 2 * D_CONV:2 * D_CONV + D_ATTN] = wq_ref[...].astype(BF16)
        wb[...] = jnp.broadcast_to(cw_ref[...][:, None, :], wb.shape)
        _fill_group_maps(bsel, bexp)

    n_ct = D_CONV // LANES

    @pl.when(s == 0)
    def _():
        gs[0, :, 0:HALO, :] = jnp.zeros((n_ct, HALO, LANES), F32)

    @pl.when(s > 0)
    def _():
        gs[0, :, 0:HALO, :] = gs[0, :, TS:TS + HALO, :]

    h = (x_ref[...] * (1.0 + sc_ref[...]) + sh_ref[...]).astype(BF16)
    q0 = 2 * D_CONV
    k0 = q0 + D_ATTN
    cb = cb_ref[...]
    lg = lg_ref[...]
    lb = lb_ref[...]
    og = og_ref[...]
    n_sh = TS + HALO - SUBLANES
    half = TS // 2
    late_cols = 2 * LANES

    def glu_unit(rows, j):
        cs = slice(j * late_cols, (j + 1) * late_cols)
        u_a = _dot(h[rows], wbf[:, cs]) + b_ref[:, cs]
        u_b = _dot(h[rows], wbf[:, D_CONV + j * late_cols:D_CONV + (j + 1) * late_cols]) \
            + b_ref[:, D_CONV + j * late_cols:D_CONV + (j + 1) * late_cols]
        glu = u_a * jax.nn.sigmoid(u_b)
        for cc in range(late_cols // LANES):
            c = j * (late_cols // LANES) + cc
            gs[0, c, HALO + rows.start:HALO + rows.stop, :] = glu[:, cc * LANES:(cc + 1) * LANES]

    def late_proj(j):
        c0 = j * late_cols
        if c0 < D_ATTN:
            q = _dot(h, wbf[:, q0 + c0:q0 + c0 + late_cols]) + bq_ref[:, c0:c0 + late_cols]
            q_ref[:, c0:c0 + late_cols] = (q * (HEAD_DIM ** -0.5)).astype(BF16)
        else:
            kv_ref[...] = (_dot(h, wbf[:, k0:k0 + 2 * D_KV]) + b_ref[:, k0:k0 + 2 * D_KV]).astype(BF16)

    def shifted(lo, hi):
        for j in range(1, SUBLANES):
            gs[j, :, lo:hi, :] = gs[0, :, lo + j:hi + j, :]

    def conv_chunk(c, r0):
        ls = slice(c * LANES, (c + 1) * LANES)
        acc = jnp.broadcast_to(cb[:, ls], (CR, LANES)).reshape(CR // SUBLANES, SUBLANES, LANES)
        for k in range(CONV_WIDTH):
            off = HALO - (CONV_WIDTH - 1) + k
            a0 = r0 + (off // SUBLANES) * SUBLANES
            seg = gs[off % SUBLANES, c, a0:a0 + CR, :]
            acc = acc + wb[k, :, ls][None] * seg.reshape(CR // SUBLANES, SUBLANES, LANES)
        cacc[c, r0:r0 + CR, :] = acc.reshape(CR, LANES)

    first, second = slice(0, half), slice(half, TS)
    n_glu = D_CONV // late_cols
    n_late = (D_ATTN + 2 * D_KV) // late_cols
    for j in range(n_glu):
        glu_unit(first, j)
    split = HALO - SUBLANES + half
    shifted(0, split)
    chunks_a = [(c, r0) for c in range(n_ct) for r0 in range(0, half, CR)]
    chunks_b = [(c, r0) for c in range(n_ct) for r0 in range(half, TS, CR)]
    units_a = [lambda j=j: glu_unit(second, j) for j in range(n_glu)] + [lambda: late_proj(0)]
    units_b = [lambda j=j: late_proj(j) for j in range(1, n_late)]
    for chunks, units in ((chunks_a, units_a), (chunks_b, units_b)):
        if chunks is chunks_b:
            shifted(split, n_sh)
        every = len(chunks) // len(units)
        for n, (c, r0) in enumerate(chunks):
            conv_chunk(c, r0)
            if n % every == 0 and n // every < len(units):
                units[n // every]()
    y = _layer_norm(jnp.concatenate([cacc[c] for c in range(n_ct)], axis=1), lg, lb)
    y = y * jax.nn.sigmoid(y)
    yc_ref[...] = (_group_rms(y, bsel[...], bexp[...]) * og).astype(BF16)


def _inproj_conv(x, sh1, sc1, w_in, b_in, w_q, b_q, conv_w, conv_b, ln_g, ln_b, out_g):
    bsz, seq, _ = x.shape
    vec = lambda n: pl.BlockSpec((1, n), lambda b, s: (0, 0))
    mod = pl.BlockSpec((None, None, 1, D_MODEL), lambda b, s: (b, 0, 0, 0))
    tile = lambda n: pl.BlockSpec((None, TS, n), lambda b, s: (b, s, 0))
    return pl.pallas_call(
        _inproj_conv_kernel,
        grid=(bsz, seq // TS),
        in_specs=[tile(D_MODEL), mod, mod,
                  pl.BlockSpec((D_MODEL, D_IN), lambda b, s: (0, 0)), vec(D_IN),
                  pl.BlockSpec((D_MODEL, D_ATTN), lambda b, s: (0, 0)), vec(D_ATTN),
                  pl.BlockSpec((CONV_WIDTH, D_CONV), lambda b, s: (0, 0)),
                  vec(D_CONV), vec(D_CONV), vec(D_CONV), vec(D_CONV)],
        out_specs=[tile(D_CONV), tile(D_ATTN), tile(2 * D_KV)],
        out_shape=[jax.ShapeDtypeStruct((bsz, seq, D_CONV), BF16),
                   jax.ShapeDtypeStruct((bsz, seq, D_ATTN), BF16),
                   jax.ShapeDtypeStruct((bsz, seq, 2 * D_KV), BF16)],
        scratch_shapes=[pltpu.VMEM((D_MODEL, D_IN), BF16),
                        pltpu.VMEM((SUBLANES, D_CONV // LANES, TS + HALO, LANES), F32),
                        pltpu.VMEM((D_CONV // LANES, TS, LANES), F32),
                        pltpu.VMEM((CONV_WIDTH, SUBLANES, D_CONV), F32),
                        pltpu.VMEM((D_CONV, LANES), BF16),
                        pltpu.VMEM((LANES, D_CONV), BF16)],
        compiler_params=pltpu.CompilerParams(
            dimension_semantics=("arbitrary", "arbitrary"), vmem_limit_bytes=VMEM_LIMIT),
        name="inproj_conv",
    )(x, sh1, sc1, w_in, b_in, w_q, b_q, conv_w, conv_b, ln_g, ln_b, out_g)


def _attn_kernel(sinks_ref, q_ref, kvc_ref, kvp_ref, og_ref, o_ref, bsel, bexp):
    n = pl.program_id(1)

    @pl.when((pl.program_id(0) == 0) & (n == 0))
    def _():
        _fill_group_maps(bsel, bexp)

    qr = lax.broadcasted_iota(jnp.int32, (BLOCK, BLOCK), 0)
    kc = lax.broadcasted_iota(jnp.int32, (BLOCK, BLOCK), 1)
    from_prev = kc > qr
    lane = lax.broadcasted_iota(jnp.int32, (BLOCK, LANES), 1)
    lower = lane < HEAD_DIM
    lower_bf = jnp.where(lower, 1.0, 0.0).astype(BF16)
    upper_bf = jnp.where(lower, 0.0, 1.0).astype(BF16)
    og = og_ref[...]
    blocks = range(AB)
    kvs = []
    for blk in blocks:
        r0 = blk * BLOCK
        if blk == 0:
            kvs.append(jnp.concatenate([kvp_ref[...], kvc_ref[0:BLOCK, :]], axis=0))
        else:
            kvs.append(kvc_ref[r0 - BLOCK:r0 + BLOCK, :])
    s_all = []
    for blk in blocks:
        r0 = blk * BLOCK
        parts = []
        for j in range(D_ATTN // LANES):
            col = q_ref[r0:r0 + BLOCK, j * LANES:(j + 1) * LANES]
            parts += [col * lower_bf, col * upper_bf]
        s_all.append(_dot_nt(jnp.concatenate(parts, axis=0), kvs[blk][:, 0:D_KV]))
    probs, dens = [], []
    for blk in blocks:
        pb = []
        den = jnp.ones((BLOCK, LANES), F32)
        for slot in range(N_HEADS):
            s_prev = s_all[blk][slot * BLOCK:(slot + 1) * BLOCK, 0:BLOCK]
            s_own = s_all[blk][slot * BLOCK:(slot + 1) * BLOCK, BLOCK:2 * BLOCK]
            if blk == 0:
                s_prev = jnp.where(n > 0, s_prev, NEG)
            sc = jnp.where(from_prev, s_prev, s_own)
            sink = sinks_ref[PAIR_SLOTS[slot]]
            m = jnp.maximum(jnp.max(sc, axis=-1, keepdims=True), sink)
            p = jnp.exp(sc - m)
            den = jnp.where(lane == slot, jnp.sum(p, axis=-1, keepdims=True) + jnp.exp(sink - m), den)
            p = p.astype(BF16)
            zero = jnp.zeros_like(p)
            pb.append(jnp.concatenate([jnp.where(from_prev, p, zero), jnp.where(from_prev, zero, p)], axis=1))
        probs.append(jnp.concatenate(pb, axis=0))
        dens.append(den)
    pvs = [_dot(probs[blk], kvs[blk][:, D_KV:2 * D_KV]) for blk in blocks]
    for blk in blocks:
        pv, den = pvs[blk], dens[blk]
        o = jnp.concatenate(
            [jnp.where(lower, pv[(2 * j) * BLOCK:(2 * j + 1) * BLOCK], pv[(2 * j + 1) * BLOCK:(2 * j + 2) * BLOCK])
             for j in range(D_ATTN // LANES)], axis=1)
        ms = _dot((o * o).astype(BF16), bsel[...])
        r = lax.rsqrt(ms + EPS * den * den)
        r_hi = r.astype(BF16)
        r_lo = (r - r_hi.astype(F32)).astype(BF16)
        scale = _dot(r_hi, bexp[...]) + _dot(r_lo, bexp[...])
        o_ref[blk * BLOCK:(blk + 1) * BLOCK, :] = (o * scale * og).astype(BF16)


def _attention(q, kv, sinks, out_g):
    bsz, seq, _ = q.shape
    rows = AB * BLOCK
    grid_spec = pltpu.PrefetchScalarGridSpec(
        num_scalar_prefetch=1,
        grid=(bsz, seq // rows),
        in_specs=[pl.BlockSpec((None, rows, D_ATTN), lambda b, n, sk: (b, n, 0)),
                  pl.BlockSpec((None, rows, 2 * D_KV), lambda b, n, sk: (b, n, 0)),
                  pl.BlockSpec((None, BLOCK, 2 * D_KV), lambda b, n, sk: (b, jnp.maximum(AB * n - 1, 0), 0)),
                  pl.BlockSpec((1, D_ATTN), lambda b, n, sk: (0, 0))],
        out_specs=pl.BlockSpec((None, rows, D_ATTN), lambda b, n, sk: (b, n, 0)),
        scratch_shapes=[pltpu.VMEM((D_ATTN, LANES), BF16), pltpu.VMEM((LANES, D_ATTN), BF16)],
    )
    return pl.pallas_call(
        _attn_kernel,
        grid_spec=grid_spec,
        out_shape=jax.ShapeDtypeStruct((bsz, seq, D_ATTN), BF16),
        compiler_params=pltpu.CompilerParams(dimension_semantics=("arbitrary", "arbitrary")),
        name="swa_attention",
    )(sinks, q, kv, kv, out_g)


def _group_softmax_top(logits, lane):
    is_g = (lane >= N_EXPERTS) & (lane < N_EXPERTS + N_GROUPS)
    gl = jnp.where(is_g, logits, NEG)
    gmax = jnp.max(gl, axis=-1, keepdims=True)
    p_top = 1.0 / jnp.sum(jnp.where(is_g, jnp.exp(gl - gmax), 0.0), axis=-1, keepdims=True)
    return is_g, gl, gmax, p_top


def _route_bucket(logits):
    rows = logits.shape[0]
    lane = lax.broadcasted_iota(jnp.int32, (rows, ROUTER_COLS), 1)
    lanef = lane.astype(F32)
    is_g, gl, gmax, _ = _group_softmax_top(logits, lane)
    gidx = jnp.min(jnp.where(is_g & (gl == gmax), lanef - N_EXPERTS, 99.0), axis=-1, keepdims=True)
    in_grp = (lane < N_EXPERTS) & ((lane // EXPERTS_PER_GROUP) == gidx.astype(jnp.int32))
    el = jnp.where(in_grp, logits, NEG)
    m1 = jnp.max(el, axis=-1, keepdims=True)
    i1 = jnp.min(jnp.where(in_grp & (el == m1), lanef, 999.0), axis=-1, keepdims=True)
    rest = in_grp & (lanef != i1)
    el2 = jnp.where(rest, logits, NEG)
    m2 = jnp.max(el2, axis=-1, keepdims=True)
    i2 = jnp.min(jnp.where(rest & (el2 == m2), lanef, 999.0), axis=-1, keepdims=True)
    lo = jnp.minimum(i1, i2) - EXPERTS_PER_GROUP * gidx
    hi = jnp.maximum(i1, i2) - EXPERTS_PER_GROUP * gidx
    pair = lo * (EXPERTS_PER_GROUP - 1) - lo * (lo - 1.0) * 0.5 + (hi - lo - 1.0)
    return gidx * N_PAIRS + pair, lanef


def _outproj_kernel(x_ref, yc_ref, ya_ref, w_ref, b_ref, g1_ref, lg_ref, lb_ref, sc2_ref, sh2_ref,
                    wr_ref, br_ref, x1_ref, h2t_ref, code_ref, counts_ref, wbf, wrbf, ltri, running):
    @pl.when((pl.program_id(0) == 0) & (pl.program_id(1) == 0))
    def _():
        wbf[...] = w_ref[...].astype(BF16)
        wrbf[...] = wr_ref[...].astype(BF16)
        r = lax.broadcasted_iota(jnp.int32, (TS, TS), 0)
        c = lax.broadcasted_iota(jnp.int32, (TS, TS), 1)
        ltri[...] = jnp.where(c < r, 1.0, 0.0).astype(BF16)
        running[...] = jnp.zeros_like(running)

    nsl = OUT_SLICES
    rows = TS // nsl
    sl = [slice(k * rows, (k + 1) * rows) for k in range(nsl)]
    mix = [_dot(yc_ref[r, :], wbf[0:D_CONV, :]) + _dot(ya_ref[r, :], wbf[D_CONV:, :]) + b_ref[...] for r in sl]
    x1 = [_layer_norm(ALPHA * x_ref[r, :] + g1_ref[...] * m, lg_ref[...], lb_ref[...]) for r, m in zip(sl, mix)]
    h2 = [v * (1.0 + sc2_ref[...]) + sh2_ref[...] for v in x1]
    for k in range(nsl):
        x1_ref[sl[k], :] = x1[k]
        _to_tiles(h2t_ref.at[pl.ds(k * rows * CHUNKS, rows * CHUNKS)], h2[k])
    logits = [_dot(v.astype(BF16), wrbf[...]) + br_ref[...] for v in h2]
    routed = [_route_bucket(lg) for lg in logits]
    mine = [lanef == bucket for bucket, lanef in routed]
    onehot = [jnp.where(mk, 1.0, 0.0) for mk in mine]
    within = [_dot(ltri[0:rows, 0:rows], oh.astype(BF16)) for oh in onehot]
    base = running[...]
    for k in range(nsl):
        rank = jnp.sum(jnp.where(mine[k], within[k] + base, 0.0), axis=-1, keepdims=True)
        base = base + jnp.sum(onehot[k], axis=0, keepdims=True)
        code = routed[k][0].astype(jnp.int32) * (1 << CODE_SHIFT) + rank.astype(jnp.int32)
        code_ref[sl[k], :] = jnp.broadcast_to(code, (rows, SUBLANES))
    running[...] = base
    counts_ref[...] = running[...]


def _outproj_route(x, y_conv, y_attn, w_out, b_out, g1, ln_g, ln_b, sc2, sh2, w_r, b_r):
    bsz, seq, _ = x.shape
    nst = seq // TS
    vec = lambda n: pl.BlockSpec((1, n), lambda b, s: (0, 0))
    mod = pl.BlockSpec((None, None, 1, D_MODEL), lambda b, s: (b, 0, 0, 0))
    tile = lambda n: pl.BlockSpec((None, TS, n), lambda b, s: (b, s, 0))
    return pl.pallas_call(
        _outproj_kernel,
        grid=(bsz, seq // TS),
        in_specs=[tile(D_MODEL), tile(D_CONV), tile(D_ATTN),
                  pl.BlockSpec((D_MODEL, D_MODEL), lambda b, s: (0, 0)), vec(D_MODEL),
                  mod, vec(D_MODEL), vec(D_MODEL), mod, mod,
                  pl.BlockSpec((D_MODEL, ROUTER_COLS), lambda b, s: (0, 0)), vec(ROUTER_COLS)],
        out_specs=[tile(D_MODEL), pl.BlockSpec((TS * CHUNKS, LANES), lambda b, s: (b * nst + s, 0)),
                   tile(SUBLANES), vec(LANES)],
        out_shape=[jax.ShapeDtypeStruct((bsz, seq, D_MODEL), F32),
                   jax.ShapeDtypeStruct((bsz * seq * CHUNKS, LANES), F32),
                   jax.ShapeDtypeStruct((bsz, seq, SUBLANES), jnp.int32),
                   jax.ShapeDtypeStruct((1, LANES), F32)],
        scratch_shapes=[pltpu.VMEM((D_MODEL, D_MODEL), BF16), pltpu.VMEM((D_MODEL, ROUTER_COLS), BF16),
                        pltpu.VMEM((TS, TS), BF16), pltpu.VMEM((1, LANES), F32)],
        compiler_params=pltpu.CompilerParams(
            dimension_semantics=("arbitrary", "arbitrary"), vmem_limit_bytes=VMEM_LIMIT),
        name="outproj_route",
    )(x, y_conv, y_attn, w_out, b_out, g1, ln_g, ln_b, sc2, sh2, w_r, b_r)


def _row_of(starts_ref, code):
    return (starts_ref[lax.shift_right_logical(code, jnp.int32(CODE_SHIFT))]
            + (code & ((1 << CODE_SHIFT) - 1)))


CHUNKS = D_MODEL // LANES
assert CHUNKS == SUBLANES


def _tok_rows(p):
    return pl.ds(pl.multiple_of(p * CHUNKS, CHUNKS), CHUNKS)


def _to_tiles(ref, x):
    for c in range(CHUNKS):
        ref[pl.ds(c, x.shape[0], stride=CHUNKS), :] = x[:, c * LANES:(c + 1) * LANES]


def _from_tiles(ref, n):
    return jnp.concatenate([ref[pl.ds(c, n, stride=CHUNKS), :] for c in range(CHUNKS)], axis=1)


def _wait_tokens(hbm_ref, n, sem):
    pltpu.make_async_copy(hbm_ref.at[pl.ds(0, n * CHUNKS)], hbm_ref.at[pl.ds(0, n * CHUNKS)], sem).wait()


def _scatter_kernel(starts_ref, zrow_ref, valid_ref, code_ref, h2t_ref, hs_ref, pos_ref, hbuf, zeros, sems, zsem):
    nsteps = pl.num_programs(0) * pl.num_programs(1)
    step = pl.program_id(0) * pl.num_programs(1) + pl.program_id(1)
    slot = step % 2

    def _tile_rows(tok):
        return pl.ds(pl.multiple_of(tok * CHUNKS, TMM * CHUNKS), TMM * CHUNKS)

    def zero_copy(b):
        return pltpu.make_async_copy(zeros, hs_ref.at[_tile_rows(zrow_ref[b])], zsem)

    def tail_copy(t):
        return pltpu.make_async_copy(zeros, hs_ref.at[_tile_rows(t * TMM)], zsem)

    @pl.when(step == 0)
    def _():
        zeros[...] = jnp.zeros_like(zeros)
        n_tiles = hs_ref.shape[0] // (TMM * CHUNKS)

        def start(b, c):
            @pl.when(zrow_ref[b] >= 0)
            def _():
                zero_copy(b).start()
            return c

        def wait(b, c):
            @pl.when(zrow_ref[b] >= 0)
            def _():
                zero_copy(b).wait()
            return c

        def start_tail(t, c):
            @pl.when(valid_ref[t] == 0)
            def _():
                tail_copy(t).start()
            return c

        def wait_tail(t, c):
            @pl.when(valid_ref[t] == 0)
            def _():
                tail_copy(t).wait()
            return c

        lax.fori_loop(0, N_BUCKETS, start, 0)
        lax.fori_loop(0, n_tiles, start_tail, 0)
        lax.fori_loop(0, N_BUCKETS, wait, 0)
        lax.fori_loop(0, n_tiles, wait_tail, 0)

    @pl.when(step >= 2)
    def _():
        _wait_tokens(hs_ref, TD, sems.at[slot])

    hbuf[slot] = h2t_ref[...]

    for r in range(TD):
        pos = _row_of(starts_ref, code_ref[0, r])
        pos_ref[0, r] = pos
        pltpu.make_async_copy(hbuf.at[slot, _tok_rows(r)], hs_ref.at[_tok_rows(pos)],
                              sems.at[slot]).start(priority=r % 2)

    @pl.when(step == nsteps - 1)
    def _():
        _wait_tokens(hs_ref, TD, sems.at[slot])
        _wait_tokens(hs_ref, TD, sems.at[1 - slot])


def _scatter_rows(starts, zrow, valid, code, h2t, n_rows):
    bsz, seq = code.shape
    nst = seq // TD
    grid_spec = pltpu.PrefetchScalarGridSpec(
        num_scalar_prefetch=3,
        grid=(bsz, nst),
        in_specs=[pl.BlockSpec((None, 1, TD), lambda b, s, *_: (b * nst + s, 0, 0), memory_space=pltpu.SMEM),
                  pl.BlockSpec((TD * CHUNKS, LANES), lambda b, s, *_: (b * nst + s, 0))],
        out_specs=[pl.BlockSpec(memory_space=pl.ANY),
                   pl.BlockSpec((None, 1, TD), lambda b, s, *_: (b * nst + s, 0, 0), memory_space=pltpu.SMEM)],
        scratch_shapes=[pltpu.VMEM((2, TD * CHUNKS, LANES), F32), pltpu.VMEM((TMM * CHUNKS, LANES), F32),
                        pltpu.SemaphoreType.DMA((2,)), pltpu.SemaphoreType.DMA],
    )
    return pl.pallas_call(
        _scatter_kernel,
        grid_spec=grid_spec,
        out_shape=[jax.ShapeDtypeStruct((n_rows * CHUNKS, LANES), F32),
                   jax.ShapeDtypeStruct((bsz * nst, 1, TD), jnp.int32)],
        compiler_params=pltpu.CompilerParams(dimension_semantics=("arbitrary", "arbitrary")),
        name="moe_scatter",
    )(starts, zrow, valid, code.reshape(bsz * nst, 1, TD), h2t)


def _moe_kernel(tlo_ref, thi_ref, valid_ref, h_ref, wg_hbm, wu_hbm, wd_hbm, wr_ref, br_ref, y_ref,
                wrbf, wg, wu, wd, sg, su, sd, sems):
    i = pl.program_id(0)
    t0 = i * NSUB
    group = lax.shift_right_logical(tlo_ref[t0], jnp.int32(3))
    prev_group = lax.shift_right_logical(tlo_ref[jnp.maximum(t0 - NSUB, 0)], jnp.int32(3))

    @pl.when(i == 0)
    def _():
        wrbf[...] = wr_ref[...].astype(BF16)

    @pl.when((i == 0) | (group != prev_group))
    def _():
        def copies(e, slot):
            ex = group * EXPERTS_PER_GROUP + e
            return (pltpu.make_async_copy(wg_hbm.at[ex], sg.at[slot], sems.at[slot, 0]),
                    pltpu.make_async_copy(wu_hbm.at[ex], su.at[slot], sems.at[slot, 1]),
                    pltpu.make_async_copy(wd_hbm.at[ex], sd.at[slot], sems.at[slot, 2]))

        for cp in copies(0, 0):
            cp.start()

        def land(e, c):
            slot = e % 2

            @pl.when(e + 1 < EXPERTS_PER_GROUP)
            def _():
                for cp in copies(e + 1, 1 - slot):
                    cp.start()

            for cp in copies(e, slot):
                cp.wait()
            wg[e] = sg[slot].astype(BF16)
            wu[e] = su[slot].astype(BF16)
            wd[e] = sd[slot].astype(BF16)
            return c

        lax.fori_loop(0, EXPERTS_PER_GROUP, land, 0)

    @pl.when(valid_ref[t0] == 1)
    def _():
        lane = lax.broadcasted_iota(jnp.int32, (TMM, ROUTER_COLS), 1)
        local = EXPERTS_PER_GROUP - 1
        subs = range(NSUB)
        tile = lambda ref, j: ref.at[pl.ds(j * TMM * CHUNKS, TMM * CHUNKS)]
        hs = [_from_tiles(tile(h_ref, j), TMM).astype(BF16) for j in subs]
        es = [(tlo_ref[t0 + j] & local, thi_ref[t0 + j] & local) for j in subs]
        pre = [[(_dot(hs[j], wg[e]), _dot(hs[j], wu[e])) for e in es[j]] for j in subs]
        act = [[(a * jax.nn.sigmoid(a) * u).astype(BF16) for a, u in pre[j]] for j in subs]
        ys = [[_dot(act[j][k], wd[es[j][k]]) for k in (0, 1)] for j in subs]
        for j in subs:
            logits = _dot(hs[j], wrbf[...]) + br_ref[...]
            p_top = _group_softmax_top(logits, lane)[3]
            l_lo = jnp.sum(jnp.where(lane == tlo_ref[t0 + j], logits, 0.0), axis=-1, keepdims=True)
            l_hi = jnp.sum(jnp.where(lane == thi_ref[t0 + j], logits, 0.0), axis=-1, keepdims=True)
            w_lo = p_top / (1.0 + jnp.exp(l_hi - l_lo))
            w_hi = p_top / (1.0 + jnp.exp(l_lo - l_hi))
            _to_tiles(tile(y_ref, j), w_lo * ys[j][0] + w_hi * ys[j][1])

    @pl.when(valid_ref[t0] == 0)
    def _():
        y_ref[...] = jnp.zeros_like(y_ref)


def _moe_sorted(tlo, thi, valid, h_sorted, w_gate, w_up, w_down, w_r, b_r):
    n_rows = h_sorted.shape[0] // CHUNKS
    rows = pl.BlockSpec((NSUB * TMM * CHUNKS, LANES), lambda i, *_: (i, 0))
    hbm = pl.BlockSpec(memory_space=pl.ANY)
    w_in_shape, w_out_shape = (D_MODEL, D_EXPERT), (D_EXPERT, D_MODEL)
    grid_spec = pltpu.PrefetchScalarGridSpec(
        num_scalar_prefetch=3,
        grid=(n_rows // (NSUB * TMM),),
        in_specs=[rows, hbm, hbm, hbm,
                  pl.BlockSpec((D_MODEL, ROUTER_COLS), lambda i, *_: (0, 0)),
                  pl.BlockSpec((1, ROUTER_COLS), lambda i, *_: (0, 0))],
        out_specs=rows,
        scratch_shapes=[pltpu.VMEM((D_MODEL, ROUTER_COLS), BF16),
                        pltpu.VMEM((EXPERTS_PER_GROUP,) + w_in_shape, BF16),
                        pltpu.VMEM((EXPERTS_PER_GROUP,) + w_in_shape, BF16),
                        pltpu.VMEM((EXPERTS_PER_GROUP,) + w_out_shape, BF16),
                        pltpu.VMEM((2,) + w_in_shape, F32), pltpu.VMEM((2,) + w_in_shape, F32),
                        pltpu.VMEM((2,) + w_out_shape, F32), pltpu.SemaphoreType.DMA((2, 3))],
    )
    return pl.pallas_call(
        _moe_kernel,
        grid_spec=grid_spec,
        out_shape=jax.ShapeDtypeStruct((n_rows * CHUNKS, LANES), F32),
        compiler_params=pltpu.CompilerParams(dimension_semantics=("arbitrary",), vmem_limit_bytes=VMEM_LIMIT),
        name="moe_sorted",
    )(tlo, thi, valid, h_sorted, w_gate, w_up, w_down, w_r, b_r)


def _final_kernel(pos_ref, posn_ref, x1_ref, g2_ref, lg_ref, lb_ref, ys_ref, o_ref, ybuf, sems):
    nsteps = pl.num_programs(0) * pl.num_programs(1)
    step = pl.program_id(0) * pl.num_programs(1) + pl.program_id(1)
    slot = step % 2

    def gather(rows_ref, sl):
        def issue(i, c):
            for u in range(SUBLANES):
                r = i * SUBLANES + u
                pltpu.make_async_copy(ys_ref.at[_tok_rows(rows_ref[0, r])], ybuf.at[sl, _tok_rows(r)],
                                      sems.at[sl]).start(priority=u % 2)
            return c

        lax.fori_loop(0, TD // SUBLANES, issue, 0)

    @pl.when(step == 0)
    def _():
        gather(pos_ref, slot)

    @pl.when(step + 1 < nsteps)
    def _():
        gather(posn_ref, 1 - slot)

    _wait_tokens(ys_ref, TD, sems.at[slot])
    y = _from_tiles(ybuf.at[slot], TD)
    o_ref[...] = _layer_norm(ALPHA * x1_ref[...] + g2_ref[...] * y, lg_ref[...], lb_ref[...])


def _unsort_ln2(pos, x1, g2, ln_g, ln_b, y_sorted):
    bsz, seq, _ = x1.shape
    nst = seq // TD
    last = bsz * nst - 1
    return pl.pallas_call(
        _final_kernel,
        grid=(bsz, nst),
        in_specs=[pl.BlockSpec((None, 1, TD), lambda b, s: (b * nst + s, 0, 0), memory_space=pltpu.SMEM),
                  pl.BlockSpec((None, 1, TD), lambda b, s: (jnp.minimum(b * nst + s + 1, last), 0, 0),
                               memory_space=pltpu.SMEM),
                  pl.BlockSpec((None, TD, D_MODEL), lambda b, s: (b, s, 0)),
                  pl.BlockSpec((None, None, 1, D_MODEL), lambda b, s: (b, 0, 0, 0)),
                  pl.BlockSpec((1, D_MODEL), lambda b, s: (0, 0)),
                  pl.BlockSpec((1, D_MODEL), lambda b, s: (0, 0)),
                  pl.BlockSpec(memory_space=pl.ANY)],
        out_specs=pl.BlockSpec((None, TD, D_MODEL), lambda b, s: (b, s, 0)),
        out_shape=jax.ShapeDtypeStruct((bsz, seq, D_MODEL), F32),
        scratch_shapes=[pltpu.VMEM((2, TD * CHUNKS, LANES), F32), pltpu.SemaphoreType.DMA((2,))],
        compiler_params=pltpu.CompilerParams(dimension_semantics=("arbitrary", "arbitrary")),
        name="unsort_ln2",
    )(pos, pos, x1, g2, ln_g, ln_b, y_sorted)


def _pair_layout(a, axis):
    shp = a.shape
    a = a.reshape(shp[:axis] + (N_KV_HEADS, N_HEADS // N_KV_HEADS, HEAD_DIM) + shp[axis + 1:])
    return jnp.swapaxes(a, axis, axis + 1).reshape(shp)


def _tile_tables(counts, n_tiles_max):
    b_np = np.arange(N_BUCKETS)
    g_np = b_np // N_PAIRS
    in_group_before = jnp.asarray((g_np[None, :] == g_np[:, None]) & (b_np[None, :] < b_np[:, None]))
    member = jnp.asarray(g_np[None, :] == np.arange(N_GROUPS)[:, None])
    earlier_group = jnp.asarray(np.arange(N_GROUPS)[None, :] < np.arange(N_GROUPS)[:, None])
    e_lo = jnp.asarray(g_np * EXPERTS_PER_GROUP + np.asarray(PAIRS)[b_np % N_PAIRS, 0], jnp.int32)
    e_hi = jnp.asarray(g_np * EXPERTS_PER_GROUP + np.asarray(PAIRS)[b_np % N_PAIRS, 1], jnp.int32)

    n_tile_b = jnp.floor((counts + (TMM - 1.0)) * (1.0 / TMM))
    g_tiles = jnp.sum(jnp.where(member, n_tile_b[None, :], 0.0), axis=1)
    g_padded = jnp.floor((g_tiles + (NSUB - 1.0)) * (1.0 / NSUB)) * NSUB
    g_start = jnp.sum(jnp.where(earlier_group, g_padded[None, :], 0.0), axis=1)
    g_end = g_start + g_padded
    b_start = (jnp.sum(jnp.where(member.T, g_start[None, :], 0.0), axis=1)
               + jnp.sum(jnp.where(in_group_before, n_tile_b[None, :], 0.0), axis=1))
    b_end = b_start + n_tile_b
    starts = (b_start * TMM).astype(jnp.int32)
    zrow = jnp.where(n_tile_b > 0, (b_end - 1.0) * TMM, -1.0).astype(jnp.int32)

    t = jnp.arange(n_tiles_max, dtype=jnp.int32).astype(F32)
    g_of_t = jnp.minimum(jnp.sum(jnp.where(g_end[None, :] <= t[:, None], 1, 0), axis=1), N_GROUPS - 1)
    g_hot = g_of_t[:, None] == jnp.arange(N_GROUPS, dtype=jnp.int32)[None, :]
    local_t = t - jnp.sum(jnp.where(g_hot, g_start[None, :], 0.0), axis=1)
    valid = (local_t < jnp.sum(jnp.where(g_hot, g_tiles[None, :], 0.0), axis=1)) & (t < jnp.sum(g_padded))
    bucket = jnp.sum(jnp.where(b_end[None, :] <= t[:, None], 1, 0), axis=1)
    bucket = jnp.where(valid, bucket, g_of_t * N_PAIRS + (N_PAIRS - 1))
    b_hot = bucket[:, None] == jnp.asarray(b_np, jnp.int32)[None, :]
    tlo = jnp.sum(jnp.where(b_hot, e_lo[None, :], 0), axis=1)
    thi = jnp.sum(jnp.where(b_hot, e_hi[None, :], 0), axis=1)
    pad = jnp.zeros((LANES - N_BUCKETS,), jnp.int32)
    return (jnp.concatenate([starts, pad]), jnp.concatenate([zrow, pad - 1]), tlo, thi,
            jnp.where(valid, 1, 0).astype(jnp.int32))


def kernel(x, c, w_ada, b_ada, w_in, b_in, conv_w, conv_b, conv_ln_g, conv_ln_b, conv_out_g, sinks,
           attn_out_g, w_out, b_out, ln1_g, ln1_b, w_router_group, b_router_group, w_router_expert,
           b_router_expert, w_gate, w_up, w_down, ln2_g, ln2_b):
    assert w_ada.shape[0] == DEPTH
    bsz, seq, _ = x.shape
    row = lambda v: v[0][None, :]

    mod = _modulation(c, w_ada[0], row(b_ada)).reshape(bsz, 6, 1, D_MODEL)
    sh1, sc1, g1, sh2, sc2, g2 = [mod[:, j:j + 1] for j in range(6)]

    q0 = 2 * D_CONV
    w_q = _pair_layout(w_in[0][:, q0:q0 + D_ATTN], 1)
    b_q = _pair_layout(b_in[0][q0:q0 + D_ATTN], 0)[None, :]
    og_attn = _pair_layout(attn_out_g[0], 0)[None, :]
    w_o = jnp.concatenate([w_out[0][:D_CONV], _pair_layout(w_out[0][D_CONV:], 0)], axis=0)

    y_conv, q, kv = _inproj_conv(x, sh1, sc1, w_in[0], row(b_in), w_q, b_q, conv_w[0], row(conv_b),
                                 row(conv_ln_g), row(conv_ln_b), row(conv_out_g))
    y_attn = _attention(q, kv, sinks[0], og_attn)

    pad = ROUTER_COLS - N_EXPERTS - N_GROUPS
    w_r = jnp.concatenate([w_router_expert[0], w_router_group[0], jnp.zeros((D_MODEL, pad), F32)], axis=1)
    b_r = jnp.concatenate([b_router_expert[0], b_router_group[0], jnp.zeros((pad,), F32)])[None, :]
    x1, h2t, code, counts = _outproj_route(x, y_conv, y_attn, w_o, row(b_out), g1, row(ln1_g), row(ln1_b),
                                      sc2, sh2, w_r, b_r)

    n_tiles_max = (bsz * seq) // TMM + N_BUCKETS + N_GROUPS * (NSUB - 1)
    n_tiles_max = -(-n_tiles_max // NSUB) * NSUB
    starts, zrow, tlo, thi, valid = _tile_tables(counts[0, :N_BUCKETS], n_tiles_max)
    h_sorted, pos = _scatter_rows(starts, zrow, valid, code[:, :, 0], h2t, n_tiles_max * TMM)
    y_sorted = _moe_sorted(tlo, thi, valid, h_sorted, w_gate[0], w_up[0], w_down[0], w_r, b_r)
    return _unsort_ln2(pos, x1, g2, row(ln2_g), row(ln2_b), y_sorted)
```

```python
import jax
import jax.numpy as jnp
import numpy as np
from jax import lax
from jax.experimental import pallas as pl
from jax.experimental.pallas import tpu as pltpu

F32 = jnp.float32
BF16 = jnp.bfloat16

D_MODEL = 1024
D_CONV = 512
CONV_WIDTH = 31
GROUP_DIM = 64
N_HEADS = 8
N_KV_HEADS = 2
HEAD_DIM = 64
D_ATTN = N_HEADS * HEAD_DIM
D_KV = N_KV_HEADS * HEAD_DIM
BLOCK = 128
D_IN = 2 * D_CONV + D_ATTN + 2 * D_KV
N_GROUPS = 4
EXPERTS_PER_GROUP = 8
N_EXPERTS = N_GROUPS * EXPERTS_PER_GROUP
D_EXPERT = D_MODEL // 4
DEPTH = 1
ALPHA = (2.0 * DEPTH) ** 0.25
EPS = 1e-5
NEG = -1e30

LANES = 128
SUBLANES = 8
TS = 512
TD = 1024
TF = 256
HALO = 32
CR = 32
AB = 8
PAIR_SLOTS = [j + (N_HEADS // N_KV_HEADS) * g for j in range(N_HEADS // N_KV_HEADS) for g in range(N_KV_HEADS)]
ROUTER_COLS = LANES
VMEM_LIMIT = 56 * 1024 * 1024

PAIRS = [(lo, hi) for lo in range(EXPERTS_PER_GROUP) for hi in range(lo + 1, EXPERTS_PER_GROUP)]
N_PAIRS = len(PAIRS)
N_BUCKETS = N_GROUPS * N_PAIRS
TMM = 128
OUT_SLICES = 4
NSUB = 4
CODE_SHIFT = 16


def _dot(a, b):
    return jnp.dot(a, b, preferred_element_type=F32)


def _dot_nt(a, b):
    return lax.dot_general(a, b, (((1,), (1,)), ((), ())), preferred_element_type=F32)


def _fill_group_maps(bsel_ref, bexp_ref):
    c = bsel_ref.shape[0]
    ch = lax.broadcasted_iota(jnp.int32, (c, LANES), 0) // GROUP_DIM
    gi = lax.broadcasted_iota(jnp.int32, (c, LANES), 1)
    bsel_ref[...] = jnp.where(ch == gi, 1.0 / GROUP_DIM, 0.0).astype(BF16)
    gi2 = lax.broadcasted_iota(jnp.int32, (LANES, c), 0)
    ch2 = lax.broadcasted_iota(jnp.int32, (LANES, c), 1) // GROUP_DIM
    bexp_ref[...] = jnp.where(ch2 == gi2, 1.0, 0.0).astype(BF16)


def _group_rms(y, bsel, bexp):
    ms = _dot((y * y).astype(BF16), bsel)
    r = lax.rsqrt(ms + EPS)
    r_hi = r.astype(BF16)
    r_lo = (r - r_hi.astype(F32)).astype(BF16)
    return y * (_dot(r_hi, bexp) + _dot(r_lo, bexp))


def _layer_norm(y, g, b):
    mu = jnp.mean(y, axis=-1, keepdims=True)
    d = y - mu
    var = jnp.mean(d * d, axis=-1, keepdims=True)
    return d * lax.rsqrt(var + EPS) * g + b


def _mod_kernel(c_ref, w_ref, b_ref, o_ref):
    c = c_ref[...]
    c_act = c * jax.nn.sigmoid(c)
    o_ref[...] = _dot(c_act.astype(BF16), w_ref[...].astype(BF16)) + b_ref[...]


def _modulation(c, w_ada, b_ada):
    bsz = c.shape[0]
    n = w_ada.shape[1]
    return pl.pallas_call(
        _mod_kernel,
        grid=(n // D_MODEL,),
        in_specs=[pl.BlockSpec((bsz, D_MODEL), lambda j: (0, 0)),
                  pl.BlockSpec((D_MODEL, D_MODEL), lambda j: (0, j)),
                  pl.BlockSpec((1, D_MODEL), lambda j: (0, j))],
        out_specs=pl.BlockSpec((bsz, D_MODEL), lambda j: (0, j)),
        out_shape=jax.ShapeDtypeStruct((bsz, n), F32),
        name="adaln_mod",
    )(c, w_ada, b_ada)


def _inproj_conv_kernel(x_ref, sh_ref, sc_ref, w_ref, b_ref, wq_ref, bq_ref, cw_ref, cb_ref, lg_ref, lb_ref, og_ref,
                        yc_ref, q_ref, kv_ref, wbf, gs, cacc, wb, bsel, bexp):
    first = (pl.program_id(0) == 0) & (pl.program_id(1) == 0)
    s = pl.program_id(1)

    @pl.when(first)
    def _():
        wbf[...] = w_ref[...].astype(BF16)
        wbf[:, 2 * D_CONV:2 * D_CONV + D_ATTN] = wq_ref[...].astype(BF16)
        wb[...] = jnp.broadcast_to(cw_ref[...][:, None, :], wb.shape)
        _fill_group_maps(bsel, bexp)

    n_ct = D_CONV // LANES

    @pl.when(s == 0)
    def _():
        gs[0, :, 0:HALO, :] = jnp.zeros((n_ct, HALO, LANES), F32)

    @pl.when(s > 0)
    def _():
        gs[0, :, 0:HALO, :] = gs[0, :, TS:TS + HALO, :]

    h = (x_ref[...] * (1.0 + sc_ref[...]) + sh_ref[...]).astype(BF16)
    q0 = 2 * D_CONV
    k0 = q0 + D_ATTN
    cb = cb_ref[...]
    lg = lg_ref[...]
    lb = lb_ref[...]
    og = og_ref[...]
    n_sh = TS + HALO - SUBLANES
    half = TS // 2
    late_cols = 2 * LANES

    def glu_unit(rows, j):
        cs = slice(j * late_cols, (j + 1) * late_cols)
        u_a = _dot(h[rows], wbf[:, cs]) + b_ref[:, cs]
        u_b = _dot(h[rows], wbf[:, D_CONV + j * late_cols:D_CONV + (j + 1) * late_cols]) \
            + b_ref[:, D_CONV + j * late_cols:D_CONV + (j + 1) * late_cols]
        glu = u_a * jax.nn.sigmoid(u_b)
        for cc in range(late_cols // LANES):
            c = j * (late_cols // LANES) + cc
            gs[0, c, HALO + rows.start:HALO + rows.stop, :] = glu[:, cc * LANES:(cc + 1) * LANES]

    def late_proj(j):
        c0 = j * late_cols
        if c0 < D_ATTN:
            q = _dot(h, wbf[:, q0 + c0:q0 + c0 + late_cols]) + bq_ref[:, c0:c0 + late_cols]
            q_ref[:, c0:c0 + late_cols] = (q * (HEAD_DIM ** -0.5)).astype(BF16)
        else:
            kv_ref[...] = (_dot(h, wbf[:, k0:k0 + 2 * D_KV]) + b_ref[:, k0:k0 + 2 * D_KV]).astype(BF16)

    def shifted(lo, hi):
        for j in range(1, SUBLANES):
            gs[j, :, lo:hi, :] = gs[0, :, lo + j:hi + j, :]

    def conv_chunk(c, r0):
        ls = slice(c * LANES, (c + 1) * LANES)
        acc = jnp.broadcast_to(cb[:, ls], (CR, LANES)).reshape(CR // SUBLANES, SUBLANES, LANES)
        for k in range(CONV_WIDTH):
            off = HALO - (CONV_WIDTH - 1) + k
            a0 = r0 + (off // SUBLANES) * SUBLANES
            seg = gs[off % SUBLANES, c, a0:a0 + CR, :]
            acc = acc + wb[k, :, ls][None] * seg.reshape(CR // SUBLANES, SUBLANES, LANES)
        cacc[c, r0:r0 + CR, :] = acc.reshape(CR, LANES)

    first, second = slice(0, half), slice(half, TS)
    n_glu = D_CONV // late_cols
    n_late = (D_ATTN + 2 * D_KV) // late_cols
    for j in range(n_glu):
        glu_unit(first, j)
    split = HALO - SUBLANES + half
    shifted(0, split)
    chunks_a = [(c, r0) for c in range(n_ct) for r0 in range(0, half, CR)]
    chunks_b = [(c, r0) for c in range(n_ct) for r0 in range(half, TS, CR)]
    units_a = [lambda j=j: glu_unit(second, j) for j in range(n_glu)] + [lambda: late_proj(0)]
    units_b = [lambda j=j: late_proj(j) for j in range(1, n_late)]
    for chunks, units in ((chunks_a, units_a), (chunks_b, units_b)):
        if chunks is chunks_b:
            shifted(split, n_sh)
        every = len(chunks) // len(units)
        for n, (c, r0) in enumerate(chunks):
            conv_chunk(c, r0)
            if n % every == 0 and n // every < len(units):
                units[n // every]()
    y = _layer_norm(jnp.concatenate([cacc[c] for c in range(n_ct)], axis=1), lg, lb)
    y = y * jax.nn.sigmoid(y)
    yc_ref[...] = (_group_rms(y, bsel[...], bexp[...]) * og).astype(BF16)


def _inproj_conv(x, sh1, sc1, w_in, b_in, w_q, b_q, conv_w, conv_b, ln_g, ln_b, out_g):
    bsz, seq, _ = x.shape
    vec = lambda n: pl.BlockSpec((1, n), lambda b, s: (0, 0))
    mod = pl.BlockSpec((None, None, 1, D_MODEL), lambda b, s: (b, 0, 0, 0))
    tile = lambda n: pl.BlockSpec((None, TS, n), lambda b, s: (b, s, 0))
    return pl.pallas_call(
        _inproj_conv_kernel,
        grid=(bsz, seq // TS),
        in_specs=[tile(D_MODEL), mod, mod,
                  pl.BlockSpec((D_MODEL, D_IN), lambda b, s: (0, 0)), vec(D_IN),
                  pl.BlockSpec((D_MODEL, D_ATTN), lambda b, s: (0, 0)), vec(D_ATTN),
                  pl.BlockSpec((CONV_WIDTH, D_CONV), lambda b, s: (0, 0)),
                  vec(D_CONV), vec(D_CONV), vec(D_CONV), vec(D_CONV)],
        out_specs=[tile(D_CONV), tile(D_ATTN), tile(2 * D_KV)],
        out_shape=[jax.ShapeDtypeStruct((bsz, seq, D_CONV), BF16),
                   jax.ShapeDtypeStruct((bsz, seq, D_ATTN), BF16),
                   jax.ShapeDtypeStruct((bsz, seq, 2 * D_KV), BF16)],
        scratch_shapes=[pltpu.VMEM((D_MODEL, D_IN), BF16),
                        pltpu.VMEM((SUBLANES, D_CONV // LANES, TS + HALO, LANES), F32),
                        pltpu.VMEM((D_CONV // LANES, TS, LANES), F32),
                        pltpu.VMEM((CONV_WIDTH, SUBLANES, D_CONV), F32),
                        pltpu.VMEM((D_CONV, LANES), BF16),
                        pltpu.VMEM((LANES, D_CONV), BF16)],
        compiler_params=pltpu.CompilerParams(
            dimension_semantics=("arbitrary", "arbitrary"), vmem_limit_bytes=VMEM_LIMIT),
        name="inproj_conv",
    )(x, sh1, sc1, w_in, b_in, w_q, b_q, conv_w, conv_b, ln_g, ln_b, out_g)


def _attn_kernel(sinks_ref, q_ref, kvc_ref, kvp_ref, og_ref, o_ref, bsel, bexp):
    n = pl.program_id(1)

    @pl.when((pl.program_id(0) == 0) & (n == 0))
    def _():
        _fill_group_maps(bsel, bexp)

    qr = lax.broadcasted_iota(jnp.int32, (BLOCK, BLOCK), 0)
    kc = lax.broadcasted_iota(jnp.int32, (BLOCK, BLOCK), 1)
    from_prev = kc > qr
    lane = lax.broadcasted_iota(jnp.int32, (BLOCK, LANES), 1)
    lower = lane < HEAD_DIM
    lower_bf = jnp.where(lower, 1.0, 0.0).astype(BF16)
    upper_bf = jnp.where(lower, 0.0, 1.0).astype(BF16)
    og = og_ref[...]
    blocks = range(AB)
    kvs = []
    for blk in blocks:
        r0 = blk * BLOCK
        if blk == 0:
            kvs.append(jnp.concatenate([kvp_ref[...], kvc_ref[0:BLOCK, :]], axis=0))
        else:
            kvs.append(kvc_ref[r0 - BLOCK:r0 + BLOCK, :])
    s_all = []
    for blk in blocks:
        r0 = blk * BLOCK
        parts = []
        for j in range(D_ATTN // LANES):
            col = q_ref[r0:r0 + BLOCK, j * LANES:(j + 1) * LANES]
            parts += [col * lower_bf, col * upper_bf]
        s_all.append(_dot_nt(jnp.concatenate(parts, axis=0), kvs[blk][:, 0:D_KV]))
    probs, dens = [], []
    for blk in blocks:
        pb = []
        den = jnp.ones((BLOCK, LANES), F32)
        for slot in range(N_HEADS):
            s_prev = s_all[blk][slot * BLOCK:(slot + 1) * BLOCK, 0:BLOCK]
            s_own = s_all[blk][slot * BLOCK:(slot + 1) * BLOCK, BLOCK:2 * BLOCK]
            if blk == 0:
                s_prev = jnp.where(n > 0, s_prev, NEG)
            sc = jnp.where(from_prev, s_prev, s_own)
            sink = sinks_ref[PAIR_SLOTS[slot]]
            m = jnp.maximum(jnp.max(sc, axis=-1, keepdims=True), sink)
            p = jnp.exp(sc - m)
            den = jnp.where(lane == slot, jnp.sum(p, axis=-1, keepdims=True) + jnp.exp(sink - m), den)
            p = p.astype(BF16)
            zero = jnp.zeros_like(p)
            pb.append(jnp.concatenate([jnp.where(from_prev, p, zero), jnp.where(from_prev, zero, p)], axis=1))
        probs.append(jnp.concatenate(pb, axis=0))
        dens.append(den)
    pvs = [_dot(probs[blk], kvs[blk][:, D_KV:2 * D_KV]) for blk in blocks]
    for blk in blocks:
        pv, den = pvs[blk], dens[blk]
        o = jnp.concatenate(
            [jnp.where(lower, pv[(2 * j) * BLOCK:(2 * j + 1) * BLOCK], pv[(2 * j + 1) * BLOCK:(2 * j + 2) * BLOCK])
             for j in range(D_ATTN // LANES)], axis=1)
        ms = _dot((o * o).astype(BF16), bsel[...])
        r = lax.rsqrt(ms + EPS * den * den)
        r_hi = r.astype(BF16)
        r_lo = (r - r_hi.astype(F32)).astype(BF16)
        scale = _dot(r_hi, bexp[...]) + _dot(r_lo, bexp[...])
        o_ref[blk * BLOCK:(blk + 1) * BLOCK, :] = (o * scale * og).astype(BF16)


def _attention(q, kv, sinks, out_g):
    bsz, seq, _ = q.shape
    rows = AB * BLOCK
    grid_spec = pltpu.PrefetchScalarGridSpec(
        num_scalar_prefetch=1,
        grid=(bsz, seq // rows),
        in_specs=[pl.BlockSpec((None, rows, D_ATTN), lambda b, n, sk: (b, n, 0)),
                  pl.BlockSpec((None, rows, 2 * D_KV), lambda b, n, sk: (b, n, 0)),
                  pl.BlockSpec((None, BLOCK, 2 * D_KV), lambda b, n, sk: (b, jnp.maximum(AB * n - 1, 0), 0)),
                  pl.BlockSpec((1, D_ATTN), lambda b, n, sk: (0, 0))],
        out_specs=pl.BlockSpec((None, rows, D_ATTN), lambda b, n, sk: (b, n, 0)),
        scratch_shapes=[pltpu.VMEM((D_ATTN, LANES), BF16), pltpu.VMEM((LANES, D_ATTN), BF16)],
    )
    return pl.pallas_call(
        _attn_kernel,
        grid_spec=grid_spec,
        out_shape=jax.ShapeDtypeStruct((bsz, seq, D_ATTN), BF16),
        compiler_params=pltpu.CompilerParams(dimension_semantics=("arbitrary", "arbitrary")),
        name="swa_attention",
    )(sinks, q, kv, kv, out_g)


def _group_softmax_top(logits, lane):
    is_g = (lane >= N_EXPERTS) & (lane < N_EXPERTS + N_GROUPS)
    gl = jnp.where(is_g, logits, NEG)
    gmax = jnp.max(gl, axis=-1, keepdims=True)
    p_top = 1.0 / jnp.sum(jnp.where(is_g, jnp.exp(gl - gmax), 0.0), axis=-1, keepdims=True)
    return is_g, gl, gmax, p_top


def _route_bucket(logits):
    rows = logits.shape[0]
    lane = lax.broadcasted_iota(jnp.int32, (rows, ROUTER_COLS), 1)
    lanef = lane.astype(F32)
    is_g, gl, gmax, _ = _group_softmax_top(logits, lane)
    gidx = jnp.min(jnp.where(is_g & (gl == gmax), lanef - N_EXPERTS, 99.0), axis=-1, keepdims=True)
    in_grp = (lane < N_EXPERTS) & ((lane // EXPERTS_PER_GROUP) == gidx.astype(jnp.int32))
    el = jnp.where(in_grp, logits, NEG)
    m1 = jnp.max(el, axis=-1, keepdims=True)
    i1 = jnp.min(jnp.where(in_grp & (el == m1), lanef, 999.0), axis=-1, keepdims=True)
    rest = in_grp & (lanef != i1)
    el2 = jnp.where(rest, logits, NEG)
    m2 = jnp.max(el2, axis=-1, keepdims=True)
    i2 = jnp.min(jnp.where(rest & (el2 == m2), lanef, 999.0), axis=-1, keepdims=True)
    lo = jnp.minimum(i1, i2) - EXPERTS_PER_GROUP * gidx
    hi = jnp.maximum(i1, i2) - EXPERTS_PER_GROUP * gidx
    pair = lo * (EXPERTS_PER_GROUP - 1) - lo * (lo - 1.0) * 0.5 + (hi - lo - 1.0)
    return gidx * N_PAIRS + pair, lanef


def _outproj_kernel(x_ref, yc_ref, ya_ref, w_ref, b_ref, g1_ref, lg_ref, lb_ref, sc2_ref, sh2_ref,
                    wr_ref, br_ref, x1_ref, h2t_ref, code_ref, counts_ref, wbf, wrbf, ltri, running):
    @pl.when((pl.program_id(0) == 0) & (pl.program_id(1) == 0))
    def _():
        wbf[...] = w_ref[...].astype(BF16)
        wrbf[...] = wr_ref[...].astype(BF16)
        r = lax.broadcasted_iota(jnp.int32, (TS, TS), 0)
        c = lax.broadcasted_iota(jnp.int32, (TS, TS), 1)
        ltri[...] = jnp.where(c < r, 1.0, 0.0).astype(BF16)
        running[...] = jnp.zeros_like(running)

    nsl = OUT_SLICES
    rows = TS // nsl
    sl = [slice(k * rows, (k + 1) * rows) for k in range(nsl)]
    mix = [_dot(yc_ref[r, :], wbf[0:D_CONV, :]) + _dot(ya_ref[r, :], wbf[D_CONV:, :]) + b_ref[...] for r in sl]
    x1 = [_layer_norm(ALPHA * x_ref[r, :] + g1_ref[...] * m, lg_ref[...], lb_ref[...]) for r, m in zip(sl, mix)]
    h2 = [v * (1.0 + sc2_ref[...]) + sh2_ref[...] for v in x1]
    for k in range(nsl):
        x1_ref[sl[k], :] = x1[k]
        _to_tiles(h2t_ref.at[pl.ds(k * rows * CHUNKS, rows * CHUNKS)], h2[k])
    logits = [_dot(v.astype(BF16), wrbf[...]) + br_ref[...] for v in h2]
    routed = [_route_bucket(lg) for lg in logits]
    mine = [lanef == bucket for bucket, lanef in routed]
    onehot = [jnp.where(mk, 1.0, 0.0) for mk in mine]
    within = [_dot(ltri[0:rows, 0:rows], oh.astype(BF16)) for oh in onehot]
    base = running[...]
    for k in range(nsl):
        rank = jnp.sum(jnp.where(mine[k], within[k] + base, 0.0), axis=-1, keepdims=True)
        base = base + jnp.sum(onehot[k], axis=0, keepdims=True)
        code = routed[k][0].astype(jnp.int32) * (1 << CODE_SHIFT) + rank.astype(jnp.int32)
        code_ref[sl[k], :] = jnp.broadcast_to(code, (rows, SUBLANES))
    running[...] = base
    counts_ref[...] = running[...]


def _outproj_route(x, y_conv, y_attn, w_out, b_out, g1, ln_g, ln_b, sc2, sh2, w_r, b_r):
    bsz, seq, _ = x.shape
    nst = seq // TS
    vec = lambda n: pl.BlockSpec((1, n), lambda b, s: (0, 0))
    mod = pl.BlockSpec((None, None, 1, D_MODEL), lambda b, s: (b, 0, 0, 0))
    tile = lambda n: pl.BlockSpec((None, TS, n), lambda b, s: (b, s, 0))
    return pl.pallas_call(
        _outproj_kernel,
        grid=(bsz, seq // TS),
        in_specs=[tile(D_MODEL), tile(D_CONV), tile(D_ATTN),
                  pl.BlockSpec((D_MODEL, D_MODEL), lambda b, s: (0, 0)), vec(D_MODEL),
                  mod, vec(D_MODEL), vec(D_MODEL), mod, mod,
                  pl.BlockSpec((D_MODEL, ROUTER_COLS), lambda b, s: (0, 0)), vec(ROUTER_COLS)],
        out_specs=[tile(D_MODEL), pl.BlockSpec((TS * CHUNKS, LANES), lambda b, s: (b * nst + s, 0)),
                   tile(SUBLANES), vec(LANES)],
        out_shape=[jax.ShapeDtypeStruct((bsz, seq, D_MODEL), F32),
                   jax.ShapeDtypeStruct((bsz * seq * CHUNKS, LANES), F32),
                   jax.ShapeDtypeStruct((bsz, seq, SUBLANES), jnp.int32),
                   jax.ShapeDtypeStruct((1, LANES), F32)],
        scratch_shapes=[pltpu.VMEM((D_MODEL, D_MODEL), BF16), pltpu.VMEM((D_MODEL, ROUTER_COLS), BF16),
                        pltpu.VMEM((TS, TS), BF16), pltpu.VMEM((1, LANES), F32)],
        compiler_params=pltpu.CompilerParams(
            dimension_semantics=("arbitrary", "arbitrary"), vmem_limit_bytes=VMEM_LIMIT),
        name="outproj_route",
    )(x, y_conv, y_attn, w_out, b_out, g1, ln_g, ln_b, sc2, sh2, w_r, b_r)


def _row_of(starts_ref, code):
    return (starts_ref[lax.shift_right_logical(code, jnp.int32(CODE_SHIFT))]
            + (code & ((1 << CODE_SHIFT) - 1)))


CHUNKS = D_MODEL // LANES
assert CHUNKS == SUBLANES


def _tok_rows(p):
    return pl.ds(pl.multiple_of(p * CHUNKS, CHUNKS), CHUNKS)


def _to_tiles(ref, x):
    for c in range(CHUNKS):
        ref[pl.ds(c, x.shape[0], stride=CHUNKS), :] = x[:, c * LANES:(c + 1) * LANES]


def _from_tiles(ref, n):
    return jnp.concatenate([ref[pl.ds(c, n, stride=CHUNKS), :] for c in range(CHUNKS)], axis=1)


def _wait_tokens(hbm_ref, n, sem):
    pltpu.make_async_copy(hbm_ref.at[pl.ds(0, n * CHUNKS)], hbm_ref.at[pl.ds(0, n * CHUNKS)], sem).wait()


def _scatter_kernel(starts_ref, zrow_ref, valid_ref, code_ref, h2t_ref, hs_ref, pos_ref, hbuf, zeros, sems, zsem):
    nsteps = pl.num_programs(0) * pl.num_programs(1)
    step = pl.program_id(0) * pl.num_programs(1) + pl.program_id(1)
    slot = step % 2

    def _tile_rows(tok):
        return pl.ds(pl.multiple_of(tok * CHUNKS, TMM * CHUNKS), TMM * CHUNKS)

    def zero_copy(b):
        return pltpu.make_async_copy(zeros, hs_ref.at[_tile_rows(zrow_ref[b])], zsem)

    def tail_copy(t):
        return pltpu.make_async_copy(zeros, hs_ref.at[_tile_rows(t * TMM)], zsem)

    @pl.when(step == 0)
    def _():
        zeros[...] = jnp.zeros_like(zeros)
        n_tiles = hs_ref.shape[0] // (TMM * CHUNKS)

        def start(b, c):
            @pl.when(zrow_ref[b] >= 0)
            def _():
                zero_copy(b).start()
            return c

        def wait(b, c):
            @pl.when(zrow_ref[b] >= 0)
            def _():
                zero_copy(b).wait()
            return c

        def start_tail(t, c):
            @pl.when(valid_ref[t] == 0)
            def _():
                tail_copy(t).start()
            return c

        def wait_tail(t, c):
            @pl.when(valid_ref[t] == 0)
            def _():
                tail_copy(t).wait()
            return c

        lax.fori_loop(0, N_BUCKETS, start, 0)
        lax.fori_loop(0, n_tiles, start_tail, 0)
        lax.fori_loop(0, N_BUCKETS, wait, 0)
        lax.fori_loop(0, n_tiles, wait_tail, 0)

    @pl.when(step >= 2)
    def _():
        _wait_tokens(hs_ref, TD, sems.at[slot])

    hbuf[slot] = h2t_ref[...]

    for r in range(TD):
        pos = _row_of(starts_ref, code_ref[0, r])
        pos_ref[0, r] = pos
        pltpu.make_async_copy(hbuf.at[slot, _tok_rows(r)], hs_ref.at[_tok_rows(pos)],
                              sems.at[slot]).start(priority=r % 2)

    @pl.when(step == nsteps - 1)
    def _():
        _wait_tokens(hs_ref, TD, sems.at[slot])
        _wait_tokens(hs_ref, TD, sems.at[1 - slot])


def _scatter_rows(starts, zrow, valid, code, h2t, n_rows):
    bsz, seq = code.shape
    nst = seq // TD
    grid_spec = pltpu.PrefetchScalarGridSpec(
        num_scalar_prefetch=3,
        grid=(bsz, nst),
        in_specs=[pl.BlockSpec((None, 1, TD), lambda b, s, *_: (b * nst + s, 0, 0), memory_space=pltpu.SMEM),
                  pl.BlockSpec((TD * CHUNKS, LANES), lambda b, s, *_: (b * nst + s, 0))],
        out_specs=[pl.BlockSpec(memory_space=pl.ANY),
                   pl.BlockSpec((None, 1, TD), lambda b, s, *_: (b * nst + s, 0, 0), memory_space=pltpu.SMEM)],
        scratch_shapes=[pltpu.VMEM((2, TD * CHUNKS, LANES), F32), pltpu.VMEM((TMM * CHUNKS, LANES), F32),
                        pltpu.SemaphoreType.DMA((2,)), pltpu.SemaphoreType.DMA],
    )
    return pl.pallas_call(
        _scatter_kernel,
        grid_spec=grid_spec,
        out_shape=[jax.ShapeDtypeStruct((n_rows * CHUNKS, LANES), F32),
                   jax.ShapeDtypeStruct((bsz * nst, 1, TD), jnp.int32)],
        compiler_params=pltpu.CompilerParams(dimension_semantics=("arbitrary", "arbitrary")),
        name="moe_scatter",
    )(starts, zrow, valid, code.reshape(bsz * nst, 1, TD), h2t)


def _moe_kernel(tlo_ref, thi_ref, valid_ref, h_ref, wg_hbm, wu_hbm, wd_hbm, wr_ref, br_ref, y_ref,
                wrbf, wg, wu, wd, sg, su, sd, sems):
    i = pl.program_id(0)
    t0 = i * NSUB
    group = lax.shift_right_logical(tlo_ref[t0], jnp.int32(3))
    prev_group = lax.shift_right_logical(tlo_ref[jnp.maximum(t0 - NSUB, 0)], jnp.int32(3))

    @pl.when(i == 0)
    def _():
        wrbf[...] = wr_ref[...].astype(BF16)

    @pl.when((i == 0) | (group != prev_group))
    def _():
        def copies(e, slot):
            ex = group * EXPERTS_PER_GROUP + e
            return (pltpu.make_async_copy(wg_hbm.at[ex], sg.at[slot], sems.at[slot, 0]),
                    pltpu.make_async_copy(wu_hbm.at[ex], su.at[slot], sems.at[slot, 1]),
                    pltpu.make_async_copy(wd_hbm.at[ex], sd.at[slot], sems.at[slot, 2]))

        for cp in copies(0, 0):
            cp.start()

        def land(e, c):
            slot = e % 2

            @pl.when(e + 1 < EXPERTS_PER_GROUP)
            def _():
                for cp in copies(e + 1, 1 - slot):
                    cp.start()

            for cp in copies(e, slot):
                cp.wait()
            wg[e] = sg[slot].astype(BF16)
            wu[e] = su[slot].astype(BF16)
            wd[e] = sd[slot].astype(BF16)
            return c

        lax.fori_loop(0, EXPERTS_PER_GROUP, land, 0)

    @pl.when(valid_ref[t0] == 1)
    def _():
        lane = lax.broadcasted_iota(jnp.int32, (TMM, ROUTER_COLS), 1)
        local = EXPERTS_PER_GROUP - 1
        subs = range(NSUB)
        tile = lambda ref, j: ref.at[pl.ds(j * TMM * CHUNKS, TMM * CHUNKS)]
        hs = [_from_tiles(tile(h_ref, j), TMM).astype(BF16) for j in subs]
        es = [(tlo_ref[t0 + j] & local, thi_ref[t0 + j] & local) for j in subs]
        pre = [[(_dot(hs[j], wg[e]), _dot(hs[j], wu[e])) for e in es[j]] for j in subs]
        act = [[(a * jax.nn.sigmoid(a) * u).astype(BF16) for a, u in pre[j]] for j in subs]
        ys = [[_dot(act[j][k], wd[es[j][k]]) for k in (0, 1)] for j in subs]
        for j in subs:
            logits = _dot(hs[j], wrbf[...]) + br_ref[...]
            p_top = _group_softmax_top(logits, lane)[3]
            l_lo = jnp.sum(jnp.where(lane == tlo_ref[t0 + j], logits, 0.0), axis=-1, keepdims=True)
            l_hi = jnp.sum(jnp.where(lane == thi_ref[t0 + j], logits, 0.0), axis=-1, keepdims=True)
            w_lo = p_top / (1.0 + jnp.exp(l_hi - l_lo))
            w_hi = p_top / (1.0 + jnp.exp(l_lo - l_hi))
            _to_tiles(tile(y_ref, j), w_lo * ys[j][0] + w_hi * ys[j][1])

    @pl.when(valid_ref[t0] == 0)
    def _():
        y_ref[...] = jnp.zeros_like(y_ref)


def _moe_sorted(tlo, thi, valid, h_sorted, w_gate, w_up, w_down, w_r, b_r):
    n_rows = h_sorted.shape[0] // CHUNKS
    rows = pl.BlockSpec((NSUB * TMM * CHUNKS, LANES), lambda i, *_: (i, 0))
    hbm = pl.BlockSpec(memory_space=pl.ANY)
    w_in_shape, w_out_shape = (D_MODEL, D_EXPERT), (D_EXPERT, D_MODEL)
    grid_spec = pltpu.PrefetchScalarGridSpec(
        num_scalar_prefetch=3,
        grid=(n_rows // (NSUB * TMM),),
        in_specs=[rows, hbm, hbm, hbm,
                  pl.BlockSpec((D_MODEL, ROUTER_COLS), lambda i, *_: (0, 0)),
                  pl.BlockSpec((1, ROUTER_COLS), lambda i, *_: (0, 0))],
        out_specs=rows,
        scratch_shapes=[pltpu.VMEM((D_MODEL, ROUTER_COLS), BF16),
                        pltpu.VMEM((EXPERTS_PER_GROUP,) + w_in_shape, BF16),
                        pltpu.VMEM((EXPERTS_PER_GROUP,) + w_in_shape, BF16),
                        pltpu.VMEM((EXPERTS_PER_GROUP,) + w_out_shape, BF16),
                        pltpu.VMEM((2,) + w_in_shape, F32), pltpu.VMEM((2,) + w_in_shape, F32),
                        pltpu.VMEM((2,) + w_out_shape, F32), pltpu.SemaphoreType.DMA((2, 3))],
    )
    return pl.pallas_call(
        _moe_kernel,
        grid_spec=grid_spec,
        out_shape=jax.ShapeDtypeStruct((n_rows * CHUNKS, LANES), F32),
        compiler_params=pltpu.CompilerParams(dimension_semantics=("arbitrary",), vmem_limit_bytes=VMEM_LIMIT),
        name="moe_sorted",
    )(tlo, thi, valid, h_sorted, w_gate, w_up, w_down, w_r, b_r)


def _final_kernel(pos_ref, posn_ref, x1_ref, g2_ref, lg_ref, lb_ref, ys_ref, o_ref, ybuf, sems):
    nsteps = pl.num_programs(0) * pl.num_programs(1)
    step = pl.program_id(0) * pl.num_programs(1) + pl.program_id(1)
    slot = step % 2

    def gather(rows_ref, sl):
        def issue(i, c):
            for u in range(SUBLANES):
                r = i * SUBLANES + u
                pltpu.make_async_copy(ys_ref.at[_tok_rows(rows_ref[0, r])], ybuf.at[sl, _tok_rows(r)],
                                      sems.at[sl]).start(priority=u % 2)
            return c

        lax.fori_loop(0, TF // SUBLANES, issue, 0)

    @pl.when(step == 0)
    def _():
        gather(pos_ref, slot)

    @pl.when(step + 1 < nsteps)
    def _():
        gather(posn_ref, 1 - slot)

    _wait_tokens(ys_ref, TF, sems.at[slot])
    y = _from_tiles(ybuf.at[slot], TF)
    o_ref[...] = _layer_norm(ALPHA * x1_ref[...] + g2_ref[...] * y, lg_ref[...], lb_ref[...])


def _unsort_ln2(pos, x1, g2, ln_g, ln_b, y_sorted):
    bsz, seq, _ = x1.shape
    nst = seq // TF
    last = bsz * nst - 1
    return pl.pallas_call(
        _final_kernel,
        grid=(bsz, nst),
        in_specs=[pl.BlockSpec((None, 1, TF), lambda b, s: (b * nst + s, 0, 0), memory_space=pltpu.SMEM),
                  pl.BlockSpec((None, 1, TF), lambda b, s: (jnp.minimum(b * nst + s + 1, last), 0, 0),
                               memory_space=pltpu.SMEM),
                  pl.BlockSpec((None, TF, D_MODEL), lambda b, s: (b, s, 0)),
                  pl.BlockSpec((None, None, 1, D_MODEL), lambda b, s: (b, 0, 0, 0)),
                  pl.BlockSpec((1, D_MODEL), lambda b, s: (0, 0)),
                  pl.BlockSpec((1, D_MODEL), lambda b, s: (0, 0)),
                  pl.BlockSpec(memory_space=pl.ANY)],
        out_specs=pl.BlockSpec((None, TF, D_MODEL), lambda b, s: (b, s, 0)),
        out_shape=jax.ShapeDtypeStruct((bsz, seq, D_MODEL), F32),
        scratch_shapes=[pltpu.VMEM((2, TF * CHUNKS, LANES), F32), pltpu.SemaphoreType.DMA((2,))],
        compiler_params=pltpu.CompilerParams(dimension_semantics=("arbitrary", "arbitrary")),
        name="unsort_ln2",
    )(pos.reshape(-1, 1, TF), pos.reshape(-1, 1, TF), x1, g2, ln_g, ln_b, y_sorted)


def _pair_layout(a, axis):
    shp = a.shape
    a = a.reshape(shp[:axis] + (N_KV_HEADS, N_HEADS // N_KV_HEADS, HEAD_DIM) + shp[axis + 1:])
    return jnp.swapaxes(a, axis, axis + 1).reshape(shp)


def _tile_tables(counts, n_tiles_max):
    b_np = np.arange(N_BUCKETS)
    g_np = b_np // N_PAIRS
    in_group_before = jnp.asarray((g_np[None, :] == g_np[:, None]) & (b_np[None, :] < b_np[:, None]))
    member = jnp.asarray(g_np[None, :] == np.arange(N_GROUPS)[:, None])
    earlier_group = jnp.asarray(np.arange(N_GROUPS)[None, :] < np.arange(N_GROUPS)[:, None])
    e_lo = jnp.asarray(g_np * EXPERTS_PER_GROUP + np.asarray(PAIRS)[b_np % N_PAIRS, 0], jnp.int32)
    e_hi = jnp.asarray(g_np * EXPERTS_PER_GROUP + np.asarray(PAIRS)[b_np % N_PAIRS, 1], jnp.int32)

    n_tile_b = jnp.floor((counts + (TMM - 1.0)) * (1.0 / TMM))
    g_tiles = jnp.sum(jnp.where(member, n_tile_b[None, :], 0.0), axis=1)
    g_padded = jnp.floor((g_tiles + (NSUB - 1.0)) * (1.0 / NSUB)) * NSUB
    g_start = jnp.sum(jnp.where(earlier_group, g_padded[None, :], 0.0), axis=1)
    g_end = g_start + g_padded
    b_start = (jnp.sum(jnp.where(member.T, g_start[None, :], 0.0), axis=1)
               + jnp.sum(jnp.where(in_group_before, n_tile_b[None, :], 0.0), axis=1))
    b_end = b_start + n_tile_b
    starts = (b_start * TMM).astype(jnp.int32)
    zrow = jnp.where(n_tile_b > 0, (b_end - 1.0) * TMM, -1.0).astype(jnp.int32)

    t = jnp.arange(n_tiles_max, dtype=jnp.int32).astype(F32)
    g_of_t = jnp.minimum(jnp.sum(jnp.where(g_end[None, :] <= t[:, None], 1, 0), axis=1), N_GROUPS - 1)
    g_hot = g_of_t[:, None] == jnp.arange(N_GROUPS, dtype=jnp.int32)[None, :]
    local_t = t - jnp.sum(jnp.where(g_hot, g_start[None, :], 0.0), axis=1)
    valid = (local_t < jnp.sum(jnp.where(g_hot, g_tiles[None, :], 0.0), axis=1)) & (t < jnp.sum(g_padded))
    bucket = jnp.sum(jnp.where(b_end[None, :] <= t[:, None], 1, 0), axis=1)
    bucket = jnp.where(valid, bucket, g_of_t * N_PAIRS + (N_PAIRS - 1))
    b_hot = bucket[:, None] == jnp.asarray(b_np, jnp.int32)[None, :]
    tlo = jnp.sum(jnp.where(b_hot, e_lo[None, :], 0), axis=1)
    thi = jnp.sum(jnp.where(b_hot, e_hi[None, :], 0), axis=1)
    pad = jnp.zeros((LANES - N_BUCKETS,), jnp.int32)
    return (jnp.concatenate([starts, pad]), jnp.concatenate([zrow, pad - 1]), tlo, thi,
            jnp.where(valid, 1, 0).astype(jnp.int32))


def kernel(x, c, w_ada, b_ada, w_in, b_in, conv_w, conv_b, conv_ln_g, conv_ln_b, conv_out_g, sinks,
           attn_out_g, w_out, b_out, ln1_g, ln1_b, w_router_group, b_router_group, w_router_expert,
           b_router_expert, w_gate, w_up, w_down, ln2_g, ln2_b):
    assert w_ada.shape[0] == DEPTH
    bsz, seq, _ = x.shape
    row = lambda v: v[0][None, :]

    mod = _modulation(c, w_ada[0], row(b_ada)).reshape(bsz, 6, 1, D_MODEL)
    sh1, sc1, g1, sh2, sc2, g2 = [mod[:, j:j + 1] for j in range(6)]

    q0 = 2 * D_CONV
    w_q = _pair_layout(w_in[0][:, q0:q0 + D_ATTN], 1)
    b_q = _pair_layout(b_in[0][q0:q0 + D_ATTN], 0)[None, :]
    og_attn = _pair_layout(attn_out_g[0], 0)[None, :]
    w_o = jnp.concatenate([w_out[0][:D_CONV], _pair_layout(w_out[0][D_CONV:], 0)], axis=0)

    y_conv, q, kv = _inproj_conv(x, sh1, sc1, w_in[0], row(b_in), w_q, b_q, conv_w[0], row(conv_b),
                                 row(conv_ln_g), row(conv_ln_b), row(conv_out_g))
    y_attn = _attention(q, kv, sinks[0], og_attn)

    pad = ROUTER_COLS - N_EXPERTS - N_GROUPS
    w_r = jnp.concatenate([w_router_expert[0], w_router_group[0], jnp.zeros((D_MODEL, pad), F32)], axis=1)
    b_r = jnp.concatenate([b_router_expert[0], b_router_group[0], jnp.zeros((pad,), F32)])[None, :]
    x1, h2t, code, counts = _outproj_route(x, y_conv, y_attn, w_o, row(b_out), g1, row(ln1_g), row(ln1_b),
                                      sc2, sh2, w_r, b_r)

    n_tiles_max = (bsz * seq) // TMM + N_BUCKETS + N_GROUPS * (NSUB - 1)
    n_tiles_max = -(-n_tiles_max // NSUB) * NSUB
    starts, zrow, tlo, thi, valid = _tile_tables(counts[0, :N_BUCKETS], n_tiles_max)
    h_sorted, pos = _scatter_rows(starts, zrow, valid, code[:, :, 0], h2t, n_tiles_max * TMM)
    y_sorted = _moe_sorted(tlo, thi, valid, h_sorted, w_gate[0], w_up[0], w_down[0], w_r, b_r)
    return _unsort_ln2(pos, x1, g2, row(ln2_g), row(ln2_b), y_sorted)
```

```python
import jax
import jax.numpy as jnp
import numpy as np
from jax import lax
from jax.experimental import pallas as pl
from jax.experimental.pallas import tpu as pltpu

F32 = jnp.float32
BF16 = jnp.bfloat16

D_MODEL = 1024
D_CONV = 512
CONV_WIDTH = 31
GROUP_DIM = 64
N_HEADS = 8
N_KV_HEADS = 2
HEAD_DIM = 64
D_ATTN = N_HEADS * HEAD_DIM
D_KV = N_KV_HEADS * HEAD_DIM
BLOCK = 128
D_IN = 2 * D_CONV + D_ATTN + 2 * D_KV
N_GROUPS = 4
EXPERTS_PER_GROUP = 8
N_EXPERTS = N_GROUPS * EXPERTS_PER_GROUP
D_EXPERT = D_MODEL // 4
DEPTH = 1
ALPHA = (2.0 * DEPTH) ** 0.25
EPS = 1e-5
NEG = -1e30

LANES = 128
SUBLANES = 8
TS = 512
TD = 1024
HALO = 32
CR = 32
AB = 16
PAIR_SLOTS = [j + (N_HEADS // N_KV_HEADS) * g for j in range(N_HEADS // N_KV_HEADS) for g in range(N_KV_HEADS)]
ROUTER_COLS = LANES
VMEM_LIMIT = 56 * 1024 * 1024

PAIRS = [(lo, hi) for lo in range(EXPERTS_PER_GROUP) for hi in range(lo + 1, EXPERTS_PER_GROUP)]
N_PAIRS = len(PAIRS)
N_BUCKETS = N_GROUPS * N_PAIRS
TMM = 128
OUT_SLICES = 4
NSUB = 4
CODE_SHIFT = 16


def _dot(a, b):
    return jnp.dot(a, b, preferred_element_type=F32)


def _dot_nt(a, b):
    return lax.dot_general(a, b, (((1,), (1,)), ((), ())), preferred_element_type=F32)


def _fill_group_maps(bsel_ref, bexp_ref):
    c = bsel_ref.shape[0]
    ch = lax.broadcasted_iota(jnp.int32, (c, LANES), 0) // GROUP_DIM
    gi = lax.broadcasted_iota(jnp.int32, (c, LANES), 1)
    bsel_ref[...] = jnp.where(ch == gi, 1.0 / GROUP_DIM, 0.0).astype(BF16)
    gi2 = lax.broadcasted_iota(jnp.int32, (LANES, c), 0)
    ch2 = lax.broadcasted_iota(jnp.int32, (LANES, c), 1) // GROUP_DIM
    bexp_ref[...] = jnp.where(ch2 == gi2, 1.0, 0.0).astype(BF16)


def _group_rms(y, bsel, bexp):
    ms = _dot((y * y).astype(BF16), bsel)
    r = lax.rsqrt(ms + EPS)
    r_hi = r.astype(BF16)
    r_lo = (r - r_hi.astype(F32)).astype(BF16)
    return y * (_dot(r_hi, bexp) + _dot(r_lo, bexp))


def _layer_norm(y, g, b):
    mu = jnp.mean(y, axis=-1, keepdims=True)
    d = y - mu
    var = jnp.mean(d * d, axis=-1, keepdims=True)
    return d * lax.rsqrt(var + EPS) * g + b


def _mod_kernel(c_ref, w_ref, b_ref, o_ref):
    c = c_ref[...]
    c_act = c * jax.nn.sigmoid(c)
    o_ref[...] = _dot(c_act.astype(BF16), w_ref[...].astype(BF16)) + b_ref[...]


def _modulation(c, w_ada, b_ada):
    bsz = c.shape[0]
    n = w_ada.shape[1]
    return pl.pallas_call(
        _mod_kernel,
        grid=(n // D_MODEL,),
        in_specs=[pl.BlockSpec((bsz, D_MODEL), lambda j: (0, 0)),
                  pl.BlockSpec((D_MODEL, D_MODEL), lambda j: (0, j)),
                  pl.BlockSpec((1, D_MODEL), lambda j: (0, j))],
        out_specs=pl.BlockSpec((bsz, D_MODEL), lambda j: (0, j)),
        out_shape=jax.ShapeDtypeStruct((bsz, n), F32),
        name="adaln_mod",
    )(c, w_ada, b_ada)


def _inproj_conv_kernel(x_ref, sh_ref, sc_ref, w_ref, b_ref, wq_ref, bq_ref, cw_ref, cb_ref, lg_ref, lb_ref, og_ref,
                        yc_ref, q_ref, kv_ref, wbf, gs, cacc, wb, bsel, bexp):
    first = (pl.program_id(0) == 0) & (pl.program_id(1) == 0)
    s = pl.program_id(1)

    @pl.when(first)
    def _():
        wbf[...] = w_ref[...].astype(BF16)
        wbf[:, 2 * D_CONV:2 * D_CONV + D_ATTN] = wq_ref[...].astype(BF16)
        wb[...] = jnp.broadcast_to(cw_ref[...][:, None, :], wb.shape)
        _fill_group_maps(bsel, bexp)

    n_ct = D_CONV // LANES

    @pl.when(s == 0)
    def _():
        gs[0, :, 0:HALO, :] = jnp.zeros((n_ct, HALO, LANES), F32)

    @pl.when(s > 0)
    def _():
        gs[0, :, 0:HALO, :] = gs[0, :, TS:TS + HALO, :]

    h = (x_ref[...] * (1.0 + sc_ref[...]) + sh_ref[...]).astype(BF16)
    q0 = 2 * D_CONV
    k0 = q0 + D_ATTN
    cb = cb_ref[...]
    lg = lg_ref[...]
    lb = lb_ref[...]
    og = og_ref[...]
    n_sh = TS + HALO - SUBLANES
    half = TS // 2
    late_cols = 2 * LANES

    def glu_unit(rows, j):
        cs = slice(j * late_cols, (j + 1) * late_cols)
        u_a = _dot(h[rows], wbf[:, cs]) + b_ref[:, cs]
        u_b = _dot(h[rows], wbf[:, D_CONV + j * late_cols:D_CONV + (j + 1) * late_cols]) \
            + b_ref[:, D_CONV + j * late_cols:D_CONV + (j + 1) * late_cols]
        glu = u_a * jax.nn.sigmoid(u_b)
        for cc in range(late_cols // LANES):
            c = j * (late_cols // LANES) + cc
            gs[0, c, HALO + rows.start:HALO + rows.stop, :] = glu[:, cc * LANES:(cc + 1) * LANES]

    def late_proj(j):
        c0 = j * late_cols
        if c0 < D_ATTN:
            q = _dot(h, wbf[:, q0 + c0:q0 + c0 + late_cols]) + bq_ref[:, c0:c0 + late_cols]
            q_ref[:, c0:c0 + late_cols] = (q * (HEAD_DIM ** -0.5)).astype(BF16)
        else:
            kv_ref[...] = (_dot(h, wbf[:, k0:k0 + 2 * D_KV]) + b_ref[:, k0:k0 + 2 * D_KV]).astype(BF16)

    def shifted(lo, hi):
        for j in range(1, SUBLANES):
            gs[j, :, lo:hi, :] = gs[0, :, lo + j:hi + j, :]

    def conv_chunk(c, r0):
        ls = slice(c * LANES, (c + 1) * LANES)
        acc = jnp.broadcast_to(cb[:, ls], (CR, LANES)).reshape(CR // SUBLANES, SUBLANES, LANES)
        for k in range(CONV_WIDTH):
            off = HALO - (CONV_WIDTH - 1) + k
            a0 = r0 + (off // SUBLANES) * SUBLANES
            seg = gs[off % SUBLANES, c, a0:a0 + CR, :]
            acc = acc + wb[k, :, ls][None] * seg.reshape(CR // SUBLANES, SUBLANES, LANES)
        cacc[c, r0:r0 + CR, :] = acc.reshape(CR, LANES)

    first, second = slice(0, half), slice(half, TS)
    n_glu = D_CONV // late_cols
    n_late = (D_ATTN + 2 * D_KV) // late_cols
    for j in range(n_glu):
        glu_unit(first, j)
    split = HALO - SUBLANES + half
    shifted(0, split)
    chunks_a = [(c, r0) for c in range(n_ct) for r0 in range(0, half, CR)]
    chunks_b = [(c, r0) for c in range(n_ct) for r0 in range(half, TS, CR)]
    units_a = [lambda j=j: glu_unit(second, j) for j in range(n_glu)] + [lambda: late_proj(0)]
    units_b = [lambda j=j: late_proj(j) for j in range(1, n_late)]
    for chunks, units in ((chunks_a, units_a), (chunks_b, units_b)):
        if chunks is chunks_b:
            shifted(split, n_sh)
        every = len(chunks) // len(units)
        for n, (c, r0) in enumerate(chunks):
            conv_chunk(c, r0)
            if n % every == 0 and n // every < len(units):
                units[n // every]()
    y = _layer_norm(jnp.concatenate([cacc[c] for c in range(n_ct)], axis=1), lg, lb)
    y = y * jax.nn.sigmoid(y)
    yc_ref[...] = (_group_rms(y, bsel[...], bexp[...]) * og).astype(BF16)


def _inproj_conv(x, sh1, sc1, w_in, b_in, w_q, b_q, conv_w, conv_b, ln_g, ln_b, out_g):
    bsz, seq, _ = x.shape
    vec = lambda n: pl.BlockSpec((1, n), lambda b, s: (0, 0))
    mod = pl.BlockSpec((None, None, 1, D_MODEL), lambda b, s: (b, 0, 0, 0))
    tile = lambda n: pl.BlockSpec((None, TS, n), lambda b, s: (b, s, 0))
    return pl.pallas_call(
        _inproj_conv_kernel,
        grid=(bsz, seq // TS),
        in_specs=[tile(D_MODEL), mod, mod,
                  pl.BlockSpec((D_MODEL, D_IN), lambda b, s: (0, 0)), vec(D_IN),
                  pl.BlockSpec((D_MODEL, D_ATTN), lambda b, s: (0, 0)), vec(D_ATTN),
                  pl.BlockSpec((CONV_WIDTH, D_CONV), lambda b, s: (0, 0)),
                  vec(D_CONV), vec(D_CONV), vec(D_CONV), vec(D_CONV)],
        out_specs=[tile(D_CONV), tile(D_ATTN), tile(2 * D_KV)],
        out_shape=[jax.ShapeDtypeStruct((bsz, seq, D_CONV), BF16),
                   jax.ShapeDtypeStruct((bsz, seq, D_ATTN), BF16),
                   jax.ShapeDtypeStruct((bsz, seq, 2 * D_KV), BF16)],
        scratch_shapes=[pltpu.VMEM((D_MODEL, D_IN), BF16),
                        pltpu.VMEM((SUBLANES, D_CONV // LANES, TS + HALO, LANES), F32),
                        pltpu.VMEM((D_CONV // LANES, TS, LANES), F32),
                        pltpu.VMEM((CONV_WIDTH, SUBLANES, D_CONV), F32),
                        pltpu.VMEM((D_CONV, LANES), BF16),
                        pltpu.VMEM((LANES, D_CONV), BF16)],
        compiler_params=pltpu.CompilerParams(
            dimension_semantics=("arbitrary", "arbitrary"), vmem_limit_bytes=VMEM_LIMIT),
        name="inproj_conv",
    )(x, sh1, sc1, w_in, b_in, w_q, b_q, conv_w, conv_b, ln_g, ln_b, out_g)


def _attn_kernel(sinks_ref, q_ref, kvc_ref, kvp_ref, og_ref, o_ref, bsel, bexp):
    n = pl.program_id(1)

    @pl.when((pl.program_id(0) == 0) & (n == 0))
    def _():
        _fill_group_maps(bsel, bexp)

    qr = lax.broadcasted_iota(jnp.int32, (BLOCK, BLOCK), 0)
    kc = lax.broadcasted_iota(jnp.int32, (BLOCK, BLOCK), 1)
    from_prev = kc > qr
    lane = lax.broadcasted_iota(jnp.int32, (BLOCK, LANES), 1)
    lower = lane < HEAD_DIM
    lower_bf = jnp.where(lower, 1.0, 0.0).astype(BF16)
    upper_bf = jnp.where(lower, 0.0, 1.0).astype(BF16)
    og = og_ref[...]
    blocks = range(AB)
    kvs = []
    for blk in blocks:
        r0 = blk * BLOCK
        if blk == 0:
            kvs.append(jnp.concatenate([kvp_ref[...], kvc_ref[0:BLOCK, :]], axis=0))
        else:
            kvs.append(kvc_ref[r0 - BLOCK:r0 + BLOCK, :])
    s_all = []
    for blk in blocks:
        r0 = blk * BLOCK
        parts = []
        for j in range(D_ATTN // LANES):
            col = q_ref[r0:r0 + BLOCK, j * LANES:(j + 1) * LANES]
            parts += [col * lower_bf, col * upper_bf]
        s_all.append(_dot_nt(jnp.concatenate(parts, axis=0), kvs[blk][:, 0:D_KV]))
    probs, dens = [], []
    for blk in blocks:
        pb = []
        den = jnp.ones((BLOCK, LANES), F32)
        for slot in range(N_HEADS):
            s_prev = s_all[blk][slot * BLOCK:(slot + 1) * BLOCK, 0:BLOCK]
            s_own = s_all[blk][slot * BLOCK:(slot + 1) * BLOCK, BLOCK:2 * BLOCK]
            if blk == 0:
                s_prev = jnp.where(n > 0, s_prev, NEG)
            sc = jnp.where(from_prev, s_prev, s_own)
            sink = sinks_ref[PAIR_SLOTS[slot]]
            m = jnp.maximum(jnp.max(sc, axis=-1, keepdims=True), sink)
            p = jnp.exp(sc - m)
            den = jnp.where(lane == slot, jnp.sum(p, axis=-1, keepdims=True) + jnp.exp(sink - m), den)
            p = p.astype(BF16)
            zero = jnp.zeros_like(p)
            pb.append(jnp.concatenate([jnp.where(from_prev, p, zero), jnp.where(from_prev, zero, p)], axis=1))
        probs.append(jnp.concatenate(pb, axis=0))
        dens.append(den)
    pvs = [_dot(probs[blk], kvs[blk][:, D_KV:2 * D_KV]) for blk in blocks]
    for blk in blocks:
        pv, den = pvs[blk], dens[blk]
        o = jnp.concatenate(
            [jnp.where(lower, pv[(2 * j) * BLOCK:(2 * j + 1) * BLOCK], pv[(2 * j + 1) * BLOCK:(2 * j + 2) * BLOCK])
             for j in range(D_ATTN // LANES)], axis=1)
        ms = _dot((o * o).astype(BF16), bsel[...])
        r = lax.rsqrt(ms + EPS * den * den)
        r_hi = r.astype(BF16)
        r_lo = (r - r_hi.astype(F32)).astype(BF16)
        scale = _dot(r_hi, bexp[...]) + _dot(r_lo, bexp[...])
        o_ref[blk * BLOCK:(blk + 1) * BLOCK, :] = (o * scale * og).astype(BF16)


def _attention(q, kv, sinks, out_g):
    bsz, seq, _ = q.shape
    rows = AB * BLOCK
    grid_spec = pltpu.PrefetchScalarGridSpec(
        num_scalar_prefetch=1,
        grid=(bsz, seq // rows),
        in_specs=[pl.BlockSpec((None, rows, D_ATTN), lambda b, n, sk: (b, n, 0)),
                  pl.BlockSpec((None, rows, 2 * D_KV), lambda b, n, sk: (b, n, 0)),
                  pl.BlockSpec((None, BLOCK, 2 * D_KV), lambda b, n, sk: (b, jnp.maximum(AB * n - 1, 0), 0)),
                  pl.BlockSpec((1, D_ATTN), lambda b, n, sk: (0, 0))],
        out_specs=pl.BlockSpec((None, rows, D_ATTN), lambda b, n, sk: (b, n, 0)),
        scratch_shapes=[pltpu.VMEM((D_ATTN, LANES), BF16), pltpu.VMEM((LANES, D_ATTN), BF16)],
    )
    return pl.pallas_call(
        _attn_kernel,
        grid_spec=grid_spec,
        out_shape=jax.ShapeDtypeStruct((bsz, seq, D_ATTN), BF16),
        compiler_params=pltpu.CompilerParams(dimension_semantics=("arbitrary", "arbitrary")),
        name="swa_attention",
    )(sinks, q, kv, kv, out_g)


def _group_softmax_top(logits, lane):
    is_g = (lane >= N_EXPERTS) & (lane < N_EXPERTS + N_GROUPS)
    gl = jnp.where(is_g, logits, NEG)
    gmax = jnp.max(gl, axis=-1, keepdims=True)
    p_top = 1.0 / jnp.sum(jnp.where(is_g, jnp.exp(gl - gmax), 0.0), axis=-1, keepdims=True)
    return is_g, gl, gmax, p_top


def _route_bucket(logits):
    rows = logits.shape[0]
    lane = lax.broadcasted_iota(jnp.int32, (rows, ROUTER_COLS), 1)
    lanef = lane.astype(F32)
    is_g, gl, gmax, _ = _group_softmax_top(logits, lane)
    gidx = jnp.min(jnp.where(is_g & (gl == gmax), lanef - N_EXPERTS, 99.0), axis=-1, keepdims=True)
    in_grp = (lane < N_EXPERTS) & ((lane // EXPERTS_PER_GROUP) == gidx.astype(jnp.int32))
    el = jnp.where(in_grp, logits, NEG)
    m1 = jnp.max(el, axis=-1, keepdims=True)
    i1 = jnp.min(jnp.where(in_grp & (el == m1), lanef, 999.0), axis=-1, keepdims=True)
    rest = in_grp & (lanef != i1)
    el2 = jnp.where(rest, logits, NEG)
    m2 = jnp.max(el2, axis=-1, keepdims=True)
    i2 = jnp.min(jnp.where(rest & (el2 == m2), lanef, 999.0), axis=-1, keepdims=True)
    lo = jnp.minimum(i1, i2) - EXPERTS_PER_GROUP * gidx
    hi = jnp.maximum(i1, i2) - EXPERTS_PER_GROUP * gidx
    pair = lo * (EXPERTS_PER_GROUP - 1) - lo * (lo - 1.0) * 0.5 + (hi - lo - 1.0)
    return gidx * N_PAIRS + pair, lanef


def _outproj_kernel(x_ref, yc_ref, ya_ref, w_ref, b_ref, g1_ref, lg_ref, lb_ref, sc2_ref, sh2_ref,
                    wr_ref, br_ref, x1_ref, h2t_ref, code_ref, counts_ref, wbf, wrbf, ltri, running):
    @pl.when((pl.program_id(0) == 0) & (pl.program_id(1) == 0))
    def _():
        wbf[...] = w_ref[...].astype(BF16)
        wrbf[...] = wr_ref[...].astype(BF16)
        r = lax.broadcasted_iota(jnp.int32, (TS, TS), 0)
        c = lax.broadcasted_iota(jnp.int32, (TS, TS), 1)
        ltri[...] = jnp.where(c < r, 1.0, 0.0).astype(BF16)
        running[...] = jnp.zeros_like(running)

    nsl = OUT_SLICES
    rows = TS // nsl
    sl = [slice(k * rows, (k + 1) * rows) for k in range(nsl)]
    mix = [_dot(yc_ref[r, :], wbf[0:D_CONV, :]) + _dot(ya_ref[r, :], wbf[D_CONV:, :]) + b_ref[...] for r in sl]
    x1 = [_layer_norm(ALPHA * x_ref[r, :] + g1_ref[...] * m, lg_ref[...], lb_ref[...]) for r, m in zip(sl, mix)]
    h2 = [v * (1.0 + sc2_ref[...]) + sh2_ref[...] for v in x1]
    for k in range(nsl):
        x1_ref[sl[k], :] = x1[k]
        _to_tiles(h2t_ref.at[pl.ds(k * rows * CHUNKS, rows * CHUNKS)], h2[k])
    logits = [_dot(v.astype(BF16), wrbf[...]) + br_ref[...] for v in h2]
    routed = [_route_bucket(lg) for lg in logits]
    mine = [lanef == bucket for bucket, lanef in routed]
    onehot = [jnp.where(mk, 1.0, 0.0) for mk in mine]
    within = [_dot(ltri[0:rows, 0:rows], oh.astype(BF16)) for oh in onehot]
    base = running[...]
    for k in range(nsl):
        rank = jnp.sum(jnp.where(mine[k], within[k] + base, 0.0), axis=-1, keepdims=True)
        base = base + jnp.sum(onehot[k], axis=0, keepdims=True)
        code = routed[k][0].astype(jnp.int32) * (1 << CODE_SHIFT) + rank.astype(jnp.int32)
        code_ref[sl[k], :] = jnp.broadcast_to(code, (rows, SUBLANES))
    running[...] = base
    counts_ref[...] = running[...]


def _outproj_route(x, y_conv, y_attn, w_out, b_out, g1, ln_g, ln_b, sc2, sh2, w_r, b_r):
    bsz, seq, _ = x.shape
    nst = seq // TS
    vec = lambda n: pl.BlockSpec((1, n), lambda b, s: (0, 0))
    mod = pl.BlockSpec((None, None, 1, D_MODEL), lambda b, s: (b, 0, 0, 0))
    tile = lambda n: pl.BlockSpec((None, TS, n), lambda b, s: (b, s, 0))
    return pl.pallas_call(
        _outproj_kernel,
        grid=(bsz, seq // TS),
        in_specs=[tile(D_MODEL), tile(D_CONV), tile(D_ATTN),
                  pl.BlockSpec((D_MODEL, D_MODEL), lambda b, s: (0, 0)), vec(D_MODEL),
                  mod, vec(D_MODEL), vec(D_MODEL), mod, mod,
                  pl.BlockSpec((D_MODEL, ROUTER_COLS), lambda b, s: (0, 0)), vec(ROUTER_COLS)],
        out_specs=[tile(D_MODEL), pl.BlockSpec((TS * CHUNKS, LANES), lambda b, s: (b * nst + s, 0)),
                   tile(SUBLANES), vec(LANES)],
        out_shape=[jax.ShapeDtypeStruct((bsz, seq, D_MODEL), F32),
                   jax.ShapeDtypeStruct((bsz * seq * CHUNKS, LANES), F32),
                   jax.ShapeDtypeStruct((bsz, seq, SUBLANES), jnp.int32),
                   jax.ShapeDtypeStruct((1, LANES), F32)],
        scratch_shapes=[pltpu.VMEM((D_MODEL, D_MODEL), BF16), pltpu.VMEM((D_MODEL, ROUTER_COLS), BF16),
                        pltpu.VMEM((TS, TS), BF16), pltpu.VMEM((1, LANES), F32)],
        compiler_params=pltpu.CompilerParams(
            dimension_semantics=("arbitrary", "arbitrary"), vmem_limit_bytes=VMEM_LIMIT),
        name="outproj_route",
    )(x, y_conv, y_attn, w_out, b_out, g1, ln_g, ln_b, sc2, sh2, w_r, b_r)


def _row_of(starts_ref, code):
    return (starts_ref[lax.shift_right_logical(code, jnp.int32(CODE_SHIFT))]
            + (code & ((1 << CODE_SHIFT) - 1)))


CHUNKS = D_MODEL // LANES
assert CHUNKS == SUBLANES


def _tok_rows(p):
    return pl.ds(pl.multiple_of(p * CHUNKS, CHUNKS), CHUNKS)


def _to_tiles(ref, x):
    for c in range(CHUNKS):
        ref[pl.ds(c, x.shape[0], stride=CHUNKS), :] = x[:, c * LANES:(c + 1) * LANES]


def _from_tiles(ref, n):
    return jnp.concatenate([ref[pl.ds(c, n, stride=CHUNKS), :] for c in range(CHUNKS)], axis=1)


def _wait_tokens(hbm_ref, n, sem):
    pltpu.make_async_copy(hbm_ref.at[pl.ds(0, n * CHUNKS)], hbm_ref.at[pl.ds(0, n * CHUNKS)], sem).wait()


def _scatter_kernel(starts_ref, zrow_ref, valid_ref, code_ref, h2t_ref, hs_ref, pos_ref, hbuf, zeros, sems, zsem):
    nsteps = pl.num_programs(0) * pl.num_programs(1)
    step = pl.program_id(0) * pl.num_programs(1) + pl.program_id(1)
    slot = step % 2

    def _tile_rows(tok):
        return pl.ds(pl.multiple_of(tok * CHUNKS, TMM * CHUNKS), TMM * CHUNKS)

    def zero_copy(b):
        return pltpu.make_async_copy(zeros, hs_ref.at[_tile_rows(zrow_ref[b])], zsem)

    def tail_copy(t):
        return pltpu.make_async_copy(zeros, hs_ref.at[_tile_rows(t * TMM)], zsem)

    @pl.when(step == 0)
    def _():
        zeros[...] = jnp.zeros_like(zeros)
        n_tiles = hs_ref.shape[0] // (TMM * CHUNKS)

        def start(b, c):
            @pl.when(zrow_ref[b] >= 0)
            def _():
                zero_copy(b).start()
            return c

        def wait(b, c):
            @pl.when(zrow_ref[b] >= 0)
            def _():
                zero_copy(b).wait()
            return c

        def start_tail(t, c):
            @pl.when(valid_ref[t] == 0)
            def _():
                tail_copy(t).start()
            return c

        def wait_tail(t, c):
            @pl.when(valid_ref[t] == 0)
            def _():
                tail_copy(t).wait()
            return c

        lax.fori_loop(0, N_BUCKETS, start, 0)
        lax.fori_loop(0, n_tiles, start_tail, 0)
        lax.fori_loop(0, N_BUCKETS, wait, 0)
        lax.fori_loop(0, n_tiles, wait_tail, 0)

    @pl.when(step >= 2)
    def _():
        _wait_tokens(hs_ref, TD, sems.at[slot])

    hbuf[slot] = h2t_ref[...]

    for r in range(TD):
        pos = _row_of(starts_ref, code_ref[0, r])
        pos_ref[0, r] = pos
        pltpu.make_async_copy(hbuf.at[slot, _tok_rows(r)], hs_ref.at[_tok_rows(pos)],
                              sems.at[slot]).start(priority=r % 2)

    @pl.when(step == nsteps - 1)
    def _():
        _wait_tokens(hs_ref, TD, sems.at[slot])
        _wait_tokens(hs_ref, TD, sems.at[1 - slot])


def _scatter_rows(starts, zrow, valid, code, h2t, n_rows):
    bsz, seq = code.shape
    nst = seq // TD
    grid_spec = pltpu.PrefetchScalarGridSpec(
        num_scalar_prefetch=3,
        grid=(bsz, nst),
        in_specs=[pl.BlockSpec((None, 1, TD), lambda b, s, *_: (b * nst + s, 0, 0), memory_space=pltpu.SMEM),
                  pl.BlockSpec((TD * CHUNKS, LANES), lambda b, s, *_: (b * nst + s, 0))],
        out_specs=[pl.BlockSpec(memory_space=pl.ANY),
                   pl.BlockSpec((None, 1, TD), lambda b, s, *_: (b * nst + s, 0, 0), memory_space=pltpu.SMEM)],
        scratch_shapes=[pltpu.VMEM((2, TD * CHUNKS, LANES), F32), pltpu.VMEM((TMM * CHUNKS, LANES), F32),
                        pltpu.SemaphoreType.DMA((2,)), pltpu.SemaphoreType.DMA],
    )
    return pl.pallas_call(
        _scatter_kernel,
        grid_spec=grid_spec,
        out_shape=[jax.ShapeDtypeStruct((n_rows * CHUNKS, LANES), F32),
                   jax.ShapeDtypeStruct((bsz * nst, 1, TD), jnp.int32)],
        compiler_params=pltpu.CompilerParams(dimension_semantics=("arbitrary", "arbitrary")),
        name="moe_scatter",
    )(starts, zrow, valid, code.reshape(bsz * nst, 1, TD), h2t)


def _moe_kernel(tlo_ref, thi_ref, valid_ref, h_ref, wg_hbm, wu_hbm, wd_hbm, wr_ref, br_ref, y_ref,
                wrbf, wg, wu, wd, sg, su, sd, sems):
    i = pl.program_id(0)
    t0 = i * NSUB
    group = lax.shift_right_logical(tlo_ref[t0], jnp.int32(3))
    prev_group = lax.shift_right_logical(tlo_ref[jnp.maximum(t0 - NSUB, 0)], jnp.int32(3))

    @pl.when(i == 0)
    def _():
        wrbf[...] = wr_ref[...].astype(BF16)

    @pl.when((i == 0) | (group != prev_group))
    def _():
        def copies(e, slot):
            ex = group * EXPERTS_PER_GROUP + e
            return (pltpu.make_async_copy(wg_hbm.at[ex], sg.at[slot], sems.at[slot, 0]),
                    pltpu.make_async_copy(wu_hbm.at[ex], su.at[slot], sems.at[slot, 1]),
                    pltpu.make_async_copy(wd_hbm.at[ex], sd.at[slot], sems.at[slot, 2]))

        for cp in copies(0, 0):
            cp.start()

        def land(e, c):
            slot = e % 2

            @pl.when(e + 1 < EXPERTS_PER_GROUP)
            def _():
                for cp in copies(e + 1, 1 - slot):
                    cp.start()

            for cp in copies(e, slot):
                cp.wait()
            wg[e] = sg[slot].astype(BF16)
            wu[e] = su[slot].astype(BF16)
            wd[e] = sd[slot].astype(BF16)
            return c

        lax.fori_loop(0, EXPERTS_PER_GROUP, land, 0)

    @pl.when(valid_ref[t0] == 1)
    def _():
        lane = lax.broadcasted_iota(jnp.int32, (TMM, ROUTER_COLS), 1)
        local = EXPERTS_PER_GROUP - 1
        subs = range(NSUB)
        tile = lambda ref, j: ref.at[pl.ds(j * TMM * CHUNKS, TMM * CHUNKS)]
        hs = [_from_tiles(tile(h_ref, j), TMM).astype(BF16) for j in subs]
        es = [(tlo_ref[t0 + j] & local, thi_ref[t0 + j] & local) for j in subs]
        pre = [[(_dot(hs[j], wg[e]), _dot(hs[j], wu[e])) for e in es[j]] for j in subs]
        act = [[(a * jax.nn.sigmoid(a) * u).astype(BF16) for a, u in pre[j]] for j in subs]
        ys = [[_dot(act[j][k], wd[es[j][k]]) for k in (0, 1)] for j in subs]
        for j in subs:
            logits = _dot(hs[j], wrbf[...]) + br_ref[...]
            p_top = _group_softmax_top(logits, lane)[3]
            l_lo = jnp.sum(jnp.where(lane == tlo_ref[t0 + j], logits, 0.0), axis=-1, keepdims=True)
            l_hi = jnp.sum(jnp.where(lane == thi_ref[t0 + j], logits, 0.0), axis=-1, keepdims=True)
            w_lo = p_top / (1.0 + jnp.exp(l_hi - l_lo))
            w_hi = p_top / (1.0 + jnp.exp(l_lo - l_hi))
            _to_tiles(tile(y_ref, j), w_lo * ys[j][0] + w_hi * ys[j][1])

    @pl.when(valid_ref[t0] == 0)
    def _():
        y_ref[...] = jnp.zeros_like(y_ref)


def _moe_sorted(tlo, thi, valid, h_sorted, w_gate, w_up, w_down, w_r, b_r):
    n_rows = h_sorted.shape[0] // CHUNKS
    rows = pl.BlockSpec((NSUB * TMM * CHUNKS, LANES), lambda i, *_: (i, 0))
    hbm = pl.BlockSpec(memory_space=pl.ANY)
    w_in_shape, w_out_shape = (D_MODEL, D_EXPERT), (D_EXPERT, D_MODEL)
    grid_spec = pltpu.PrefetchScalarGridSpec(
        num_scalar_prefetch=3,
        grid=(n_rows // (NSUB * TMM),),
        in_specs=[rows, hbm, hbm, hbm,
                  pl.BlockSpec((D_MODEL, ROUTER_COLS), lambda i, *_: (0, 0)),
                  pl.BlockSpec((1, ROUTER_COLS), lambda i, *_: (0, 0))],
        out_specs=rows,
        scratch_shapes=[pltpu.VMEM((D_MODEL, ROUTER_COLS), BF16),
                        pltpu.VMEM((EXPERTS_PER_GROUP,) + w_in_shape, BF16),
                        pltpu.VMEM((EXPERTS_PER_GROUP,) + w_in_shape, BF16),
                        pltpu.VMEM((EXPERTS_PER_GROUP,) + w_out_shape, BF16),
                        pltpu.VMEM((2,) + w_in_shape, F32), pltpu.VMEM((2,) + w_in_shape, F32),
                        pltpu.VMEM((2,) + w_out_shape, F32), pltpu.SemaphoreType.DMA((2, 3))],
    )
    return pl.pallas_call(
        _moe_kernel,
        grid_spec=grid_spec,
        out_shape=jax.ShapeDtypeStruct((n_rows * CHUNKS, LANES), F32),
        compiler_params=pltpu.CompilerParams(dimension_semantics=("arbitrary",), vmem_limit_bytes=VMEM_LIMIT),
        name="moe_sorted",
    )(tlo, thi, valid, h_sorted, w_gate, w_up, w_down, w_r, b_r)


def _final_kernel(pos_ref, posn_ref, x1_ref, g2_ref, lg_ref, lb_ref, ys_ref, o_ref, ybuf, sems):
    nsteps = pl.num_programs(0) * pl.num_programs(1)
    step = pl.program_id(0) * pl.num_programs(1) + pl.program_id(1)
    slot = step % 2

    def gather(rows_ref, sl):
        def issue(i, c):
            for u in range(SUBLANES):
                r = i * SUBLANES + u
                pltpu.make_async_copy(ys_ref.at[_tok_rows(rows_ref[0, r])], ybuf.at[sl, _tok_rows(r)],
                                      sems.at[sl]).start(priority=u % 2)
            return c

        lax.fori_loop(0, TS // SUBLANES, issue, 0)

    @pl.when(step == 0)
    def _():
        gather(pos_ref, slot)

    @pl.when(step + 1 < nsteps)
    def _():
        gather(posn_ref, 1 - slot)

    _wait_tokens(ys_ref, TS, sems.at[slot])
    y = _from_tiles(ybuf.at[slot], TS)
    o_ref[...] = _layer_norm(ALPHA * x1_ref[...] + g2_ref[...] * y, lg_ref[...], lb_ref[...])


def _unsort_ln2(pos, x1, g2, ln_g, ln_b, y_sorted):
    bsz, seq, _ = x1.shape
    nst = seq // TS
    last = bsz * nst - 1
    return pl.pallas_call(
        _final_kernel,
        grid=(bsz, nst),
        in_specs=[pl.BlockSpec((None, 1, TS), lambda b, s: (b * nst + s, 0, 0), memory_space=pltpu.SMEM),
                  pl.BlockSpec((None, 1, TS), lambda b, s: (jnp.minimum(b * nst + s + 1, last), 0, 0),
                               memory_space=pltpu.SMEM),
                  pl.BlockSpec((None, TS, D_MODEL), lambda b, s: (b, s, 0)),
                  pl.BlockSpec((None, None, 1, D_MODEL), lambda b, s: (b, 0, 0, 0)),
                  pl.BlockSpec((1, D_MODEL), lambda b, s: (0, 0)),
                  pl.BlockSpec((1, D_MODEL), lambda b, s: (0, 0)),
                  pl.BlockSpec(memory_space=pl.ANY)],
        out_specs=pl.BlockSpec((None, TS, D_MODEL), lambda b, s: (b, s, 0)),
        out_shape=jax.ShapeDtypeStruct((bsz, seq, D_MODEL), F32),
        scratch_shapes=[pltpu.VMEM((2, TS * CHUNKS, LANES), F32), pltpu.SemaphoreType.DMA((2,))],
        compiler_params=pltpu.CompilerParams(dimension_semantics=("arbitrary", "arbitrary")),
        name="unsort_ln2",
    )(pos.reshape(-1, 1, TS), pos.reshape(-1, 1, TS), x1, g2, ln_g, ln_b, y_sorted)


def _pair_layout(a, axis):
    shp = a.shape
    a = a.reshape(shp[:axis] + (N_KV_HEADS, N_HEADS // N_KV_HEADS, HEAD_DIM) + shp[axis + 1:])
    return jnp.swapaxes(a, axis, axis + 1).reshape(shp)


def _tile_tables(counts, n_tiles_max):
    b_np = np.arange(N_BUCKETS)
    g_np = b_np // N_PAIRS
    in_group_before = jnp.asarray((g_np[None, :] == g_np[:, None]) & (b_np[None, :] < b_np[:, None]))
    member = jnp.asarray(g_np[None, :] == np.arange(N_GROUPS)[:, None])
    earlier_group = jnp.asarray(np.arange(N_GROUPS)[None, :] < np.arange(N_GROUPS)[:, None])
    e_lo = jnp.asarray(g_np * EXPERTS_PER_GROUP + np.asarray(PAIRS)[b_np % N_PAIRS, 0], jnp.int32)
    e_hi = jnp.asarray(g_np * EXPERTS_PER_GROUP + np.asarray(PAIRS)[b_np % N_PAIRS, 1], jnp.int32)

    n_tile_b = jnp.floor((counts + (TMM - 1.0)) * (1.0 / TMM))
    g_tiles = jnp.sum(jnp.where(member, n_tile_b[None, :], 0.0), axis=1)
    g_padded = jnp.floor((g_tiles + (NSUB - 1.0)) * (1.0 / NSUB)) * NSUB
    g_start = jnp.sum(jnp.where(earlier_group, g_padded[None, :], 0.0), axis=1)
    g_end = g_start + g_padded
    b_start = (jnp.sum(jnp.where(member.T, g_start[None, :], 0.0), axis=1)
               + jnp.sum(jnp.where(in_group_before, n_tile_b[None, :], 0.0), axis=1))
    b_end = b_start + n_tile_b
    starts = (b_start * TMM).astype(jnp.int32)
    zrow = jnp.where(n_tile_b > 0, (b_end - 1.0) * TMM, -1.0).astype(jnp.int32)

    t = jnp.arange(n_tiles_max, dtype=jnp.int32).astype(F32)
    g_of_t = jnp.minimum(jnp.sum(jnp.where(g_end[None, :] <= t[:, None], 1, 0), axis=1), N_GROUPS - 1)
    g_hot = g_of_t[:, None] == jnp.arange(N_GROUPS, dtype=jnp.int32)[None, :]
    local_t = t - jnp.sum(jnp.where(g_hot, g_start[None, :], 0.0), axis=1)
    valid = (local_t < jnp.sum(jnp.where(g_hot, g_tiles[None, :], 0.0), axis=1)) & (t < jnp.sum(g_padded))
    bucket = jnp.sum(jnp.where(b_end[None, :] <= t[:, None], 1, 0), axis=1)
    bucket = jnp.where(valid, bucket, g_of_t * N_PAIRS + (N_PAIRS - 1))
    b_hot = bucket[:, None] == jnp.asarray(b_np, jnp.int32)[None, :]
    tlo = jnp.sum(jnp.where(b_hot, e_lo[None, :], 0), axis=1)
    thi = jnp.sum(jnp.where(b_hot, e_hi[None, :], 0), axis=1)
    pad = jnp.zeros((LANES - N_BUCKETS,), jnp.int32)
    return (jnp.concatenate([starts, pad]), jnp.concatenate([zrow, pad - 1]), tlo, thi,
            jnp.where(valid, 1, 0).astype(jnp.int32))


def kernel(x, c, w_ada, b_ada, w_in, b_in, conv_w, conv_b, conv_ln_g, conv_ln_b, conv_out_g, sinks,
           attn_out_g, w_out, b_out, ln1_g, ln1_b, w_router_group, b_router_group, w_router_expert,
           b_router_expert, w_gate, w_up, w_down, ln2_g, ln2_b):
    assert w_ada.shape[0] == DEPTH
    bsz, seq, _ = x.shape
    row = lambda v: v[0][None, :]

    mod = _modulation(c, w_ada[0], row(b_ada)).reshape(bsz, 6, 1, D_MODEL)
    sh1, sc1, g1, sh2, sc2, g2 = [mod[:, j:j + 1] for j in range(6)]

    q0 = 2 * D_CONV
    w_q = _pair_layout(w_in[0][:, q0:q0 + D_ATTN], 1)
    b_q = _pair_layout(b_in[0][q0:q0 + D_ATTN], 0)[None, :]
    og_attn = _pair_layout(attn_out_g[0], 0)[None, :]
    w_o = jnp.concatenate([w_out[0][:D_CONV], _pair_layout(w_out[0][D_CONV:], 0)], axis=0)

    y_conv, q, kv = _inproj_conv(x, sh1, sc1, w_in[0], row(b_in), w_q, b_q, conv_w[0], row(conv_b),
                                 row(conv_ln_g), row(conv_ln_b), row(conv_out_g))
    y_attn = _attention(q, kv, sinks[0], og_attn)

    pad = ROUTER_COLS - N_EXPERTS - N_GROUPS
    w_r = jnp.concatenate([w_router_expert[0], w_router_group[0], jnp.zeros((D_MODEL, pad), F32)], axis=1)
    b_r = jnp.concatenate([b_router_expert[0], b_router_group[0], jnp.zeros((pad,), F32)])[None, :]
    x1, h2t, code, counts = _outproj_route(x, y_conv, y_attn, w_o, row(b_out), g1, row(ln1_g), row(ln1_b),
                                      sc2, sh2, w_r, b_r)

    n_tiles_max = (bsz * seq) // TMM + N_BUCKETS + N_GROUPS * (NSUB - 1)
    n_tiles_max = -(-n_tiles_max // NSUB) * NSUB
    starts, zrow, tlo, thi, valid = _tile_tables(counts[0, :N_BUCKETS], n_tiles_max)
    h_sorted, pos = _scatter_rows(starts, zrow, valid, code[:, :, 0], h2t, n_tiles_max * TMM)
    y_sorted = _moe_sorted(tlo, thi, valid, h_sorted, w_gate[0], w_up[0], w_down[0], w_r, b_r)
    return _unsort_ln2(pos, x1, g2, row(ln2_g), row(ln2_b), y_sorted)
```

```python
import jax
import jax.numpy as jnp
import numpy as np
from jax import lax
from jax.experimental import pallas as pl
from jax.experimental.pallas import tpu as pltpu

F32 = jnp.float32
BF16 = jnp.bfloat16

D_MODEL = 1024
D_CONV = 512
CONV_WIDTH = 31
GROUP_DIM = 64
N_HEADS = 8
N_KV_HEADS = 2
HEAD_DIM = 64
D_ATTN = N_HEADS * HEAD_DIM
D_KV = N_KV_HEADS * HEAD_DIM
BLOCK = 128
D_IN = 2 * D_CONV + D_ATTN + 2 * D_KV
N_GROUPS = 4
EXPERTS_PER_GROUP = 8
N_EXPERTS = N_GROUPS * EXPERTS_PER_GROUP
D_EXPERT = D_MODEL // 4
DEPTH = 1
ALPHA = (2.0 * DEPTH) ** 0.25
EPS = 1e-5
NEG = -1e30

LANES = 128
SUBLANES = 8
TS = 512
TD = 1024
HALO = 32
CR = 32
IN_FIRST_ROWS = 384
AB = 16
PAIR_SLOTS = [j + (N_HEADS // N_KV_HEADS) * g for j in range(N_HEADS // N_KV_HEADS) for g in range(N_KV_HEADS)]
ROUTER_COLS = LANES
VMEM_LIMIT = 56 * 1024 * 1024

PAIRS = [(lo, hi) for lo in range(EXPERTS_PER_GROUP) for hi in range(lo + 1, EXPERTS_PER_GROUP)]
N_PAIRS = len(PAIRS)
N_BUCKETS = N_GROUPS * N_PAIRS
TMM = 128
OUT_SLICES = 4
NSUB = 4
CODE_SHIFT = 16


def _dot(a, b):
    return jnp.dot(a, b, preferred_element_type=F32)


def _dot_nt(a, b):
    return lax.dot_general(a, b, (((1,), (1,)), ((), ())), preferred_element_type=F32)


def _fill_group_maps(bsel_ref, bexp_ref):
    c = bsel_ref.shape[0]
    ch = lax.broadcasted_iota(jnp.int32, (c, LANES), 0) // GROUP_DIM
    gi = lax.broadcasted_iota(jnp.int32, (c, LANES), 1)
    bsel_ref[...] = jnp.where(ch == gi, 1.0 / GROUP_DIM, 0.0).astype(BF16)
    gi2 = lax.broadcasted_iota(jnp.int32, (LANES, c), 0)
    ch2 = lax.broadcasted_iota(jnp.int32, (LANES, c), 1) // GROUP_DIM
    bexp_ref[...] = jnp.where(ch2 == gi2, 1.0, 0.0).astype(BF16)


def _group_rms(y, bsel, bexp):
    ms = _dot((y * y).astype(BF16), bsel)
    r = lax.rsqrt(ms + EPS)
    r_hi = r.astype(BF16)
    r_lo = (r - r_hi.astype(F32)).astype(BF16)
    return y * (_dot(r_hi, bexp) + _dot(r_lo, bexp))


def _layer_norm(y, g, b):
    mu = jnp.mean(y, axis=-1, keepdims=True)
    d = y - mu
    var = jnp.mean(d * d, axis=-1, keepdims=True)
    return d * lax.rsqrt(var + EPS) * g + b


def _mod_kernel(c_ref, w_ref, b_ref, o_ref):
    c = c_ref[...]
    c_act = c * jax.nn.sigmoid(c)
    o_ref[...] = _dot(c_act.astype(BF16), w_ref[...].astype(BF16)) + b_ref[...]


def _modulation(c, w_ada, b_ada):
    bsz = c.shape[0]
    n = w_ada.shape[1]
    return pl.pallas_call(
        _mod_kernel,
        grid=(n // D_MODEL,),
        in_specs=[pl.BlockSpec((bsz, D_MODEL), lambda j: (0, 0)),
                  pl.BlockSpec((D_MODEL, D_MODEL), lambda j: (0, j)),
                  pl.BlockSpec((1, D_MODEL), lambda j: (0, j))],
        out_specs=pl.BlockSpec((bsz, D_MODEL), lambda j: (0, j)),
        out_shape=jax.ShapeDtypeStruct((bsz, n), F32),
        name="adaln_mod",
    )(c, w_ada, b_ada)


def _inproj_conv_kernel(x_ref, sh_ref, sc_ref, w_ref, b_ref, wq_ref, bq_ref, cw_ref, cb_ref, lg_ref, lb_ref, og_ref,
                        yc_ref, q_ref, kv_ref, wbf, gs, cacc, wb, bsel, bexp):
    first = (pl.program_id(0) == 0) & (pl.program_id(1) == 0)
    s = pl.program_id(1)

    @pl.when(first)
    def _():
        wbf[...] = w_ref[...].astype(BF16)
        wbf[:, 2 * D_CONV:2 * D_CONV + D_ATTN] = wq_ref[...].astype(BF16)
        wb[...] = jnp.broadcast_to(cw_ref[...][:, None, :], wb.shape)
        _fill_group_maps(bsel, bexp)

    n_ct = D_CONV // LANES

    @pl.when(s == 0)
    def _():
        gs[0, :, 0:HALO, :] = jnp.zeros((n_ct, HALO, LANES), F32)

    @pl.when(s > 0)
    def _():
        gs[0, :, 0:HALO, :] = gs[0, :, TS:TS + HALO, :]

    h = (x_ref[...] * (1.0 + sc_ref[...]) + sh_ref[...]).astype(BF16)
    q0 = 2 * D_CONV
    k0 = q0 + D_ATTN
    cb = cb_ref[...]
    lg = lg_ref[...]
    lb = lb_ref[...]
    og = og_ref[...]
    n_sh = TS + HALO - SUBLANES
    half = IN_FIRST_ROWS
    late_cols = 2 * LANES

    def glu_unit(rows, j):
        cs = slice(j * late_cols, (j + 1) * late_cols)
        u_a = _dot(h[rows], wbf[:, cs]) + b_ref[:, cs]
        u_b = _dot(h[rows], wbf[:, D_CONV + j * late_cols:D_CONV + (j + 1) * late_cols]) \
            + b_ref[:, D_CONV + j * late_cols:D_CONV + (j + 1) * late_cols]
        glu = u_a * jax.nn.sigmoid(u_b)
        for cc in range(late_cols // LANES):
            c = j * (late_cols // LANES) + cc
            gs[0, c, HALO + rows.start:HALO + rows.stop, :] = glu[:, cc * LANES:(cc + 1) * LANES]

    def late_proj(j):
        c0 = j * late_cols
        if c0 < D_ATTN:
            q = _dot(h, wbf[:, q0 + c0:q0 + c0 + late_cols]) + bq_ref[:, c0:c0 + late_cols]
            q_ref[:, c0:c0 + late_cols] = (q * (HEAD_DIM ** -0.5)).astype(BF16)
        else:
            kv_ref[...] = (_dot(h, wbf[:, k0:k0 + 2 * D_KV]) + b_ref[:, k0:k0 + 2 * D_KV]).astype(BF16)

    def shifted(lo, hi):
        for j in range(1, SUBLANES):
            gs[j, :, lo:hi, :] = gs[0, :, lo + j:hi + j, :]

    def conv_chunk(c, r0):
        ls = slice(c * LANES, (c + 1) * LANES)
        acc = jnp.broadcast_to(cb[:, ls], (CR, LANES)).reshape(CR // SUBLANES, SUBLANES, LANES)
        for k in range(CONV_WIDTH):
            off = HALO - (CONV_WIDTH - 1) + k
            a0 = r0 + (off // SUBLANES) * SUBLANES
            seg = gs[off % SUBLANES, c, a0:a0 + CR, :]
            acc = acc + wb[k, :, ls][None] * seg.reshape(CR // SUBLANES, SUBLANES, LANES)
        cacc[c, r0:r0 + CR, :] = acc.reshape(CR, LANES)

    first, second = slice(0, half), slice(half, TS)
    n_glu = D_CONV // late_cols
    n_late = (D_ATTN + 2 * D_KV) // late_cols
    for j in range(n_glu):
        glu_unit(first, j)
    split = HALO - SUBLANES + half
    shifted(0, split)
    chunks_a = [(c, r0) for c in range(n_ct) for r0 in range(0, half, CR)]
    chunks_b = [(c, r0) for c in range(n_ct) for r0 in range(half, TS, CR)]
    units_a = [lambda j=j: glu_unit(second, j) for j in range(n_glu)] + [lambda: late_proj(0)]
    units_b = [lambda j=j: late_proj(j) for j in range(1, n_late)]
    for chunks, units in ((chunks_a, units_a), (chunks_b, units_b)):
        if chunks is chunks_b:
            shifted(split, n_sh)
        every = len(chunks) // len(units)
        for n, (c, r0) in enumerate(chunks):
            conv_chunk(c, r0)
            if n % every == 0 and n // every < len(units):
                units[n // every]()
    y = _layer_norm(jnp.concatenate([cacc[c] for c in range(n_ct)], axis=1), lg, lb)
    y = y * jax.nn.sigmoid(y)
    yc_ref[...] = (_group_rms(y, bsel[...], bexp[...]) * og).astype(BF16)


def _inproj_conv(x, sh1, sc1, w_in, b_in, w_q, b_q, conv_w, conv_b, ln_g, ln_b, out_g):
    bsz, seq, _ = x.shape
    vec = lambda n: pl.BlockSpec((1, n), lambda b, s: (0, 0))
    mod = pl.BlockSpec((None, None, 1, D_MODEL), lambda b, s: (b, 0, 0, 0))
    tile = lambda n: pl.BlockSpec((None, TS, n), lambda b, s: (b, s, 0))
    return pl.pallas_call(
        _inproj_conv_kernel,
        grid=(bsz, seq // TS),
        in_specs=[tile(D_MODEL), mod, mod,
                  pl.BlockSpec((D_MODEL, D_IN), lambda b, s: (0, 0)), vec(D_IN),
                  pl.BlockSpec((D_MODEL, D_ATTN), lambda b, s: (0, 0)), vec(D_ATTN),
                  pl.BlockSpec((CONV_WIDTH, D_CONV), lambda b, s: (0, 0)),
                  vec(D_CONV), vec(D_CONV), vec(D_CONV), vec(D_CONV)],
        out_specs=[tile(D_CONV), tile(D_ATTN), tile(2 * D_KV)],
        out_shape=[jax.ShapeDtypeStruct((bsz, seq, D_CONV), BF16),
                   jax.ShapeDtypeStruct((bsz, seq, D_ATTN), BF16),
                   jax.ShapeDtypeStruct((bsz, seq, 2 * D_KV), BF16)],
        scratch_shapes=[pltpu.VMEM((D_MODEL, D_IN), BF16),
                        pltpu.VMEM((SUBLANES, D_CONV // LANES, TS + HALO, LANES), F32),
                        pltpu.VMEM((D_CONV // LANES, TS, LANES), F32),
                        pltpu.VMEM((CONV_WIDTH, SUBLANES, D_CONV), F32),
                        pltpu.VMEM((D_CONV, LANES), BF16),
                        pltpu.VMEM((LANES, D_CONV), BF16)],
        compiler_params=pltpu.CompilerParams(
            dimension_semantics=("arbitrary", "arbitrary"), vmem_limit_bytes=VMEM_LIMIT),
        name="inproj_conv",
    )(x, sh1, sc1, w_in, b_in, w_q, b_q, conv_w, conv_b, ln_g, ln_b, out_g)


def _attn_kernel(sinks_ref, q_ref, kvc_ref, kvp_ref, og_ref, o_ref, bsel, bexp):
    n = pl.program_id(1)

    @pl.when((pl.program_id(0) == 0) & (n == 0))
    def _():
        _fill_group_maps(bsel, bexp)

    qr = lax.broadcasted_iota(jnp.int32, (BLOCK, BLOCK), 0)
    kc = lax.broadcasted_iota(jnp.int32, (BLOCK, BLOCK), 1)
    from_prev = kc > qr
    lane = lax.broadcasted_iota(jnp.int32, (BLOCK, LANES), 1)
    lower = lane < HEAD_DIM
    lower_bf = jnp.where(lower, 1.0, 0.0).astype(BF16)
    upper_bf = jnp.where(lower, 0.0, 1.0).astype(BF16)
    og = og_ref[...]
    blocks = range(AB)
    kvs = []
    for blk in blocks:
        r0 = blk * BLOCK
        if blk == 0:
            kvs.append(jnp.concatenate([kvp_ref[...], kvc_ref[0:BLOCK, :]], axis=0))
        else:
            kvs.append(kvc_ref[r0 - BLOCK:r0 + BLOCK, :])
    s_all = []
    for blk in blocks:
        r0 = blk * BLOCK
        parts = []
        for j in range(D_ATTN // LANES):
            col = q_ref[r0:r0 + BLOCK, j * LANES:(j + 1) * LANES]
            parts += [col * lower_bf, col * upper_bf]
        s_all.append(_dot_nt(jnp.concatenate(parts, axis=0), kvs[blk][:, 0:D_KV]))
    probs, dens = [], []
    for blk in blocks:
        pb = []
        den = jnp.ones((BLOCK, LANES), F32)
        for slot in range(N_HEADS):
            s_prev = s_all[blk][slot * BLOCK:(slot + 1) * BLOCK, 0:BLOCK]
            s_own = s_all[blk][slot * BLOCK:(slot + 1) * BLOCK, BLOCK:2 * BLOCK]
            if blk == 0:
                s_prev = jnp.where(n > 0, s_prev, NEG)
            sc = jnp.where(from_prev, s_prev, s_own)
            sink = sinks_ref[PAIR_SLOTS[slot]]
            m = jnp.maximum(jnp.max(sc, axis=-1, keepdims=True), sink)
            p = jnp.exp(sc - m)
            den = jnp.where(lane == slot, jnp.sum(p, axis=-1, keepdims=True) + jnp.exp(sink - m), den)
            p = p.astype(BF16)
            zero = jnp.zeros_like(p)
            pb.append(jnp.concatenate([jnp.where(from_prev, p, zero), jnp.where(from_prev, zero, p)], axis=1))
        probs.append(jnp.concatenate(pb, axis=0))
        dens.append(den)
    pvs = [_dot(probs[blk], kvs[blk][:, D_KV:2 * D_KV]) for blk in blocks]
    for blk in blocks:
        pv, den = pvs[blk], dens[blk]
        o = jnp.concatenate(
            [jnp.where(lower, pv[(2 * j) * BLOCK:(2 * j + 1) * BLOCK], pv[(2 * j + 1) * BLOCK:(2 * j + 2) * BLOCK])
             for j in range(D_ATTN // LANES)], axis=1)
        ms = _dot((o * o).astype(BF16), bsel[...])
        r = lax.rsqrt(ms + EPS * den * den)
        r_hi = r.astype(BF16)
        r_lo = (r - r_hi.astype(F32)).astype(BF16)
        scale = _dot(r_hi, bexp[...]) + _dot(r_lo, bexp[...])
        o_ref[blk * BLOCK:(blk + 1) * BLOCK, :] = (o * scale * og).astype(BF16)


def _attention(q, kv, sinks, out_g):
    bsz, seq, _ = q.shape
    rows = AB * BLOCK
    grid_spec = pltpu.PrefetchScalarGridSpec(
        num_scalar_prefetch=1,
        grid=(bsz, seq // rows),
        in_specs=[pl.BlockSpec((None, rows, D_ATTN), lambda b, n, sk: (b, n, 0)),
                  pl.BlockSpec((None, rows, 2 * D_KV), lambda b, n, sk: (b, n, 0)),
                  pl.BlockSpec((None, BLOCK, 2 * D_KV), lambda b, n, sk: (b, jnp.maximum(AB * n - 1, 0), 0)),
                  pl.BlockSpec((1, D_ATTN), lambda b, n, sk: (0, 0))],
        out_specs=pl.BlockSpec((None, rows, D_ATTN), lambda b, n, sk: (b, n, 0)),
        scratch_shapes=[pltpu.VMEM((D_ATTN, LANES), BF16), pltpu.VMEM((LANES, D_ATTN), BF16)],
    )
    return pl.pallas_call(
        _attn_kernel,
        grid_spec=grid_spec,
        out_shape=jax.ShapeDtypeStruct((bsz, seq, D_ATTN), BF16),
        compiler_params=pltpu.CompilerParams(dimension_semantics=("arbitrary", "arbitrary")),
        name="swa_attention",
    )(sinks, q, kv, kv, out_g)


def _group_softmax_top(logits, lane):
    is_g = (lane >= N_EXPERTS) & (lane < N_EXPERTS + N_GROUPS)
    gl = jnp.where(is_g, logits, NEG)
    gmax = jnp.max(gl, axis=-1, keepdims=True)
    p_top = 1.0 / jnp.sum(jnp.where(is_g, jnp.exp(gl - gmax), 0.0), axis=-1, keepdims=True)
    return is_g, gl, gmax, p_top


def _route_bucket(logits):
    rows = logits.shape[0]
    lane = lax.broadcasted_iota(jnp.int32, (rows, ROUTER_COLS), 1)
    lanef = lane.astype(F32)
    is_g, gl, gmax, _ = _group_softmax_top(logits, lane)
    gidx = jnp.min(jnp.where(is_g & (gl == gmax), lanef - N_EXPERTS, 99.0), axis=-1, keepdims=True)
    in_grp = (lane < N_EXPERTS) & ((lane // EXPERTS_PER_GROUP) == gidx.astype(jnp.int32))
    el = jnp.where(in_grp, logits, NEG)
    m1 = jnp.max(el, axis=-1, keepdims=True)
    i1 = jnp.min(jnp.where(in_grp & (el == m1), lanef, 999.0), axis=-1, keepdims=True)
    rest = in_grp & (lanef != i1)
    el2 = jnp.where(rest, logits, NEG)
    m2 = jnp.max(el2, axis=-1, keepdims=True)
    i2 = jnp.min(jnp.where(rest & (el2 == m2), lanef, 999.0), axis=-1, keepdims=True)
    lo = jnp.minimum(i1, i2) - EXPERTS_PER_GROUP * gidx
    hi = jnp.maximum(i1, i2) - EXPERTS_PER_GROUP * gidx
    pair = lo * (EXPERTS_PER_GROUP - 1) - lo * (lo - 1.0) * 0.5 + (hi - lo - 1.0)
    return gidx * N_PAIRS + pair, lanef


def _outproj_kernel(x_ref, yc_ref, ya_ref, w_ref, b_ref, g1_ref, lg_ref, lb_ref, sc2_ref, sh2_ref,
                    wr_ref, br_ref, x1_ref, h2t_ref, code_ref, counts_ref, wbf, wrbf, ltri, running):
    @pl.when((pl.program_id(0) == 0) & (pl.program_id(1) == 0))
    def _():
        wbf[...] = w_ref[...].astype(BF16)
        wrbf[...] = wr_ref[...].astype(BF16)
        r = lax.broadcasted_iota(jnp.int32, (TS, TS), 0)
        c = lax.broadcasted_iota(jnp.int32, (TS, TS), 1)
        ltri[...] = jnp.where(c < r, 1.0, 0.0).astype(BF16)
        running[...] = jnp.zeros_like(running)

    nsl = OUT_SLICES
    rows = TS // nsl
    sl = [slice(k * rows, (k + 1) * rows) for k in range(nsl)]
    mix = [_dot(yc_ref[r, :], wbf[0:D_CONV, :]) + _dot(ya_ref[r, :], wbf[D_CONV:, :]) + b_ref[...] for r in sl]
    x1 = [_layer_norm(ALPHA * x_ref[r, :] + g1_ref[...] * m, lg_ref[...], lb_ref[...]) for r, m in zip(sl, mix)]
    h2 = [v * (1.0 + sc2_ref[...]) + sh2_ref[...] for v in x1]
    for k in range(nsl):
        x1_ref[sl[k], :] = x1[k]
        _to_tiles(h2t_ref.at[pl.ds(k * rows * CHUNKS, rows * CHUNKS)], h2[k])
    logits = [_dot(v.astype(BF16), wrbf[...]) + br_ref[...] for v in h2]
    routed = [_route_bucket(lg) for lg in logits]
    mine = [lanef == bucket for bucket, lanef in routed]
    onehot = [jnp.where(mk, 1.0, 0.0) for mk in mine]
    within = [_dot(ltri[0:rows, 0:rows], oh.astype(BF16)) for oh in onehot]
    base = running[...]
    for k in range(nsl):
        rank = jnp.sum(jnp.where(mine[k], within[k] + base, 0.0), axis=-1, keepdims=True)
        base = base + jnp.sum(onehot[k], axis=0, keepdims=True)
        code = routed[k][0].astype(jnp.int32) * (1 << CODE_SHIFT) + rank.astype(jnp.int32)
        code_ref[sl[k], :] = jnp.broadcast_to(code, (rows, SUBLANES))
    running[...] = base
    counts_ref[...] = running[...]


def _outproj_route(x, y_conv, y_attn, w_out, b_out, g1, ln_g, ln_b, sc2, sh2, w_r, b_r):
    bsz, seq, _ = x.shape
    nst = seq // TS
    vec = lambda n: pl.BlockSpec((1, n), lambda b, s: (0, 0))
    mod = pl.BlockSpec((None, None, 1, D_MODEL), lambda b, s: (b, 0, 0, 0))
    tile = lambda n: pl.BlockSpec((None, TS, n), lambda b, s: (b, s, 0))
    return pl.pallas_call(
        _outproj_kernel,
        grid=(bsz, seq // TS),
        in_specs=[tile(D_MODEL), tile(D_CONV), tile(D_ATTN),
                  pl.BlockSpec((D_MODEL, D_MODEL), lambda b, s: (0, 0)), vec(D_MODEL),
                  mod, vec(D_MODEL), vec(D_MODEL), mod, mod,
                  pl.BlockSpec((D_MODEL, ROUTER_COLS), lambda b, s: (0, 0)), vec(ROUTER_COLS)],
        out_specs=[tile(D_MODEL), pl.BlockSpec((TS * CHUNKS, LANES), lambda b, s: (b * nst + s, 0)),
                   tile(SUBLANES), vec(LANES)],
        out_shape=[jax.ShapeDtypeStruct((bsz, seq, D_MODEL), F32),
                   jax.ShapeDtypeStruct((bsz * seq * CHUNKS, LANES), F32),
                   jax.ShapeDtypeStruct((bsz, seq, SUBLANES), jnp.int32),
                   jax.ShapeDtypeStruct((1, LANES), F32)],
        scratch_shapes=[pltpu.VMEM((D_MODEL, D_MODEL), BF16), pltpu.VMEM((D_MODEL, ROUTER_COLS), BF16),
                        pltpu.VMEM((TS, TS), BF16), pltpu.VMEM((1, LANES), F32)],
        compiler_params=pltpu.CompilerParams(
            dimension_semantics=("arbitrary", "arbitrary"), vmem_limit_bytes=VMEM_LIMIT),
        name="outproj_route",
    )(x, y_conv, y_attn, w_out, b_out, g1, ln_g, ln_b, sc2, sh2, w_r, b_r)


def _row_of(starts_ref, code):
    return (starts_ref[lax.shift_right_logical(code, jnp.int32(CODE_SHIFT))]
            + (code & ((1 << CODE_SHIFT) - 1)))


CHUNKS = D_MODEL // LANES
assert CHUNKS == SUBLANES


def _tok_rows(p):
    return pl.ds(pl.multiple_of(p * CHUNKS, CHUNKS), CHUNKS)


def _to_tiles(ref, x):
    for c in range(CHUNKS):
        ref[pl.ds(c, x.shape[0], stride=CHUNKS), :] = x[:, c * LANES:(c + 1) * LANES]


def _from_tiles(ref, n):
    return jnp.concatenate([ref[pl.ds(c, n, stride=CHUNKS), :] for c in range(CHUNKS)], axis=1)


def _wait_tokens(hbm_ref, n, sem):
    pltpu.make_async_copy(hbm_ref.at[pl.ds(0, n * CHUNKS)], hbm_ref.at[pl.ds(0, n * CHUNKS)], sem).wait()


def _scatter_kernel(starts_ref, zrow_ref, valid_ref, code_ref, h2t_ref, hs_ref, pos_ref, hbuf, zeros, sems, zsem):
    nsteps = pl.num_programs(0) * pl.num_programs(1)
    step = pl.program_id(0) * pl.num_programs(1) + pl.program_id(1)
    slot = step % 2

    def _tile_rows(tok):
        return pl.ds(pl.multiple_of(tok * CHUNKS, TMM * CHUNKS), TMM * CHUNKS)

    def zero_copy(b):
        return pltpu.make_async_copy(zeros, hs_ref.at[_tile_rows(zrow_ref[b])], zsem)

    def tail_copy(t):
        return pltpu.make_async_copy(zeros, hs_ref.at[_tile_rows(t * TMM)], zsem)

    @pl.when(step == 0)
    def _():
        zeros[...] = jnp.zeros_like(zeros)
        n_tiles = hs_ref.shape[0] // (TMM * CHUNKS)

        def start(b, c):
            @pl.when(zrow_ref[b] >= 0)
            def _():
                zero_copy(b).start()
            return c

        def wait(b, c):
            @pl.when(zrow_ref[b] >= 0)
            def _():
                zero_copy(b).wait()
            return c

        def start_tail(t, c):
            @pl.when(valid_ref[t] == 0)
            def _():
                tail_copy(t).start()
            return c

        def wait_tail(t, c):
            @pl.when(valid_ref[t] == 0)
            def _():
                tail_copy(t).wait()
            return c

        lax.fori_loop(0, N_BUCKETS, start, 0)
        lax.fori_loop(0, n_tiles, start_tail, 0)
        lax.fori_loop(0, N_BUCKETS, wait, 0)
        lax.fori_loop(0, n_tiles, wait_tail, 0)

    @pl.when(step >= 2)
    def _():
        _wait_tokens(hs_ref, TD, sems.at[slot])

    hbuf[slot] = h2t_ref[...]

    for r in range(TD):
        pos = _row_of(starts_ref, code_ref[0, r])
        pos_ref[0, r] = pos
        pltpu.make_async_copy(hbuf.at[slot, _tok_rows(r)], hs_ref.at[_tok_rows(pos)],
                              sems.at[slot]).start(priority=r % 2)

    @pl.when(step == nsteps - 1)
    def _():
        _wait_tokens(hs_ref, TD, sems.at[slot])
        _wait_tokens(hs_ref, TD, sems.at[1 - slot])


def _scatter_rows(starts, zrow, valid, code, h2t, n_rows):
    bsz, seq = code.shape
    nst = seq // TD
    grid_spec = pltpu.PrefetchScalarGridSpec(
        num_scalar_prefetch=3,
        grid=(bsz, nst),
        in_specs=[pl.BlockSpec((None, 1, TD), lambda b, s, *_: (b * nst + s, 0, 0), memory_space=pltpu.SMEM),
                  pl.BlockSpec((TD * CHUNKS, LANES), lambda b, s, *_: (b * nst + s, 0))],
        out_specs=[pl.BlockSpec(memory_space=pl.ANY),
                   pl.BlockSpec((None, 1, TD), lambda b, s, *_: (b * nst + s, 0, 0), memory_space=pltpu.SMEM)],
        scratch_shapes=[pltpu.VMEM((2, TD * CHUNKS, LANES), F32), pltpu.VMEM((TMM * CHUNKS, LANES), F32),
                        pltpu.SemaphoreType.DMA((2,)), pltpu.SemaphoreType.DMA],
    )
    return pl.pallas_call(
        _scatter_kernel,
        grid_spec=grid_spec,
        out_shape=[jax.ShapeDtypeStruct((n_rows * CHUNKS, LANES), F32),
                   jax.ShapeDtypeStruct((bsz * nst, 1, TD), jnp.int32)],
        compiler_params=pltpu.CompilerParams(dimension_semantics=("arbitrary", "arbitrary")),
        name="moe_scatter",
    )(starts, zrow, valid, code.reshape(bsz * nst, 1, TD), h2t)


def _moe_kernel(tlo_ref, thi_ref, valid_ref, h_ref, wg_hbm, wu_hbm, wd_hbm, wr_ref, br_ref, y_ref,
                wrbf, wg, wu, wd, sg, su, sd, sems):
    i = pl.program_id(0)
    t0 = i * NSUB
    group = lax.shift_right_logical(tlo_ref[t0], jnp.int32(3))
    prev_group = lax.shift_right_logical(tlo_ref[jnp.maximum(t0 - NSUB, 0)], jnp.int32(3))

    @pl.when(i == 0)
    def _():
        wrbf[...] = wr_ref[...].astype(BF16)

    @pl.when((i == 0) | (group != prev_group))
    def _():
        def copies(e, slot):
            ex = group * EXPERTS_PER_GROUP + e
            return (pltpu.make_async_copy(wg_hbm.at[ex], sg.at[slot], sems.at[slot, 0]),
                    pltpu.make_async_copy(wu_hbm.at[ex], su.at[slot], sems.at[slot, 1]),
                    pltpu.make_async_copy(wd_hbm.at[ex], sd.at[slot], sems.at[slot, 2]))

        for cp in copies(0, 0):
            cp.start()

        def land(e, c):
            slot = e % 2

            @pl.when(e + 1 < EXPERTS_PER_GROUP)
            def _():
                for cp in copies(e + 1, 1 - slot):
                    cp.start()

            for cp in copies(e, slot):
                cp.wait()
            wg[e] = sg[slot].astype(BF16)
            wu[e] = su[slot].astype(BF16)
            wd[e] = sd[slot].astype(BF16)
            return c

        lax.fori_loop(0, EXPERTS_PER_GROUP, land, 0)

    @pl.when(valid_ref[t0] == 1)
    def _():
        lane = lax.broadcasted_iota(jnp.int32, (TMM, ROUTER_COLS), 1)
        local = EXPERTS_PER_GROUP - 1
        subs = range(NSUB)
        tile = lambda ref, j: ref.at[pl.ds(j * TMM * CHUNKS, TMM * CHUNKS)]
        hs = [_from_tiles(tile(h_ref, j), TMM).astype(BF16) for j in subs]
        es = [(tlo_ref[t0 + j] & local, thi_ref[t0 + j] & local) for j in subs]
        pre = [[(_dot(hs[j], wg[e]), _dot(hs[j], wu[e])) for e in es[j]] for j in subs]
        act = [[(a * jax.nn.sigmoid(a) * u).astype(BF16) for a, u in pre[j]] for j in subs]
        ys = [[_dot(act[j][k], wd[es[j][k]]) for k in (0, 1)] for j in subs]
        for j in subs:
            logits = _dot(hs[j], wrbf[...]) + br_ref[...]
            p_top = _group_softmax_top(logits, lane)[3]
            l_lo = jnp.sum(jnp.where(lane == tlo_ref[t0 + j], logits, 0.0), axis=-1, keepdims=True)
            l_hi = jnp.sum(jnp.where(lane == thi_ref[t0 + j], logits, 0.0), axis=-1, keepdims=True)
            w_lo = p_top / (1.0 + jnp.exp(l_hi - l_lo))
            w_hi = p_top / (1.0 + jnp.exp(l_lo - l_hi))
            _to_tiles(tile(y_ref, j), w_lo * ys[j][0] + w_hi * ys[j][1])

    @pl.when(valid_ref[t0] == 0)
    def _():
        y_ref[...] = jnp.zeros_like(y_ref)


def _moe_sorted(tlo, thi, valid, h_sorted, w_gate, w_up, w_down, w_r, b_r):
    n_rows = h_sorted.shape[0] // CHUNKS
    rows = pl.BlockSpec((NSUB * TMM * CHUNKS, LANES), lambda i, *_: (i, 0))
    hbm = pl.BlockSpec(memory_space=pl.ANY)
    w_in_shape, w_out_shape = (D_MODEL, D_EXPERT), (D_EXPERT, D_MODEL)
    grid_spec = pltpu.PrefetchScalarGridSpec(
        num_scalar_prefetch=3,
        grid=(n_rows // (NSUB * TMM),),
        in_specs=[rows, hbm, hbm, hbm,
                  pl.BlockSpec((D_MODEL, ROUTER_COLS), lambda i, *_: (0, 0)),
                  pl.BlockSpec((1, ROUTER_COLS), lambda i, *_: (0, 0))],
        out_specs=rows,
        scratch_shapes=[pltpu.VMEM((D_MODEL, ROUTER_COLS), BF16),
                        pltpu.VMEM((EXPERTS_PER_GROUP,) + w_in_shape, BF16),
                        pltpu.VMEM((EXPERTS_PER_GROUP,) + w_in_shape, BF16),
                        pltpu.VMEM((EXPERTS_PER_GROUP,) + w_out_shape, BF16),
                        pltpu.VMEM((2,) + w_in_shape, F32), pltpu.VMEM((2,) + w_in_shape, F32),
                        pltpu.VMEM((2,) + w_out_shape, F32), pltpu.SemaphoreType.DMA((2, 3))],
    )
    return pl.pallas_call(
        _moe_kernel,
        grid_spec=grid_spec,
        out_shape=jax.ShapeDtypeStruct((n_rows * CHUNKS, LANES), F32),
        compiler_params=pltpu.CompilerParams(dimension_semantics=("arbitrary",), vmem_limit_bytes=VMEM_LIMIT),
        name="moe_sorted",
    )(tlo, thi, valid, h_sorted, w_gate, w_up, w_down, w_r, b_r)


def _final_kernel(pos_ref, posn_ref, x1_ref, g2_ref, lg_ref, lb_ref, ys_ref, o_ref, ybuf, sems):
    nsteps = pl.num_programs(0) * pl.num_programs(1)
    step = pl.program_id(0) * pl.num_programs(1) + pl.program_id(1)
    slot = step % 2

    def gather(rows_ref, sl):
        def issue(i, c):
            for u in range(SUBLANES):
                r = i * SUBLANES + u
                pltpu.make_async_copy(ys_ref.at[_tok_rows(rows_ref[0, r])], ybuf.at[sl, _tok_rows(r)],
                                      sems.at[sl]).start(priority=u % 2)
            return c

        lax.fori_loop(0, TS // SUBLANES, issue, 0)

    @pl.when(step == 0)
    def _():
        gather(pos_ref, slot)

    @pl.when(step + 1 < nsteps)
    def _():
        gather(posn_ref, 1 - slot)

    _wait_tokens(ys_ref, TS, sems.at[slot])
    y = _from_tiles(ybuf.at[slot], TS)
    o_ref[...] = _layer_norm(ALPHA * x1_ref[...] + g2_ref[...] * y, lg_ref[...], lb_ref[...])


def _unsort_ln2(pos, x1, g2, ln_g, ln_b, y_sorted):
    bsz, seq, _ = x1.shape
    nst = seq // TS
    last = bsz * nst - 1
    return pl.pallas_call(
        _final_kernel,
        grid=(bsz, nst),
        in_specs=[pl.BlockSpec((None, 1, TS), lambda b, s: (b * nst + s, 0, 0), memory_space=pltpu.SMEM),
                  pl.BlockSpec((None, 1, TS), lambda b, s: (jnp.minimum(b * nst + s + 1, last), 0, 0),
                               memory_space=pltpu.SMEM),
                  pl.BlockSpec((None, TS, D_MODEL), lambda b, s: (b, s, 0)),
                  pl.BlockSpec((None, None, 1, D_MODEL), lambda b, s: (b, 0, 0, 0)),
                  pl.BlockSpec((1, D_MODEL), lambda b, s: (0, 0)),
                  pl.BlockSpec((1, D_MODEL), lambda b, s: (0, 0)),
                  pl.BlockSpec(memory_space=pl.ANY)],
        out_specs=pl.BlockSpec((None, TS, D_MODEL), lambda b, s: (b, s, 0)),
        out_shape=jax.ShapeDtypeStruct((bsz, seq, D_MODEL), F32),
        scratch_shapes=[pltpu.VMEM((2, TS * CHUNKS, LANES), F32), pltpu.SemaphoreType.DMA((2,))],
        compiler_params=pltpu.CompilerParams(dimension_semantics=("arbitrary", "arbitrary")),
        name="unsort_ln2",
    )(pos.reshape(-1, 1, TS), pos.reshape(-1, 1, TS), x1, g2, ln_g, ln_b, y_sorted)


def _pair_layout(a, axis):
    shp = a.shape
    a = a.reshape(shp[:axis] + (N_KV_HEADS, N_HEADS // N_KV_HEADS, HEAD_DIM) + shp[axis + 1:])
    return jnp.swapaxes(a, axis, axis + 1).reshape(shp)


def _tile_tables(counts, n_tiles_max):
    b_np = np.arange(N_BUCKETS)
    g_np = b_np // N_PAIRS
    in_group_before = jnp.asarray((g_np[None, :] == g_np[:, None]) & (b_np[None, :] < b_np[:, None]))
    member = jnp.asarray(g_np[None, :] == np.arange(N_GROUPS)[:, None])
    earlier_group = jnp.asarray(np.arange(N_GROUPS)[None, :] < np.arange(N_GROUPS)[:, None])
    e_lo = jnp.asarray(g_np * EXPERTS_PER_GROUP + np.asarray(PAIRS)[b_np % N_PAIRS, 0], jnp.int32)
    e_hi = jnp.asarray(g_np * EXPERTS_PER_GROUP + np.asarray(PAIRS)[b_np % N_PAIRS, 1], jnp.int32)

    n_tile_b = jnp.floor((counts + (TMM - 1.0)) * (1.0 / TMM))
    g_tiles = jnp.sum(jnp.where(member, n_tile_b[None, :], 0.0), axis=1)
    g_padded = jnp.floor((g_tiles + (NSUB - 1.0)) * (1.0 / NSUB)) * NSUB
    g_start = jnp.sum(jnp.where(earlier_group, g_padded[None, :], 0.0), axis=1)
    g_end = g_start + g_padded
    b_start = (jnp.sum(jnp.where(member.T, g_start[None, :], 0.0), axis=1)
               + jnp.sum(jnp.where(in_group_before, n_tile_b[None, :], 0.0), axis=1))
    b_end = b_start + n_tile_b
    starts = (b_start * TMM).astype(jnp.int32)
    zrow = jnp.where(n_tile_b > 0, (b_end - 1.0) * TMM, -1.0).astype(jnp.int32)

    t = jnp.arange(n_tiles_max, dtype=jnp.int32).astype(F32)
    g_of_t = jnp.minimum(jnp.sum(jnp.where(g_end[None, :] <= t[:, None], 1, 0), axis=1), N_GROUPS - 1)
    g_hot = g_of_t[:, None] == jnp.arange(N_GROUPS, dtype=jnp.int32)[None, :]
    local_t = t - jnp.sum(jnp.where(g_hot, g_start[None, :], 0.0), axis=1)
    valid = (local_t < jnp.sum(jnp.where(g_hot, g_tiles[None, :], 0.0), axis=1)) & (t < jnp.sum(g_padded))
    bucket = jnp.sum(jnp.where(b_end[None, :] <= t[:, None], 1, 0), axis=1)
    bucket = jnp.where(valid, bucket, g_of_t * N_PAIRS + (N_PAIRS - 1))
    b_hot = bucket[:, None] == jnp.asarray(b_np, jnp.int32)[None, :]
    tlo = jnp.sum(jnp.where(b_hot, e_lo[None, :], 0), axis=1)
    thi = jnp.sum(jnp.where(b_hot, e_hi[None, :], 0), axis=1)
    pad = jnp.zeros((LANES - N_BUCKETS,), jnp.int32)
    return (jnp.concatenate([starts, pad]), jnp.concatenate([zrow, pad - 1]), tlo, thi,
            jnp.where(valid, 1, 0).astype(jnp.int32))


def kernel(x, c, w_ada, b_ada, w_in, b_in, conv_w, conv_b, conv_ln_g, conv_ln_b, conv_out_g, sinks,
           attn_out_g, w_out, b_out, ln1_g, ln1_b, w_router_group, b_router_group, w_router_expert,
           b_router_expert, w_gate, w_up, w_down, ln2_g, ln2_b):
    assert w_ada.shape[0] == DEPTH
    bsz, seq, _ = x.shape
    row = lambda v: v[0][None, :]

    mod = _modulation(c, w_ada[0], row(b_ada)).reshape(bsz, 6, 1, D_MODEL)
    sh1, sc1, g1, sh2, sc2, g2 = [mod[:, j:j + 1] for j in range(6)]

    q0 = 2 * D_CONV
    w_q = _pair_layout(w_in[0][:, q0:q0 + D_ATTN], 1)
    b_q = _pair_layout(b_in[0][q0:q0 + D_ATTN], 0)[None, :]
    og_attn = _pair_layout(attn_out_g[0], 0)[None, :]
    w_o = jnp.concatenate([w_out[0][:D_CONV], _pair_layout(w_out[0][D_CONV:], 0)], axis=0)

    y_conv, q, kv = _inproj_conv(x, sh1, sc1, w_in[0], row(b_in), w_q, b_q, conv_w[0], row(conv_b),
                                 row(conv_ln_g), row(conv_ln_b), row(conv_out_g))
    y_attn = _attention(q, kv, sinks[0], og_attn)

    pad = ROUTER_COLS - N_EXPERTS - N_GROUPS
    w_r = jnp.concatenate([w_router_expert[0], w_router_group[0], jnp.zeros((D_MODEL, pad), F32)], axis=1)
    b_r = jnp.concatenate([b_router_expert[0], b_router_group[0], jnp.zeros((pad,), F32)])[None, :]
    x1, h2t, code, counts = _outproj_route(x, y_conv, y_attn, w_o, row(b_out), g1, row(ln1_g), row(ln1_b),
                                      sc2, sh2, w_r, b_r)

    n_tiles_max = (bsz * seq) // TMM + N_BUCKETS + N_GROUPS * (NSUB - 1)
    n_tiles_max = -(-n_tiles_max // NSUB) * NSUB
    starts, zrow, tlo, thi, valid = _tile_tables(counts[0, :N_BUCKETS], n_tiles_max)
    h_sorted, pos = _scatter_rows(starts, zrow, valid, code[:, :, 0], h2t, n_tiles_max * TMM)
    y_sorted = _moe_sorted(tlo, thi, valid, h_sorted, w_gate[0], w_up[0], w_down[0], w_r, b_r)
    return _unsort_ln2(pos, x1, g2, row(ln2_g), row(ln2_b), y_sorted)
```
